```python
import math
import jax, jax.numpy as jnp
from jax import lax
import numpy as np

D_MODEL = 1024
BATCH = 8
SEQ = 2048
DEPTH = 2
DEC_BATCH = 128
DEC_SEQ = 4
PAST_LEN = 16384
PAGE_SIZE = 128

N_AB_LAYERS = (DEPTH + 1) // 2
N_C_LAYERS = DEPTH // 2
S5_WIDTH = D_MODEL // 2
S5_GROUP_CH = 16
S5_GROUPS = S5_WIDTH // S5_GROUP_CH
S5_STATE = 64
RET_HEADS = 4
RET_DK = (D_MODEL // 2) // RET_HEADS
RET_DV = RET_DK
RET_QK = RET_HEADS * RET_DK
RET_WIDTH = RET_HEADS * RET_DV
RET_CHUNK = 128
HG_EXPAND = 128
HG_HEADS = D_MODEL // HG_EXPAND
HG_DK = HG_EXPAND
HG_DV = D_MODEL // HG_HEADS
HG_QF = HG_HEADS * HG_DK
HG_WIDTH = HG_HEADS * HG_DV
HG_CHUNK = 16
D_FF = 2816
CONV_W = 3
AB_IN = S5_WIDTH + 2 * RET_QK + 2 * RET_WIDTH
C_IN = 2 * HG_QF + 2 * HG_WIDTH
NORM_EPS = 1e-6
ROPE_BASE = 10000.0

kernel_name = 'hybrid_s5_retention_hgrn2_convffn_step'

F32 = jnp.float32


def _rmsnorm(x, w):
    xf = x.astype(F32)
    y = xf * lax.rsqrt(jnp.mean(xf * xf, axis=-1, keepdims=True) + NORM_EPS)
    return (y * w.astype(F32)).astype(x.dtype)


def _chunk_len(length, c):
    return c if length % c == 0 else length


def _to_chunks(t, c):
    b, l = t.shape[0], t.shape[1]
    return jnp.moveaxis(t.reshape((b, l // c, c) + t.shape[2:]), 1, 0)


def _from_chunks(t):
    nc, b, c = t.shape[0], t.shape[1], t.shape[2]
    return jnp.moveaxis(t, 0, 1).reshape((b, nc * c) + t.shape[3:])


def _rotary(t, pos):
    half = t.shape[-1] // 2
    inv = 1.0 / (ROPE_BASE ** jnp.linspace(0.0, 1.0, half, dtype=F32))
    ang = pos.astype(F32)[..., None] * inv
    cos = jnp.cos(ang)[:, :, None, :]
    sin = jnp.sin(ang)[:, :, None, :]
    t1, t2 = t[..., :half], t[..., half:]
    return jnp.concatenate([t1 * cos - t2 * sin, t1 * sin + t2 * cos], axis=-1)


def _s5(u, s_re0, s_im0, lam_re, lam_im, log_dt, b_re, b_im, c_re, c_im, d_skip, glu_w, glu_b):
    bsz, l, _ = u.shape
    ug = u.reshape(bsz, l, S5_GROUPS, S5_GROUP_CH)
    lam_re = lam_re.astype(F32)
    lam_im = lam_im.astype(F32)
    dt = jnp.exp(log_dt.astype(F32))[:, None]
    mag = jnp.exp(lam_re * dt)
    ang = lam_im * dt
    ab_re, ab_im = mag * jnp.cos(ang), mag * jnp.sin(ang)
    nr, ni = ab_re - 1.0, ab_im
    den = lam_re * lam_re + lam_im * lam_im
    f_re = (nr * lam_re + ni * lam_im) / den
    f_im = (ni * lam_re - nr * lam_im) / den
    b_re = b_re.astype(F32)
    b_im = b_im.astype(F32)
    bb_re = f_re[..., None] * b_re - f_im[..., None] * b_im
    bb_im = f_re[..., None] * b_im + f_im[..., None] * b_re
    bu_re = jnp.einsum('blgh,gph->blgp', ug, bb_re)
    bu_im = jnp.einsum('blgh,gph->blgp', ug, bb_im)
    s_re0 = s_re0.astype(F32)
    s_im0 = s_im0.astype(F32)
    bu_re = bu_re.at[:, 0].add(ab_re * s_re0 - ab_im * s_im0)
    bu_im = bu_im.at[:, 0].add(ab_re * s_im0 + ab_im * s_re0)
    a_re = jnp.broadcast_to(ab_re, bu_re.shape)
    a_im = jnp.broadcast_to(ab_im, bu_im.shape)

    def combine(e1, e2):
        a1r, a1i, b1r, b1i = e1
        a2r, a2i, b2r, b2i = e2
        return (a1r * a2r - a1i * a2i, a1r * a2i + a1i * a2r,
                a2r * b1r - a2i * b1i + b2r, a2r * b1i + a2i * b1r + b2i)

    _, _, s_re, s_im = lax.associative_scan(combine, (a_re, a_im, bu_re, bu_im), axis=1)
    y = (jnp.einsum('blgp,ghp->blgh', s_re, c_re.astype(F32))
         - jnp.einsum('blgp,ghp->blgh', s_im, c_im.astype(F32))
         + d_skip.astype(F32) * ug)
    y = jax.nn.gelu(y.reshape(bsz, l, S5_WIDTH))
    out = y * jax.nn.sigmoid(y @ glu_w.astype(F32) + glu_b.astype(F32))
    return out, s_re[:, -1], s_im[:, -1]


def _retention(q, k, v, s0):
    l = q.shape[1]
    c = _chunk_len(l, RET_CHUNK)
    lg = jnp.log(1.0 - 2.0 ** (-5.0 - jnp.arange(RET_HEADS, dtype=F32)))
    idx = jnp.arange(c)
    diff = idx[:, None] - idx[None, :]
    decay = jnp.where(diff[None] >= 0, jnp.exp(jnp.maximum(diff, 0)[None].astype(F32) * lg[:, None, None]), 0.0)
    q_dec = jnp.exp((idx + 1).astype(F32)[:, None] * lg[None])[None, :, :, None]
    k_dec = jnp.exp((c - 1 - idx).astype(F32)[:, None] * lg[None])[None, :, :, None]
    chunk_dec = jnp.exp(c * lg)[None, :, None, None]

    def step(s, inp):
        qc, kc, vc = inp
        scores = jnp.einsum('bihd,bjhd->bhij', qc, kc) * decay
        o = jnp.einsum('bhij,bjhe->bihe', scores, vc) + jnp.einsum('bihd,bhde->bihe', qc * q_dec, s)
        s = chunk_dec * s + jnp.einsum('bjhd,bjhe->bhde', kc * k_dec, vc)
        return s, o

    s, o = lax.scan(step, s0.astype(F32), (_to_chunks(q, c), _to_chunks(k, c), _to_chunks(v, c)))
    return _from_chunks(o), s


def _ab_mixer(h, pos, s_re0, s_im0, ret0, w_in, lam_re, lam_im, log_dt, b_re, b_im, c_re, c_im,
              d_skip, glu_w, glu_b, w_out):
    bsz, l, _ = h.shape
    z = (h @ w_in).astype(F32)
    cuts = [S5_WIDTH, S5_WIDTH + RET_QK, S5_WIDTH + 2 * RET_QK, S5_WIDTH + 2 * RET_QK + RET_WIDTH]
    u, q, k, v, g = jnp.split(z, cuts, axis=-1)
    y_s5, s_re, s_im = _s5(u, s_re0, s_im0, lam_re, lam_im, log_dt, b_re, b_im, c_re, c_im,
                           d_skip, glu_w, glu_b)
    q = _rotary(q.reshape(bsz, l, RET_HEADS, RET_DK), pos)
    k = _rotary(k.reshape(bsz, l, RET_HEADS, RET_DK), pos) * (RET_DK ** -0.5)
    v = v.reshape(bsz, l, RET_HEADS, RET_DV)
    o, ret_new = _retention(q, k, v, ret0)
    o = o * lax.rsqrt(jnp.mean(o * o, axis=-1, keepdims=True) + NORM_EPS)
    y_ret = jax.nn.silu(g) * o.reshape(bsz, l, RET_WIDTH)
    y = jnp.concatenate([y_s5, y_ret], axis=-1).astype(h.dtype) @ w_out
    return y, s_re, s_im, ret_new


def _hgrn_recurrence(q, k, log_f, v, s0):
    l = q.shape[1]
    c = _chunk_len(l, HG_CHUNK)
    causal = jnp.tril(jnp.ones((c, c), dtype=bool))

    def step(s, inp):
        qc, kc, lfc, vc = inp
        b = jnp.cumsum(lfc, axis=1)
        q_t = qc * jnp.exp(b)
        k_t = kc * jnp.exp(-b)
        scores = jnp.where(causal, jnp.einsum('bihd,bjhd->bhij', q_t, k_t), 0.0)
        o = jnp.einsum('bhij,bjhe->bihe', scores, vc) + jnp.einsum('bihd,bhde->bihe', q_t, s)
        b_last = b[:, -1]
        s = jnp.exp(b_last)[..., None] * s + jnp.einsum('bjhd,bjhe->bhde', kc * jnp.exp(b_last[:, None] - b), vc)
        return s, o

    s, o = lax.scan(step, s0.astype(F32), (_to_chunks(q, c), _to_chunks(k, c), _to_chunks(log_f, c), _to_chunks(v, c)))
    return _from_chunks(o), s


def _hgrn_mixer(h, s0, lb, w_in, norm_w, w_out):
    bsz, l, _ = h.shape
    z = (h @ w_in).astype(F32)
    q, fl, i, g = jnp.split(z, [HG_QF, 2 * HG_QF, 2 * HG_QF + HG_WIDTH], axis=-1)
    q = q.reshape(bsz, l, HG_HEADS, HG_DK)
    fl = fl.reshape(bsz, l, HG_HEADS, HG_DK)
    i = i.reshape(bsz, l, HG_HEADS, HG_DV)
    lbh = lb.reshape(HG_HEADS, HG_DK)
    forget = lbh + (1.0 - lbh) * jax.nn.sigmoid(fl)
    o, s = _hgrn_recurrence(q, 1.0 - forget, jnp.log(forget), i, s0)
    o = o * lax.rsqrt(jnp.mean(o * o, axis=-1, keepdims=True) + NORM_EPS) * norm_w.astype(F32)
    o = o.reshape(bsz, l, HG_WIDTH) * jax.nn.silu(g)
    return o.astype(h.dtype) @ w_out, s


def _conv_ffn(h, buf, w_gate, w_up, conv_w, conv_b, w_down):
    l = h.shape[1]
    gpre = h @ w_gate
    ext = jnp.concatenate([buf.astype(gpre.dtype), gpre], axis=1)
    conv = conv_b + conv_w[0] * ext[:, 0:l]
    for j in range(1, CONV_W):
        conv = conv + conv_w[j] * ext[:, j:j + l]
    y = (jax.nn.silu(conv) * (h @ w_up)) @ w_down
    return y, ext[:, l:]


def _trunk(x, pos, s5_re, s5_im, ret, hg, conv, p):
    new_re, new_im, new_ret, new_hg, new_conv = [], [], [], [], []
    lb_soft = jax.nn.softmax(p['hg_lb_logits'].astype(F32), axis=0)
    lb_all = jnp.cumsum(lb_soft, axis=0) - lb_soft[0]
    for layer in range(DEPTH):
        j = layer // 2
        h = _rmsnorm(x, p['norm_mix'][layer])
        if layer % 2 == 0:
            y, r, im, st = _ab_mixer(h, pos, s5_re[j], s5_im[j], ret[j], p['w_in_ab'][j],
                                     p['s5_lam_re'][j], p['s5_lam_im'][j], p['s5_log_dt'][j],
                                     p['s5_b_re'][j], p['s5_b_im'][j], p['s5_c_re'][j], p['s5_c_im'][j],
                                     p['s5_d'][j], p['s5_glu_w'][j], p['s5_glu_b'][j], p['w_out_ab'][j])
            new_re.append(r)
            new_im.append(im)
            new_ret.append(st)
        else:
            y, st = _hgrn_mixer(h, hg[j], lb_all[layer], p['w_in_c'][j], p['hg_norm_w'][j], p['w_out_c'][j])
            new_hg.append(st)
        x = x + y.astype(x.dtype)
        h = _rmsnorm(x, p['norm_ffn'][layer])
        y, buf = _conv_ffn(h, conv[layer], p['ffn_w_gate'][layer], p['ffn_w_up'][layer],
                           p['ffn_conv_w'][layer], p['ffn_conv_b'][layer], p['ffn_w_down'][layer])
        new_conv.append(buf)
        x = x + y.astype(x.dtype)
    x = _rmsnorm(x, p['norm_final'])
    return x, jnp.stack(new_re), jnp.stack(new_im), jnp.stack(new_ret), jnp.stack(new_hg), jnp.stack(new_conv)


def setup_inputs(seed: int = 0) -> dict:
    key = jax.random.key(seed)
    ks = iter(jax.random.split(key, 48))

    def nrm(shape, scale):
        return scale * jax.random.normal(next(ks), shape, F32)

    x_prompt = nrm((BATCH, SEQ, D_MODEL), 1.0)
    x_sample = nrm((DEC_BATCH, DEC_SEQ, D_MODEL), 1.0)
    state_s5_re = nrm((N_AB_LAYERS, DEC_BATCH, S5_GROUPS, S5_STATE), 0.1)
    state_s5_im = nrm((N_AB_LAYERS, DEC_BATCH, S5_GROUPS, S5_STATE), 0.1)
    state_ret = nrm((N_AB_LAYERS, DEC_BATCH, RET_HEADS, RET_DK, RET_DV), 0.5)
    state_hgrn = nrm((N_C_LAYERS, DEC_BATCH, HG_HEADS, HG_DK, HG_DV), 0.5)
    state_ffn_conv = nrm((DEPTH, DEC_BATCH, CONV_W - 1, D_FF), 1.0)
    pos_sample = jnp.full((DEC_BATCH,), PAST_LEN, dtype=jnp.int32)
    norm_mix = 1.0 + nrm((DEPTH, D_MODEL), 0.01)
    norm_ffn = 1.0 + nrm((DEPTH, D_MODEL), 0.01)
    norm_final = 1.0 + nrm((D_MODEL,), 0.01)
    w_in_ab = nrm((N_AB_LAYERS, D_MODEL, AB_IN), D_MODEL ** -0.5)
    s5_lam_re = -0.5 + nrm((N_AB_LAYERS, S5_GROUPS, S5_STATE), 0.01)
    s5_lam_im = math.pi * jnp.arange(S5_STATE, dtype=F32) + nrm((N_AB_LAYERS, S5_GROUPS, S5_STATE), 0.01)
    s5_log_dt = jax.random.uniform(next(ks), (N_AB_LAYERS, S5_GROUPS), F32, math.log(1e-3), math.log(1e-1))
    s5_b_re = nrm((N_AB_LAYERS, S5_GROUPS, S5_STATE, S5_GROUP_CH), (2 * S5_GROUP_CH) ** -0.5)
    s5_b_im = nrm((N_AB_LAYERS, S5_GROUPS, S5_STATE, S5_GROUP_CH), (2 * S5_GROUP_CH) ** -0.5)
    s5_c_re = nrm((N_AB_LAYERS, S5_GROUPS, S5_GROUP_CH, S5_STATE), S5_STATE ** -0.5)
    s5_c_im = nrm((N_AB_LAYERS, S5_GROUPS, S5_GROUP_CH, S5_STATE), S5_STATE ** -0.5)
    s5_d = nrm((N_AB_LAYERS, S5_GROUPS, S5_GROUP_CH), 1.0)
    s5_glu_w = nrm((N_AB_LAYERS, S5_WIDTH, S5_WIDTH), S5_WIDTH ** -0.5)
    s5_glu_b = nrm((N_AB_LAYERS, S5_WIDTH), 0.01)
    w_out_ab = nrm((N_AB_LAYERS, S5_WIDTH + RET_WIDTH, D_MODEL), (S5_WIDTH + RET_WIDTH) ** -0.5)
    w_in_c = nrm((N_C_LAYERS, D_MODEL, C_IN), D_MODEL ** -0.5)
    hg_lb_logits = nrm((DEPTH, HG_QF), 0.1)
    hg_norm_w = 1.0 + nrm((N_C_LAYERS, HG_DV), 0.01)
    w_out_c = nrm((N_C_LAYERS, HG_WIDTH, D_MODEL), HG_WIDTH ** -0.5)
    ffn_w_gate = nrm((DEPTH, D_MODEL, D_FF), D_MODEL ** -0.5)
    ffn_w_up = nrm((DEPTH, D_MODEL, D_FF), D_MODEL ** -0.5)
    ffn_conv_w = nrm((DEPTH, CONV_W, D_FF), CONV_W ** -0.5)
    ffn_conv_b = nrm((DEPTH, D_FF), 0.01)
    ffn_w_down = nrm((DEPTH, D_FF, D_MODEL), D_FF ** -0.5)
    return {'x_prompt': x_prompt, 'x_sample': x_sample, 'state_s5_re': state_s5_re,
            'state_s5_im': state_s5_im, 'state_ret': state_ret, 'state_hgrn': state_hgrn,
            'state_ffn_conv': state_ffn_conv, 'pos_sample': pos_sample,
            'norm_mix': norm_mix, 'norm_ffn': norm_ffn, 'norm_final': norm_final,
            'w_in_ab': w_in_ab, 's5_lam_re': s5_lam_re, 's5_lam_im': s5_lam_im, 's5_log_dt': s5_log_dt,
            's5_b_re': s5_b_re, 's5_b_im': s5_b_im, 's5_c_re': s5_c_re, 's5_c_im': s5_c_im,
            's5_d': s5_d, 's5_glu_w': s5_glu_w, 's5_glu_b': s5_glu_b, 'w_out_ab': w_out_ab,
            'w_in_c': w_in_c, 'hg_lb_logits': hg_lb_logits, 'hg_norm_w': hg_norm_w, 'w_out_c': w_out_c,
            'ffn_w_gate': ffn_w_gate, 'ffn_w_up': ffn_w_up, 'ffn_conv_w': ffn_conv_w,
            'ffn_conv_b': ffn_conv_b, 'ffn_w_down': ffn_w_down}


def reference(x_prompt, x_sample, state_s5_re, state_s5_im, state_ret, state_hgrn, state_ffn_conv,
              pos_sample, norm_mix, norm_ffn, norm_final, w_in_ab, s5_lam_re, s5_lam_im, s5_log_dt,
              s5_b_re, s5_b_im, s5_c_re, s5_c_im, s5_d, s5_glu_w, s5_glu_b, w_out_ab, w_in_c,
              hg_lb_logits, hg_norm_w, w_out_c, ffn_w_gate, ffn_w_up, ffn_conv_w, ffn_conv_b, ffn_w_down):
    p = {'norm_mix': norm_mix, 'norm_ffn': norm_ffn, 'norm_final': norm_final, 'w_in_ab': w_in_ab,
         's5_lam_re': s5_lam_re, 's5_lam_im': s5_lam_im, 's5_log_dt': s5_log_dt, 's5_b_re': s5_b_re,
         's5_b_im': s5_b_im, 's5_c_re': s5_c_re, 's5_c_im': s5_c_im, 's5_d': s5_d, 's5_glu_w': s5_glu_w,
         's5_glu_b': s5_glu_b, 'w_out_ab': w_out_ab, 'w_in_c': w_in_c, 'hg_lb_logits': hg_lb_logits,
         'hg_norm_w': hg_norm_w, 'w_out_c': w_out_c, 'ffn_w_gate': ffn_w_gate, 'ffn_w_up': ffn_w_up,
         'ffn_conv_w': ffn_conv_w, 'ffn_conv_b': ffn_conv_b, 'ffn_w_down': ffn_w_down}
    bp, lp = x_prompt.shape[0], x_prompt.shape[1]
    pos_p = jnp.broadcast_to(jnp.arange(lp, dtype=jnp.int32)[None, :], (bp, lp))
    z_re = jnp.zeros((N_AB_LAYERS, bp, S5_GROUPS, S5_STATE), F32)
    z_ret = jnp.zeros((N_AB_LAYERS, bp, RET_HEADS, RET_DK, RET_DV), F32)
    z_hg = jnp.zeros((N_C_LAYERS, bp, HG_HEADS, HG_DK, HG_DV), F32)
    z_conv = jnp.zeros((DEPTH, bp, CONV_W - 1, D_FF), x_prompt.dtype)
    y_prompt, re_p, im_p, ret_p, hg_p, conv_p = _trunk(x_prompt, pos_p, z_re, z_re, z_ret, z_hg, z_conv, p)
    ls = x_sample.shape[1]
    pos_s = pos_sample[:, None] + jnp.arange(ls, dtype=jnp.int32)[None, :]
    y_sample, re_s, im_s, ret_s, hg_s, conv_s = _trunk(x_sample, pos_s, state_s5_re, state_s5_im,
                                                       state_ret, state_hgrn, state_ffn_conv, p)
    return (y_prompt, y_sample, re_p, im_p, ret_p, hg_p, conv_p, re_s, im_s, ret_s, hg_s, conv_s)
```

```python
import functools

import jax
import jax.numpy as jnp
from jax import lax
from jax.experimental import pallas as pl
from jax.experimental.pallas import tpu as pltpu

F32 = jnp.float32
BF16 = jnp.bfloat16

NORM_EPS = 1e-6
ROPE_BASE = 10000.0
S5_GROUP_CH = 16
RET_HEADS = 4
RET_CHUNK = 128
HG_HEADS = 8
HG_BLOCK = 16
HG_CHUNK = 128
CONV_W = 3

LANES = 128
MXU_DIM = 256
VMEM_LIMIT_BYTES = 56 * 1024 * 1024
ROW_TILE = 512
FF_CHUNK = 256
S5_SCAN_COLS = 512


def _cparams(n_grid_dims):
    return pltpu.CompilerParams(dimension_semantics=("arbitrary",) * n_grid_dims,
                                vmem_limit_bytes=VMEM_LIMIT_BYTES)


def _dot(a, b):
    return jnp.dot(a.astype(BF16), b.astype(BF16), preferred_element_type=F32)


def _dot_nt(a, b):
    return lax.dot_general(a.astype(BF16), b.astype(BF16), (((1,), (1,)), ((), ())),
                           preferred_element_type=F32)


def _rmsnorm(x, w):
    return x * lax.rsqrt(jnp.mean(x * x, axis=-1, keepdims=True) + NORM_EPS) * w


def _full(shape):
    nd = len(shape)
    return pl.BlockSpec(shape, lambda *_: (0,) * nd)


def _proj_kernel(x_ref, nw_ref, w_ref, o_ref, *, col_chunk):
    h = _rmsnorm(x_ref[...], nw_ref[...]).astype(BF16)
    for c in range(0, w_ref.shape[1], col_chunk):
        o_ref[:, c:c + col_chunk] = jnp.dot(h, w_ref[:, c:c + col_chunk], preferred_element_type=F32)


def _proj(x, nw, w_bf):
    n, d = x.shape
    n_out = w_bf.shape[1]
    tm = min(ROW_TILE, n)
    return pl.pallas_call(
        functools.partial(_proj_kernel, col_chunk=2 * MXU_DIM),
        grid=(n // tm,),
        in_specs=[pl.BlockSpec((tm, d), lambda i: (i, 0)), _full((1, d)), _full((d, n_out))],
        out_specs=pl.BlockSpec((tm, n_out), lambda i: (i, 0)),
        out_shape=jax.ShapeDtypeStruct((n, n_out), F32),
        compiler_params=_cparams(1), name="proj")(x, nw, w_bf)


def _outproj_kernel(*refs, n_parts):
    x_ref, parts, w_ref, o_ref = refs[0], refs[1:1 + n_parts], refs[1 + n_parts], refs[2 + n_parts]
    y = jnp.concatenate([p[...].astype(BF16) for p in parts], axis=1)
    o_ref[...] = x_ref[...] + jnp.dot(y, w_ref[...], preferred_element_type=F32)


def _outproj(x, parts, w_bf):
    n, d = x.shape
    tm = min(ROW_TILE, n)
    in_specs = [pl.BlockSpec((tm, d), lambda i: (i, 0))]
    in_specs += [pl.BlockSpec((tm, p.shape[1]), lambda i: (i, 0)) for p in parts]
    in_specs += [_full(w_bf.shape)]
    return pl.pallas_call(
        functools.partial(_outproj_kernel, n_parts=len(parts)),
        grid=(n // tm,), in_specs=in_specs,
        out_specs=pl.BlockSpec((tm, d), lambda i: (i, 0)),
        out_shape=jax.ShapeDtypeStruct((n, d), F32),
        compiler_params=_cparams(1), name="outproj")(x, *parts, w_bf)


def _s5_prep_kernel(lre_ref, lim_ref, ldt_ref, lre16_ref, lim16_ref, ldt16_ref, bre_ref, bim_ref,
                    are_ref, aim_ref, bbre_ref, bbim_ref):
    def disc(lre, lim, ldt):
        dt = jnp.exp(ldt)
        mag = jnp.exp(lre * dt)
        ang = lim * dt
        return mag * jnp.cos(ang), mag * jnp.sin(ang)

    ab_re, ab_im = disc(lre_ref[...], lim_ref[...], ldt_ref[...])
    are_ref[...] = ab_re
    aim_ref[...] = ab_im
    lre, lim = lre16_ref[...], lim16_ref[...]
    ab_re, ab_im = disc(lre, lim, ldt16_ref[...])
    nr, ni = ab_re - 1.0, ab_im
    den = lre * lre + lim * lim
    f_re = (nr * lre + ni * lim) / den
    f_im = (ni * lre - nr * lim) / den
    b_re, b_im = bre_ref[...], bim_ref[...]
    bbre_ref[...] = f_re * b_re - f_im * b_im
    bbim_ref[...] = f_re * b_im + f_im * b_re


def _s5_prep(lam_re, lam_im, log_dt, b_re, b_im):
    g, p = lam_re.shape
    ch = b_re.shape[-1]
    rep = lambda a: jnp.repeat(a, ch, axis=0)
    ldt = log_dt.reshape(g, 1)
    bt = lambda b: jnp.swapaxes(b, 1, 2).reshape(g * ch, p)
    args = (lam_re, lam_im, ldt, rep(lam_re), rep(lam_im), rep(ldt), bt(b_re), bt(b_im))
    return pl.pallas_call(
        _s5_prep_kernel,
        in_specs=[_full(a.shape) for a in args],
        out_specs=[_full((g, p)), _full((g, p)), _full((g * ch, p)), _full((g * ch, p))],
        out_shape=[jax.ShapeDtypeStruct((g, p), F32)] * 2 + [jax.ShapeDtypeStruct((g * ch, p), F32)] * 2,
        name="s5_prep")(*args)


def _block_diag(m, n_blk):
    r, c = m.shape[0] // n_blk, m.shape[1]
    eye = jnp.eye(n_blk, dtype=m.dtype)
    return (m.reshape(n_blk, r, 1, c) * eye[:, None, :, None]).reshape(n_blk * r, n_blk * c)


def _s5_kernel(x_ref, nw_ref, wu_ref, bre_ref, bim_ref, cre_ref, cim_ref, d_ref, gw_ref, gb_ref,
               are_ref, aim_ref, s0re_ref, s0im_ref,
               y_ref, sre_ref, sim_ref, bure_s, buim_s, stre_s, stim_s, *, tb, tl):
    @pl.when(pl.program_id(0) == 0)
    def _():
        stre_s[...] = s0re_ref[...]
        stim_s[...] = s0im_ref[...]

    d = nw_ref.shape[-1]
    width = wu_ref.shape[1]
    x = jnp.concatenate([x_ref[:, t * d:(t + 1) * d] for t in range(tl)], axis=0)
    u = jnp.dot(_rmsnorm(x, nw_ref[...]).astype(BF16), wu_ref[...], preferred_element_type=F32)
    ub = u.astype(BF16)

    n_kb, kw, cw = bre_ref.shape
    for kb in range(n_kb):
        uk = ub[:, kb * kw:(kb + 1) * kw]
        bure_s[:, kb * cw:(kb + 1) * cw] = jnp.dot(uk, bre_ref[kb], preferred_element_type=F32)
        buim_s[:, kb * cw:(kb + 1) * cw] = jnp.dot(uk, bim_ref[kb], preferred_element_type=F32)

    n_state = are_ref.shape[1]
    sc = min(S5_SCAN_COLS, n_state)
    for c0 in range(0, n_state, sc):
        cs = slice(c0, c0 + sc)
        a_re = jnp.broadcast_to(are_ref[:, cs], (tb, sc))
        a_im = jnp.broadcast_to(aim_ref[:, cs], (tb, sc))

        def step(t, carry, cs=cs, a_re=a_re, a_im=a_im):
            s_re, s_im = carry
            rows = pl.ds(pl.multiple_of(t * tb, tb), tb)
            n_re = a_re * s_re - a_im * s_im + bure_s[rows, cs]
            n_im = a_re * s_im + a_im * s_re + buim_s[rows, cs]
            bure_s[rows, cs] = n_re
            buim_s[rows, cs] = n_im
            return n_re, n_im

        s_re, s_im = lax.fori_loop(0, tl, step, (stre_s[:, cs], stim_s[:, cs]), unroll=min(tl, 8))
        stre_s[:, cs] = s_re
        stim_s[:, cs] = s_im

    n_ob, ckw, ocw = cre_ref.shape
    ys = []
    for ob in range(n_ob):
        ks = slice(ob * ckw, (ob + 1) * ckw)
        ys.append(_dot(bure_s[:, ks], cre_ref[ob]) - _dot(buim_s[:, ks], cim_ref[ob]))
    y = jnp.concatenate(ys, axis=1) + d_ref[...] * u
    y = jax.nn.gelu(y)
    out = y * jax.nn.sigmoid(_dot(y, gw_ref[...]) + gb_ref[...])
    for t in range(tl):
        y_ref[:, t * width:(t + 1) * width] = out[t * tb:(t + 1) * tb, :]
    sre_ref[...] = stre_s[...]
    sim_ref[...] = stim_s[...]


def _s5_mixer(x3, nw, wu_bf, bmat_re, bmat_im, cmat_re, cmat_im, d_row, glu_w_bf, glu_b, a_re, a_im,
              s0_re, s0_im, tl):
    b, l, d = x3.shape
    width = wu_bf.shape[1]
    n_state = a_re.shape[1]
    rows = b * tl
    args = (x3.reshape(b, l * d), nw, wu_bf, bmat_re, bmat_im, cmat_re, cmat_im, d_row, glu_w_bf, glu_b,
            a_re, a_im, s0_re, s0_im)
    in_specs = [pl.BlockSpec((b, tl * d), lambda i: (0, i))] + [_full(a.shape) for a in args[1:]]
    y, s_re, s_im = pl.pallas_call(
        functools.partial(_s5_kernel, tb=b, tl=tl),
        grid=(l // tl,), in_specs=in_specs,
        out_specs=[pl.BlockSpec((b, tl * width), lambda i: (0, i)), _full((b, n_state)), _full((b, n_state))],
        out_shape=[jax.ShapeDtypeStruct((b, l * width), F32),
                   jax.ShapeDtypeStruct((b, n_state), F32), jax.ShapeDtypeStruct((b, n_state), F32)],
        scratch_shapes=[pltpu.VMEM((rows, n_state), F32), pltpu.VMEM((rows, n_state), F32),
                        pltpu.VMEM((b, n_state), F32), pltpu.VMEM((b, n_state), F32)],
        compiler_params=_cparams(1), name="s5_mixer")(*args)
    return y.reshape(b * l, width), s_re, s_im


def _rope_kernel(pos_ref, inv_ref, cc_ref, ss_ref):
    ang = pos_ref[...] * inv_ref[...]
    lane = lax.broadcasted_iota(jnp.int32, ang.shape, 1)
    cc_ref[...] = jnp.cos(ang)
    ss_ref[...] = jnp.where(lane < ang.shape[1] // 2, -jnp.sin(ang), jnp.sin(ang))


def _rope_tables(pos, dk):
    n = pos.shape[0]
    inv = 1.0 / (ROPE_BASE ** jnp.linspace(0.0, 1.0, dk // 2, dtype=F32))
    inv = jnp.concatenate([inv, inv]).reshape(1, dk)
    return pl.pallas_call(
        _rope_kernel,
        in_specs=[_full((n, 1)), _full((1, dk))],
        out_specs=[_full((n, dk)), _full((n, dk))],
        out_shape=[jax.ShapeDtypeStruct((n, dk), F32)] * 2,
        name="rope_tables")(pos.astype(F32).reshape(n, 1), inv)


def _rotate(t, cc, ss):
    return t * cc + pltpu.roll(t, t.shape[1] // 2, 1) * ss


def _ret_consts(rows, c):
    lg = jnp.log(1.0 - 2.0 ** (-5.0 - jnp.arange(RET_HEADS, dtype=F32)))
    r = jnp.arange(rows)
    idx, blk = r % c, r // c
    diff = idx[:, None] - idx[None, :]
    same = blk[:, None] == blk[None, :]
    decay = jnp.where((same & (diff >= 0))[None],
                      jnp.exp(jnp.maximum(diff, 0)[None].astype(F32) * lg[:, None, None]), 0.0)
    wide = lambda v: jnp.broadcast_to(v[:, :, None], (RET_HEADS, rows, LANES))
    q_dec = wide(jnp.exp((idx + 1).astype(F32)[None, :] * lg[:, None]))
    k_dec = wide(jnp.exp((c - 1 - idx).astype(F32)[None, :] * lg[:, None]))
    chunk_dec = jnp.broadcast_to(jnp.exp(c * lg)[:, None, None], (RET_HEADS, 8, LANES))
    return decay, q_dec, k_dec, chunk_dec


def _ret_gate(o, g):
    o = o * lax.rsqrt(jnp.mean(o * o, axis=-1, keepdims=True) + NORM_EPS)
    return jax.nn.silu(g) * o


def _ret_long_kernel(q_ref, k_ref, v_ref, g_ref, cc_ref, ss_ref, dec_ref, qd_ref, kd_ref, cd_ref, s0_ref,
                     y_ref, so_ref, *, c):
    scale = q_ref.shape[1] ** -0.5
    dec, qd, kd, cd = dec_ref[0], qd_ref[0], kd_ref[0], cd_ref[0, 0:1, :]

    def chunk(i, s):
        rows = pl.ds(pl.multiple_of(i * c, c), c)
        cc, ss = cc_ref[rows, :], ss_ref[rows, :]
        q = _rotate(q_ref[rows, :], cc, ss)
        k = _rotate(k_ref[rows, :], cc, ss) * scale
        v = v_ref[rows, :]
        scores = _dot_nt(q, k) * dec
        o = _dot(scores, v) + _dot(q * qd, s)
        s = cd * s + _dot((k * kd).T, v)
        y_ref[rows, :] = _ret_gate(o, g_ref[rows, :]).astype(y_ref.dtype)
        return s

    so_ref[0, 0] = lax.fori_loop(0, q_ref.shape[0] // c, chunk, s0_ref[0, 0])


def _ret_short_kernel(q_ref, k_ref, v_ref, g_ref, cc_ref, ss_ref, dec_ref, qd_ref, kd_ref, cd_ref, s0_ref,
                      y_ref, so_ref, qd_s, oi_s, *, l):
    rows, dk = q_ref.shape
    scale = dk ** -0.5
    cc, ss = cc_ref[...], ss_ref[...]
    q = _rotate(q_ref[...], cc, ss)
    k = _rotate(k_ref[...], cc, ss) * scale
    v = v_ref[...]
    vb = v.astype(BF16)
    o_intra = _dot(_dot_nt(q, k) * dec_ref[0], vb)
    qd_s[...] = q * qd_ref[0]
    kdt = (k * kd_ref[0]).T
    cd = cd_ref[0, 0:1, :]
    win = 16
    per = win // l
    lane = lax.broadcasted_iota(jnp.int32, (dk, rows), 1)
    wrow = lax.broadcasted_iota(jnp.int32, (win, dk), 0)

    def window(w, carry):
        wr = pl.ds(pl.multiple_of(w * win, win), win)
        qw = qd_s[wr, :].astype(BF16)
        oi = jnp.zeros((win, dk), F32)
        for j in range(per):
            b = w * per + j
            s = s0_ref[b, 0]
            oi = jnp.where(wrow // l == j, jnp.dot(qw, s.astype(BF16), preferred_element_type=F32), oi)
            so_ref[b, 0] = cd * s + jnp.dot(jnp.where(lane // l == b, kdt, 0.0).astype(BF16), vb,
                                            preferred_element_type=F32)
        oi_s[wr, :] = oi
        return carry

    lax.fori_loop(0, rows // win, window, 0)
    y_ref[...] = _ret_gate(o_intra + oi_s[...], g_ref[...]).astype(y_ref.dtype)


def _retention(z, col0, cc, ss, s0, n_seq, l):
    n = z.shape[0]
    dk = s0.shape[-1]
    cb0 = col0 // dk
    zspec = lambda rows, field: pl.BlockSpec((rows, dk), lambda b, h: (b, cb0 + field * RET_HEADS + h))
    cspec = lambda shape: pl.BlockSpec((1,) + shape, lambda b, h: (h, 0, 0))
    if l % RET_CHUNK == 0:
        rows, c, n_blk = l, RET_CHUNK, n_seq
        kern = functools.partial(_ret_long_kernel, c=c)
        tspec = _full((l, dk))
        sspec = pl.BlockSpec((1, 1, dk, dk), lambda b, h: (b, h, 0, 0))
        consts = _ret_consts(c, c)
        scratch = []
    else:
        rows, c = LANES, l
        per_blk = rows // l
        n_blk = n_seq // per_blk
        kern = functools.partial(_ret_short_kernel, l=l)
        tspec = pl.BlockSpec((rows, dk), lambda b, h: (b, 0))
        sspec = pl.BlockSpec((per_blk, 1, dk, dk), lambda b, h: (b, h, 0, 0))
        consts = _ret_consts(rows, c)
        scratch = [pltpu.VMEM((rows, dk), F32), pltpu.VMEM((rows, dk), F32)]
    decay, q_dec, k_dec, chunk_dec = consts
    y, s_new = pl.pallas_call(
        kern, grid=(n_blk, RET_HEADS),
        in_specs=[zspec(rows, 0), zspec(rows, 1), zspec(rows, 2), zspec(rows, 3), tspec, tspec,
                  cspec(decay.shape[1:]), cspec(q_dec.shape[1:]), cspec(k_dec.shape[1:]),
                  cspec(chunk_dec.shape[1:]), sspec],
        out_specs=[pl.BlockSpec((rows, dk), lambda b, h: (b, h)), sspec],
        out_shape=[jax.ShapeDtypeStruct((n, RET_HEADS * dk), BF16), jax.ShapeDtypeStruct(s0.shape, F32)],
        scratch_shapes=scratch,
        compiler_params=_cparams(2), name="retention")(z, z, z, z, cc, ss, decay, q_dec, k_dec, chunk_dec, s0)
    return y, s_new


def _cumsum_rows(x, period):
    row = lax.broadcasted_iota(jnp.int32, x.shape, 0)
    s = 1
    while s < period:
        x = x + jnp.where(row % period >= s, pltpu.roll(x, s, 0), 0.0)
        s *= 2
    return x


def _row_of_block(x, period, offset):
    n, w = x.shape
    x3 = x.reshape(n // period, period, w)
    return jnp.broadcast_to(x3[:, offset:offset + 1, :], x3.shape).reshape(n, w)


def _hgrn_gates(fl, lbl_ref, layer):
    lg = lbl_ref[...]
    e = jnp.exp(lg - jnp.max(lg, axis=0, keepdims=True))
    soft = e / jnp.sum(e, axis=0, keepdims=True)
    cum = soft[0:1]
    for i in range(1, layer + 1):
        cum = cum + soft[i:i + 1]
    lb = cum - soft[0:1]
    forget = lb + (1.0 - lb) * jax.nn.sigmoid(fl)
    return 1.0 - forget, jnp.log(forget)


def _hgrn_out(o, g, nw):
    o = o * lax.rsqrt(jnp.mean(o * o, axis=-1, keepdims=True) + NORM_EPS) * nw
    return o * jax.nn.silu(g)


def _hgrn_long_kernel(q_ref, f_ref, v_ref, g_ref, lbl_ref, nw_ref, s0_ref, y_ref, so_ref, *, c, layer):
    dk = q_ref.shape[1]
    nw = nw_ref[...]
    row = lax.broadcasted_iota(jnp.int32, (c, c), 0)
    col = lax.broadcasted_iota(jnp.int32, (c, c), 1)
    rw = lax.broadcasted_iota(jnp.int32, (c, dk), 0)

    def chunk(i, s):
        rows = pl.ds(pl.multiple_of(i * c, c), c)
        q, v = q_ref[rows, :], v_ref[rows, :]
        k, lf = _hgrn_gates(f_ref[rows, :], lbl_ref, layer)
        b = _cumsum_rows(lf, c)
        b_prev = jnp.where(rw == 0, 0.0, pltpu.roll(b, 1, 0))
        d0 = b - _row_of_block(b_prev, HG_BLOCK, 0)
        scores = jnp.where((row // HG_BLOCK == col // HG_BLOCK) & (col <= row),
                           _dot_nt(q * jnp.exp(d0), k * jnp.exp(-d0)), 0.0)
        half = HG_BLOCK
        while half < c:
            dd = b - _row_of_block(b, 2 * half, half - 1)
            is_q = rw % (2 * half) >= half
            e = jnp.exp(jnp.where(is_q, dd, -dd))
            p = _dot_nt(jnp.where(is_q, q * e, 0.0), jnp.where(is_q, 0.0, k * e))
            if 2 * half < c:
                p = jnp.where(row // (2 * half) == col // (2 * half), p, 0.0)
            scores = scores + p
            half *= 2
        o = _dot(scores, v) + _dot(q * jnp.exp(b), s)
        k_hat = k * jnp.exp(b[c - 1:c, :] - b)
        d_col = jnp.exp(b.T[:, c - 1:c])
        s = d_col * s + _dot(k_hat.T, v)
        y_ref[rows, :] = _hgrn_out(o, g_ref[rows, :], nw).astype(y_ref.dtype)
        return s

    so_ref[0, 0] = lax.fori_loop(0, q_ref.shape[0] // c, chunk, s0_ref[0, 0])


def _hgrn_short_kernel(q_ref, f_ref, v_ref, g_ref, lbl_ref, nw_ref, s0_ref, y_ref, so_ref, qd_s, oi_s,
                       *, l, layer):
    rows, dk = q_ref.shape
    q, v = q_ref[...], v_ref[...]
    vb = v.astype(BF16)
    k, lf = _hgrn_gates(f_ref[...], lbl_ref, layer)
    b = _cumsum_rows(lf, l)
    row = lax.broadcasted_iota(jnp.int32, (rows, rows), 0)
    col = lax.broadcasted_iota(jnp.int32, (rows, rows), 1)
    qe = q * jnp.exp(b)
    scores = jnp.where((row // l == col // l) & (col <= row), _dot_nt(qe, k * jnp.exp(-b)), 0.0)
    o_intra = _dot(scores, vb)
    qd_s[...] = qe
    b_last = _row_of_block(b, l, l - 1)
    kht = (k * jnp.exp(b_last - b)).T
    dect = jnp.exp(b_last).T
    win = 16
    per = win // l
    lane = lax.broadcasted_iota(jnp.int32, (dk, rows), 1)
    wrow = lax.broadcasted_iota(jnp.int32, (win, dk), 0)

    def window(w, carry):
        wr = pl.ds(pl.multiple_of(w * win, win), win)
        qw = qd_s[wr, :].astype(BF16)
        oi = jnp.zeros((win, dk), F32)
        for j in range(per):
            bi = w * per + j
            s = s0_ref[bi, 0]
            oi = jnp.where(wrow // l == j, jnp.dot(qw, s.astype(BF16), preferred_element_type=F32), oi)
            d_col = jnp.sum(jnp.where(lane == bi * l, dect, 0.0), axis=1, keepdims=True)
            so_ref[bi, 0] = d_col * s + jnp.dot(jnp.where(lane // l == bi, kht, 0.0).astype(BF16), vb,
                                                preferred_element_type=F32)
        oi_s[wr, :] = oi
        return carry

    lax.fori_loop(0, rows // win, window, 0)
    y_ref[...] = _hgrn_out(o_intra + oi_s[...], g_ref[...], nw_ref[...]).astype(y_ref.dtype)


def _hgrn(z, lb_logits, norm_w, s0, n_seq, l, layer):
    n = z.shape[0]
    dk = s0.shape[-1]
    depth = lb_logits.shape[0]
    zspec = lambda rows, field: pl.BlockSpec((rows, dk), lambda b, h: (b, field * HG_HEADS + h))
    if l % HG_CHUNK == 0:
        rows, n_blk = l, n_seq
        kern = functools.partial(_hgrn_long_kernel, c=HG_CHUNK, layer=layer)
        sspec = pl.BlockSpec((1, 1, dk, dk), lambda b, h: (b, h, 0, 0))
        scratch = []
    else:
        assert HG_BLOCK % l == 0
        rows = LANES
        per_blk = rows // l
        n_blk = n_seq // per_blk
        kern = functools.partial(_hgrn_short_kernel, l=l, layer=layer)
        sspec = pl.BlockSpec((per_blk, 1, dk, dk), lambda b, h: (b, h, 0, 0))
        scratch = [pltpu.VMEM((rows, dk), F32), pltpu.VMEM((rows, dk), F32)]
    y, s_new = pl.pallas_call(
        kern, grid=(n_blk, HG_HEADS),
        in_specs=[zspec(rows, 0), zspec(rows, 1), zspec(rows, 2), zspec(rows, 3),
                  pl.BlockSpec((depth, dk), lambda b, h: (0, h)), _full((1, dk)), sspec],
        out_specs=[pl.BlockSpec((rows, dk), lambda b, h: (b, h)), sspec],
        out_shape=[jax.ShapeDtypeStruct((n, HG_HEADS * dk), BF16), jax.ShapeDtypeStruct(s0.shape, F32)],
        scratch_shapes=scratch,
        compiler_params=_cparams(2), name="hgrn")(z, z, z, z, lb_logits, norm_w, s0)
    return y, s_new


def _ffn_kernel(*refs, long_mode, final_norm, l):
    if long_mode:
        (x_ref, nw_ref, wg_ref, wu_ref, cw_ref, cb_ref, wd_ref, fnw_ref, buf_ref,
         o_ref, bufo_ref, acc_s, tail_s) = refs
    else:
        (x_ref, nw_ref, wg_ref, wu_ref, cw_ref, cb_ref, wd_ref, fnw_ref, e1_ref, e2_ref,
         o_ref, gpre_ref, acc_s) = refs
    x = x_ref[0] if long_mode else x_ref[...]
    tm = x.shape[0]
    d_ff = wg_ref.shape[1]
    h = _rmsnorm(x, nw_ref[...]).astype(BF16)
    if long_mode:
        @pl.when(pl.program_id(1) == 0)
        def _():
            tail_s[...] = buf_ref[0]

    acc_s[...] = jnp.zeros_like(acc_s)
    for c0 in range(0, d_ff, FF_CHUNK):
        cs = slice(c0, c0 + FF_CHUNK)
        g = jnp.dot(h, wg_ref[:, cs], preferred_element_type=F32)
        up = jnp.dot(h, wu_ref[:, cs], preferred_element_type=F32)
        row = lax.broadcasted_iota(jnp.int32, g.shape, 0)
        r1, r2 = pltpu.roll(g, 1, 0), pltpu.roll(g, 2, 0)
        if long_mode:
            t = row
            e1 = jnp.broadcast_to(tail_s[1:2, cs], g.shape)
            e2 = jnp.where(row == 0, tail_s[0:1, cs], tail_s[1:2, cs])
            tail_s[:, cs] = g[tm - (CONV_W - 1):, :]
        else:
            t = row % l
            e1, e2 = e1_ref[:, cs], e2_ref[:, cs]
            gpre_ref[:, cs] = g
        p1 = jnp.where(t < 1, e1, r1)
        p2 = jnp.where(t < 2, e2, r2)
        conv = cb_ref[:, cs] + cw_ref[0:1, cs] * p2
        conv = conv + cw_ref[1:2, cs] * p1
        conv = conv + cw_ref[2:3, cs] * g
        act = (jax.nn.silu(conv) * up).astype(BF16)
        acc_s[...] += jnp.dot(act, wd_ref[cs, :], preferred_element_type=F32)
    y = x + acc_s[...]
    if final_norm:
        y = _rmsnorm(y, fnw_ref[...])
    if long_mode:
        o_ref[0] = y
        bufo_ref[0] = tail_s[...]
    else:
        o_ref[...] = y


def _ffn(x, nw, wg, wu, conv_w, conv_b, wd, fnw, buf, n_seq, l, final_norm):
    n, d = x.shape
    d_ff = wg.shape[1]
    weights = (nw, wg, wu, conv_w, conv_b, wd, fnw)
    wspecs = [_full(w.shape) for w in weights]
    if l % ROW_TILE == 0:
        tm = ROW_TILE
        y, buf_new = pl.pallas_call(
            functools.partial(_ffn_kernel, long_mode=True, final_norm=final_norm, l=l),
            grid=(n_seq, l // tm),
            in_specs=[pl.BlockSpec((1, tm, d), lambda b, i: (b, i, 0))] + wspecs
                     + [pl.BlockSpec((1, CONV_W - 1, d_ff), lambda b, i: (b, 0, 0))],
            out_specs=[pl.BlockSpec((1, tm, d), lambda b, i: (b, i, 0)),
                       pl.BlockSpec((1, CONV_W - 1, d_ff), lambda b, i: (b, 0, 0))],
            out_shape=[jax.ShapeDtypeStruct((n_seq, l, d), F32),
                       jax.ShapeDtypeStruct((n_seq, CONV_W - 1, d_ff), F32)],
            scratch_shapes=[pltpu.VMEM((tm, d), F32), pltpu.VMEM((CONV_W - 1, d_ff), F32)],
            compiler_params=_cparams(2), name="ffn")(x.reshape(n_seq, l, d), *weights, buf)
        return y.reshape(n, d), buf_new
    assert l >= CONV_W - 1
    pad = jnp.zeros((n_seq, l - 1, d_ff), F32)
    e1 = jnp.concatenate([buf[:, 1:2], pad], axis=1).reshape(n, d_ff)
    e2 = jnp.concatenate([buf, pad[:, 1:]], axis=1).reshape(n, d_ff)
    y, gpre = pl.pallas_call(
        functools.partial(_ffn_kernel, long_mode=False, final_norm=final_norm, l=l),
        in_specs=[_full((n, d))] + wspecs + [_full((n, d_ff)), _full((n, d_ff))],
        out_specs=[_full((n, d)), _full((n, d_ff))],
        out_shape=[jax.ShapeDtypeStruct((n, d), F32), jax.ShapeDtypeStruct((n, d_ff), F32)],
        scratch_shapes=[pltpu.VMEM((n, d), F32)],
        compiler_params=pltpu.CompilerParams(vmem_limit_bytes=VMEM_LIMIT_BYTES), name="ffn")(
            x, *weights, e1, e2)
    return y, gpre.reshape(n_seq, l, d_ff)[:, l - (CONV_W - 1):]


def _trunk(x3, pos, s5_re, s5_im, ret, hg, conv, p, s5_tl):
    b, l, d = x3.shape
    n = b * l
    x = x3.reshape(n, d)
    depth = p['norm_mix'].shape[0]
    new_re, new_im, new_ret, new_hg, new_conv = [], [], [], [], []
    for layer in range(depth):
        j = layer // 2
        nw = p['norm_mix'][layer].reshape(1, d)
        if layer % 2 == 0:
            s5w = p['s5'][j]
            width = s5w['wu'].shape[1]
            y_s5, r, im = _s5_mixer(x.reshape(b, l, d), nw, s5w['wu'], s5w['bmat_re'], s5w['bmat_im'],
                                    s5w['cmat_re'], s5w['cmat_im'], s5w['d'], s5w['glu_w'], s5w['glu_b'],
                                    s5w['a_re'], s5w['a_im'], s5_re[j].reshape(b, -1), s5_im[j].reshape(b, -1),
                                    s5_tl)
            z = _proj(x, nw, p['w_in_ab_ret'][j])
            dk = ret.shape[-1]
            cc, ss = _rope_tables(pos, dk)
            y_ret, st = _retention(z, 0, cc, ss, ret[j], b, l)
            x = _outproj(x, [y_s5, y_ret], p['w_out_ab'][j])
            new_re.append(r.reshape(s5_re.shape[1:]))
            new_im.append(im.reshape(s5_im.shape[1:]))
            new_ret.append(st)
            del width
        else:
            z = _proj(x, nw, p['w_in_c'][j])
            y_hg, st = _hgrn(z, p['hg_lb_logits'], p['hg_norm_w'][j].reshape(1, -1), hg[j], b, l, layer)
            x = _outproj(x, [y_hg], p['w_out_c'][j])
            new_hg.append(st)
        x, buf = _ffn(x, p['norm_ffn'][layer].reshape(1, d), p['ffn_w_gate'][layer], p['ffn_w_up'][layer],
                      p['ffn_conv_w'][layer], p['ffn_conv_b'][layer].reshape(1, -1), p['ffn_w_down'][layer],
                      p['norm_final'].reshape(1, d), conv[layer], b, l, final_norm=(layer == depth - 1))
        new_conv.append(buf)
    return (x.reshape(b, l, d), jnp.stack(new_re), jnp.stack(new_im), jnp.stack(new_ret),
            jnp.stack(new_hg), jnp.stack(new_conv))


def kernel(x_prompt, x_sample, state_s5_re, state_s5_im, state_ret, state_hgrn, state_ffn_conv, pos_sample, norm_mix, norm_ffn, norm_final, w_in_ab, s5_lam_re, s5_lam_im, s5_log_dt, s5_b_re, s5_b_im, s5_c_re, s5_c_im, s5_d, s5_glu_w, s5_glu_b, w_out_ab, w_in_c, hg_lb_logits, hg_norm_w, w_out_c, ffn_w_gate, ffn_w_up, ffn_conv_w, ffn_conv_b, ffn_w_down):
    n_ab, n_grp, n_st = s5_lam_re.shape
    ch = s5_b_re.shape[-1]
    width = n_grp * ch
    grp_per_blk = MXU_DIM // ch
    n_blk = n_grp // grp_per_blk
    s5 = []
    for j in range(n_ab):
        a_re, a_im, bb_re, bb_im = _s5_prep(s5_lam_re[j], s5_lam_im[j], s5_log_dt[j], s5_b_re[j], s5_b_im[j])
        bmat = lambda m: jnp.stack([_block_diag(blk, grp_per_blk)
                                    for blk in m.reshape(n_blk, grp_per_blk * ch, n_st)]).astype(BF16)
        cmat = lambda c: jnp.stack([_block_diag(blk, grp_per_blk).T for blk in
                                    c.reshape(n_blk, grp_per_blk * ch, n_st)]).astype(BF16)
        s5.append(dict(wu=w_in_ab[j][:, :width].astype(BF16),
                       bmat_re=bmat(bb_re), bmat_im=bmat(bb_im),
                       cmat_re=cmat(s5_c_re[j]), cmat_im=cmat(s5_c_im[j]),
                       d=s5_d[j].reshape(1, width), glu_w=s5_glu_w[j].astype(BF16),
                       glu_b=s5_glu_b[j].reshape(1, width),
                       a_re=a_re.reshape(1, n_grp * n_st), a_im=a_im.reshape(1, n_grp * n_st)))
    p = dict(norm_mix=norm_mix, norm_ffn=norm_ffn, norm_final=norm_final, s5=s5,
             w_in_ab_ret=w_in_ab[:, :, width:].astype(BF16), w_out_ab=w_out_ab.astype(BF16),
             w_in_c=w_in_c.astype(BF16), hg_lb_logits=hg_lb_logits, hg_norm_w=hg_norm_w,
             w_out_c=w_out_c.astype(BF16), ffn_w_gate=ffn_w_gate.astype(BF16),
             ffn_w_up=ffn_w_up.astype(BF16), ffn_conv_w=ffn_conv_w, ffn_conv_b=ffn_conv_b,
             ffn_w_down=ffn_w_down.astype(BF16))

    bp, lp, _ = x_prompt.shape
    z_s5 = jnp.zeros((n_ab, bp) + state_s5_re.shape[2:], F32)
    z_ret = jnp.zeros((n_ab, bp) + state_ret.shape[2:], F32)
    z_hg = jnp.zeros((state_hgrn.shape[0], bp) + state_hgrn.shape[2:], F32)
    z_conv = jnp.zeros((norm_mix.shape[0], bp) + state_ffn_conv.shape[2:], F32)
    outs_p = _trunk(x_prompt, jnp.arange(lp, dtype=jnp.int32), z_s5, z_s5, z_ret, z_hg, z_conv, p,
                    s5_tl=ROW_TILE // bp)
    bs, ls, _ = x_sample.shape
    pos_s = (pos_sample[:, None] + jnp.arange(ls, dtype=jnp.int32)[None, :]).reshape(-1)
    outs_s = _trunk(x_sample, pos_s, state_s5_re, state_s5_im, state_ret, state_hgrn, state_ffn_conv, p,
                    s5_tl=ls)
    return (outs_p[0], outs_s[0]) + outs_p[1:] + outs_s[1:]
```

```python
import functools

import jax
import jax.numpy as jnp
from jax import lax
from jax.experimental import pallas as pl
from jax.experimental.pallas import tpu as pltpu

F32 = jnp.float32
BF16 = jnp.bfloat16

NORM_EPS = 1e-6
ROPE_BASE = 10000.0
S5_GROUP_CH = 16
RET_HEADS = 4
RET_CHUNK = 128
HG_HEADS = 8
HG_BLOCK = 16
HG_CHUNK = 128
RET_HEADS_PER_STEP = 4
HG_HEADS_PER_STEP = 4
CONV_W = 3

LANES = 128
MXU_DIM = 256
VMEM_LIMIT_BYTES = 56 * 1024 * 1024
ROW_TILE = 512
FF_CHUNK = 256
S5_SCAN_COLS = 512


def _cparams(n_grid_dims):
    return pltpu.CompilerParams(dimension_semantics=("arbitrary",) * n_grid_dims,
                                vmem_limit_bytes=VMEM_LIMIT_BYTES)


def _dot(a, b):
    return jnp.dot(a.astype(BF16), b.astype(BF16), preferred_element_type=F32)


def _dot_nt(a, b):
    return lax.dot_general(a.astype(BF16), b.astype(BF16), (((1,), (1,)), ((), ())),
                           preferred_element_type=F32)


def _rmsnorm(x, w):
    return x * lax.rsqrt(jnp.mean(x * x, axis=-1, keepdims=True) + NORM_EPS) * w


def _full(shape):
    nd = len(shape)
    return pl.BlockSpec(shape, lambda *_: (0,) * nd)


def _proj_kernel(x_ref, nw_ref, w_ref, o_ref, *, col_chunk):
    h = _rmsnorm(x_ref[...], nw_ref[...]).astype(BF16)
    for c in range(0, w_ref.shape[1], col_chunk):
        o_ref[:, c:c + col_chunk] = jnp.dot(h, w_ref[:, c:c + col_chunk], preferred_element_type=F32)


def _proj(x, nw, w_bf):
    n, d = x.shape
    n_out = w_bf.shape[1]
    tm = min(ROW_TILE, n)
    return pl.pallas_call(
        functools.partial(_proj_kernel, col_chunk=2 * MXU_DIM),
        grid=(n // tm,),
        in_specs=[pl.BlockSpec((tm, d), lambda i: (i, 0)), _full((1, d)), _full((d, n_out))],
        out_specs=pl.BlockSpec((tm, n_out), lambda i: (i, 0)),
        out_shape=jax.ShapeDtypeStruct((n, n_out), F32),
        compiler_params=_cparams(1), name="proj")(x, nw, w_bf)


def _outproj_kernel(*refs, n_parts):
    x_ref, parts, w_ref, o_ref = refs[0], refs[1:1 + n_parts], refs[1 + n_parts], refs[2 + n_parts]
    y = jnp.concatenate([p[...].astype(BF16) for p in parts], axis=1)
    o_ref[...] = x_ref[...] + jnp.dot(y, w_ref[...], preferred_element_type=F32)


def _outproj(x, parts, w_bf):
    n, d = x.shape
    tm = min(ROW_TILE, n)
    in_specs = [pl.BlockSpec((tm, d), lambda i: (i, 0))]
    in_specs += [pl.BlockSpec((tm, p.shape[1]), lambda i: (i, 0)) for p in parts]
    in_specs += [_full(w_bf.shape)]
    return pl.pallas_call(
        functools.partial(_outproj_kernel, n_parts=len(parts)),
        grid=(n // tm,), in_specs=in_specs,
        out_specs=pl.BlockSpec((tm, d), lambda i: (i, 0)),
        out_shape=jax.ShapeDtypeStruct((n, d), F32),
        compiler_params=_cparams(1), name="outproj")(x, *parts, w_bf)


def _s5_prep_kernel(lre_ref, lim_ref, ldt_ref, lre16_ref, lim16_ref, ldt16_ref, bre_ref, bim_ref,
                    are_ref, aim_ref, bbre_ref, bbim_ref):
    def disc(lre, lim, ldt):
        dt = jnp.exp(ldt)
        mag = jnp.exp(lre * dt)
        ang = lim * dt
        return mag * jnp.cos(ang), mag * jnp.sin(ang)

    ab_re, ab_im = disc(lre_ref[...], lim_ref[...], ldt_ref[...])
    are_ref[...] = ab_re
    aim_ref[...] = ab_im
    lre, lim = lre16_ref[...], lim16_ref[...]
    ab_re, ab_im = disc(lre, lim, ldt16_ref[...])
    nr, ni = ab_re - 1.0, ab_im
    den = lre * lre + lim * lim
    f_re = (nr * lre + ni * lim) / den
    f_im = (ni * lre - nr * lim) / den
    b_re, b_im = bre_ref[...], bim_ref[...]
    bbre_ref[...] = f_re * b_re - f_im * b_im
    bbim_ref[...] = f_re * b_im + f_im * b_re


def _s5_prep(lam_re, lam_im, log_dt, b_re, b_im):
    g, p = lam_re.shape
    ch = b_re.shape[-1]
    rep = lambda a: jnp.repeat(a, ch, axis=0)
    ldt = log_dt.reshape(g, 1)
    bt = lambda b: jnp.swapaxes(b, 1, 2).reshape(g * ch, p)
    args = (lam_re, lam_im, ldt, rep(lam_re), rep(lam_im), rep(ldt), bt(b_re), bt(b_im))
    return pl.pallas_call(
        _s5_prep_kernel,
        in_specs=[_full(a.shape) for a in args],
        out_specs=[_full((g, p)), _full((g, p)), _full((g * ch, p)), _full((g * ch, p))],
        out_shape=[jax.ShapeDtypeStruct((g, p), F32)] * 2 + [jax.ShapeDtypeStruct((g * ch, p), F32)] * 2,
        name="s5_prep")(*args)


def _block_diag(m, n_blk):
    r, c = m.shape[0] // n_blk, m.shape[1]
    eye = jnp.eye(n_blk, dtype=m.dtype)
    return (m.reshape(n_blk, r, 1, c) * eye[:, None, :, None]).reshape(n_blk * r, n_blk * c)


def _s5_kernel(x_ref, nw_ref, wu_ref, bre_ref, bim_ref, cre_ref, cim_ref, d_ref, gw_ref, gb_ref,
               are_ref, aim_ref, s0re_ref, s0im_ref,
               y_ref, sre_ref, sim_ref, bure_s, buim_s, stre_s, stim_s, *, tb, tl):
    @pl.when(pl.program_id(0) == 0)
    def _():
        stre_s[...] = s0re_ref[...]
        stim_s[...] = s0im_ref[...]

    d = nw_ref.shape[-1]
    width = wu_ref.shape[1]
    x = jnp.concatenate([x_ref[:, t * d:(t + 1) * d] for t in range(tl)], axis=0)
    u = jnp.dot(_rmsnorm(x, nw_ref[...]).astype(BF16), wu_ref[...], preferred_element_type=F32)
    ub = u.astype(BF16)

    n_kb, kw, cw = bre_ref.shape
    for kb in range(n_kb):
        uk = ub[:, kb * kw:(kb + 1) * kw]
        bure_s[:, kb * cw:(kb + 1) * cw] = jnp.dot(uk, bre_ref[kb], preferred_element_type=F32)
        buim_s[:, kb * cw:(kb + 1) * cw] = jnp.dot(uk, bim_ref[kb], preferred_element_type=F32)

    n_state = are_ref.shape[1]
    sc = min(S5_SCAN_COLS, n_state)
    for c0 in range(0, n_state, sc):
        cs = slice(c0, c0 + sc)
        a_re = jnp.broadcast_to(are_ref[:, cs], (tb, sc))
        a_im = jnp.broadcast_to(aim_ref[:, cs], (tb, sc))

        def step(t, carry, cs=cs, a_re=a_re, a_im=a_im):
            s_re, s_im = carry
            rows = pl.ds(pl.multiple_of(t * tb, tb), tb)
            n_re = a_re * s_re - a_im * s_im + bure_s[rows, cs]
            n_im = a_re * s_im + a_im * s_re + buim_s[rows, cs]
            bure_s[rows, cs] = n_re
            buim_s[rows, cs] = n_im
            return n_re, n_im

        s_re, s_im = lax.fori_loop(0, tl, step, (stre_s[:, cs], stim_s[:, cs]), unroll=min(tl, 8))
        stre_s[:, cs] = s_re
        stim_s[:, cs] = s_im

    n_ob, ckw, ocw = cre_ref.shape
    ys = []
    for ob in range(n_ob):
        ks = slice(ob * ckw, (ob + 1) * ckw)
        ys.append(_dot(bure_s[:, ks], cre_ref[ob]) - _dot(buim_s[:, ks], cim_ref[ob]))
    y = jnp.concatenate(ys, axis=1) + d_ref[...] * u
    y = jax.nn.gelu(y)
    out = y * jax.nn.sigmoid(_dot(y, gw_ref[...]) + gb_ref[...])
    for t in range(tl):
        y_ref[:, t * width:(t + 1) * width] = out[t * tb:(t + 1) * tb, :]
    sre_ref[...] = stre_s[...]
    sim_ref[...] = stim_s[...]


def _s5_mixer(x3, nw, wu_bf, bmat_re, bmat_im, cmat_re, cmat_im, d_row, glu_w_bf, glu_b, a_re, a_im,
              s0_re, s0_im, tl):
    b, l, d = x3.shape
    width = wu_bf.shape[1]
    n_state = a_re.shape[1]
    rows = b * tl
    args = (x3.reshape(b, l * d), nw, wu_bf, bmat_re, bmat_im, cmat_re, cmat_im, d_row, glu_w_bf, glu_b,
            a_re, a_im, s0_re, s0_im)
    in_specs = [pl.BlockSpec((b, tl * d), lambda i: (0, i))] + [_full(a.shape) for a in args[1:]]
    y, s_re, s_im = pl.pallas_call(
        functools.partial(_s5_kernel, tb=b, tl=tl),
        grid=(l // tl,), in_specs=in_specs,
        out_specs=[pl.BlockSpec((b, tl * width), lambda i: (0, i)), _full((b, n_state)), _full((b, n_state))],
        out_shape=[jax.ShapeDtypeStruct((b, l * width), F32),
                   jax.ShapeDtypeStruct((b, n_state), F32), jax.ShapeDtypeStruct((b, n_state), F32)],
        scratch_shapes=[pltpu.VMEM((rows, n_state), F32), pltpu.VMEM((rows, n_state), F32),
                        pltpu.VMEM((b, n_state), F32), pltpu.VMEM((b, n_state), F32)],
        compiler_params=_cparams(1), name="s5_mixer")(*args)
    return y.reshape(b * l, width), s_re, s_im


def _rope_kernel(pos_ref, inv_ref, cc_ref, ss_ref):
    ang = pos_ref[...] * inv_ref[...]
    lane = lax.broadcasted_iota(jnp.int32, ang.shape, 1)
    cc_ref[...] = jnp.cos(ang)
    ss_ref[...] = jnp.where(lane < ang.shape[1] // 2, -jnp.sin(ang), jnp.sin(ang))


def _rope_tables(pos, dk):
    n = pos.shape[0]
    inv = 1.0 / (ROPE_BASE ** jnp.linspace(0.0, 1.0, dk // 2, dtype=F32))
    inv = jnp.concatenate([inv, inv]).reshape(1, dk)
    return pl.pallas_call(
        _rope_kernel,
        in_specs=[_full((n, 1)), _full((1, dk))],
        out_specs=[_full((n, dk)), _full((n, dk))],
        out_shape=[jax.ShapeDtypeStruct((n, dk), F32)] * 2,
        name="rope_tables")(pos.astype(F32).reshape(n, 1), inv)


def _rotate(t, cc, ss):
    return t * cc + pltpu.roll(t, t.shape[1] // 2, 1) * ss


def _ret_consts(rows, c):
    lg = jnp.log(1.0 - 2.0 ** (-5.0 - jnp.arange(RET_HEADS, dtype=F32)))
    r = jnp.arange(rows)
    idx, blk = r % c, r // c
    diff = idx[:, None] - idx[None, :]
    same = blk[:, None] == blk[None, :]
    decay = jnp.where((same & (diff >= 0))[None],
                      jnp.exp(jnp.maximum(diff, 0)[None].astype(F32) * lg[:, None, None]), 0.0)
    wide = lambda v: jnp.broadcast_to(v[:, :, None], (RET_HEADS, rows, LANES))
    q_dec = wide(jnp.exp((idx + 1).astype(F32)[None, :] * lg[:, None]))
    k_dec = wide(jnp.exp((c - 1 - idx).astype(F32)[None, :] * lg[:, None]))
    chunk_dec = jnp.broadcast_to(jnp.exp(c * lg)[:, None, None], (RET_HEADS, 8, LANES))
    return decay, q_dec, k_dec, chunk_dec


def _ret_gate(o, g):
    o = o * lax.rsqrt(jnp.mean(o * o, axis=-1, keepdims=True) + NORM_EPS)
    return jax.nn.silu(g) * o


def _ret_long_kernel(q_ref, k_ref, v_ref, g_ref, cc_ref, ss_ref, dec_ref, qd_ref, kd_ref, cd_ref, s0_ref,
                     y_ref, so_ref, *, c, hp):
    dk = q_ref.shape[1] // hp
    scale = dk ** -0.5

    def chunk(i, states):
        rows = pl.ds(pl.multiple_of(i * c, c), c)
        cc, ss = cc_ref[rows, :], ss_ref[rows, :]
        new_states = []
        for j in range(hp):
            cols = slice(j * dk, (j + 1) * dk)
            q = _rotate(q_ref[rows, cols], cc, ss)
            k = _rotate(k_ref[rows, cols], cc, ss) * scale
            v = v_ref[rows, cols]
            scores = _dot_nt(q, k) * dec_ref[j]
            o = _dot(scores, v) + _dot(q * qd_ref[j], states[j])
            new_states.append(cd_ref[j, 0:1, :] * states[j] + _dot((k * kd_ref[j]).T, v))
            y_ref[rows, cols] = _ret_gate(o, g_ref[rows, cols]).astype(y_ref.dtype)
        return tuple(new_states)

    states = lax.fori_loop(0, q_ref.shape[0] // c, chunk, tuple(s0_ref[0, j] for j in range(hp)))
    for j in range(hp):
        so_ref[0, j] = states[j]


def _ret_short_kernel(q_ref, k_ref, v_ref, g_ref, cc_ref, ss_ref, dec_ref, qd_ref, kd_ref, cd_ref, s0_ref,
                      y_ref, so_ref, qd_s, oi_s, *, l):
    rows, dk = q_ref.shape
    scale = dk ** -0.5
    cc, ss = cc_ref[...], ss_ref[...]
    q = _rotate(q_ref[...], cc, ss)
    k = _rotate(k_ref[...], cc, ss) * scale
    v = v_ref[...]
    vb = v.astype(BF16)
    o_intra = _dot(_dot_nt(q, k) * dec_ref[0], vb)
    qd_s[...] = q * qd_ref[0]
    kdt = (k * kd_ref[0]).T
    cd = cd_ref[0, 0:1, :]
    win = 16
    per = win // l
    lane = lax.broadcasted_iota(jnp.int32, (dk, rows), 1)
    wrow = lax.broadcasted_iota(jnp.int32, (win, dk), 0)

    def window(w, carry):
        wr = pl.ds(pl.multiple_of(w * win, win), win)
        qw = qd_s[wr, :].astype(BF16)
        oi = jnp.zeros((win, dk), F32)
        for j in range(per):
            b = w * per + j
            s = s0_ref[b, 0]
            oi = jnp.where(wrow // l == j, jnp.dot(qw, s.astype(BF16), preferred_element_type=F32), oi)
            so_ref[b, 0] = cd * s + jnp.dot(jnp.where(lane // l == b, kdt, 0.0).astype(BF16), vb,
                                            preferred_element_type=F32)
        oi_s[wr, :] = oi
        return carry

    lax.fori_loop(0, rows // win, window, 0)
    y_ref[...] = _ret_gate(o_intra + oi_s[...], g_ref[...]).astype(y_ref.dtype)


def _retention(z, col0, cc, ss, s0, n_seq, l):
    n = z.shape[0]
    dk = s0.shape[-1]
    if l % RET_CHUNK == 0:
        rows, c, n_blk, hp = l, RET_CHUNK, n_seq, RET_HEADS_PER_STEP
        kern = functools.partial(_ret_long_kernel, c=c, hp=hp)
        tspec = _full((l, dk))
        sspec = pl.BlockSpec((1, hp, dk, dk), lambda b, h: (b, h, 0, 0))
        consts = _ret_consts(c, c)
        scratch = []
    else:
        rows, c, hp = LANES, l, 1
        per_blk = rows // l
        n_blk = n_seq // per_blk
        kern = functools.partial(_ret_short_kernel, l=l)
        tspec = pl.BlockSpec((rows, dk), lambda b, h: (b, 0))
        sspec = pl.BlockSpec((per_blk, 1, dk, dk), lambda b, h: (b, h, 0, 0))
        consts = _ret_consts(rows, c)
        scratch = [pltpu.VMEM((rows, dk), F32), pltpu.VMEM((rows, dk), F32)]
    decay, q_dec, k_dec, chunk_dec = consts
    n_hb = RET_HEADS // hp
    cb0 = col0 // (hp * dk)
    zspec = lambda rows, field: pl.BlockSpec((rows, hp * dk), lambda b, h: (b, cb0 + field * n_hb + h))
    cspec = lambda shape: pl.BlockSpec((hp,) + shape, lambda b, h: (h, 0, 0))
    y, s_new = pl.pallas_call(
        kern, grid=(n_blk, n_hb),
        in_specs=[zspec(rows, 0), zspec(rows, 1), zspec(rows, 2), zspec(rows, 3), tspec, tspec,
                  cspec(decay.shape[1:]), cspec(q_dec.shape[1:]), cspec(k_dec.shape[1:]),
                  cspec(chunk_dec.shape[1:]), sspec],
        out_specs=[pl.BlockSpec((rows, hp * dk), lambda b, h: (b, h)), sspec],
        out_shape=[jax.ShapeDtypeStruct((n, RET_HEADS * dk), BF16), jax.ShapeDtypeStruct(s0.shape, F32)],
        scratch_shapes=scratch,
        compiler_params=_cparams(2), name="retention")(z, z, z, z, cc, ss, decay, q_dec, k_dec, chunk_dec, s0)
    return y, s_new


def _cumsum_rows(x, period):
    row = lax.broadcasted_iota(jnp.int32, x.shape, 0)
    s = 1
    while s < period:
        x = x + jnp.where(row % period >= s, pltpu.roll(x, s, 0), 0.0)
        s *= 2
    return x


def _row_of_block(x, period, offset):
    n, w = x.shape
    x3 = x.reshape(n // period, period, w)
    return jnp.broadcast_to(x3[:, offset:offset + 1, :], x3.shape).reshape(n, w)


def _hgrn_gates(fl, lg, layer):
    e = jnp.exp(lg - jnp.max(lg, axis=0, keepdims=True))
    soft = e / jnp.sum(e, axis=0, keepdims=True)
    cum = soft[0:1]
    for i in range(1, layer + 1):
        cum = cum + soft[i:i + 1]
    lb = cum - soft[0:1]
    forget = lb + (1.0 - lb) * jax.nn.sigmoid(fl)
    return 1.0 - forget, jnp.log(forget)


def _hgrn_out(o, g, nw):
    o = o * lax.rsqrt(jnp.mean(o * o, axis=-1, keepdims=True) + NORM_EPS) * nw
    return o * jax.nn.silu(g)


def _hgrn_long_kernel(q_ref, f_ref, v_ref, g_ref, lbl_ref, nw_ref, s0_ref, y_ref, so_ref, *, c, layer, hp):
    dk = q_ref.shape[1] // hp
    nw = nw_ref[...]
    row = lax.broadcasted_iota(jnp.int32, (c, c), 0)
    col = lax.broadcasted_iota(jnp.int32, (c, c), 1)
    rw = lax.broadcasted_iota(jnp.int32, (c, dk), 0)

    def one_head(rows, cols, s):
        q, v = q_ref[rows, cols], v_ref[rows, cols]
        k, lf = _hgrn_gates(f_ref[rows, cols], lbl_ref[:, cols], layer)
        b = _cumsum_rows(lf, c)
        b_prev = jnp.where(rw == 0, 0.0, pltpu.roll(b, 1, 0))
        d0 = b - _row_of_block(b_prev, HG_BLOCK, 0)
        scores = jnp.where((row // HG_BLOCK == col // HG_BLOCK) & (col <= row),
                           _dot_nt(q * jnp.exp(d0), k * jnp.exp(-d0)), 0.0)
        half = HG_BLOCK
        while half < c:
            dd = b - _row_of_block(b, 2 * half, half - 1)
            is_q = rw % (2 * half) >= half
            e = jnp.exp(jnp.where(is_q, dd, -dd))
            p = _dot_nt(jnp.where(is_q, q * e, 0.0), jnp.where(is_q, 0.0, k * e))
            if 2 * half < c:
                p = jnp.where(row // (2 * half) == col // (2 * half), p, 0.0)
            scores = scores + p
            half *= 2
        o = _dot(scores, v) + _dot(q * jnp.exp(b), s)
        k_hat = k * jnp.exp(b[c - 1:c, :] - b)
        d_col = jnp.exp(b.T[:, c - 1:c])
        y_ref[rows, cols] = _hgrn_out(o, g_ref[rows, cols], nw).astype(y_ref.dtype)
        return d_col * s + _dot(k_hat.T, v)

    def chunk(i, states):
        rows = pl.ds(pl.multiple_of(i * c, c), c)
        return tuple(one_head(rows, slice(j * dk, (j + 1) * dk), states[j]) for j in range(hp))

    states = lax.fori_loop(0, q_ref.shape[0] // c, chunk, tuple(s0_ref[0, j] for j in range(hp)))
    for j in range(hp):
        so_ref[0, j] = states[j]


def _hgrn_short_kernel(q_ref, f_ref, v_ref, g_ref, lbl_ref, nw_ref, s0_ref, y_ref, so_ref, qd_s, oi_s,
                       *, l, layer):
    rows, dk = q_ref.shape
    q, v = q_ref[...], v_ref[...]
    vb = v.astype(BF16)
    k, lf = _hgrn_gates(f_ref[...], lbl_ref[...], layer)
    b = _cumsum_rows(lf, l)
    row = lax.broadcasted_iota(jnp.int32, (rows, rows), 0)
    col = lax.broadcasted_iota(jnp.int32, (rows, rows), 1)
    qe = q * jnp.exp(b)
    scores = jnp.where((row // l == col // l) & (col <= row), _dot_nt(qe, k * jnp.exp(-b)), 0.0)
    o_intra = _dot(scores, vb)
    qd_s[...] = qe
    b_last = _row_of_block(b, l, l - 1)
    kht = (k * jnp.exp(b_last - b)).T
    dect = jnp.exp(b_last).T
    win = 16
    per = win // l
    lane = lax.broadcasted_iota(jnp.int32, (dk, rows), 1)
    wrow = lax.broadcasted_iota(jnp.int32, (win, dk), 0)

    def window(w, carry):
        wr = pl.ds(pl.multiple_of(w * win, win), win)
        qw = qd_s[wr, :].astype(BF16)
        oi = jnp.zeros((win, dk), F32)
        for j in range(per):
            bi = w * per + j
            s = s0_ref[bi, 0]
            oi = jnp.where(wrow // l == j, jnp.dot(qw, s.astype(BF16), preferred_element_type=F32), oi)
            d_col = jnp.sum(jnp.where(lane == bi * l, dect, 0.0), axis=1, keepdims=True)
            so_ref[bi, 0] = d_col * s + jnp.dot(jnp.where(lane // l == bi, kht, 0.0).astype(BF16), vb,
                                                preferred_element_type=F32)
        oi_s[wr, :] = oi
        return carry

    lax.fori_loop(0, rows // win, window, 0)
    y_ref[...] = _hgrn_out(o_intra + oi_s[...], g_ref[...], nw_ref[...]).astype(y_ref.dtype)


def _hgrn(z, lb_logits, norm_w, s0, n_seq, l, layer):
    n = z.shape[0]
    dk = s0.shape[-1]
    depth = lb_logits.shape[0]
    if l % HG_CHUNK == 0:
        rows, n_blk, hp = l, n_seq, HG_HEADS_PER_STEP
        kern = functools.partial(_hgrn_long_kernel, c=HG_CHUNK, layer=layer, hp=hp)
        sspec = pl.BlockSpec((1, hp, dk, dk), lambda b, h: (b, h, 0, 0))
        scratch = []
    else:
        assert HG_BLOCK % l == 0
        rows, hp = LANES, 1
        per_blk = rows // l
        n_blk = n_seq // per_blk
        kern = functools.partial(_hgrn_short_kernel, l=l, layer=layer)
        sspec = pl.BlockSpec((per_blk, 1, dk, dk), lambda b, h: (b, h, 0, 0))
        scratch = [pltpu.VMEM((rows, dk), F32), pltpu.VMEM((rows, dk), F32)]
    n_hb = HG_HEADS // hp
    zspec = lambda rows, field: pl.BlockSpec((rows, hp * dk), lambda b, h: (b, field * n_hb + h))
    y, s_new = pl.pallas_call(
        kern, grid=(n_blk, n_hb),
        in_specs=[zspec(rows, 0), zspec(rows, 1), zspec(rows, 2), zspec(rows, 3),
                  pl.BlockSpec((depth, hp * dk), lambda b, h: (0, h)), _full((1, dk)), sspec],
        out_specs=[pl.BlockSpec((rows, hp * dk), lambda b, h: (b, h)), sspec],
        out_shape=[jax.ShapeDtypeStruct((n, HG_HEADS * dk), BF16), jax.ShapeDtypeStruct(s0.shape, F32)],
        scratch_shapes=scratch,
        compiler_params=_cparams(2), name="hgrn")(z, z, z, z, lb_logits, norm_w, s0)
    return y, s_new


def _ffn_kernel(*refs, long_mode, final_norm, l):
    if long_mode:
        (x_ref, nw_ref, wg_ref, wu_ref, cw_ref, cb_ref, wd_ref, fnw_ref, buf_ref,
         o_ref, bufo_ref, act_s, tail_s) = refs
    else:
        (x_ref, nw_ref, wg_ref, wu_ref, cw_ref, cb_ref, wd_ref, fnw_ref, e1_ref, e2_ref,
         o_ref, gpre_ref, act_s) = refs
    x = x_ref[0] if long_mode else x_ref[...]
    tm = x.shape[0]
    d_ff = wg_ref.shape[1]
    h = _rmsnorm(x, nw_ref[...]).astype(BF16)
    if long_mode:
        @pl.when(pl.program_id(1) == 0)
        def _():
            tail_s[...] = buf_ref[0]

    for c0 in range(0, d_ff, FF_CHUNK):
        cs = slice(c0, c0 + FF_CHUNK)
        g = jnp.dot(h, wg_ref[:, cs], preferred_element_type=F32)
        up = jnp.dot(h, wu_ref[:, cs], preferred_element_type=F32)
        row = lax.broadcasted_iota(jnp.int32, g.shape, 0)
        r1, r2 = pltpu.roll(g, 1, 0), pltpu.roll(g, 2, 0)
        if long_mode:
            t = row
            e1 = jnp.broadcast_to(tail_s[1:2, cs], g.shape)
            e2 = jnp.where(row == 0, tail_s[0:1, cs], tail_s[1:2, cs])
            tail_s[:, cs] = g[tm - (CONV_W - 1):, :]
        else:
            t = row % l
            e1, e2 = e1_ref[:, cs], e2_ref[:, cs]
            gpre_ref[:, cs] = g
        p1 = jnp.where(t < 1, e1, r1)
        p2 = jnp.where(t < 2, e2, r2)
        conv = cb_ref[:, cs] + cw_ref[0:1, cs] * p2
        conv = conv + cw_ref[1:2, cs] * p1
        conv = conv + cw_ref[2:3, cs] * g
        act_s[:, cs] = (jax.nn.silu(conv) * up).astype(BF16)
    y = x + jnp.dot(act_s[...], wd_ref[...], preferred_element_type=F32)
    if final_norm:
        y = _rmsnorm(y, fnw_ref[...])
    if long_mode:
        o_ref[0] = y
        bufo_ref[0] = tail_s[...]
    else:
        o_ref[...] = y


def _ffn(x, nw, wg, wu, conv_w, conv_b, wd, fnw, buf, n_seq, l, final_norm):
    n, d = x.shape
    d_ff = wg.shape[1]
    weights = (nw, wg, wu, conv_w, conv_b, wd, fnw)
    wspecs = [_full(w.shape) for w in weights]
    if l % ROW_TILE == 0:
        tm = ROW_TILE
        y, buf_new = pl.pallas_call(
            functools.partial(_ffn_kernel, long_mode=True, final_norm=final_norm, l=l),
            grid=(n_seq, l // tm),
            in_specs=[pl.BlockSpec((1, tm, d), lambda b, i: (b, i, 0))] + wspecs
                     + [pl.BlockSpec((1, CONV_W - 1, d_ff), lambda b, i: (b, 0, 0))],
            out_specs=[pl.BlockSpec((1, tm, d), lambda b, i: (b, i, 0)),
                       pl.BlockSpec((1, CONV_W - 1, d_ff), lambda b, i: (b, 0, 0))],
            out_shape=[jax.ShapeDtypeStruct((n_seq, l, d), F32),
                       jax.ShapeDtypeStruct((n_seq, CONV_W - 1, d_ff), F32)],
            scratch_shapes=[pltpu.VMEM((tm, d_ff), BF16), pltpu.VMEM((CONV_W - 1, d_ff), F32)],
            compiler_params=_cparams(2), name="ffn")(x.reshape(n_seq, l, d), *weights, buf)
        return y.reshape(n, d), buf_new
    assert l >= CONV_W - 1
    pad = jnp.zeros((n_seq, l - 1, d_ff), F32)
    e1 = jnp.concatenate([buf[:, 1:2], pad], axis=1).reshape(n, d_ff)
    e2 = jnp.concatenate([buf, pad[:, 1:]], axis=1).reshape(n, d_ff)
    y, gpre = pl.pallas_call(
        functools.partial(_ffn_kernel, long_mode=False, final_norm=final_norm, l=l),
        in_specs=[_full((n, d))] + wspecs + [_full((n, d_ff)), _full((n, d_ff))],
        out_specs=[_full((n, d)), _full((n, d_ff))],
        out_shape=[jax.ShapeDtypeStruct((n, d), F32), jax.ShapeDtypeStruct((n, d_ff), F32)],
        scratch_shapes=[pltpu.VMEM((n, d_ff), BF16)],
        compiler_params=pltpu.CompilerParams(vmem_limit_bytes=VMEM_LIMIT_BYTES), name="ffn")(
            x, *weights, e1, e2)
    return y, gpre.reshape(n_seq, l, d_ff)[:, l - (CONV_W - 1):]


def _trunk(x3, pos, s5_re, s5_im, ret, hg, conv, p, s5_tl):
    b, l, d = x3.shape
    n = b * l
    x = x3.reshape(n, d)
    depth = p['norm_mix'].shape[0]
    new_re, new_im, new_ret, new_hg, new_conv = [], [], [], [], []
    for layer in range(depth):
        j = layer // 2
        nw = p['norm_mix'][layer].reshape(1, d)
        if layer % 2 == 0:
            s5w = p['s5'][j]
            width = s5w['wu'].shape[1]
            y_s5, r, im = _s5_mixer(x.reshape(b, l, d), nw, s5w['wu'], s5w['bmat_re'], s5w['bmat_im'],
                                    s5w['cmat_re'], s5w['cmat_im'], s5w['d'], s5w['glu_w'], s5w['glu_b'],
                                    s5w['a_re'], s5w['a_im'], s5_re[j].reshape(b, -1), s5_im[j].reshape(b, -1),
                                    s5_tl)
            z = _proj(x, nw, p['w_in_ab_ret'][j])
            dk = ret.shape[-1]
            cc, ss = _rope_tables(pos, dk)
            y_ret, st = _retention(z, 0, cc, ss, ret[j], b, l)
            x = _outproj(x, [y_s5, y_ret], p['w_out_ab'][j])
            new_re.append(r.reshape(s5_re.shape[1:]))
            new_im.append(im.reshape(s5_im.shape[1:]))
            new_ret.append(st)
            del width
        else:
            z = _proj(x, nw, p['w_in_c'][j])
            y_hg, st = _hgrn(z, p['hg_lb_logits'], p['hg_norm_w'][j].reshape(1, -1), hg[j], b, l, layer)
            x = _outproj(x, [y_hg], p['w_out_c'][j])
            new_hg.append(st)
        x, buf = _ffn(x, p['norm_ffn'][layer].reshape(1, d), p['ffn_w_gate'][layer], p['ffn_w_up'][layer],
                      p['ffn_conv_w'][layer], p['ffn_conv_b'][layer].reshape(1, -1), p['ffn_w_down'][layer],
                      p['norm_final'].reshape(1, d), conv[layer], b, l, final_norm=(layer == depth - 1))
        new_conv.append(buf)
    return (x.reshape(b, l, d), jnp.stack(new_re), jnp.stack(new_im), jnp.stack(new_ret),
            jnp.stack(new_hg), jnp.stack(new_conv))


def kernel(x_prompt, x_sample, state_s5_re, state_s5_im, state_ret, state_hgrn, state_ffn_conv, pos_sample, norm_mix, norm_ffn, norm_final, w_in_ab, s5_lam_re, s5_lam_im, s5_log_dt, s5_b_re, s5_b_im, s5_c_re, s5_c_im, s5_d, s5_glu_w, s5_glu_b, w_out_ab, w_in_c, hg_lb_logits, hg_norm_w, w_out_c, ffn_w_gate, ffn_w_up, ffn_conv_w, ffn_conv_b, ffn_w_down):
    n_ab, n_grp, n_st = s5_lam_re.shape
    ch = s5_b_re.shape[-1]
    width = n_grp * ch
    grp_per_blk = MXU_DIM // ch
    n_blk = n_grp // grp_per_blk
    s5 = []
    for j in range(n_ab):
        a_re, a_im, bb_re, bb_im = _s5_prep(s5_lam_re[j], s5_lam_im[j], s5_log_dt[j], s5_b_re[j], s5_b_im[j])
        bmat = lambda m: jnp.stack([_block_diag(blk, grp_per_blk)
                                    for blk in m.reshape(n_blk, grp_per_blk * ch, n_st)]).astype(BF16)
        cmat = lambda c: jnp.stack([_block_diag(blk, grp_per_blk).T for blk in
                                    c.reshape(n_blk, grp_per_blk * ch, n_st)]).astype(BF16)
        s5.append(dict(wu=w_in_ab[j][:, :width].astype(BF16),
                       bmat_re=bmat(bb_re), bmat_im=bmat(bb_im),
                       cmat_re=cmat(s5_c_re[j]), cmat_im=cmat(s5_c_im[j]),
                       d=s5_d[j].reshape(1, width), glu_w=s5_glu_w[j].astype(BF16),
                       glu_b=s5_glu_b[j].reshape(1, width),
                       a_re=a_re.reshape(1, n_grp * n_st), a_im=a_im.reshape(1, n_grp * n_st)))
    p = dict(norm_mix=norm_mix, norm_ffn=norm_ffn, norm_final=norm_final, s5=s5,
             w_in_ab_ret=w_in_ab[:, :, width:].astype(BF16), w_out_ab=w_out_ab.astype(BF16),
             w_in_c=w_in_c.astype(BF16), hg_lb_logits=hg_lb_logits, hg_norm_w=hg_norm_w,
             w_out_c=w_out_c.astype(BF16), ffn_w_gate=ffn_w_gate.astype(BF16),
             ffn_w_up=ffn_w_up.astype(BF16), ffn_conv_w=ffn_conv_w, ffn_conv_b=ffn_conv_b,
             ffn_w_down=ffn_w_down.astype(BF16))

    bp, lp, _ = x_prompt.shape
    z_s5 = jnp.zeros((n_ab, bp) + state_s5_re.shape[2:], F32)
    z_ret = jnp.zeros((n_ab, bp) + state_ret.shape[2:], F32)
    z_hg = jnp.zeros((state_hgrn.shape[0], bp) + state_hgrn.shape[2:], F32)
    z_conv = jnp.zeros((norm_mix.shape[0], bp) + state_ffn_conv.shape[2:], F32)
    outs_p = _trunk(x_prompt, jnp.arange(lp, dtype=jnp.int32), z_s5, z_s5, z_ret, z_hg, z_conv, p,
                    s5_tl=ROW_TILE // bp)
    bs, ls, _ = x_sample.shape
    pos_s = (pos_sample[:, None] + jnp.arange(ls, dtype=jnp.int32)[None, :]).reshape(-1)
    outs_s = _trunk(x_sample, pos_s, state_s5_re, state_s5_im, state_ret, state_hgrn, state_ffn_conv, p,
                    s5_tl=ls)
    return (outs_p[0], outs_s[0]) + outs_p[1:] + outs_s[1:]
```

```python
import functools

import jax
import jax.numpy as jnp
from jax import lax
from jax.experimental import pallas as pl
from jax.experimental.pallas import tpu as pltpu

F32 = jnp.float32
BF16 = jnp.bfloat16

NORM_EPS = 1e-6
ROPE_BASE = 10000.0
S5_GROUP_CH = 16
RET_HEADS = 4
RET_CHUNK = 128
HG_HEADS = 8
HG_BLOCK = 16
HG_CHUNK = 128
RET_HEADS_PER_STEP = 4
HG_HEADS_PER_STEP = 4
CONV_W = 3

LANES = 128
SUBLANES = 8
MXU_DIM = 256
VMEM_LIMIT_BYTES = 56 * 1024 * 1024
ROW_TILE = 512
FF_CHUNK = 256
S5_SCAN_COLS = 512


def _cparams(n_grid_dims):
    return pltpu.CompilerParams(dimension_semantics=("arbitrary",) * n_grid_dims,
                                vmem_limit_bytes=VMEM_LIMIT_BYTES)


def _dot(a, b):
    return jnp.dot(a.astype(BF16), b.astype(BF16), preferred_element_type=F32)


def _dot_nt(a, b):
    return lax.dot_general(a.astype(BF16), b.astype(BF16), (((1,), (1,)), ((), ())),
                           preferred_element_type=F32)


def _rmsnorm(x, w):
    return x * lax.rsqrt(jnp.mean(x * x, axis=-1, keepdims=True) + NORM_EPS) * w


def _full(shape):
    nd = len(shape)
    return pl.BlockSpec(shape, lambda *_: (0,) * nd)


def _proj_kernel(x_ref, nw_ref, w_ref, o_ref, *, col_chunk):
    h = _rmsnorm(x_ref[...], nw_ref[...]).astype(BF16)
    for c in range(0, w_ref.shape[1], col_chunk):
        o_ref[:, c:c + col_chunk] = jnp.dot(h, w_ref[:, c:c + col_chunk], preferred_element_type=F32)


def _proj(x, nw, w_bf):
    n, d = x.shape
    n_out = w_bf.shape[1]
    tm = min(ROW_TILE, n)
    return pl.pallas_call(
        functools.partial(_proj_kernel, col_chunk=2 * MXU_DIM),
        grid=(n // tm,),
        in_specs=[pl.BlockSpec((tm, d), lambda i: (i, 0)), _full((1, d)), _full((d, n_out))],
        out_specs=pl.BlockSpec((tm, n_out), lambda i: (i, 0)),
        out_shape=jax.ShapeDtypeStruct((n, n_out), F32),
        compiler_params=_cparams(1), name="proj")(x, nw, w_bf)


def _outproj_kernel(*refs, n_parts):
    x_ref, parts, w_ref, o_ref = refs[0], refs[1:1 + n_parts], refs[1 + n_parts], refs[2 + n_parts]
    y = jnp.concatenate([p[...].astype(BF16) for p in parts], axis=1)
    o_ref[...] = x_ref[...] + jnp.dot(y, w_ref[...], preferred_element_type=F32)


def _outproj(x, parts, w_bf):
    n, d = x.shape
    tm = min(ROW_TILE, n)
    in_specs = [pl.BlockSpec((tm, d), lambda i: (i, 0))]
    in_specs += [pl.BlockSpec((tm, p.shape[1]), lambda i: (i, 0)) for p in parts]
    in_specs += [_full(w_bf.shape)]
    return pl.pallas_call(
        functools.partial(_outproj_kernel, n_parts=len(parts)),
        grid=(n // tm,), in_specs=in_specs,
        out_specs=pl.BlockSpec((tm, d), lambda i: (i, 0)),
        out_shape=jax.ShapeDtypeStruct((n, d), F32),
        compiler_params=_cparams(1), name="outproj")(x, *parts, w_bf)


def _s5_prep_kernel(lre_ref, lim_ref, ldt_ref, lre16_ref, lim16_ref, ldt16_ref, bre_ref, bim_ref,
                    are_ref, aim_ref, bbre_ref, bbim_ref):
    def disc(lre, lim, ldt):
        dt = jnp.exp(ldt)
        mag = jnp.exp(lre * dt)
        ang = lim * dt
        return mag * jnp.cos(ang), mag * jnp.sin(ang)

    ab_re, ab_im = disc(lre_ref[...], lim_ref[...], ldt_ref[...])
    are_ref[...] = ab_re
    aim_ref[...] = ab_im
    lre, lim = lre16_ref[...], lim16_ref[...]
    ab_re, ab_im = disc(lre, lim, ldt16_ref[...])
    nr, ni = ab_re - 1.0, ab_im
    den = lre * lre + lim * lim
    f_re = (nr * lre + ni * lim) / den
    f_im = (ni * lre - nr * lim) / den
    b_re, b_im = bre_ref[...], bim_ref[...]
    bbre_ref[...] = f_re * b_re - f_im * b_im
    bbim_ref[...] = f_re * b_im + f_im * b_re


def _s5_prep(lam_re, lam_im, log_dt, b_re, b_im):
    g, p = lam_re.shape
    ch = b_re.shape[-1]
    rep = lambda a: jnp.repeat(a, ch, axis=0)
    ldt = log_dt.reshape(g, 1)
    bt = lambda b: jnp.swapaxes(b, 1, 2).reshape(g * ch, p)
    args = (lam_re, lam_im, ldt, rep(lam_re), rep(lam_im), rep(ldt), bt(b_re), bt(b_im))
    return pl.pallas_call(
        _s5_prep_kernel,
        in_specs=[_full(a.shape) for a in args],
        out_specs=[_full((g, p)), _full((g, p)), _full((g * ch, p)), _full((g * ch, p))],
        out_shape=[jax.ShapeDtypeStruct((g, p), F32)] * 2 + [jax.ShapeDtypeStruct((g * ch, p), F32)] * 2,
        name="s5_prep")(*args)


def _block_diag(m, n_blk):
    r, c = m.shape[0] // n_blk, m.shape[1]
    eye = jnp.eye(n_blk, dtype=m.dtype)
    return (m.reshape(n_blk, r, 1, c) * eye[:, None, :, None]).reshape(n_blk * r, n_blk * c)


def _s5_kernel(x_ref, nw_ref, wu_ref, bre_ref, bim_ref, cre_ref, cim_ref, d_ref, gw_ref, gb_ref,
               are_ref, aim_ref, s0re_ref, s0im_ref,
               y_ref, sre_ref, sim_ref, bure_s, buim_s, stre_s, stim_s, *, tb, tl):
    @pl.when(pl.program_id(0) == 0)
    def _():
        stre_s[...] = s0re_ref[...]
        stim_s[...] = s0im_ref[...]

    d = nw_ref.shape[-1]
    width = wu_ref.shape[1]
    x = jnp.concatenate([x_ref[:, t, :] for t in range(tl)], axis=0)
    u = jnp.dot(_rmsnorm(x, nw_ref[...]).astype(BF16), wu_ref[...], preferred_element_type=F32)
    ub = u.astype(BF16)

    n_kb, kw, cw = bre_ref.shape
    for kb in range(n_kb):
        uk = ub[:, kb * kw:(kb + 1) * kw]
        bure_s[:, kb * cw:(kb + 1) * cw] = jnp.dot(uk, bre_ref[kb], preferred_element_type=F32)
        buim_s[:, kb * cw:(kb + 1) * cw] = jnp.dot(uk, bim_ref[kb], preferred_element_type=F32)

    n_state = are_ref.shape[1]
    sc = min(S5_SCAN_COLS, n_state)
    for c0 in range(0, n_state, sc):
        cs = slice(c0, c0 + sc)
        a_re = jnp.broadcast_to(are_ref[:, cs], (tb, sc))
        a_im = jnp.broadcast_to(aim_ref[:, cs], (tb, sc))

        def step(t, carry, cs=cs, a_re=a_re, a_im=a_im):
            s_re, s_im = carry
            rows = pl.ds(pl.multiple_of(t * tb, tb), tb)
            n_re = a_re * s_re - a_im * s_im + bure_s[rows, cs]
            n_im = a_re * s_im + a_im * s_re + buim_s[rows, cs]
            bure_s[rows, cs] = n_re
            buim_s[rows, cs] = n_im
            return n_re, n_im

        s_re, s_im = lax.fori_loop(0, tl, step, (stre_s[:, cs], stim_s[:, cs]), unroll=min(tl, 8))
        stre_s[:, cs] = s_re
        stim_s[:, cs] = s_im

    n_ob, ckw, ocw = cre_ref.shape
    ys = []
    for ob in range(n_ob):
        ks = slice(ob * ckw, (ob + 1) * ckw)
        ys.append(_dot(bure_s[:, ks], cre_ref[ob]) - _dot(buim_s[:, ks], cim_ref[ob]))
    y = jnp.concatenate(ys, axis=1) + d_ref[...] * u
    y = jax.nn.gelu(y)
    out = y * jax.nn.sigmoid(_dot(y, gw_ref[...]) + gb_ref[...])
    for t in range(tl):
        y_ref[:, t, :] = out[t * tb:(t + 1) * tb, :]
    sre_ref[...] = stre_s[...]
    sim_ref[...] = stim_s[...]


def _s5_mixer(x3, nw, wu_bf, bmat_re, bmat_im, cmat_re, cmat_im, d_row, glu_w_bf, glu_b, a_re, a_im,
              s0_re, s0_im, tl):
    b, l, d = x3.shape
    width = wu_bf.shape[1]
    n_state = a_re.shape[1]
    rows = b * tl
    args = (x3, nw, wu_bf, bmat_re, bmat_im, cmat_re, cmat_im, d_row, glu_w_bf, glu_b,
            a_re, a_im, s0_re, s0_im)
    in_specs = [pl.BlockSpec((b, tl, d), lambda i: (0, i, 0))] + [_full(a.shape) for a in args[1:]]
    y, s_re, s_im = pl.pallas_call(
        functools.partial(_s5_kernel, tb=b, tl=tl),
        grid=(l // tl,), in_specs=in_specs,
        out_specs=[pl.BlockSpec((b, tl, width), lambda i: (0, i, 0)), _full((b, n_state)), _full((b, n_state))],
        out_shape=[jax.ShapeDtypeStruct((b, l, width), F32),
                   jax.ShapeDtypeStruct((b, n_state), F32), jax.ShapeDtypeStruct((b, n_state), F32)],
        scratch_shapes=[pltpu.VMEM((rows, n_state), F32), pltpu.VMEM((rows, n_state), F32),
                        pltpu.VMEM((b, n_state), F32), pltpu.VMEM((b, n_state), F32)],
        compiler_params=_cparams(1), name="s5_mixer")(*args)
    return y.reshape(b * l, width), s_re, s_im


def _rope_kernel(pos_ref, inv_ref, cc_ref, ss_ref):
    ang = pos_ref[...] * inv_ref[...]
    lane = lax.broadcasted_iota(jnp.int32, ang.shape, 1)
    cc_ref[...] = jnp.cos(ang)
    ss_ref[...] = jnp.where(lane < ang.shape[1] // 2, -jnp.sin(ang), jnp.sin(ang))


def _rope_tables(pos, dk):
    n = pos.shape[0]
    inv = 1.0 / (ROPE_BASE ** jnp.linspace(0.0, 1.0, dk // 2, dtype=F32))
    inv = jnp.concatenate([inv, inv]).reshape(1, dk)
    return pl.pallas_call(
        _rope_kernel,
        in_specs=[_full((n, 1)), _full((1, dk))],
        out_specs=[_full((n, dk)), _full((n, dk))],
        out_shape=[jax.ShapeDtypeStruct((n, dk), F32)] * 2,
        name="rope_tables")(pos.astype(F32).reshape(n, 1), inv)


def _rotate(t, cc, ss):
    return t * cc + pltpu.roll(t, t.shape[1] // 2, 1) * ss


def _ret_consts(rows, c):
    lg = jnp.log(1.0 - 2.0 ** (-5.0 - jnp.arange(RET_HEADS, dtype=F32)))
    r = jnp.arange(rows)
    idx, blk = r % c, r // c
    diff = idx[:, None] - idx[None, :]
    same = blk[:, None] == blk[None, :]
    decay = jnp.where((same & (diff >= 0))[None],
                      jnp.exp(jnp.maximum(diff, 0)[None].astype(F32) * lg[:, None, None]), 0.0)
    wide = lambda v: jnp.broadcast_to(v[:, :, None], (RET_HEADS, rows, LANES))
    q_dec = wide(jnp.exp((idx + 1).astype(F32)[None, :] * lg[:, None]))
    k_dec = wide(jnp.exp((c - 1 - idx).astype(F32)[None, :] * lg[:, None]))
    chunk_dec = jnp.broadcast_to(jnp.exp(c * lg)[:, None, None], (RET_HEADS, 8, LANES))
    return decay, q_dec, k_dec, chunk_dec


def _ret_gate(o, g):
    o = o * lax.rsqrt(jnp.mean(o * o, axis=-1, keepdims=True) + NORM_EPS)
    return jax.nn.silu(g) * o


def _ret_long_kernel(q_ref, k_ref, v_ref, g_ref, cc_ref, ss_ref, dec_ref, qd_ref, kd_ref, cd_ref, s0_ref,
                     y_ref, so_ref, *, c, hp):
    dk = q_ref.shape[1] // hp
    scale = dk ** -0.5

    def chunk(i, states):
        rows = pl.ds(pl.multiple_of(i * c, c), c)
        cc, ss = cc_ref[rows, :], ss_ref[rows, :]
        new_states = []
        for j in range(hp):
            cols = slice(j * dk, (j + 1) * dk)
            q = _rotate(q_ref[rows, cols], cc, ss)
            k = _rotate(k_ref[rows, cols], cc, ss) * scale
            v = v_ref[rows, cols]
            scores = _dot_nt(q, k) * dec_ref[j]
            o = _dot(scores, v) + _dot(q * qd_ref[j], states[j])
            new_states.append(cd_ref[j, 0:1, :] * states[j] + _dot((k * kd_ref[j]).T, v))
            y_ref[rows, cols] = _ret_gate(o, g_ref[rows, cols]).astype(y_ref.dtype)
        return tuple(new_states)

    states = lax.fori_loop(0, q_ref.shape[0] // c, chunk, tuple(s0_ref[0, j] for j in range(hp)))
    for j in range(hp):
        so_ref[0, j] = states[j]


def _ret_short_kernel(q_ref, k_ref, v_ref, g_ref, cc_ref, ss_ref, dec_ref, qd_ref, kd_ref, cd_ref, s0_ref,
                      y_ref, so_ref, qd_s, oi_s, *, l):
    rows, dk = q_ref.shape
    scale = dk ** -0.5
    cc, ss = cc_ref[...], ss_ref[...]
    q = _rotate(q_ref[...], cc, ss)
    k = _rotate(k_ref[...], cc, ss) * scale
    v = v_ref[...]
    vb = v.astype(BF16)
    o_intra = _dot(_dot_nt(q, k) * dec_ref[0], vb)
    qd_s[...] = q * qd_ref[0]
    kdt = (k * kd_ref[0]).T
    cd = cd_ref[0, 0:1, :]
    win = 16
    per = win // l
    lane = lax.broadcasted_iota(jnp.int32, (dk, rows), 1)
    wrow = lax.broadcasted_iota(jnp.int32, (win, dk), 0)

    def window(w, carry):
        wr = pl.ds(pl.multiple_of(w * win, win), win)
        qw = qd_s[wr, :].astype(BF16)
        oi = jnp.zeros((win, dk), F32)
        for j in range(per):
            b = w * per + j
            s = s0_ref[b, 0]
            oi = jnp.where(wrow // l == j, jnp.dot(qw, s.astype(BF16), preferred_element_type=F32), oi)
            so_ref[b, 0] = cd * s + jnp.dot(jnp.where(lane // l == b, kdt, 0.0).astype(BF16), vb,
                                            preferred_element_type=F32)
        oi_s[wr, :] = oi
        return carry

    lax.fori_loop(0, rows // win, window, 0)
    y_ref[...] = _ret_gate(o_intra + oi_s[...], g_ref[...]).astype(y_ref.dtype)


def _retention(z, col0, cc, ss, s0, n_seq, l):
    n = z.shape[0]
    dk = s0.shape[-1]
    if l % RET_CHUNK == 0:
        rows, c, n_blk, hp = l, RET_CHUNK, n_seq, RET_HEADS_PER_STEP
        kern = functools.partial(_ret_long_kernel, c=c, hp=hp)
        tspec = _full((l, dk))
        sspec = pl.BlockSpec((1, hp, dk, dk), lambda b, h: (b, h, 0, 0))
        consts = _ret_consts(c, c)
        scratch = []
    else:
        rows, c, hp = LANES, l, 1
        per_blk = rows // l
        n_blk = n_seq // per_blk
        kern = functools.partial(_ret_short_kernel, l=l)
        tspec = pl.BlockSpec((rows, dk), lambda b, h: (b, 0))
        sspec = pl.BlockSpec((per_blk, 1, dk, dk), lambda b, h: (b, h, 0, 0))
        consts = _ret_consts(rows, c)
        scratch = [pltpu.VMEM((rows, dk), F32), pltpu.VMEM((rows, dk), F32)]
    decay, q_dec, k_dec, chunk_dec = consts
    n_hb = RET_HEADS // hp
    cb0 = col0 // (hp * dk)
    zspec = lambda rows, field: pl.BlockSpec((rows, hp * dk), lambda b, h: (b, cb0 + field * n_hb + h))
    cspec = lambda shape: pl.BlockSpec((hp,) + shape, lambda b, h: (h, 0, 0))
    y, s_new = pl.pallas_call(
        kern, grid=(n_blk, n_hb),
        in_specs=[zspec(rows, 0), zspec(rows, 1), zspec(rows, 2), zspec(rows, 3), tspec, tspec,
                  cspec(decay.shape[1:]), cspec(q_dec.shape[1:]), cspec(k_dec.shape[1:]),
                  cspec(chunk_dec.shape[1:]), sspec],
        out_specs=[pl.BlockSpec((rows, hp * dk), lambda b, h: (b, h)), sspec],
        out_shape=[jax.ShapeDtypeStruct((n, RET_HEADS * dk), BF16), jax.ShapeDtypeStruct(s0.shape, F32)],
        scratch_shapes=scratch,
        compiler_params=_cparams(2), name="retention")(z, z, z, z, cc, ss, decay, q_dec, k_dec, chunk_dec, s0)
    return y, s_new


def _cumsum_rows(x, period):
    row = lax.broadcasted_iota(jnp.int32, x.shape, 0)
    s = 1
    while s < period:
        x = x + jnp.where(row % period >= s, pltpu.roll(x, s, 0), 0.0)
        s *= 2
    return x


def _row_of_block(x, period, offset):
    n, w = x.shape
    x3 = x.reshape(n // period, period, w)
    return jnp.broadcast_to(x3[:, offset:offset + 1, :], x3.shape).reshape(n, w)


def _hgrn_gates(fl, lg, layer):
    e = jnp.exp(lg - jnp.max(lg, axis=0, keepdims=True))
    soft = e / jnp.sum(e, axis=0, keepdims=True)
    cum = soft[0:1]
    for i in range(1, layer + 1):
        cum = cum + soft[i:i + 1]
    lb = cum - soft[0:1]
    forget = lb + (1.0 - lb) * jax.nn.sigmoid(fl)
    return 1.0 - forget, jnp.log(forget)


def _hgrn_out(o, g, nw):
    o = o * lax.rsqrt(jnp.mean(o * o, axis=-1, keepdims=True) + NORM_EPS) * nw
    return o * jax.nn.silu(g)


def _hgrn_long_kernel(q_ref, f_ref, v_ref, g_ref, lbl_ref, nw_ref, s0_ref, y_ref, so_ref, *, c, layer, hp):
    dk = q_ref.shape[1] // hp
    nw = nw_ref[...]
    nv = c // SUBLANES
    bv = HG_BLOCK // SUBLANES
    levels = []
    half = HG_BLOCK
    while half < c:
        levels.append(half)
        half *= 2
    row = lax.broadcasted_iota(jnp.int32, (c, c), 0)
    col = lax.broadcasted_iota(jnp.int32, (c, c), 1)
    lvl = jnp.where((row // HG_BLOCK == col // HG_BLOCK) & (col <= row), 0, -1)
    for li, half in enumerate(levels, 1):
        lvl = jnp.where((row // (2 * half) == col // (2 * half)) & (row % (2 * half) >= half)
                        & (col % (2 * half) < half), li, lvl)
    sub = lax.broadcasted_iota(jnp.int32, (SUBLANES, dk), 0)
    zeros = jnp.zeros((SUBLANES, dk), F32)
    cat = lambda slabs: jnp.concatenate(slabs, axis=0)

    def one_head(rows, cols, s):
        q, v = q_ref[rows, cols], v_ref[rows, cols]
        k, lf = _hgrn_gates(f_ref[rows, cols], lbl_ref[:, cols], layer)
        slabs = lambda a: [a[j * SUBLANES:(j + 1) * SUBLANES, :] for j in range(nv)]
        qs, ks = slabs(q), slabs(k)
        loc, run = [], [jnp.zeros((1, dk), F32)]
        for x in slabs(lf):
            step = 1
            while step < SUBLANES:
                x = x + jnp.where(sub >= step, pltpu.roll(x, step, 0), 0.0)
                step *= 2
            loc.append(x)
            run.append(run[-1] + x[SUBLANES - 1:, :])
        d0 = [loc[j] + (run[j] - run[j - j % bv]) if j % bv else loc[j] for j in range(nv)]
        scores = jnp.where(lvl == 0, _dot_nt(cat([qs[j] * jnp.exp(d0[j]) for j in range(nv)]),
                                             cat([ks[j] * jnp.exp(-d0[j]) for j in range(nv)])), 0.0)
        for li, half in enumerate(levels, 1):
            hv = half // SUBLANES
            q_side, k_side = [], []
            for j in range(nv):
                anchor = run[j - j % (2 * hv) + hv]
                if j % (2 * hv) >= hv:
                    q_side.append(qs[j] * jnp.exp(loc[j] + (run[j] - anchor)))
                    k_side.append(zeros)
                else:
                    q_side.append(zeros)
                    k_side.append(ks[j] * jnp.exp((anchor - run[j]) - loc[j]))
            scores = jnp.where(lvl == li, _dot_nt(cat(q_side), cat(k_side)), scores)
        o = _dot(scores, v) + _dot(cat([qs[j] * jnp.exp(loc[j] + run[j]) for j in range(nv)]), s)
        k_hat = cat([ks[j] * jnp.exp((run[nv] - run[j]) - loc[j]) for j in range(nv)])
        d_col = jnp.exp(jnp.broadcast_to(run[nv], (dk, dk)).T)
        y_ref[rows, cols] = _hgrn_out(o, g_ref[rows, cols], nw).astype(y_ref.dtype)
        return d_col * s + _dot(k_hat.T, v)

    def chunk(i, states):
        rows = pl.ds(pl.multiple_of(i * c, c), c)
        return tuple(one_head(rows, slice(j * dk, (j + 1) * dk), states[j]) for j in range(hp))

    states = lax.fori_loop(0, q_ref.shape[0] // c, chunk, tuple(s0_ref[0, j] for j in range(hp)))
    for j in range(hp):
        so_ref[0, j] = states[j]


def _hgrn_short_kernel(q_ref, f_ref, v_ref, g_ref, lbl_ref, nw_ref, s0_ref, y_ref, so_ref, qd_s, oi_s,
                       *, l, layer):
    rows, dk = q_ref.shape
    q, v = q_ref[...], v_ref[...]
    vb = v.astype(BF16)
    k, lf = _hgrn_gates(f_ref[...], lbl_ref[...], layer)
    b = _cumsum_rows(lf, l)
    row = lax.broadcasted_iota(jnp.int32, (rows, rows), 0)
    col = lax.broadcasted_iota(jnp.int32, (rows, rows), 1)
    qe = q * jnp.exp(b)
    scores = jnp.where((row // l == col // l) & (col <= row), _dot_nt(qe, k * jnp.exp(-b)), 0.0)
    o_intra = _dot(scores, vb)
    qd_s[...] = qe
    b_last = _row_of_block(b, l, l - 1)
    kht = (k * jnp.exp(b_last - b)).T
    dect = jnp.exp(b_last).T
    win = 16
    per = win // l
    lane = lax.broadcasted_iota(jnp.int32, (dk, rows), 1)
    wrow = lax.broadcasted_iota(jnp.int32, (win, dk), 0)

    def window(w, carry):
        wr = pl.ds(pl.multiple_of(w * win, win), win)
        qw = qd_s[wr, :].astype(BF16)
        oi = jnp.zeros((win, dk), F32)
        for j in range(per):
            bi = w * per + j
            s = s0_ref[bi, 0]
            oi = jnp.where(wrow // l == j, jnp.dot(qw, s.astype(BF16), preferred_element_type=F32), oi)
            d_col = jnp.sum(jnp.where(lane == bi * l, dect, 0.0), axis=1, keepdims=True)
            so_ref[bi, 0] = d_col * s + jnp.dot(jnp.where(lane // l == bi, kht, 0.0).astype(BF16), vb,
                                                preferred_element_type=F32)
        oi_s[wr, :] = oi
        return carry

    lax.fori_loop(0, rows // win, window, 0)
    y_ref[...] = _hgrn_out(o_intra + oi_s[...], g_ref[...], nw_ref[...]).astype(y_ref.dtype)


def _hgrn(z, lb_logits, norm_w, s0, n_seq, l, layer):
    n = z.shape[0]
    dk = s0.shape[-1]
    depth = lb_logits.shape[0]
    if l % HG_CHUNK == 0:
        rows, n_blk, hp = l, n_seq, HG_HEADS_PER_STEP
        kern = functools.partial(_hgrn_long_kernel, c=HG_CHUNK, layer=layer, hp=hp)
        sspec = pl.BlockSpec((1, hp, dk, dk), lambda b, h: (b, h, 0, 0))
        scratch = []
    else:
        assert HG_BLOCK % l == 0
        rows, hp = LANES, 1
        per_blk = rows // l
        n_blk = n_seq // per_blk
        kern = functools.partial(_hgrn_short_kernel, l=l, layer=layer)
        sspec = pl.BlockSpec((per_blk, 1, dk, dk), lambda b, h: (b, h, 0, 0))
        scratch = [pltpu.VMEM((rows, dk), F32), pltpu.VMEM((rows, dk), F32)]
    n_hb = HG_HEADS // hp
    zspec = lambda rows, field: pl.BlockSpec((rows, hp * dk), lambda b, h: (b, field * n_hb + h))
    y, s_new = pl.pallas_call(
        kern, grid=(n_blk, n_hb),
        in_specs=[zspec(rows, 0), zspec(rows, 1), zspec(rows, 2), zspec(rows, 3),
                  pl.BlockSpec((depth, hp * dk), lambda b, h: (0, h)), _full((1, dk)), sspec],
        out_specs=[pl.BlockSpec((rows, hp * dk), lambda b, h: (b, h)), sspec],
        out_shape=[jax.ShapeDtypeStruct((n, HG_HEADS * dk), BF16), jax.ShapeDtypeStruct(s0.shape, F32)],
        scratch_shapes=scratch,
        compiler_params=_cparams(2), name="hgrn")(z, z, z, z, lb_logits, norm_w, s0)
    return y, s_new


def _ffn_kernel(*refs, long_mode, final_norm, l):
    if long_mode:
        (x_ref, nw_ref, wg_ref, wu_ref, cw_ref, cb_ref, wd_ref, fnw_ref, buf_ref,
         o_ref, bufo_ref, act_s, tail_s) = refs
    else:
        (x_ref, nw_ref, wg_ref, wu_ref, cw_ref, cb_ref, wd_ref, fnw_ref, e1_ref, e2_ref,
         o_ref, gpre_ref, act_s) = refs
    x = x_ref[0] if long_mode else x_ref[...]
    tm = x.shape[0]
    d_ff = wg_ref.shape[1]
    h = _rmsnorm(x, nw_ref[...]).astype(BF16)
    if long_mode:
        @pl.when(pl.program_id(1) == 0)
        def _():
            tail_s[...] = buf_ref[0]

    for c0 in range(0, d_ff, FF_CHUNK):
        cs = slice(c0, c0 + FF_CHUNK)
        g = jnp.dot(h, wg_ref[:, cs], preferred_element_type=F32)
        up = jnp.dot(h, wu_ref[:, cs], preferred_element_type=F32)
        row = lax.broadcasted_iota(jnp.int32, g.shape, 0)
        r1, r2 = pltpu.roll(g, 1, 0), pltpu.roll(g, 2, 0)
        if long_mode:
            t = row
            e1 = jnp.broadcast_to(tail_s[1:2, cs], g.shape)
            e2 = jnp.where(row == 0, tail_s[0:1, cs], tail_s[1:2, cs])
            tail_s[:, cs] = g[tm - (CONV_W - 1):, :]
        else:
            t = row % l
            e1, e2 = e1_ref[:, cs], e2_ref[:, cs]
            gpre_ref[:, cs] = g
        p1 = jnp.where(t < 1, e1, r1)
        p2 = jnp.where(t < 2, e2, r2)
        conv = cb_ref[:, cs] + cw_ref[0:1, cs] * p2
        conv = conv + cw_ref[1:2, cs] * p1
        conv = conv + cw_ref[2:3, cs] * g
        act_s[:, cs] = (jax.nn.silu(conv) * up).astype(BF16)
    y = x + jnp.dot(act_s[...], wd_ref[...], preferred_element_type=F32)
    if final_norm:
        y = _rmsnorm(y, fnw_ref[...])
    if long_mode:
        o_ref[0] = y
        bufo_ref[0] = tail_s[...]
    else:
        o_ref[...] = y


def _ffn(x, nw, wg, wu, conv_w, conv_b, wd, fnw, buf, n_seq, l, final_norm):
    n, d = x.shape
    d_ff = wg.shape[1]
    weights = (nw, wg, wu, conv_w, conv_b, wd, fnw)
    wspecs = [_full(w.shape) for w in weights]
    if l % ROW_TILE == 0:
        tm = ROW_TILE
        y, buf_new = pl.pallas_call(
            functools.partial(_ffn_kernel, long_mode=True, final_norm=final_norm, l=l),
            grid=(n_seq, l // tm),
            in_specs=[pl.BlockSpec((1, tm, d), lambda b, i: (b, i, 0))] + wspecs
                     + [pl.BlockSpec((1, CONV_W - 1, d_ff), lambda b, i: (b, 0, 0))],
            out_specs=[pl.BlockSpec((1, tm, d), lambda b, i: (b, i, 0)),
                       pl.BlockSpec((1, CONV_W - 1, d_ff), lambda b, i: (b, 0, 0))],
            out_shape=[jax.ShapeDtypeStruct((n_seq, l, d), F32),
                       jax.ShapeDtypeStruct((n_seq, CONV_W - 1, d_ff), F32)],
            scratch_shapes=[pltpu.VMEM((tm, d_ff), BF16), pltpu.VMEM((CONV_W - 1, d_ff), F32)],
            compiler_params=_cparams(2), name="ffn")(x.reshape(n_seq, l, d), *weights, buf)
        return y.reshape(n, d), buf_new
    assert l >= CONV_W - 1
    pad = jnp.zeros((n_seq, l - 1, d_ff), F32)
    e1 = jnp.concatenate([buf[:, 1:2], pad], axis=1).reshape(n, d_ff)
    e2 = jnp.concatenate([buf, pad[:, 1:]], axis=1).reshape(n, d_ff)
    y, gpre = pl.pallas_call(
        functools.partial(_ffn_kernel, long_mode=False, final_norm=final_norm, l=l),
        in_specs=[_full((n, d))] + wspecs + [_full((n, d_ff)), _full((n, d_ff))],
        out_specs=[_full((n, d)), _full((n, d_ff))],
        out_shape=[jax.ShapeDtypeStruct((n, d), F32), jax.ShapeDtypeStruct((n, d_ff), F32)],
        scratch_shapes=[pltpu.VMEM((n, d_ff), BF16)],
        compiler_params=pltpu.CompilerParams(vmem_limit_bytes=VMEM_LIMIT_BYTES), name="ffn")(
            x, *weights, e1, e2)
    return y, gpre.reshape(n_seq, l, d_ff)[:, l - (CONV_W - 1):]


def _trunk(x3, pos, s5_re, s5_im, ret, hg, conv, p, s5_tl):
    b, l, d = x3.shape
    n = b * l
    x = x3.reshape(n, d)
    depth = p['norm_mix'].shape[0]
    new_re, new_im, new_ret, new_hg, new_conv = [], [], [], [], []
    for layer in range(depth):
        j = layer // 2
        nw = p['norm_mix'][layer].reshape(1, d)
        if layer % 2 == 0:
            s5w = p['s5'][j]
            width = s5w['wu'].shape[1]
            y_s5, r, im = _s5_mixer(x.reshape(b, l, d), nw, s5w['wu'], s5w['bmat_re'], s5w['bmat_im'],
                                    s5w['cmat_re'], s5w['cmat_im'], s5w['d'], s5w['glu_w'], s5w['glu_b'],
                                    s5w['a_re'], s5w['a_im'], s5_re[j].reshape(b, -1), s5_im[j].reshape(b, -1),
                                    s5_tl)
            z = _proj(x, nw, p['w_in_ab_ret'][j])
            dk = ret.shape[-1]
            cc, ss = _rope_tables(pos, dk)
            y_ret, st = _retention(z, 0, cc, ss, ret[j], b, l)
            x = _outproj(x, [y_s5, y_ret], p['w_out_ab'][j])
            new_re.append(r.reshape(s5_re.shape[1:]))
            new_im.append(im.reshape(s5_im.shape[1:]))
            new_ret.append(st)
            del width
        else:
            z = _proj(x, nw, p['w_in_c'][j])
            y_hg, st = _hgrn(z, p['hg_lb_logits'], p['hg_norm_w'][j].reshape(1, -1), hg[j], b, l, layer)
            x = _outproj(x, [y_hg], p['w_out_c'][j])
            new_hg.append(st)
        x, buf = _ffn(x, p['norm_ffn'][layer].reshape(1, d), p['ffn_w_gate'][layer], p['ffn_w_up'][layer],
                      p['ffn_conv_w'][layer], p['ffn_conv_b'][layer].reshape(1, -1), p['ffn_w_down'][layer],
                      p['norm_final'].reshape(1, d), conv[layer], b, l, final_norm=(layer == depth - 1))
        new_conv.append(buf)
    return (x.reshape(b, l, d), jnp.stack(new_re), jnp.stack(new_im), jnp.stack(new_ret),
            jnp.stack(new_hg), jnp.stack(new_conv))


def kernel(x_prompt, x_sample, state_s5_re, state_s5_im, state_ret, state_hgrn, state_ffn_conv, pos_sample, norm_mix, norm_ffn, norm_final, w_in_ab, s5_lam_re, s5_lam_im, s5_log_dt, s5_b_re, s5_b_im, s5_c_re, s5_c_im, s5_d, s5_glu_w, s5_glu_b, w_out_ab, w_in_c, hg_lb_logits, hg_norm_w, w_out_c, ffn_w_gate, ffn_w_up, ffn_conv_w, ffn_conv_b, ffn_w_down):
    n_ab, n_grp, n_st = s5_lam_re.shape
    ch = s5_b_re.shape[-1]
    width = n_grp * ch
    grp_per_blk = MXU_DIM // ch
    n_blk = n_grp // grp_per_blk
    s5 = []
    for j in range(n_ab):
        a_re, a_im, bb_re, bb_im = _s5_prep(s5_lam_re[j], s5_lam_im[j], s5_log_dt[j], s5_b_re[j], s5_b_im[j])
        bmat = lambda m: jnp.stack([_block_diag(blk, grp_per_blk)
                                    for blk in m.reshape(n_blk, grp_per_blk * ch, n_st)]).astype(BF16)
        cmat = lambda c: jnp.stack([_block_diag(blk, grp_per_blk).T for blk in
                                    c.reshape(n_blk, grp_per_blk * ch, n_st)]).astype(BF16)
        s5.append(dict(wu=w_in_ab[j][:, :width].astype(BF16),
                       bmat_re=bmat(bb_re), bmat_im=bmat(bb_im),
                       cmat_re=cmat(s5_c_re[j]), cmat_im=cmat(s5_c_im[j]),
                       d=s5_d[j].reshape(1, width), glu_w=s5_glu_w[j].astype(BF16),
                       glu_b=s5_glu_b[j].reshape(1, width),
                       a_re=a_re.reshape(1, n_grp * n_st), a_im=a_im.reshape(1, n_grp * n_st)))
    p = dict(norm_mix=norm_mix, norm_ffn=norm_ffn, norm_final=norm_final, s5=s5,
             w_in_ab_ret=w_in_ab[:, :, width:].astype(BF16), w_out_ab=w_out_ab.astype(BF16),
             w_in_c=w_in_c.astype(BF16), hg_lb_logits=hg_lb_logits, hg_norm_w=hg_norm_w,
             w_out_c=w_out_c.astype(BF16), ffn_w_gate=ffn_w_gate.astype(BF16),
             ffn_w_up=ffn_w_up.astype(BF16), ffn_conv_w=ffn_conv_w, ffn_conv_b=ffn_conv_b,
             ffn_w_down=ffn_w_down.astype(BF16))

    bp, lp, _ = x_prompt.shape
    z_s5 = jnp.zeros((n_ab, bp) + state_s5_re.shape[2:], F32)
    z_ret = jnp.zeros((n_ab, bp) + state_ret.shape[2:], F32)
    z_hg = jnp.zeros((state_hgrn.shape[0], bp) + state_hgrn.shape[2:], F32)
    z_conv = jnp.zeros((norm_mix.shape[0], bp) + state_ffn_conv.shape[2:], F32)
    outs_p = _trunk(x_prompt, jnp.arange(lp, dtype=jnp.int32), z_s5, z_s5, z_ret, z_hg, z_conv, p,
                    s5_tl=ROW_TILE // bp)
    bs, ls, _ = x_sample.shape
    pos_s = (pos_sample[:, None] + jnp.arange(ls, dtype=jnp.int32)[None, :]).reshape(-1)
    outs_s = _trunk(x_sample, pos_s, state_s5_re, state_s5_im, state_ret, state_hgrn, state_ffn_conv, p,
                    s5_tl=ls)
    return (outs_p[0], outs_s[0]) + outs_p[1:] + outs_s[1:]
```

```python
import functools

import jax
import jax.numpy as jnp
from jax import lax
from jax.experimental import pallas as pl
from jax.experimental.pallas import tpu as pltpu

F32 = jnp.float32
BF16 = jnp.bfloat16

NORM_EPS = 1e-6
ROPE_BASE = 10000.0
S5_GROUP_CH = 16
RET_HEADS = 4
RET_CHUNK = 128
HG_HEADS = 8
HG_BLOCK = 16
HG_CHUNK = 128
RET_HEADS_PER_STEP = 4
HG_HEADS_PER_STEP = 4
CONV_W = 3

LANES = 128
SUBLANES = 8
MXU_DIM = 256
VMEM_LIMIT_BYTES = 56 * 1024 * 1024
ROW_TILE = 512
FF_CHUNK = 256
S5_SCAN_COLS = 512


def _cparams(n_grid_dims):
    return pltpu.CompilerParams(dimension_semantics=("arbitrary",) * n_grid_dims,
                                vmem_limit_bytes=VMEM_LIMIT_BYTES)


def _dot(a, b):
    return jnp.dot(a.astype(BF16), b.astype(BF16), preferred_element_type=F32)


def _dot_nt(a, b):
    return lax.dot_general(a.astype(BF16), b.astype(BF16), (((1,), (1,)), ((), ())),
                           preferred_element_type=F32)


def _rmsnorm(x, w):
    return x * lax.rsqrt(jnp.mean(x * x, axis=-1, keepdims=True) + NORM_EPS) * w


def _full(shape):
    nd = len(shape)
    return pl.BlockSpec(shape, lambda *_: (0,) * nd)


def _mix_dtype(l):
    return BF16 if l % ROW_TILE == 0 else F32


def _resident(shape):
    nd = len(shape)
    return pl.BlockSpec(shape, lambda *_: (0,) * nd, pipeline_mode=pl.Buffered(1))


def _proj_kernel(x_ref, nw_ref, w_ref, o_ref, *, col0, col_chunk):
    h = _rmsnorm(x_ref[...], nw_ref[...]).astype(BF16)
    for c in range(0, o_ref.shape[1], col_chunk):
        o_ref[:, c:c + col_chunk] = jnp.dot(h, w_ref[:, col0 + c:col0 + c + col_chunk],
                                            preferred_element_type=F32)


def _proj(x, nw, w_bf, col0=0):
    n, d = x.shape
    n_out = w_bf.shape[1] - col0
    tm = min(ROW_TILE, n)
    return pl.pallas_call(
        functools.partial(_proj_kernel, col0=col0, col_chunk=2 * MXU_DIM),
        grid=(n // tm,),
        in_specs=[pl.BlockSpec((tm, d), lambda i: (i, 0)), _full((1, d)), _resident(w_bf.shape)],
        out_specs=pl.BlockSpec((tm, n_out), lambda i: (i, 0)),
        out_shape=jax.ShapeDtypeStruct((n, n_out), F32),
        compiler_params=_cparams(1), name="proj")(x, nw, w_bf)


def _s5_prep_kernel(lre_ref, lim_ref, ldt_ref, lre16_ref, lim16_ref, ldt16_ref, bre_ref, bim_ref,
                    are_ref, aim_ref, bbre_ref, bbim_ref):
    def disc(lre, lim, ldt):
        dt = jnp.exp(ldt)
        mag = jnp.exp(lre * dt)
        ang = lim * dt
        return mag * jnp.cos(ang), mag * jnp.sin(ang)

    ab_re, ab_im = disc(lre_ref[...], lim_ref[...], ldt_ref[...])
    are_ref[...] = ab_re
    aim_ref[...] = ab_im
    lre, lim = lre16_ref[...], lim16_ref[...]
    ab_re, ab_im = disc(lre, lim, ldt16_ref[...])
    nr, ni = ab_re - 1.0, ab_im
    den = lre * lre + lim * lim
    f_re = (nr * lre + ni * lim) / den
    f_im = (ni * lre - nr * lim) / den
    b_re, b_im = bre_ref[...], bim_ref[...]
    bbre_ref[...] = f_re * b_re - f_im * b_im
    bbim_ref[...] = f_re * b_im + f_im * b_re


def _s5_prep(lam_re, lam_im, log_dt, b_re, b_im):
    g, p = lam_re.shape
    ch = b_re.shape[-1]
    rep = lambda a: jnp.repeat(a, ch, axis=0)
    ldt = log_dt.reshape(g, 1)
    bt = lambda b: jnp.swapaxes(b, 1, 2).reshape(g * ch, p)
    args = (lam_re, lam_im, ldt, rep(lam_re), rep(lam_im), rep(ldt), bt(b_re), bt(b_im))
    return pl.pallas_call(
        _s5_prep_kernel,
        in_specs=[_full(a.shape) for a in args],
        out_specs=[_full((g, p)), _full((g, p)), _full((g * ch, p)), _full((g * ch, p))],
        out_shape=[jax.ShapeDtypeStruct((g, p), F32)] * 2 + [jax.ShapeDtypeStruct((g * ch, p), F32)] * 2,
        name="s5_prep")(*args)


def _block_diag(m, n_blk):
    r, c = m.shape[0] // n_blk, m.shape[1]
    eye = jnp.eye(n_blk, dtype=m.dtype)
    return (m.reshape(n_blk, r, 1, c) * eye[:, None, :, None]).reshape(n_blk * r, n_blk * c)


def _s5_kernel(x_ref, nw_ref, wu_ref, bre_ref, bim_ref, cre_ref, cim_ref, d_ref, gw_ref, gb_ref,
               are_ref, aim_ref, s0re_ref, s0im_ref,
               y_ref, sre_ref, sim_ref, bure_s, buim_s, stre_s, stim_s, *, tb, tl):
    @pl.when(pl.program_id(0) == 0)
    def _():
        stre_s[...] = s0re_ref[...]
        stim_s[...] = s0im_ref[...]

    d = nw_ref.shape[-1]
    width = wu_ref.shape[1]
    x = jnp.concatenate([x_ref[:, t, :] for t in range(tl)], axis=0)
    u = jnp.dot(_rmsnorm(x, nw_ref[...]).astype(BF16), wu_ref[...], preferred_element_type=F32)
    ub = u.astype(BF16)

    n_kb, kw, cw = bre_ref.shape
    for kb in range(n_kb):
        uk = ub[:, kb * kw:(kb + 1) * kw]
        bure_s[:, kb * cw:(kb + 1) * cw] = jnp.dot(uk, bre_ref[kb], preferred_element_type=F32)
        buim_s[:, kb * cw:(kb + 1) * cw] = jnp.dot(uk, bim_ref[kb], preferred_element_type=F32)

    n_state = are_ref.shape[1]
    sc = min(S5_SCAN_COLS, n_state)
    for c0 in range(0, n_state, sc):
        cs = slice(c0, c0 + sc)
        a_re = jnp.broadcast_to(are_ref[:, cs], (tb, sc))
        a_im = jnp.broadcast_to(aim_ref[:, cs], (tb, sc))

        def step(t, carry, cs=cs, a_re=a_re, a_im=a_im):
            s_re, s_im = carry
            rows = pl.ds(pl.multiple_of(t * tb, tb), tb)
            n_re = a_re * s_re - a_im * s_im + bure_s[rows, cs]
            n_im = a_re * s_im + a_im * s_re + buim_s[rows, cs]
            bure_s[rows, cs] = n_re
            buim_s[rows, cs] = n_im
            return n_re, n_im

        s_re, s_im = lax.fori_loop(0, tl, step, (stre_s[:, cs], stim_s[:, cs]), unroll=min(tl, 8))
        stre_s[:, cs] = s_re
        stim_s[:, cs] = s_im

    n_ob, ckw, ocw = cre_ref.shape
    ys = []
    for ob in range(n_ob):
        ks = slice(ob * ckw, (ob + 1) * ckw)
        ys.append(_dot(bure_s[:, ks], cre_ref[ob]) - _dot(buim_s[:, ks], cim_ref[ob]))
    y = jnp.concatenate(ys, axis=1) + d_ref[...] * u
    y = jax.nn.gelu(y)
    out = y * jax.nn.sigmoid(_dot(y, gw_ref[...]) + gb_ref[...])
    for t in range(tl):
        y_ref[:, t, :] = out[t * tb:(t + 1) * tb, :]
    sre_ref[...] = stre_s[...]
    sim_ref[...] = stim_s[...]


def _s5_mixer(x3, nw, w_in_bf, bmat_re, bmat_im, cmat_re, cmat_im, d_row, glu_w_bf, glu_b, a_re, a_im,
              s0_re, s0_im, tl):
    b, l, d = x3.shape
    width = d_row.shape[1]
    n_state = a_re.shape[1]
    rows = b * tl
    args = (x3, nw, w_in_bf, bmat_re, bmat_im, cmat_re, cmat_im, d_row, glu_w_bf, glu_b,
            a_re, a_im, s0_re, s0_im)
    in_specs = ([pl.BlockSpec((b, tl, d), lambda i: (0, i, 0)), _full(nw.shape),
                 pl.BlockSpec((d, width), lambda i: (0, 0))] + [_full(a.shape) for a in args[3:]])
    y, s_re, s_im = pl.pallas_call(
        functools.partial(_s5_kernel, tb=b, tl=tl),
        grid=(l // tl,), in_specs=in_specs,
        out_specs=[pl.BlockSpec((b, tl, width), lambda i: (0, i, 0)), _full((b, n_state)), _full((b, n_state))],
        out_shape=[jax.ShapeDtypeStruct((b, l, width), F32),
                   jax.ShapeDtypeStruct((b, n_state), F32), jax.ShapeDtypeStruct((b, n_state), F32)],
        scratch_shapes=[pltpu.VMEM((rows, n_state), F32), pltpu.VMEM((rows, n_state), F32),
                        pltpu.VMEM((b, n_state), F32), pltpu.VMEM((b, n_state), F32)],
        compiler_params=_cparams(1), name="s5_mixer")(*args)
    return y.reshape(b * l, width), s_re, s_im


def _rope_kernel(pos_ref, inv_ref, cc_ref, ss_ref):
    ang = pos_ref[...] * inv_ref[...]
    lane = lax.broadcasted_iota(jnp.int32, ang.shape, 1)
    cc_ref[...] = jnp.cos(ang)
    ss_ref[...] = jnp.where(lane < ang.shape[1] // 2, -jnp.sin(ang), jnp.sin(ang))


def _rope_tables(pos, dk):
    n = pos.shape[0]
    inv = 1.0 / (ROPE_BASE ** jnp.linspace(0.0, 1.0, dk // 2, dtype=F32))
    inv = jnp.concatenate([inv, inv]).reshape(1, dk)
    return pl.pallas_call(
        _rope_kernel,
        in_specs=[_full((n, 1)), _full((1, dk))],
        out_specs=[_full((n, dk)), _full((n, dk))],
        out_shape=[jax.ShapeDtypeStruct((n, dk), F32)] * 2,
        name="rope_tables")(pos.astype(F32).reshape(n, 1), inv)


def _rotate(t, cc, ss):
    return t * cc + pltpu.roll(t, t.shape[1] // 2, 1) * ss


def _ret_consts(rows, c):
    lg = jnp.log(1.0 - 2.0 ** (-5.0 - jnp.arange(RET_HEADS, dtype=F32)))
    r = jnp.arange(rows)
    idx, blk = r % c, r // c
    diff = idx[:, None] - idx[None, :]
    same = blk[:, None] == blk[None, :]
    decay = jnp.where((same & (diff >= 0))[None],
                      jnp.exp(jnp.maximum(diff, 0)[None].astype(F32) * lg[:, None, None]), 0.0)
    wide = lambda v: jnp.broadcast_to(v[:, :, None], (RET_HEADS, rows, LANES))
    q_dec = wide(jnp.exp((idx + 1).astype(F32)[None, :] * lg[:, None]))
    k_dec = wide(jnp.exp((c - 1 - idx).astype(F32)[None, :] * lg[:, None]))
    chunk_dec = jnp.broadcast_to(jnp.exp(c * lg)[:, None, None], (RET_HEADS, 8, LANES))
    return decay, q_dec, k_dec, chunk_dec


def _ret_gate(o, g):
    o = o * lax.rsqrt(jnp.mean(o * o, axis=-1, keepdims=True) + NORM_EPS)
    return jax.nn.silu(g) * o


def _ret_long_kernel(q_ref, k_ref, v_ref, g_ref, cc_ref, ss_ref, dec_ref, qd_ref, kd_ref, cd_ref, s0_ref,
                     y_ref, so_ref, *, c, hp):
    dk = q_ref.shape[1] // hp
    scale = dk ** -0.5

    def chunk(i, states):
        rows = pl.ds(pl.multiple_of(i * c, c), c)
        cc, ss = cc_ref[rows, :], ss_ref[rows, :]
        new_states = []
        for j in range(hp):
            cols = slice(j * dk, (j + 1) * dk)
            q = _rotate(q_ref[rows, cols], cc, ss)
            k = _rotate(k_ref[rows, cols], cc, ss) * scale
            v = v_ref[rows, cols]
            scores = _dot_nt(q, k) * dec_ref[j]
            o = _dot(scores, v) + _dot(q * qd_ref[j], states[j])
            new_states.append(cd_ref[j, 0:1, :] * states[j] + _dot((k * kd_ref[j]).T, v))
            y_ref[rows, cols] = _ret_gate(o, g_ref[rows, cols]).astype(y_ref.dtype)
        return tuple(new_states)

    states = lax.fori_loop(0, q_ref.shape[0] // c, chunk, tuple(s0_ref[0, j] for j in range(hp)))
    for j in range(hp):
        so_ref[0, j] = states[j]


def _ret_short_kernel(q_ref, k_ref, v_ref, g_ref, cc_ref, ss_ref, dec_ref, qd_ref, kd_ref, cd_ref, s0_ref,
                      y_ref, so_ref, qd_s, oi_s, *, l):
    rows, dk = q_ref.shape
    scale = dk ** -0.5
    cc, ss = cc_ref[...], ss_ref[...]
    q = _rotate(q_ref[...], cc, ss)
    k = _rotate(k_ref[...], cc, ss) * scale
    v = v_ref[...]
    vb = v.astype(BF16)
    o_intra = _dot(_dot_nt(q, k) * dec_ref[0], vb)
    qd_s[...] = q * qd_ref[0]
    kdt = (k * kd_ref[0]).T
    cd = cd_ref[0, 0:1, :]
    win = 16
    per = win // l
    lane = lax.broadcasted_iota(jnp.int32, (dk, rows), 1)
    wrow = lax.broadcasted_iota(jnp.int32, (win, dk), 0)

    def window(w, carry):
        wr = pl.ds(pl.multiple_of(w * win, win), win)
        qw = qd_s[wr, :].astype(BF16)
        oi = jnp.zeros((win, dk), F32)
        for j in range(per):
            b = w * per + j
            s = s0_ref[b, 0]
            oi = jnp.where(wrow // l == j, jnp.dot(qw, s.astype(BF16), preferred_element_type=F32), oi)
            so_ref[b, 0] = cd * s + jnp.dot(jnp.where(lane // l == b, kdt, 0.0).astype(BF16), vb,
                                            preferred_element_type=F32)
        oi_s[wr, :] = oi
        return carry

    lax.fori_loop(0, rows // win, window, 0)
    y_ref[...] = _ret_gate(o_intra + oi_s[...], g_ref[...]).astype(y_ref.dtype)


def _retention(z, col0, cc, ss, s0, n_seq, l):
    n = z.shape[0]
    dk = s0.shape[-1]
    if l % RET_CHUNK == 0:
        rows, c, n_blk, hp = l, RET_CHUNK, n_seq, RET_HEADS_PER_STEP
        kern = functools.partial(_ret_long_kernel, c=c, hp=hp)
        tspec = _full((l, dk))
        sspec = pl.BlockSpec((1, hp, dk, dk), lambda b, h: (b, h, 0, 0))
        consts = _ret_consts(c, c)
        scratch = []
    else:
        rows, c, hp = LANES, l, 1
        per_blk = rows // l
        n_blk = n_seq // per_blk
        kern = functools.partial(_ret_short_kernel, l=l)
        tspec = pl.BlockSpec((rows, dk), lambda b, h: (b, 0))
        sspec = pl.BlockSpec((per_blk, 1, dk, dk), lambda b, h: (b, h, 0, 0))
        consts = _ret_consts(rows, c)
        scratch = [pltpu.VMEM((rows, dk), F32), pltpu.VMEM((rows, dk), F32)]
    decay, q_dec, k_dec, chunk_dec = consts
    n_hb = RET_HEADS // hp
    cb0 = col0 // (hp * dk)
    zspec = lambda rows, field: pl.BlockSpec((rows, hp * dk), lambda b, h: (b, cb0 + field * n_hb + h))
    cspec = lambda shape: pl.BlockSpec((hp,) + shape, lambda b, h: (h, 0, 0))
    y, s_new = pl.pallas_call(
        kern, grid=(n_blk, n_hb),
        in_specs=[zspec(rows, 0), zspec(rows, 1), zspec(rows, 2), zspec(rows, 3), tspec, tspec,
                  cspec(decay.shape[1:]), cspec(q_dec.shape[1:]), cspec(k_dec.shape[1:]),
                  cspec(chunk_dec.shape[1:]), sspec],
        out_specs=[pl.BlockSpec((rows, hp * dk), lambda b, h: (b, h)), sspec],
        out_shape=[jax.ShapeDtypeStruct((n, RET_HEADS * dk), _mix_dtype(l)), jax.ShapeDtypeStruct(s0.shape, F32)],
        scratch_shapes=scratch,
        compiler_params=_cparams(2), name="retention")(z, z, z, z, cc, ss, decay, q_dec, k_dec, chunk_dec, s0)
    return y, s_new


def _cumsum_rows(x, period):
    row = lax.broadcasted_iota(jnp.int32, x.shape, 0)
    s = 1
    while s < period:
        x = x + jnp.where(row % period >= s, pltpu.roll(x, s, 0), 0.0)
        s *= 2
    return x


def _row_of_block(x, period, offset):
    n, w = x.shape
    x3 = x.reshape(n // period, period, w)
    return jnp.broadcast_to(x3[:, offset:offset + 1, :], x3.shape).reshape(n, w)


def _hgrn_gates(fl, lg, layer):
    e = jnp.exp(lg - jnp.max(lg, axis=0, keepdims=True))
    soft = e / jnp.sum(e, axis=0, keepdims=True)
    cum = soft[0:1]
    for i in range(1, layer + 1):
        cum = cum + soft[i:i + 1]
    lb = cum - soft[0:1]
    forget = lb + (1.0 - lb) * jax.nn.sigmoid(fl)
    return 1.0 - forget, jnp.log(forget)


def _hgrn_out(o, g, nw):
    o = o * lax.rsqrt(jnp.mean(o * o, axis=-1, keepdims=True) + NORM_EPS) * nw
    return o * jax.nn.silu(g)


def _hgrn_long_kernel(q_ref, f_ref, v_ref, g_ref, lbl_ref, nw_ref, s0_ref, y_ref, so_ref, *, c, layer, hp):
    dk = q_ref.shape[1] // hp
    nw = nw_ref[...]
    nv = c // SUBLANES
    bv = HG_BLOCK // SUBLANES
    levels = []
    half = HG_BLOCK
    while half < c:
        levels.append(half)
        half *= 2
    row = lax.broadcasted_iota(jnp.int32, (c, c), 0)
    col = lax.broadcasted_iota(jnp.int32, (c, c), 1)
    lvl = jnp.where((row // HG_BLOCK == col // HG_BLOCK) & (col <= row), 0, -1)
    for li, half in enumerate(levels, 1):
        lvl = jnp.where((row // (2 * half) == col // (2 * half)) & (row % (2 * half) >= half)
                        & (col % (2 * half) < half), li, lvl)
    sub = lax.broadcasted_iota(jnp.int32, (SUBLANES, dk), 0)
    zeros = jnp.zeros((SUBLANES, dk), F32)
    cat = lambda slabs: jnp.concatenate(slabs, axis=0)

    def one_head(rows, cols, s):
        q, v = q_ref[rows, cols], v_ref[rows, cols]
        k, lf = _hgrn_gates(f_ref[rows, cols], lbl_ref[:, cols], layer)
        slabs = lambda a: [a[j * SUBLANES:(j + 1) * SUBLANES, :] for j in range(nv)]
        qs, ks = slabs(q), slabs(k)
        loc, run = [], [jnp.zeros((1, dk), F32)]
        for x in slabs(lf):
            step = 1
            while step < SUBLANES:
                x = x + jnp.where(sub >= step, pltpu.roll(x, step, 0), 0.0)
                step *= 2
            loc.append(x)
            run.append(run[-1] + x[SUBLANES - 1:, :])
        d0 = [loc[j] + (run[j] - run[j - j % bv]) if j % bv else loc[j] for j in range(nv)]
        scores = jnp.where(lvl == 0, _dot_nt(cat([qs[j] * jnp.exp(d0[j]) for j in range(nv)]),
                                             cat([ks[j] * jnp.exp(-d0[j]) for j in range(nv)])), 0.0)
        for li, half in enumerate(levels, 1):
            hv = half // SUBLANES
            q_side, k_side = [], []
            for j in range(nv):
                anchor = run[j - j % (2 * hv) + hv]
                if j % (2 * hv) >= hv:
                    q_side.append(qs[j] * jnp.exp(loc[j] + (run[j] - anchor)))
                    k_side.append(zeros)
                else:
                    q_side.append(zeros)
                    k_side.append(ks[j] * jnp.exp((anchor - run[j]) - loc[j]))
            scores = jnp.where(lvl == li, _dot_nt(cat(q_side), cat(k_side)), scores)
        o = _dot(scores, v) + _dot(cat([qs[j] * jnp.exp(loc[j] + run[j]) for j in range(nv)]), s)
        k_hat = cat([ks[j] * jnp.exp((run[nv] - run[j]) - loc[j]) for j in range(nv)])
        d_col = jnp.exp(jnp.broadcast_to(run[nv], (dk, dk)).T)
        y_ref[rows, cols] = _hgrn_out(o, g_ref[rows, cols], nw).astype(y_ref.dtype)
        return d_col * s + _dot(k_hat.T, v)

    def chunk(i, states):
        rows = pl.ds(pl.multiple_of(i * c, c), c)
        return tuple(one_head(rows, slice(j * dk, (j + 1) * dk), states[j]) for j in range(hp))

    states = lax.fori_loop(0, q_ref.shape[0] // c, chunk, tuple(s0_ref[0, j] for j in range(hp)))
    for j in range(hp):
        so_ref[0, j] = states[j]


def _hgrn_short_kernel(q_ref, f_ref, v_ref, g_ref, lbl_ref, nw_ref, s0_ref, y_ref, so_ref, qd_s, oi_s,
                       *, l, layer):
    rows, dk = q_ref.shape
    q, v = q_ref[...], v_ref[...]
    vb = v.astype(BF16)
    k, lf = _hgrn_gates(f_ref[...], lbl_ref[...], layer)
    b = _cumsum_rows(lf, l)
    row = lax.broadcasted_iota(jnp.int32, (rows, rows), 0)
    col = lax.broadcasted_iota(jnp.int32, (rows, rows), 1)
    qe = q * jnp.exp(b)
    scores = jnp.where((row // l == col // l) & (col <= row), _dot_nt(qe, k * jnp.exp(-b)), 0.0)
    o_intra = _dot(scores, vb)
    qd_s[...] = qe
    b_last = _row_of_block(b, l, l - 1)
    kht = (k * jnp.exp(b_last - b)).T
    dect = jnp.exp(b_last).T
    win = 16
    per = win // l
    lane = lax.broadcasted_iota(jnp.int32, (dk, rows), 1)
    wrow = lax.broadcasted_iota(jnp.int32, (win, dk), 0)

    def window(w, carry):
        wr = pl.ds(pl.multiple_of(w * win, win), win)
        qw = qd_s[wr, :].astype(BF16)
        oi = jnp.zeros((win, dk), F32)
        for j in range(per):
            bi = w * per + j
            s = s0_ref[bi, 0]
            oi = jnp.where(wrow // l == j, jnp.dot(qw, s.astype(BF16), preferred_element_type=F32), oi)
            d_col = jnp.sum(jnp.where(lane == bi * l, dect, 0.0), axis=1, keepdims=True)
            so_ref[bi, 0] = d_col * s + jnp.dot(jnp.where(lane // l == bi, kht, 0.0).astype(BF16), vb,
                                                preferred_element_type=F32)
        oi_s[wr, :] = oi
        return carry

    lax.fori_loop(0, rows // win, window, 0)
    y_ref[...] = _hgrn_out(o_intra + oi_s[...], g_ref[...], nw_ref[...]).astype(y_ref.dtype)


def _hgrn(z, lb_logits, norm_w, s0, n_seq, l, layer):
    n = z.shape[0]
    dk = s0.shape[-1]
    depth = lb_logits.shape[0]
    if l % HG_CHUNK == 0:
        rows, n_blk, hp = l, n_seq, HG_HEADS_PER_STEP
        kern = functools.partial(_hgrn_long_kernel, c=HG_CHUNK, layer=layer, hp=hp)
        sspec = pl.BlockSpec((1, hp, dk, dk), lambda b, h: (b, h, 0, 0))
        scratch = []
    else:
        assert HG_BLOCK % l == 0
        rows, hp = LANES, 1
        per_blk = rows // l
        n_blk = n_seq // per_blk
        kern = functools.partial(_hgrn_short_kernel, l=l, layer=layer)
        sspec = pl.BlockSpec((per_blk, 1, dk, dk), lambda b, h: (b, h, 0, 0))
        scratch = [pltpu.VMEM((rows, dk), F32), pltpu.VMEM((rows, dk), F32)]
    n_hb = HG_HEADS // hp
    zspec = lambda rows, field: pl.BlockSpec((rows, hp * dk), lambda b, h: (b, field * n_hb + h))
    y, s_new = pl.pallas_call(
        kern, grid=(n_blk, n_hb),
        in_specs=[zspec(rows, 0), zspec(rows, 1), zspec(rows, 2), zspec(rows, 3),
                  pl.BlockSpec((depth, hp * dk), lambda b, h: (0, h)), _full((1, dk)), sspec],
        out_specs=[pl.BlockSpec((rows, hp * dk), lambda b, h: (b, h)), sspec],
        out_shape=[jax.ShapeDtypeStruct((n, HG_HEADS * dk), _mix_dtype(l)), jax.ShapeDtypeStruct(s0.shape, F32)],
        scratch_shapes=scratch,
        compiler_params=_cparams(2), name="hgrn")(z, z, z, z, lb_logits, norm_w, s0)
    return y, s_new


def _ffn_kernel(*refs, n_parts, long_mode, final_norm, l):
    x_ref, parts = refs[0], refs[1:1 + n_parts]
    (wo_ref, nw_ref, wg_ref, wu_ref, cw_ref, cb_ref, wd_ref, fnw_ref, buf_ref, o_ref, bufo_ref,
     act_s) = refs[1 + n_parts:13 + n_parts]
    if long_mode:
        tail_s = refs[13 + n_parts]
        load = lambda ref: ref[0]
    else:
        nb = x_ref.shape[0]
        load = lambda ref: jnp.concatenate([ref[:, t, :] for t in range(l)], axis=0)
    x = load(x_ref)
    tm = x.shape[0]
    d_ff = wg_ref.shape[1]
    mix = jnp.concatenate([load(p).astype(BF16) for p in parts], axis=1)
    x = x + jnp.dot(mix, wo_ref[...], preferred_element_type=F32)
    h = _rmsnorm(x, nw_ref[...]).astype(BF16)
    if long_mode:
        @pl.when(pl.program_id(1) == 0)
        def _():
            tail_s[...] = buf_ref[0]

    for c0 in range(0, d_ff, FF_CHUNK):
        cs = slice(c0, c0 + FF_CHUNK)
        g = jnp.dot(h, wg_ref[:, cs], preferred_element_type=F32)
        up = jnp.dot(h, wu_ref[:, cs], preferred_element_type=F32)
        if long_mode:
            row = lax.broadcasted_iota(jnp.int32, g.shape, 0)
            p1 = jnp.where(row < 1, tail_s[1:2, cs], pltpu.roll(g, 1, 0))
            p2 = jnp.where(row < 2, jnp.where(row == 0, tail_s[0:1, cs], tail_s[1:2, cs]), pltpu.roll(g, 2, 0))
            tail_s[:, cs] = g[tm - (CONV_W - 1):, :]
        else:
            b0, b1 = buf_ref[:, 0, cs], buf_ref[:, 1, cs]
            p1 = jnp.concatenate([b1, g[:tm - nb]], axis=0)
            p2 = jnp.concatenate([b0, b1, g[:tm - 2 * nb]], axis=0)
            bufo_ref[:, 0, cs] = g[tm - 2 * nb:tm - nb]
            bufo_ref[:, 1, cs] = g[tm - nb:]
        conv = cb_ref[:, cs] + cw_ref[0:1, cs] * p2
        conv = conv + cw_ref[1:2, cs] * p1
        conv = conv + cw_ref[2:3, cs] * g
        act_s[:, cs] = (jax.nn.silu(conv) * up).astype(BF16)
    y = x + jnp.dot(act_s[...], wd_ref[...], preferred_element_type=F32)
    if final_norm:
        y = _rmsnorm(y, fnw_ref[...])
    if long_mode:
        o_ref[0] = y
        bufo_ref[0] = tail_s[...]
    else:
        for t in range(l):
            o_ref[:, t, :] = y[t * nb:(t + 1) * nb]


def _ffn(x, parts, wo, nw, wg, wu, conv_w, conv_b, wd, fnw, buf, n_seq, l, final_norm):
    assert CONV_W == 3 and l >= CONV_W - 1
    n, d = x.shape
    d_ff = wg.shape[1]
    weights = (wo, nw, wg, wu, conv_w, conv_b, wd, fnw)
    wspecs = [_full(w.shape) for w in weights]
    kern = functools.partial(_ffn_kernel, n_parts=len(parts), final_norm=final_norm, l=l)
    buf_shape = jax.ShapeDtypeStruct((n_seq, CONV_W - 1, d_ff), F32)
    rows3 = lambda a: a.reshape(n_seq, l, a.shape[1])
    if l % ROW_TILE == 0:
        tm = ROW_TILE
        rspec = lambda w: pl.BlockSpec((1, tm, w), lambda b, i: (b, i, 0))
        bspec = pl.BlockSpec((1, CONV_W - 1, d_ff), lambda b, i: (b, 0, 0))
        y, buf_new = pl.pallas_call(
            functools.partial(kern, long_mode=True),
            grid=(n_seq, l // tm),
            in_specs=([rspec(d)] + [rspec(p.shape[1]) for p in parts]
                      + [_resident(w.shape) for w in weights] + [bspec]),
            out_specs=[rspec(d), bspec],
            out_shape=[jax.ShapeDtypeStruct((n_seq, l, d), F32), buf_shape],
            scratch_shapes=[pltpu.VMEM((tm, d_ff), BF16), pltpu.VMEM((CONV_W - 1, d_ff), F32)],
            compiler_params=_cparams(2), name="ffn")(rows3(x), *map(rows3, parts), *weights, buf)
        return y.reshape(n, d), buf_new
    y, buf_new = pl.pallas_call(
        functools.partial(kern, long_mode=False),
        in_specs=([_full((n_seq, l, d))] + [_full((n_seq, l, p.shape[1])) for p in parts] + wspecs
                  + [_full(buf.shape)]),
        out_specs=[_full((n_seq, l, d)), _full(buf.shape)],
        out_shape=[jax.ShapeDtypeStruct((n_seq, l, d), F32), buf_shape],
        scratch_shapes=[pltpu.VMEM((n, d_ff), BF16)],
        compiler_params=pltpu.CompilerParams(vmem_limit_bytes=VMEM_LIMIT_BYTES), name="ffn")(
            rows3(x), *map(rows3, parts), *weights, buf)
    return y.reshape(n, d), buf_new


def _trunk(x3, pos, s5_re, s5_im, ret, hg, conv, p, s5_tl):
    b, l, d = x3.shape
    n = b * l
    x = x3.reshape(n, d)
    depth = p['norm_mix'].shape[0]
    new_re, new_im, new_ret, new_hg, new_conv = [], [], [], [], []
    for layer in range(depth):
        j = layer // 2
        nw = p['norm_mix'][layer].reshape(1, d)
        if layer % 2 == 0:
            s5w = p['s5'][j]
            width = s5w['d'].shape[1]
            y_s5, r, im = _s5_mixer(x.reshape(b, l, d), nw, p['w_in_ab'][j], s5w['bmat_re'], s5w['bmat_im'],
                                    s5w['cmat_re'], s5w['cmat_im'], s5w['d'], s5w['glu_w'], s5w['glu_b'],
                                    s5w['a_re'], s5w['a_im'], s5_re[j].reshape(b, -1), s5_im[j].reshape(b, -1),
                                    s5_tl)
            z = _proj(x, nw, p['w_in_ab'][j], col0=width)
            cc, ss = _rope_tables(pos, ret.shape[-1])
            y_ret, st = _retention(z, 0, cc, ss, ret[j], b, l)
            parts, w_out = [y_s5, y_ret], p['w_out_ab'][j]
            new_re.append(r.reshape(s5_re.shape[1:]))
            new_im.append(im.reshape(s5_im.shape[1:]))
            new_ret.append(st)
        else:
            z = _proj(x, nw, p['w_in_c'][j])
            y_hg, st = _hgrn(z, p['hg_lb_logits'], p['hg_norm_w'][j].reshape(1, -1), hg[j], b, l, layer)
            parts, w_out = [y_hg], p['w_out_c'][j]
            new_hg.append(st)
        x, buf = _ffn(x, parts, w_out, p['norm_ffn'][layer].reshape(1, d), p['ffn_w_gate'][layer],
                      p['ffn_w_up'][layer], p['ffn_conv_w'][layer], p['ffn_conv_b'][layer].reshape(1, -1),
                      p['ffn_w_down'][layer], p['norm_final'].reshape(1, d), conv[layer], b, l,
                      final_norm=(layer == depth - 1))
        new_conv.append(buf)
    return (x.reshape(b, l, d), jnp.stack(new_re), jnp.stack(new_im), jnp.stack(new_ret),
            jnp.stack(new_hg), jnp.stack(new_conv))


def kernel(x_prompt, x_sample, state_s5_re, state_s5_im, state_ret, state_hgrn, state_ffn_conv, pos_sample, norm_mix, norm_ffn, norm_final, w_in_ab, s5_lam_re, s5_lam_im, s5_log_dt, s5_b_re, s5_b_im, s5_c_re, s5_c_im, s5_d, s5_glu_w, s5_glu_b, w_out_ab, w_in_c, hg_lb_logits, hg_norm_w, w_out_c, ffn_w_gate, ffn_w_up, ffn_conv_w, ffn_conv_b, ffn_w_down):
    n_ab, n_grp, n_st = s5_lam_re.shape
    ch = s5_b_re.shape[-1]
    width = n_grp * ch
    grp_per_blk = MXU_DIM // ch
    n_blk = n_grp // grp_per_blk
    s5 = []
    for j in range(n_ab):
        a_re, a_im, bb_re, bb_im = _s5_prep(s5_lam_re[j], s5_lam_im[j], s5_log_dt[j], s5_b_re[j], s5_b_im[j])
        bmat = lambda m: jnp.stack([_block_diag(blk, grp_per_blk)
                                    for blk in m.reshape(n_blk, grp_per_blk * ch, n_st)]).astype(BF16)
        cmat = lambda c: jnp.stack([_block_diag(blk, grp_per_blk).T for blk in
                                    c.reshape(n_blk, grp_per_blk * ch, n_st)]).astype(BF16)
        s5.append(dict(bmat_re=bmat(bb_re), bmat_im=bmat(bb_im),
                       cmat_re=cmat(s5_c_re[j]), cmat_im=cmat(s5_c_im[j]),
                       d=s5_d[j].reshape(1, width), glu_w=s5_glu_w[j].astype(BF16),
                       glu_b=s5_glu_b[j].reshape(1, width),
                       a_re=a_re.reshape(1, n_grp * n_st), a_im=a_im.reshape(1, n_grp * n_st)))
    per_layer_bf16 = lambda w: [w[i].astype(BF16) for i in range(w.shape[0])]
    p = dict(norm_mix=norm_mix, norm_ffn=norm_ffn, norm_final=norm_final, s5=s5,
             w_in_ab=per_layer_bf16(w_in_ab), w_out_ab=per_layer_bf16(w_out_ab),
             w_in_c=per_layer_bf16(w_in_c), hg_lb_logits=hg_lb_logits, hg_norm_w=hg_norm_w,
             w_out_c=per_layer_bf16(w_out_c), ffn_w_gate=per_layer_bf16(ffn_w_gate),
             ffn_w_up=per_layer_bf16(ffn_w_up), ffn_conv_w=ffn_conv_w, ffn_conv_b=ffn_conv_b,
             ffn_w_down=per_layer_bf16(ffn_w_down))

    bp, lp, _ = x_prompt.shape
    z_s5 = jnp.zeros((n_ab, bp) + state_s5_re.shape[2:], F32)
    z_ret = jnp.zeros((n_ab, bp) + state_ret.shape[2:], F32)
    z_hg = jnp.zeros((state_hgrn.shape[0], bp) + state_hgrn.shape[2:], F32)
    z_conv = jnp.zeros((norm_mix.shape[0], bp) + state_ffn_conv.shape[2:], F32)
    outs_p = _trunk(x_prompt, jnp.arange(lp, dtype=jnp.int32), z_s5, z_s5, z_ret, z_hg, z_conv, p,
                    s5_tl=ROW_TILE // bp)
    bs, ls, _ = x_sample.shape
    pos_s = (pos_sample[:, None] + jnp.arange(ls, dtype=jnp.int32)[None, :]).reshape(-1)
    outs_s = _trunk(x_sample, pos_s, state_s5_re, state_s5_im, state_ret, state_hgrn, state_ffn_conv, p,
                    s5_tl=ls)
    return (outs_p[0], outs_s[0]) + outs_p[1:] + outs_s[1:]
```

```python
import functools

import jax
import jax.numpy as jnp
from jax import lax
from jax.experimental import pallas as pl
from jax.experimental.pallas import tpu as pltpu

F32 = jnp.float32
BF16 = jnp.bfloat16

NORM_EPS = 1e-6
ROPE_BASE = 10000.0
S5_GROUP_CH = 16
RET_HEADS = 4
RET_CHUNK = 128
HG_HEADS = 8
HG_BLOCK = 16
HG_CHUNK = 128
RET_HEADS_PER_STEP = 4
HG_HEADS_PER_STEP = 4
CONV_W = 3

LANES = 128
SUBLANES = 8
MXU_DIM = 256
VMEM_LIMIT_BYTES = 56 * 1024 * 1024
ROW_TILE = 512
FFN_ROW_TILE = 1024
FF_CHUNK = 256
S5_SCAN_COLS = 512


def _cparams(n_grid_dims):
    return pltpu.CompilerParams(dimension_semantics=("arbitrary",) * n_grid_dims,
                                vmem_limit_bytes=VMEM_LIMIT_BYTES)


def _dot(a, b):
    return jnp.dot(a.astype(BF16), b.astype(BF16), preferred_element_type=F32)


def _dot_nt(a, b):
    return lax.dot_general(a.astype(BF16), b.astype(BF16), (((1,), (1,)), ((), ())),
                           preferred_element_type=F32)


def _rmsnorm(x, w):
    return x * lax.rsqrt(jnp.mean(x * x, axis=-1, keepdims=True) + NORM_EPS) * w


def _full(shape):
    nd = len(shape)
    return pl.BlockSpec(shape, lambda *_: (0,) * nd)


def _mix_dtype(l):
    return BF16 if l % ROW_TILE == 0 else F32


def _resident(shape):
    nd = len(shape)
    return pl.BlockSpec(shape, lambda *_: (0,) * nd, pipeline_mode=pl.Buffered(1))


def _proj_kernel(x_ref, nw_ref, w_ref, o_ref, *, col0, col_chunk):
    h = _rmsnorm(x_ref[...], nw_ref[...]).astype(BF16)
    for c in range(0, o_ref.shape[1], col_chunk):
        o_ref[:, c:c + col_chunk] = jnp.dot(h, w_ref[:, col0 + c:col0 + c + col_chunk],
                                            preferred_element_type=F32)


def _proj(x, nw, w_bf, col0=0):
    n, d = x.shape
    n_out = w_bf.shape[1] - col0
    tm = min(ROW_TILE, n)
    return pl.pallas_call(
        functools.partial(_proj_kernel, col0=col0, col_chunk=2 * MXU_DIM),
        grid=(n // tm,),
        in_specs=[pl.BlockSpec((tm, d), lambda i: (i, 0)), _full((1, d)), _resident(w_bf.shape)],
        out_specs=pl.BlockSpec((tm, n_out), lambda i: (i, 0)),
        out_shape=jax.ShapeDtypeStruct((n, n_out), F32),
        compiler_params=_cparams(1), name="proj")(x, nw, w_bf)


def _s5_prep_kernel(lre_ref, lim_ref, ldt_ref, lre16_ref, lim16_ref, ldt16_ref, bre_ref, bim_ref,
                    are_ref, aim_ref, bbre_ref, bbim_ref):
    def disc(lre, lim, ldt):
        dt = jnp.exp(ldt)
        mag = jnp.exp(lre * dt)
        ang = lim * dt
        return mag * jnp.cos(ang), mag * jnp.sin(ang)

    ab_re, ab_im = disc(lre_ref[...], lim_ref[...], ldt_ref[...])
    are_ref[...] = ab_re
    aim_ref[...] = ab_im
    lre, lim = lre16_ref[...], lim16_ref[...]
    ab_re, ab_im = disc(lre, lim, ldt16_ref[...])
    nr, ni = ab_re - 1.0, ab_im
    den = lre * lre + lim * lim
    f_re = (nr * lre + ni * lim) / den
    f_im = (ni * lre - nr * lim) / den
    b_re, b_im = bre_ref[...], bim_ref[...]
    bbre_ref[...] = f_re * b_re - f_im * b_im
    bbim_ref[...] = f_re * b_im + f_im * b_re


def _s5_prep(lam_re, lam_im, log_dt, b_re, b_im):
    g, p = lam_re.shape
    ch = b_re.shape[-1]
    rep = lambda a: jnp.repeat(a, ch, axis=0)
    ldt = log_dt.reshape(g, 1)
    bt = lambda b: jnp.swapaxes(b, 1, 2).reshape(g * ch, p)
    args = (lam_re, lam_im, ldt, rep(lam_re), rep(lam_im), rep(ldt), bt(b_re), bt(b_im))
    return pl.pallas_call(
        _s5_prep_kernel,
        in_specs=[_full(a.shape) for a in args],
        out_specs=[_full((g, p)), _full((g, p)), _full((g * ch, p)), _full((g * ch, p))],
        out_shape=[jax.ShapeDtypeStruct((g, p), F32)] * 2 + [jax.ShapeDtypeStruct((g * ch, p), F32)] * 2,
        name="s5_prep")(*args)


def _block_diag(m, n_blk):
    r, c = m.shape[0] // n_blk, m.shape[1]
    eye = jnp.eye(n_blk, dtype=m.dtype)
    return (m.reshape(n_blk, r, 1, c) * eye[:, None, :, None]).reshape(n_blk * r, n_blk * c)


def _s5_kernel(x_ref, nw_ref, wu_ref, bre_ref, bim_ref, cre_ref, cim_ref, d_ref, gw_ref, gb_ref,
               are_ref, aim_ref, s0re_ref, s0im_ref,
               y_ref, sre_ref, sim_ref, bure_s, buim_s, stre_s, stim_s, *, tb, tl):
    @pl.when(pl.program_id(0) == 0)
    def _():
        stre_s[...] = s0re_ref[...]
        stim_s[...] = s0im_ref[...]

    d = nw_ref.shape[-1]
    width = wu_ref.shape[1]
    x = jnp.concatenate([x_ref[:, t, :] for t in range(tl)], axis=0)
    u = jnp.dot(_rmsnorm(x, nw_ref[...]).astype(BF16), wu_ref[...], preferred_element_type=F32)
    ub = u.astype(BF16)

    n_kb, kw, cw = bre_ref.shape
    for kb in range(n_kb):
        uk = ub[:, kb * kw:(kb + 1) * kw]
        bure_s[:, kb * cw:(kb + 1) * cw] = jnp.dot(uk, bre_ref[kb], preferred_element_type=F32)
        buim_s[:, kb * cw:(kb + 1) * cw] = jnp.dot(uk, bim_ref[kb], preferred_element_type=F32)

    n_state = are_ref.shape[1]
    sc = min(S5_SCAN_COLS, n_state)
    for c0 in range(0, n_state, sc):
        cs = slice(c0, c0 + sc)
        a_re = jnp.broadcast_to(are_ref[:, cs], (tb, sc))
        a_im = jnp.broadcast_to(aim_ref[:, cs], (tb, sc))

        def step(t, carry, cs=cs, a_re=a_re, a_im=a_im):
            s_re, s_im = carry
            rows = pl.ds(pl.multiple_of(t * tb, tb), tb)
            n_re = a_re * s_re - a_im * s_im + bure_s[rows, cs]
            n_im = a_re * s_im + a_im * s_re + buim_s[rows, cs]
            bure_s[rows, cs] = n_re
            buim_s[rows, cs] = n_im
            return n_re, n_im

        s_re, s_im = lax.fori_loop(0, tl, step, (stre_s[:, cs], stim_s[:, cs]), unroll=True)
        stre_s[:, cs] = s_re
        stim_s[:, cs] = s_im

    n_ob, ckw, ocw = cre_ref.shape
    ys = []
    for ob in range(n_ob):
        ks = slice(ob * ckw, (ob + 1) * ckw)
        ys.append(_dot(bure_s[:, ks], cre_ref[ob]) - _dot(buim_s[:, ks], cim_ref[ob]))
    y = jnp.concatenate(ys, axis=1) + d_ref[...] * u
    y = jax.nn.gelu(y)
    out = y * jax.nn.sigmoid(_dot(y, gw_ref[...]) + gb_ref[...])
    for t in range(tl):
        y_ref[:, t, :] = out[t * tb:(t + 1) * tb, :]
    sre_ref[...] = stre_s[...]
    sim_ref[...] = stim_s[...]


def _s5_mixer(x3, nw, w_in_bf, bmat_re, bmat_im, cmat_re, cmat_im, d_row, glu_w_bf, glu_b, a_re, a_im,
              s0_re, s0_im, tl):
    b, l, d = x3.shape
    width = d_row.shape[1]
    n_state = a_re.shape[1]
    rows = b * tl
    args = (x3, nw, w_in_bf, bmat_re, bmat_im, cmat_re, cmat_im, d_row, glu_w_bf, glu_b,
            a_re, a_im, s0_re, s0_im)
    in_specs = ([pl.BlockSpec((b, tl, d), lambda i: (0, i, 0)), _full(nw.shape),
                 pl.BlockSpec((d, width), lambda i: (0, 0))] + [_full(a.shape) for a in args[3:]])
    y, s_re, s_im = pl.pallas_call(
        functools.partial(_s5_kernel, tb=b, tl=tl),
        grid=(l // tl,), in_specs=in_specs,
        out_specs=[pl.BlockSpec((b, tl, width), lambda i: (0, i, 0)), _full((b, n_state)), _full((b, n_state))],
        out_shape=[jax.ShapeDtypeStruct((b, l, width), F32),
                   jax.ShapeDtypeStruct((b, n_state), F32), jax.ShapeDtypeStruct((b, n_state), F32)],
        scratch_shapes=[pltpu.VMEM((rows, n_state), F32), pltpu.VMEM((rows, n_state), F32),
                        pltpu.VMEM((b, n_state), F32), pltpu.VMEM((b, n_state), F32)],
        compiler_params=_cparams(1), name="s5_mixer")(*args)
    return y.reshape(b * l, width), s_re, s_im


def _rope_kernel(pos_ref, inv_ref, cc_ref, ss_ref):
    ang = pos_ref[...] * inv_ref[...]
    lane = lax.broadcasted_iota(jnp.int32, ang.shape, 1)
    cc_ref[...] = jnp.cos(ang)
    ss_ref[...] = jnp.where(lane < ang.shape[1] // 2, -jnp.sin(ang), jnp.sin(ang))


def _rope_tables(pos, dk):
    n = pos.shape[0]
    inv = 1.0 / (ROPE_BASE ** jnp.linspace(0.0, 1.0, dk // 2, dtype=F32))
    inv = jnp.concatenate([inv, inv]).reshape(1, dk)
    return pl.pallas_call(
        _rope_kernel,
        in_specs=[_full((n, 1)), _full((1, dk))],
        out_specs=[_full((n, dk)), _full((n, dk))],
        out_shape=[jax.ShapeDtypeStruct((n, dk), F32)] * 2,
        name="rope_tables")(pos.astype(F32).reshape(n, 1), inv)


def _rotate(t, cc, ss):
    return t * cc + pltpu.roll(t, t.shape[1] // 2, 1) * ss


def _ret_consts(rows, c):
    lg = jnp.log(1.0 - 2.0 ** (-5.0 - jnp.arange(RET_HEADS, dtype=F32)))
    r = jnp.arange(rows)
    idx, blk = r % c, r // c
    diff = idx[:, None] - idx[None, :]
    same = blk[:, None] == blk[None, :]
    decay = jnp.where((same & (diff >= 0))[None],
                      jnp.exp(jnp.maximum(diff, 0)[None].astype(F32) * lg[:, None, None]), 0.0)
    wide = lambda v: jnp.broadcast_to(v[:, :, None], (RET_HEADS, rows, LANES))
    q_dec = wide(jnp.exp((idx + 1).astype(F32)[None, :] * lg[:, None]))
    k_dec = wide(jnp.exp((c - 1 - idx).astype(F32)[None, :] * lg[:, None]))
    chunk_dec = jnp.broadcast_to(jnp.exp(c * lg)[:, None, None], (RET_HEADS, 8, LANES))
    return decay, q_dec, k_dec, chunk_dec


def _ret_gate(o, g):
    o = o * lax.rsqrt(jnp.mean(o * o, axis=-1, keepdims=True) + NORM_EPS)
    return jax.nn.silu(g) * o


def _ret_long_kernel(q_ref, k_ref, v_ref, g_ref, cc_ref, ss_ref, dec_ref, qd_ref, kd_ref, cd_ref, s0_ref,
                     y_ref, so_ref, *, c, hp):
    dk = q_ref.shape[1] // hp
    scale = dk ** -0.5

    def chunk(i, states):
        rows = pl.ds(pl.multiple_of(i * c, c), c)
        cc, ss = cc_ref[rows, :], ss_ref[rows, :]
        new_states = []
        for j in range(hp):
            cols = slice(j * dk, (j + 1) * dk)
            q = _rotate(q_ref[rows, cols], cc, ss)
            k = _rotate(k_ref[rows, cols], cc, ss) * scale
            v = v_ref[rows, cols]
            scores = _dot_nt(q, k) * dec_ref[j]
            o = _dot(scores, v) + _dot(q * qd_ref[j], states[j])
            new_states.append(cd_ref[j, 0:1, :] * states[j] + _dot((k * kd_ref[j]).T, v))
            y_ref[rows, cols] = _ret_gate(o, g_ref[rows, cols]).astype(y_ref.dtype)
        return tuple(new_states)

    states = lax.fori_loop(0, q_ref.shape[0] // c, chunk, tuple(s0_ref[0, j] for j in range(hp)), unroll=2)
    for j in range(hp):
        so_ref[0, j] = states[j]


def _ret_short_kernel(q_ref, k_ref, v_ref, g_ref, cc_ref, ss_ref, dec_ref, qd_ref, kd_ref, cd_ref, s0_ref,
                      y_ref, so_ref, qd_s, oi_s, *, l):
    rows, dk = q_ref.shape
    scale = dk ** -0.5
    cc, ss = cc_ref[...], ss_ref[...]
    q = _rotate(q_ref[...], cc, ss)
    k = _rotate(k_ref[...], cc, ss) * scale
    v = v_ref[...]
    vb = v.astype(BF16)
    o_intra = _dot(_dot_nt(q, k) * dec_ref[0], vb)
    qd_s[...] = q * qd_ref[0]
    kdt = (k * kd_ref[0]).T
    cd = cd_ref[0, 0:1, :]
    win = 16
    per = win // l
    lane = lax.broadcasted_iota(jnp.int32, (dk, rows), 1)
    wrow = lax.broadcasted_iota(jnp.int32, (win, dk), 0)

    def window(w, carry):
        wr = pl.ds(pl.multiple_of(w * win, win), win)
        qw = qd_s[wr, :].astype(BF16)
        oi = jnp.zeros((win, dk), F32)
        for j in range(per):
            b = w * per + j
            s = s0_ref[b, 0]
            oi = jnp.where(wrow // l == j, jnp.dot(qw, s.astype(BF16), preferred_element_type=F32), oi)
            so_ref[b, 0] = cd * s + jnp.dot(jnp.where(lane // l == b, kdt, 0.0).astype(BF16), vb,
                                            preferred_element_type=F32)
        oi_s[wr, :] = oi
        return carry

    lax.fori_loop(0, rows // win, window, 0, unroll=True)
    y_ref[...] = _ret_gate(o_intra + oi_s[...], g_ref[...]).astype(y_ref.dtype)


def _retention(z, col0, cc, ss, s0, n_seq, l):
    n = z.shape[0]
    dk = s0.shape[-1]
    if l % RET_CHUNK == 0:
        rows, c, n_blk, hp = l, RET_CHUNK, n_seq, RET_HEADS_PER_STEP
        kern = functools.partial(_ret_long_kernel, c=c, hp=hp)
        tspec = _full((l, dk))
        sspec = pl.BlockSpec((1, hp, dk, dk), lambda b, h: (b, h, 0, 0))
        consts = _ret_consts(c, c)
        scratch = []
    else:
        rows, c, hp = LANES, l, 1
        per_blk = rows // l
        n_blk = n_seq // per_blk
        kern = functools.partial(_ret_short_kernel, l=l)
        tspec = pl.BlockSpec((rows, dk), lambda b, h: (b, 0))
        sspec = pl.BlockSpec((per_blk, 1, dk, dk), lambda b, h: (b, h, 0, 0))
        consts = _ret_consts(rows, c)
        scratch = [pltpu.VMEM((rows, dk), F32), pltpu.VMEM((rows, dk), F32)]
    decay, q_dec, k_dec, chunk_dec = consts
    n_hb = RET_HEADS // hp
    cb0 = col0 // (hp * dk)
    zspec = lambda rows, field: pl.BlockSpec((rows, hp * dk), lambda b, h: (b, cb0 + field * n_hb + h))
    cspec = lambda shape: pl.BlockSpec((hp,) + shape, lambda b, h: (h, 0, 0))
    y, s_new = pl.pallas_call(
        kern, grid=(n_blk, n_hb),
        in_specs=[zspec(rows, 0), zspec(rows, 1), zspec(rows, 2), zspec(rows, 3), tspec, tspec,
                  cspec(decay.shape[1:]), cspec(q_dec.shape[1:]), cspec(k_dec.shape[1:]),
                  cspec(chunk_dec.shape[1:]), sspec],
        out_specs=[pl.BlockSpec((rows, hp * dk), lambda b, h: (b, h)), sspec],
        out_shape=[jax.ShapeDtypeStruct((n, RET_HEADS * dk), _mix_dtype(l)), jax.ShapeDtypeStruct(s0.shape, F32)],
        scratch_shapes=scratch,
        compiler_params=_cparams(2), name="retention")(z, z, z, z, cc, ss, decay, q_dec, k_dec, chunk_dec, s0)
    return y, s_new


def _cumsum_rows(x, period):
    row = lax.broadcasted_iota(jnp.int32, x.shape, 0)
    s = 1
    while s < period:
        x = x + jnp.where(row % period >= s, pltpu.roll(x, s, 0), 0.0)
        s *= 2
    return x


def _row_of_block(x, period, offset):
    n, w = x.shape
    x3 = x.reshape(n // period, period, w)
    return jnp.broadcast_to(x3[:, offset:offset + 1, :], x3.shape).reshape(n, w)


def _hgrn_gates(fl, lg, layer):
    e = jnp.exp(lg - jnp.max(lg, axis=0, keepdims=True))
    soft = e / jnp.sum(e, axis=0, keepdims=True)
    cum = soft[0:1]
    for i in range(1, layer + 1):
        cum = cum + soft[i:i + 1]
    lb = cum - soft[0:1]
    forget = lb + (1.0 - lb) * jax.nn.sigmoid(fl)
    return 1.0 - forget, jnp.log2(forget)


def _hgrn_out(o, g, nw):
    o = o * lax.rsqrt(jnp.mean(o * o, axis=-1, keepdims=True) + NORM_EPS) * nw
    return o * jax.nn.silu(g)


def _hgrn_long_kernel(q_ref, f_ref, v_ref, g_ref, lbl_ref, nw_ref, s0_ref, y_ref, so_ref, *, c, layer, hp):
    dk = q_ref.shape[1] // hp
    nw = nw_ref[...]
    nv = c // SUBLANES
    bv = HG_BLOCK // SUBLANES
    levels = []
    half = HG_BLOCK
    while half < c:
        levels.append(half)
        half *= 2
    row = lax.broadcasted_iota(jnp.int32, (c, c), 0)
    col = lax.broadcasted_iota(jnp.int32, (c, c), 1)
    lvl = jnp.where((row // HG_BLOCK == col // HG_BLOCK) & (col <= row), 0, -1)
    for li, half in enumerate(levels, 1):
        lvl = jnp.where((row // (2 * half) == col // (2 * half)) & (row % (2 * half) >= half)
                        & (col % (2 * half) < half), li, lvl)
    sub = lax.broadcasted_iota(jnp.int32, (SUBLANES, dk), 0)
    zeros = jnp.zeros((SUBLANES, dk), F32)
    cat = lambda slabs: jnp.concatenate(slabs, axis=0)

    def one_head(rows, cols, s):
        q, v = q_ref[rows, cols], v_ref[rows, cols]
        k, lf = _hgrn_gates(f_ref[rows, cols], lbl_ref[:, cols], layer)
        slabs = lambda a: [a[j * SUBLANES:(j + 1) * SUBLANES, :] for j in range(nv)]
        qs, ks = slabs(q), slabs(k)
        loc, run = [], [jnp.zeros((1, dk), F32)]
        for x in slabs(lf):
            step = 1
            while step < SUBLANES:
                x = x + jnp.where(sub >= step, pltpu.roll(x, step, 0), 0.0)
                step *= 2
            loc.append(x)
            run.append(run[-1] + x[SUBLANES - 1:, :])
        d0 = [loc[j] + (run[j] - run[j - j % bv]) if j % bv else loc[j] for j in range(nv)]
        scores = jnp.where(lvl == 0, _dot_nt(cat([qs[j] * jnp.exp2(d0[j]) for j in range(nv)]),
                                             cat([ks[j] * jnp.exp2(-d0[j]) for j in range(nv)])), 0.0)
        for li, half in enumerate(levels, 1):
            hv = half // SUBLANES
            q_side, k_side = [], []
            for j in range(nv):
                anchor = run[j - j % (2 * hv) + hv]
                if j % (2 * hv) >= hv:
                    q_side.append(qs[j] * jnp.exp2(loc[j] + (run[j] - anchor)))
                    k_side.append(zeros)
                else:
                    q_side.append(zeros)
                    k_side.append(ks[j] * jnp.exp2((anchor - run[j]) - loc[j]))
            scores = jnp.where(lvl == li, _dot_nt(cat(q_side), cat(k_side)), scores)
        o = _dot(scores, v) + _dot(cat([qs[j] * jnp.exp2(loc[j] + run[j]) for j in range(nv)]), s)
        k_hat = cat([ks[j] * jnp.exp2((run[nv] - run[j]) - loc[j]) for j in range(nv)])
        d_col = jnp.exp2(jnp.broadcast_to(run[nv], (dk, dk)).T)
        y_ref[rows, cols] = _hgrn_out(o, g_ref[rows, cols], nw).astype(y_ref.dtype)
        return d_col * s + _dot(k_hat.T, v)

    def chunk(i, states):
        rows = pl.ds(pl.multiple_of(i * c, c), c)
        return tuple(one_head(rows, slice(j * dk, (j + 1) * dk), states[j]) for j in range(hp))

    states = lax.fori_loop(0, q_ref.shape[0] // c, chunk, tuple(s0_ref[0, j] for j in range(hp)), unroll=2)
    for j in range(hp):
        so_ref[0, j] = states[j]


def _hgrn_short_kernel(q_ref, f_ref, v_ref, g_ref, lbl_ref, nw_ref, s0_ref, y_ref, so_ref, qd_s, oi_s,
                       *, l, layer):
    rows, dk = q_ref.shape
    q, v = q_ref[...], v_ref[...]
    vb = v.astype(BF16)
    k, lf = _hgrn_gates(f_ref[...], lbl_ref[...], layer)
    b = _cumsum_rows(lf, l)
    row = lax.broadcasted_iota(jnp.int32, (rows, rows), 0)
    col = lax.broadcasted_iota(jnp.int32, (rows, rows), 1)
    qe = q * jnp.exp2(b)
    scores = jnp.where((row // l == col // l) & (col <= row), _dot_nt(qe, k * jnp.exp2(-b)), 0.0)
    o_intra = _dot(scores, vb)
    qd_s[...] = qe
    b_last = _row_of_block(b, l, l - 1)
    kht = (k * jnp.exp2(b_last - b)).T
    dect = jnp.exp2(b_last).T
    win = 16
    per = win // l
    lane = lax.broadcasted_iota(jnp.int32, (dk, rows), 1)
    wrow = lax.broadcasted_iota(jnp.int32, (win, dk), 0)

    def window(w, carry):
        wr = pl.ds(pl.multiple_of(w * win, win), win)
        qw = qd_s[wr, :].astype(BF16)
        oi = jnp.zeros((win, dk), F32)
        for j in range(per):
            bi = w * per + j
            s = s0_ref[bi, 0]
            oi = jnp.where(wrow // l == j, jnp.dot(qw, s.astype(BF16), preferred_element_type=F32), oi)
            d_col = jnp.sum(jnp.where(lane == bi * l, dect, 0.0), axis=1, keepdims=True)
            so_ref[bi, 0] = d_col * s + jnp.dot(jnp.where(lane // l == bi, kht, 0.0).astype(BF16), vb,
                                                preferred_element_type=F32)
        oi_s[wr, :] = oi
        return carry

    lax.fori_loop(0, rows // win, window, 0, unroll=True)
    y_ref[...] = _hgrn_out(o_intra + oi_s[...], g_ref[...], nw_ref[...]).astype(y_ref.dtype)


def _hgrn(z, lb_logits, norm_w, s0, n_seq, l, layer):
    n = z.shape[0]
    dk = s0.shape[-1]
    depth = lb_logits.shape[0]
    if l % HG_CHUNK == 0:
        rows, n_blk, hp = l, n_seq, HG_HEADS_PER_STEP
        kern = functools.partial(_hgrn_long_kernel, c=HG_CHUNK, layer=layer, hp=hp)
        sspec = pl.BlockSpec((1, hp, dk, dk), lambda b, h: (b, h, 0, 0))
        scratch = []
    else:
        assert HG_BLOCK % l == 0
        rows, hp = LANES, 1
        per_blk = rows // l
        n_blk = n_seq // per_blk
        kern = functools.partial(_hgrn_short_kernel, l=l, layer=layer)
        sspec = pl.BlockSpec((per_blk, 1, dk, dk), lambda b, h: (b, h, 0, 0))
        scratch = [pltpu.VMEM((rows, dk), F32), pltpu.VMEM((rows, dk), F32)]
    n_hb = HG_HEADS // hp
    zspec = lambda rows, field: pl.BlockSpec((rows, hp * dk), lambda b, h: (b, field * n_hb + h))
    y, s_new = pl.pallas_call(
        kern, grid=(n_blk, n_hb),
        in_specs=[zspec(rows, 0), zspec(rows, 1), zspec(rows, 2), zspec(rows, 3),
                  pl.BlockSpec((depth, hp * dk), lambda b, h: (0, h)), _full((1, dk)), sspec],
        out_specs=[pl.BlockSpec((rows, hp * dk), lambda b, h: (b, h)), sspec],
        out_shape=[jax.ShapeDtypeStruct((n, HG_HEADS * dk), _mix_dtype(l)), jax.ShapeDtypeStruct(s0.shape, F32)],
        scratch_shapes=scratch,
        compiler_params=_cparams(2), name="hgrn")(z, z, z, z, lb_logits, norm_w, s0)
    return y, s_new


def _ffn_kernel(*refs, n_parts, long_mode, final_norm, l):
    x_ref, parts = refs[0], refs[1:1 + n_parts]
    (wo_ref, nw_ref, wg_ref, wu_ref, cw_ref, cb_ref, wd_ref, fnw_ref, buf_ref, o_ref, bufo_ref,
     act_s) = refs[1 + n_parts:13 + n_parts]
    if long_mode:
        tail_s = refs[13 + n_parts]
        load = lambda ref: ref[0]
    else:
        nb = x_ref.shape[0]
        load = lambda ref: jnp.concatenate([ref[:, t, :] for t in range(l)], axis=0)
    x = load(x_ref)
    tm = x.shape[0]
    d_ff = wg_ref.shape[1]
    mix = jnp.concatenate([load(p).astype(BF16) for p in parts], axis=1)
    x = x + jnp.dot(mix, wo_ref[...], preferred_element_type=F32)
    h = _rmsnorm(x, nw_ref[...]).astype(BF16)
    if long_mode:
        @pl.when(pl.program_id(1) == 0)
        def _():
            tail_s[...] = buf_ref[0]

    for c0 in range(0, d_ff, FF_CHUNK):
        cs = slice(c0, c0 + FF_CHUNK)
        g = jnp.dot(h, wg_ref[:, cs], preferred_element_type=F32)
        up = jnp.dot(h, wu_ref[:, cs], preferred_element_type=F32)
        if long_mode:
            row = lax.broadcasted_iota(jnp.int32, g.shape, 0)
            p1 = jnp.where(row < 1, tail_s[1:2, cs], pltpu.roll(g, 1, 0))
            p2 = jnp.where(row < 2, jnp.where(row == 0, tail_s[0:1, cs], tail_s[1:2, cs]), pltpu.roll(g, 2, 0))
            tail_s[:, cs] = g[tm - (CONV_W - 1):, :]
        else:
            b0, b1 = buf_ref[:, 0, cs], buf_ref[:, 1, cs]
            p1 = jnp.concatenate([b1, g[:tm - nb]], axis=0)
            p2 = jnp.concatenate([b0, b1, g[:tm - 2 * nb]], axis=0)
            bufo_ref[:, 0, cs] = g[tm - 2 * nb:tm - nb]
            bufo_ref[:, 1, cs] = g[tm - nb:]
        conv = cb_ref[:, cs] + cw_ref[0:1, cs] * p2
        conv = conv + cw_ref[1:2, cs] * p1
        conv = conv + cw_ref[2:3, cs] * g
        act_s[:, cs] = (jax.nn.silu(conv) * up).astype(BF16)
    y = x + jnp.dot(act_s[...], wd_ref[...], preferred_element_type=F32)
    if final_norm:
        y = _rmsnorm(y, fnw_ref[...])
    if long_mode:
        o_ref[0] = y
        bufo_ref[0] = tail_s[...]
    else:
        for t in range(l):
            o_ref[:, t, :] = y[t * nb:(t + 1) * nb]


def _ffn(x, parts, wo, nw, wg, wu, conv_w, conv_b, wd, fnw, buf, n_seq, l, final_norm):
    assert CONV_W == 3 and l >= CONV_W - 1
    n, d = x.shape
    d_ff = wg.shape[1]
    weights = (wo, nw, wg, wu, conv_w, conv_b, wd, fnw)
    wspecs = [_full(w.shape) for w in weights]
    kern = functools.partial(_ffn_kernel, n_parts=len(parts), final_norm=final_norm, l=l)
    buf_shape = jax.ShapeDtypeStruct((n_seq, CONV_W - 1, d_ff), F32)
    rows3 = lambda a: a.reshape(n_seq, l, a.shape[1])
    if l % ROW_TILE == 0:
        tm = FFN_ROW_TILE
        assert l % tm == 0
        rspec = lambda w: pl.BlockSpec((1, tm, w), lambda b, i: (b, i, 0))
        bspec = pl.BlockSpec((1, CONV_W - 1, d_ff), lambda b, i: (b, 0, 0))
        y, buf_new = pl.pallas_call(
            functools.partial(kern, long_mode=True),
            grid=(n_seq, l // tm),
            in_specs=([rspec(d)] + [rspec(p.shape[1]) for p in parts]
                      + [_resident(w.shape) for w in weights] + [bspec]),
            out_specs=[rspec(d), bspec],
            out_shape=[jax.ShapeDtypeStruct((n_seq, l, d), F32), buf_shape],
            scratch_shapes=[pltpu.VMEM((tm, d_ff), BF16), pltpu.VMEM((CONV_W - 1, d_ff), F32)],
            compiler_params=_cparams(2), name="ffn")(rows3(x), *map(rows3, parts), *weights, buf)
        return y.reshape(n, d), buf_new
    y, buf_new = pl.pallas_call(
        functools.partial(kern, long_mode=False),
        in_specs=([_full((n_seq, l, d))] + [_full((n_seq, l, p.shape[1])) for p in parts] + wspecs
                  + [_full(buf.shape)]),
        out_specs=[_full((n_seq, l, d)), _full(buf.shape)],
        out_shape=[jax.ShapeDtypeStruct((n_seq, l, d), F32), buf_shape],
        scratch_shapes=[pltpu.VMEM((n, d_ff), BF16)],
        compiler_params=pltpu.CompilerParams(vmem_limit_bytes=VMEM_LIMIT_BYTES), name="ffn")(
            rows3(x), *map(rows3, parts), *weights, buf)
    return y.reshape(n, d), buf_new


def _trunk(x3, pos, s5_re, s5_im, ret, hg, conv, p, s5_tl):
    b, l, d = x3.shape
    n = b * l
    x = x3.reshape(n, d)
    depth = p['norm_mix'].shape[0]
    new_re, new_im, new_ret, new_hg, new_conv = [], [], [], [], []
    for layer in range(depth):
        j = layer // 2
        nw = p['norm_mix'][layer].reshape(1, d)
        if layer % 2 == 0:
            s5w = p['s5'][j]
            width = s5w['d'].shape[1]
            y_s5, r, im = _s5_mixer(x.reshape(b, l, d), nw, p['w_in_ab'][j], s5w['bmat_re'], s5w['bmat_im'],
                                    s5w['cmat_re'], s5w['cmat_im'], s5w['d'], s5w['glu_w'], s5w['glu_b'],
                                    s5w['a_re'], s5w['a_im'], s5_re[j].reshape(b, -1), s5_im[j].reshape(b, -1),
                                    s5_tl)
            z = _proj(x, nw, p['w_in_ab'][j], col0=width)
            cc, ss = _rope_tables(pos, ret.shape[-1])
            y_ret, st = _retention(z, 0, cc, ss, ret[j], b, l)
            parts, w_out = [y_s5, y_ret], p['w_out_ab'][j]
            new_re.append(r.reshape(s5_re.shape[1:]))
            new_im.append(im.reshape(s5_im.shape[1:]))
            new_ret.append(st)
        else:
            z = _proj(x, nw, p['w_in_c'][j])
            y_hg, st = _hgrn(z, p['hg_lb_logits'], p['hg_norm_w'][j].reshape(1, -1), hg[j], b, l, layer)
            parts, w_out = [y_hg], p['w_out_c'][j]
            new_hg.append(st)
        x, buf = _ffn(x, parts, w_out, p['norm_ffn'][layer].reshape(1, d), p['ffn_w_gate'][layer],
                      p['ffn_w_up'][layer], p['ffn_conv_w'][layer], p['ffn_conv_b'][layer].reshape(1, -1),
                      p['ffn_w_down'][layer], p['norm_final'].reshape(1, d), conv[layer], b, l,
                      final_norm=(layer == depth - 1))
        new_conv.append(buf)
    return (x.reshape(b, l, d), jnp.stack(new_re), jnp.stack(new_im), jnp.stack(new_ret),
            jnp.stack(new_hg), jnp.stack(new_conv))


def kernel(x_prompt, x_sample, state_s5_re, state_s5_im, state_ret, state_hgrn, state_ffn_conv, pos_sample, norm_mix, norm_ffn, norm_final, w_in_ab, s5_lam_re, s5_lam_im, s5_log_dt, s5_b_re, s5_b_im, s5_c_re, s5_c_im, s5_d, s5_glu_w, s5_glu_b, w_out_ab, w_in_c, hg_lb_logits, hg_norm_w, w_out_c, ffn_w_gate, ffn_w_up, ffn_conv_w, ffn_conv_b, ffn_w_down):
    n_ab, n_grp, n_st = s5_lam_re.shape
    ch = s5_b_re.shape[-1]
    width = n_grp * ch
    grp_per_blk = MXU_DIM // ch
    n_blk = n_grp // grp_per_blk
    s5 = []
    for j in range(n_ab):
        a_re, a_im, bb_re, bb_im = _s5_prep(s5_lam_re[j], s5_lam_im[j], s5_log_dt[j], s5_b_re[j], s5_b_im[j])
        bmat = lambda m: jnp.stack([_block_diag(blk, grp_per_blk)
                                    for blk in m.reshape(n_blk, grp_per_blk * ch, n_st)]).astype(BF16)
        cmat = lambda c: jnp.stack([_block_diag(blk, grp_per_blk).T for blk in
                                    c.reshape(n_blk, grp_per_blk * ch, n_st)]).astype(BF16)
        s5.append(dict(bmat_re=bmat(bb_re), bmat_im=bmat(bb_im),
                       cmat_re=cmat(s5_c_re[j]), cmat_im=cmat(s5_c_im[j]),
                       d=s5_d[j].reshape(1, width), glu_w=s5_glu_w[j].astype(BF16),
                       glu_b=s5_glu_b[j].reshape(1, width),
                       a_re=a_re.reshape(1, n_grp * n_st), a_im=a_im.reshape(1, n_grp * n_st)))
    per_layer_bf16 = lambda w: [w[i].astype(BF16) for i in range(w.shape[0])]
    p = dict(norm_mix=norm_mix, norm_ffn=norm_ffn, norm_final=norm_final, s5=s5,
             w_in_ab=per_layer_bf16(w_in_ab), w_out_ab=per_layer_bf16(w_out_ab),
             w_in_c=per_layer_bf16(w_in_c), hg_lb_logits=hg_lb_logits, hg_norm_w=hg_norm_w,
             w_out_c=per_layer_bf16(w_out_c), ffn_w_gate=per_layer_bf16(ffn_w_gate),
             ffn_w_up=per_layer_bf16(ffn_w_up), ffn_conv_w=ffn_conv_w, ffn_conv_b=ffn_conv_b,
             ffn_w_down=per_layer_bf16(ffn_w_down))

    bp, lp, _ = x_prompt.shape
    z_s5 = jnp.zeros((n_ab, bp) + state_s5_re.shape[2:], F32)
    z_ret = jnp.zeros((n_ab, bp) + state_ret.shape[2:], F32)
    z_hg = jnp.zeros((state_hgrn.shape[0], bp) + state_hgrn.shape[2:], F32)
    z_conv = jnp.zeros((norm_mix.shape[0], bp) + state_ffn_conv.shape[2:], F32)
    outs_p = _trunk(x_prompt, jnp.arange(lp, dtype=jnp.int32), z_s5, z_s5, z_ret, z_hg, z_conv, p,
                    s5_tl=ROW_TILE // bp)
    bs, ls, _ = x_sample.shape
    pos_s = (pos_sample[:, None] + jnp.arange(ls, dtype=jnp.int32)[None, :]).reshape(-1)
    outs_s = _trunk(x_sample, pos_s, state_s5_re, state_s5_im, state_ret, state_hgrn, state_ffn_conv, p,
                    s5_tl=ls)
    return (outs_p[0], outs_s[0]) + outs_p[1:] + outs_s[1:]
```

```python
import functools

import jax
import jax.numpy as jnp
from jax import lax
from jax.experimental import pallas as pl
from jax.experimental.pallas import tpu as pltpu

F32 = jnp.float32
BF16 = jnp.bfloat16

NORM_EPS = 1e-6
ROPE_BASE = 10000.0
S5_GROUP_CH = 16
RET_HEADS = 4
RET_CHUNK = 128
HG_HEADS = 8
HG_BLOCK = 16
HG_CHUNK = 128
RET_HEADS_PER_STEP = 4
HG_HEADS_PER_STEP = 4
CONV_W = 3

LANES = 128
SUBLANES = 8
MXU_DIM = 256
VMEM_LIMIT_BYTES = 56 * 1024 * 1024
ROW_TILE = 512
FFN_ROW_TILE = 1024
PROJ_ROW_TILE = 1024
FF_CHUNK = 256
S5_SCAN_COLS = 512


def _cparams(n_grid_dims):
    return pltpu.CompilerParams(dimension_semantics=("arbitrary",) * n_grid_dims,
                                vmem_limit_bytes=VMEM_LIMIT_BYTES)


def _dot(a, b):
    return jnp.dot(a.astype(BF16), b.astype(BF16), preferred_element_type=F32)


def _dot_nt(a, b):
    return lax.dot_general(a.astype(BF16), b.astype(BF16), (((1,), (1,)), ((), ())),
                           preferred_element_type=F32)


def _rmsnorm(x, w):
    return x * lax.rsqrt(jnp.mean(x * x, axis=-1, keepdims=True) + NORM_EPS) * w


def _full(shape):
    nd = len(shape)
    return pl.BlockSpec(shape, lambda *_: (0,) * nd)


def _mix_dtype(l):
    return BF16 if l % ROW_TILE == 0 else F32


def _resident(shape):
    nd = len(shape)
    return pl.BlockSpec(shape, lambda *_: (0,) * nd, pipeline_mode=pl.Buffered(1))


def _proj_kernel(x_ref, nw_ref, w_ref, o_ref, *, col0, col_chunk, row_chunk):
    for r in range(0, x_ref.shape[0], row_chunk):
        rows = slice(r, r + row_chunk)
        h = _rmsnorm(x_ref[rows, :], nw_ref[...]).astype(BF16)
        for c in range(0, o_ref.shape[1], col_chunk):
            o_ref[rows, c:c + col_chunk] = jnp.dot(h, w_ref[:, col0 + c:col0 + c + col_chunk],
                                                   preferred_element_type=F32)


def _proj(x, nw, w_bf, col0=0):
    n, d = x.shape
    n_out = w_bf.shape[1] - col0
    tm = min(PROJ_ROW_TILE, n)
    return pl.pallas_call(
        functools.partial(_proj_kernel, col0=col0, col_chunk=2 * MXU_DIM, row_chunk=ROW_TILE),
        grid=(n // tm,),
        in_specs=[pl.BlockSpec((tm, d), lambda i: (i, 0)), _full((1, d)), _resident(w_bf.shape)],
        out_specs=pl.BlockSpec((tm, n_out), lambda i: (i, 0)),
        out_shape=jax.ShapeDtypeStruct((n, n_out), F32),
        compiler_params=_cparams(1), name="proj")(x, nw, w_bf)


def _s5_prep_kernel(lre_ref, lim_ref, ldt_ref, lre16_ref, lim16_ref, ldt16_ref, bre_ref, bim_ref,
                    are_ref, aim_ref, bbre_ref, bbim_ref):
    def disc(lre, lim, ldt):
        dt = jnp.exp(ldt)
        mag = jnp.exp(lre * dt)
        ang = lim * dt
        return mag * jnp.cos(ang), mag * jnp.sin(ang)

    ab_re, ab_im = disc(lre_ref[...], lim_ref[...], ldt_ref[...])
    are_ref[...] = ab_re
    aim_ref[...] = ab_im
    lre, lim = lre16_ref[...], lim16_ref[...]
    ab_re, ab_im = disc(lre, lim, ldt16_ref[...])
    nr, ni = ab_re - 1.0, ab_im
    den = lre * lre + lim * lim
    f_re = (nr * lre + ni * lim) / den
    f_im = (ni * lre - nr * lim) / den
    b_re, b_im = bre_ref[...], bim_ref[...]
    bbre_ref[...] = f_re * b_re - f_im * b_im
    bbim_ref[...] = f_re * b_im + f_im * b_re


def _s5_prep(lam_re, lam_im, log_dt, b_re, b_im):
    g, p = lam_re.shape
    ch = b_re.shape[-1]
    rep = lambda a: jnp.repeat(a, ch, axis=0)
    ldt = log_dt.reshape(g, 1)
    bt = lambda b: jnp.swapaxes(b, 1, 2).reshape(g * ch, p)
    args = (lam_re, lam_im, ldt, rep(lam_re), rep(lam_im), rep(ldt), bt(b_re), bt(b_im))
    return pl.pallas_call(
        _s5_prep_kernel,
        in_specs=[_full(a.shape) for a in args],
        out_specs=[_full((g, p)), _full((g, p)), _full((g * ch, p)), _full((g * ch, p))],
        out_shape=[jax.ShapeDtypeStruct((g, p), F32)] * 2 + [jax.ShapeDtypeStruct((g * ch, p), F32)] * 2,
        name="s5_prep")(*args)


def _block_diag(m, n_blk):
    r, c = m.shape[0] // n_blk, m.shape[1]
    eye = jnp.eye(n_blk, dtype=m.dtype)
    return (m.reshape(n_blk, r, 1, c) * eye[:, None, :, None]).reshape(n_blk * r, n_blk * c)


def _s5_kernel(x_ref, nw_ref, wu_ref, bre_ref, bim_ref, cre_ref, cim_ref, d_ref, gw_ref, gb_ref,
               are_ref, aim_ref, s0re_ref, s0im_ref,
               y_ref, sre_ref, sim_ref, bure_s, buim_s, stre_s, stim_s, *, tb, tl):
    @pl.when(pl.program_id(0) == 0)
    def _():
        stre_s[...] = s0re_ref[...]
        stim_s[...] = s0im_ref[...]

    d = nw_ref.shape[-1]
    width = wu_ref.shape[1]
    x = jnp.concatenate([x_ref[:, t, :] for t in range(tl)], axis=0)
    u = jnp.dot(_rmsnorm(x, nw_ref[...]).astype(BF16), wu_ref[...], preferred_element_type=F32)
    ub = u.astype(BF16)

    n_kb, kw, cw = bre_ref.shape
    for kb in range(n_kb):
        uk = ub[:, kb * kw:(kb + 1) * kw]
        bure_s[:, kb * cw:(kb + 1) * cw] = jnp.dot(uk, bre_ref[kb], preferred_element_type=F32)
        buim_s[:, kb * cw:(kb + 1) * cw] = jnp.dot(uk, bim_ref[kb], preferred_element_type=F32)

    n_state = are_ref.shape[1]
    sc = min(S5_SCAN_COLS, n_state)
    for c0 in range(0, n_state, sc):
        cs = slice(c0, c0 + sc)
        a_re = jnp.broadcast_to(are_ref[:, cs], (tb, sc))
        a_im = jnp.broadcast_to(aim_ref[:, cs], (tb, sc))

        def step(t, carry, cs=cs, a_re=a_re, a_im=a_im):
            s_re, s_im = carry
            rows = pl.ds(pl.multiple_of(t * tb, tb), tb)
            n_re = a_re * s_re - a_im * s_im + bure_s[rows, cs]
            n_im = a_re * s_im + a_im * s_re + buim_s[rows, cs]
            bure_s[rows, cs] = n_re
            buim_s[rows, cs] = n_im
            return n_re, n_im

        s_re, s_im = lax.fori_loop(0, tl, step, (stre_s[:, cs], stim_s[:, cs]), unroll=True)
        stre_s[:, cs] = s_re
        stim_s[:, cs] = s_im

    n_ob, ckw, ocw = cre_ref.shape
    ys = []
    for ob in range(n_ob):
        ks = slice(ob * ckw, (ob + 1) * ckw)
        ys.append(_dot(bure_s[:, ks], cre_ref[ob]) - _dot(buim_s[:, ks], cim_ref[ob]))
    y = jnp.concatenate(ys, axis=1) + d_ref[...] * u
    y = jax.nn.gelu(y)
    out = y * jax.nn.sigmoid(_dot(y, gw_ref[...]) + gb_ref[...])
    for t in range(tl):
        y_ref[:, t, :] = out[t * tb:(t + 1) * tb, :]
    sre_ref[...] = stre_s[...]
    sim_ref[...] = stim_s[...]


def _s5_mixer(x3, nw, w_in_bf, bmat_re, bmat_im, cmat_re, cmat_im, d_row, glu_w_bf, glu_b, a_re, a_im,
              s0_re, s0_im, tl):
    b, l, d = x3.shape
    width = d_row.shape[1]
    n_state = a_re.shape[1]
    rows = b * tl
    args = (x3, nw, w_in_bf, bmat_re, bmat_im, cmat_re, cmat_im, d_row, glu_w_bf, glu_b,
            a_re, a_im, s0_re, s0_im)
    in_specs = ([pl.BlockSpec((b, tl, d), lambda i: (0, i, 0)), _full(nw.shape),
                 pl.BlockSpec((d, width), lambda i: (0, 0))] + [_full(a.shape) for a in args[3:]])
    y, s_re, s_im = pl.pallas_call(
        functools.partial(_s5_kernel, tb=b, tl=tl),
        grid=(l // tl,), in_specs=in_specs,
        out_specs=[pl.BlockSpec((b, tl, width), lambda i: (0, i, 0)), _full((b, n_state)), _full((b, n_state))],
        out_shape=[jax.ShapeDtypeStruct((b, l, width), F32),
                   jax.ShapeDtypeStruct((b, n_state), F32), jax.ShapeDtypeStruct((b, n_state), F32)],
        scratch_shapes=[pltpu.VMEM((rows, n_state), F32), pltpu.VMEM((rows, n_state), F32),
                        pltpu.VMEM((b, n_state), F32), pltpu.VMEM((b, n_state), F32)],
        compiler_params=_cparams(1), name="s5_mixer")(*args)
    return y.reshape(b * l, width), s_re, s_im


def _rope_kernel(pos_ref, inv_ref, cc_ref, ss_ref):
    ang = pos_ref[...] * inv_ref[...]
    lane = lax.broadcasted_iota(jnp.int32, ang.shape, 1)
    cc_ref[...] = jnp.cos(ang)
    ss_ref[...] = jnp.where(lane < ang.shape[1] // 2, -jnp.sin(ang), jnp.sin(ang))


def _rope_tables(pos, dk):
    n = pos.shape[0]
    inv = 1.0 / (ROPE_BASE ** jnp.linspace(0.0, 1.0, dk // 2, dtype=F32))
    inv = jnp.concatenate([inv, inv]).reshape(1, dk)
    return pl.pallas_call(
        _rope_kernel,
        in_specs=[_full((n, 1)), _full((1, dk))],
        out_specs=[_full((n, dk)), _full((n, dk))],
        out_shape=[jax.ShapeDtypeStruct((n, dk), F32)] * 2,
        name="rope_tables")(pos.astype(F32).reshape(n, 1), inv)


def _rotate(t, cc, ss):
    return t * cc + pltpu.roll(t, t.shape[1] // 2, 1) * ss


def _ret_consts(rows, c):
    lg = jnp.log(1.0 - 2.0 ** (-5.0 - jnp.arange(RET_HEADS, dtype=F32)))
    r = jnp.arange(rows)
    idx, blk = r % c, r // c
    diff = idx[:, None] - idx[None, :]
    same = blk[:, None] == blk[None, :]
    decay = jnp.where((same & (diff >= 0))[None],
                      jnp.exp(jnp.maximum(diff, 0)[None].astype(F32) * lg[:, None, None]), 0.0)
    wide = lambda v: jnp.broadcast_to(v[:, :, None], (RET_HEADS, rows, LANES))
    q_dec = wide(jnp.exp((idx + 1).astype(F32)[None, :] * lg[:, None]))
    k_dec = wide(jnp.exp((c - 1 - idx).astype(F32)[None, :] * lg[:, None]))
    chunk_dec = jnp.broadcast_to(jnp.exp(c * lg)[:, None, None], (RET_HEADS, 8, LANES))
    return decay, q_dec, k_dec, chunk_dec


def _ret_gate(o, g):
    o = o * lax.rsqrt(jnp.mean(o * o, axis=-1, keepdims=True) + NORM_EPS)
    return jax.nn.silu(g) * o


def _ret_long_kernel(q_ref, k_ref, v_ref, g_ref, cc_ref, ss_ref, dec_ref, qd_ref, kd_ref, cd_ref, s0_ref,
                     y_ref, so_ref, *, c, hp):
    dk = q_ref.shape[1] // hp
    scale = dk ** -0.5

    def chunk(i, states):
        rows = pl.ds(pl.multiple_of(i * c, c), c)
        cc, ss = cc_ref[rows, :], ss_ref[rows, :]
        new_states = []
        for j in range(hp):
            cols = slice(j * dk, (j + 1) * dk)
            q = _rotate(q_ref[rows, cols], cc, ss)
            k = _rotate(k_ref[rows, cols], cc, ss) * scale
            v = v_ref[rows, cols]
            scores = _dot_nt(q, k) * dec_ref[j]
            o = _dot(scores, v) + _dot(q * qd_ref[j], states[j])
            new_states.append(cd_ref[j, 0:1, :] * states[j] + _dot((k * kd_ref[j]).T, v))
            y_ref[rows, cols] = _ret_gate(o, g_ref[rows, cols]).astype(y_ref.dtype)
        return tuple(new_states)

    states = lax.fori_loop(0, q_ref.shape[0] // c, chunk, tuple(s0_ref[0, j] for j in range(hp)), unroll=2)
    for j in range(hp):
        so_ref[0, j] = states[j]


def _ret_short_kernel(q_ref, k_ref, v_ref, g_ref, cc_ref, ss_ref, dec_ref, qd_ref, kd_ref, cd_ref, s0_ref,
                      y_ref, so_ref, qd_s, oi_s, *, l):
    rows, dk = q_ref.shape
    scale = dk ** -0.5
    cc, ss = cc_ref[...], ss_ref[...]
    q = _rotate(q_ref[...], cc, ss)
    k = _rotate(k_ref[...], cc, ss) * scale
    v = v_ref[...]
    vb = v.astype(BF16)
    o_intra = _dot(_dot_nt(q, k) * dec_ref[0], vb)
    qd_s[...] = q * qd_ref[0]
    kdt = (k * kd_ref[0]).T
    cd = cd_ref[0, 0:1, :]
    win = 16
    per = win // l
    lane = lax.broadcasted_iota(jnp.int32, (dk, rows), 1)
    wrow = lax.broadcasted_iota(jnp.int32, (win, dk), 0)

    def window(w, carry):
        wr = pl.ds(pl.multiple_of(w * win, win), win)
        qw = qd_s[wr, :].astype(BF16)
        oi = jnp.zeros((win, dk), F32)
        for j in range(per):
            b = w * per + j
            s = s0_ref[b, 0]
            oi = jnp.where(wrow // l == j, jnp.dot(qw, s.astype(BF16), preferred_element_type=F32), oi)
            so_ref[b, 0] = cd * s + jnp.dot(jnp.where(lane // l == b, kdt, 0.0).astype(BF16), vb,
                                            preferred_element_type=F32)
        oi_s[wr, :] = oi
        return carry

    lax.fori_loop(0, rows // win, window, 0, unroll=True)
    y_ref[...] = _ret_gate(o_intra + oi_s[...], g_ref[...]).astype(y_ref.dtype)


def _retention(z, col0, cc, ss, s0, n_seq, l):
    n = z.shape[0]
    dk = s0.shape[-1]
    if l % RET_CHUNK == 0:
        rows, c, n_blk, hp = l, RET_CHUNK, n_seq, RET_HEADS_PER_STEP
        kern = functools.partial(_ret_long_kernel, c=c, hp=hp)
        tspec = _full((l, dk))
        sspec = pl.BlockSpec((1, hp, dk, dk), lambda b, h: (b, h, 0, 0))
        consts = _ret_consts(c, c)
        scratch = []
    else:
        rows, c, hp = LANES, l, 1
        per_blk = rows // l
        n_blk = n_seq // per_blk
        kern = functools.partial(_ret_short_kernel, l=l)
        tspec = pl.BlockSpec((rows, dk), lambda b, h: (b, 0))
        sspec = pl.BlockSpec((per_blk, 1, dk, dk), lambda b, h: (b, h, 0, 0))
        consts = _ret_consts(rows, c)
        scratch = [pltpu.VMEM((rows, dk), F32), pltpu.VMEM((rows, dk), F32)]
    decay, q_dec, k_dec, chunk_dec = consts
    n_hb = RET_HEADS // hp
    cb0 = col0 // (hp * dk)
    zspec = lambda rows, field: pl.BlockSpec((rows, hp * dk), lambda b, h: (b, cb0 + field * n_hb + h))
    cspec = lambda shape: pl.BlockSpec((hp,) + shape, lambda b, h: (h, 0, 0))
    y, s_new = pl.pallas_call(
        kern, grid=(n_blk, n_hb),
        in_specs=[zspec(rows, 0), zspec(rows, 1), zspec(rows, 2), zspec(rows, 3), tspec, tspec,
                  cspec(decay.shape[1:]), cspec(q_dec.shape[1:]), cspec(k_dec.shape[1:]),
                  cspec(chunk_dec.shape[1:]), sspec],
        out_specs=[pl.BlockSpec((rows, hp * dk), lambda b, h: (b, h)), sspec],
        out_shape=[jax.ShapeDtypeStruct((n, RET_HEADS * dk), _mix_dtype(l)), jax.ShapeDtypeStruct(s0.shape, F32)],
        scratch_shapes=scratch,
        compiler_params=_cparams(2), name="retention")(z, z, z, z, cc, ss, decay, q_dec, k_dec, chunk_dec, s0)
    return y, s_new


def _cumsum_rows(x, period):
    row = lax.broadcasted_iota(jnp.int32, x.shape, 0)
    s = 1
    while s < period:
        x = x + jnp.where(row % period >= s, pltpu.roll(x, s, 0), 0.0)
        s *= 2
    return x


def _row_of_block(x, period, offset):
    n, w = x.shape
    x3 = x.reshape(n // period, period, w)
    return jnp.broadcast_to(x3[:, offset:offset + 1, :], x3.shape).reshape(n, w)


def _hgrn_gates(fl, lg, layer):
    e = jnp.exp(lg - jnp.max(lg, axis=0, keepdims=True))
    soft = e / jnp.sum(e, axis=0, keepdims=True)
    cum = soft[0:1]
    for i in range(1, layer + 1):
        cum = cum + soft[i:i + 1]
    lb = cum - soft[0:1]
    forget = lb + (1.0 - lb) * jax.nn.sigmoid(fl)
    return 1.0 - forget, jnp.log2(forget)


def _hgrn_out(o, g, nw):
    o = o * lax.rsqrt(jnp.mean(o * o, axis=-1, keepdims=True) + NORM_EPS) * nw
    return o * jax.nn.silu(g)


def _hgrn_long_kernel(q_ref, f_ref, v_ref, g_ref, lbl_ref, nw_ref, s0_ref, y_ref, so_ref, *, c, layer, hp):
    dk = q_ref.shape[1] // hp
    nw = nw_ref[...]
    nv = c // SUBLANES
    bv = HG_BLOCK // SUBLANES
    levels = []
    half = HG_BLOCK
    while half < c:
        levels.append(half)
        half *= 2
    row = lax.broadcasted_iota(jnp.int32, (c, c), 0)
    col = lax.broadcasted_iota(jnp.int32, (c, c), 1)
    lvl = jnp.where((row // HG_BLOCK == col // HG_BLOCK) & (col <= row), 0, -1)
    for li, half in enumerate(levels, 1):
        lvl = jnp.where((row // (2 * half) == col // (2 * half)) & (row % (2 * half) >= half)
                        & (col % (2 * half) < half), li, lvl)
    sub = lax.broadcasted_iota(jnp.int32, (SUBLANES, dk), 0)
    zeros = jnp.zeros((SUBLANES, dk), F32)
    cat = lambda slabs: jnp.concatenate(slabs, axis=0)

    def one_head(rows, cols, s, in_level):
        q, v = q_ref[rows, cols], v_ref[rows, cols]
        k, lf = _hgrn_gates(f_ref[rows, cols], lbl_ref[:, cols], layer)
        slabs = lambda a: [a[j * SUBLANES:(j + 1) * SUBLANES, :] for j in range(nv)]
        qs, ks = slabs(q), slabs(k)
        loc, run = [], [jnp.zeros((1, dk), F32)]
        for x in slabs(lf):
            step = 1
            while step < SUBLANES:
                x = x + jnp.where(sub >= step, pltpu.roll(x, step, 0), 0.0)
                step *= 2
            loc.append(x)
            run.append(run[-1] + x[SUBLANES - 1:, :])
        d0 = [loc[j] + (run[j] - run[j - j % bv]) if j % bv else loc[j] for j in range(nv)]
        scores = jnp.where(in_level[0], _dot_nt(cat([qs[j] * jnp.exp2(d0[j]) for j in range(nv)]),
                                             cat([ks[j] * jnp.exp2(-d0[j]) for j in range(nv)])), 0.0)
        for li, half in enumerate(levels, 1):
            hv = half // SUBLANES
            q_side, k_side = [], []
            for j in range(nv):
                anchor = run[j - j % (2 * hv) + hv]
                if j % (2 * hv) >= hv:
                    q_side.append(qs[j] * jnp.exp2(loc[j] + (run[j] - anchor)))
                    k_side.append(zeros)
                else:
                    q_side.append(zeros)
                    k_side.append(ks[j] * jnp.exp2((anchor - run[j]) - loc[j]))
            scores = jnp.where(in_level[li], _dot_nt(cat(q_side), cat(k_side)), scores)
        o = _dot(scores, v) + _dot(cat([qs[j] * jnp.exp2(loc[j] + run[j]) for j in range(nv)]), s)
        k_hat = cat([ks[j] * jnp.exp2((run[nv] - run[j]) - loc[j]) for j in range(nv)])
        d_col = jnp.exp2(jnp.broadcast_to(run[nv], (dk, dk)).T)
        y_ref[rows, cols] = _hgrn_out(o, g_ref[rows, cols], nw).astype(y_ref.dtype)
        return d_col * s + _dot(k_hat.T, v)

    def chunk(i, states):
        rows = pl.ds(pl.multiple_of(i * c, c), c)
        in_level = [lvl == i for i in range(len(levels) + 1)]
        return tuple(one_head(rows, slice(j * dk, (j + 1) * dk), states[j], in_level) for j in range(hp))

    states = lax.fori_loop(0, q_ref.shape[0] // c, chunk, tuple(s0_ref[0, j] for j in range(hp)), unroll=2)
    for j in range(hp):
        so_ref[0, j] = states[j]


def _hgrn_short_kernel(q_ref, f_ref, v_ref, g_ref, lbl_ref, nw_ref, s0_ref, y_ref, so_ref, qd_s, oi_s,
                       *, l, layer):
    rows, dk = q_ref.shape
    q, v = q_ref[...], v_ref[...]
    vb = v.astype(BF16)
    k, lf = _hgrn_gates(f_ref[...], lbl_ref[...], layer)
    b = _cumsum_rows(lf, l)
    row = lax.broadcasted_iota(jnp.int32, (rows, rows), 0)
    col = lax.broadcasted_iota(jnp.int32, (rows, rows), 1)
    qe = q * jnp.exp2(b)
    scores = jnp.where((row // l == col // l) & (col <= row), _dot_nt(qe, k * jnp.exp2(-b)), 0.0)
    o_intra = _dot(scores, vb)
    qd_s[...] = qe
    b_last = _row_of_block(b, l, l - 1)
    kht = (k * jnp.exp2(b_last - b)).T
    dect = jnp.exp2(b_last).T
    win = 16
    per = win // l
    lane = lax.broadcasted_iota(jnp.int32, (dk, rows), 1)
    wrow = lax.broadcasted_iota(jnp.int32, (win, dk), 0)

    def window(w, carry):
        wr = pl.ds(pl.multiple_of(w * win, win), win)
        qw = qd_s[wr, :].astype(BF16)
        oi = jnp.zeros((win, dk), F32)
        for j in range(per):
            bi = w * per + j
            s = s0_ref[bi, 0]
            oi = jnp.where(wrow // l == j, jnp.dot(qw, s.astype(BF16), preferred_element_type=F32), oi)
            d_col = jnp.sum(jnp.where(lane == bi * l, dect, 0.0), axis=1, keepdims=True)
            so_ref[bi, 0] = d_col * s + jnp.dot(jnp.where(lane // l == bi, kht, 0.0).astype(BF16), vb,
                                                preferred_element_type=F32)
        oi_s[wr, :] = oi
        return carry

    lax.fori_loop(0, rows // win, window, 0, unroll=True)
    y_ref[...] = _hgrn_out(o_intra + oi_s[...], g_ref[...], nw_ref[...]).astype(y_ref.dtype)


def _hgrn(z, lb_logits, norm_w, s0, n_seq, l, layer):
    n = z.shape[0]
    dk = s0.shape[-1]
    depth = lb_logits.shape[0]
    if l % HG_CHUNK == 0:
        rows, n_blk, hp = l, n_seq, HG_HEADS_PER_STEP
        kern = functools.partial(_hgrn_long_kernel, c=HG_CHUNK, layer=layer, hp=hp)
        sspec = pl.BlockSpec((1, hp, dk, dk), lambda b, h: (b, h, 0, 0))
        scratch = []
    else:
        assert HG_BLOCK % l == 0
        rows, hp = LANES, 1
        per_blk = rows // l
        n_blk = n_seq // per_blk
        kern = functools.partial(_hgrn_short_kernel, l=l, layer=layer)
        sspec = pl.BlockSpec((per_blk, 1, dk, dk), lambda b, h: (b, h, 0, 0))
        scratch = [pltpu.VMEM((rows, dk), F32), pltpu.VMEM((rows, dk), F32)]
    n_hb = HG_HEADS // hp
    zspec = lambda rows, field: pl.BlockSpec((rows, hp * dk), lambda b, h: (b, field * n_hb + h))
    y, s_new = pl.pallas_call(
        kern, grid=(n_blk, n_hb),
        in_specs=[zspec(rows, 0), zspec(rows, 1), zspec(rows, 2), zspec(rows, 3),
                  pl.BlockSpec((depth, hp * dk), lambda b, h: (0, h)), _full((1, dk)), sspec],
        out_specs=[pl.BlockSpec((rows, hp * dk), lambda b, h: (b, h)), sspec],
        out_shape=[jax.ShapeDtypeStruct((n, HG_HEADS * dk), _mix_dtype(l)), jax.ShapeDtypeStruct(s0.shape, F32)],
        scratch_shapes=scratch,
        compiler_params=_cparams(2), name="hgrn")(z, z, z, z, lb_logits, norm_w, s0)
    return y, s_new


def _ffn_kernel(*refs, n_parts, long_mode, final_norm, l):
    x_ref, parts = refs[0], refs[1:1 + n_parts]
    (wo_ref, nw_ref, wg_ref, wu_ref, cw_ref, cb_ref, wd_ref, fnw_ref, buf_ref, o_ref, bufo_ref,
     act_s) = refs[1 + n_parts:13 + n_parts]
    d_ff = wg_ref.shape[1]
    if long_mode:
        tail_s = refs[13 + n_parts]
        tm = x_ref.shape[1]
        rc = min(ROW_TILE, tm)
        load = lambda ref, r0: ref[0, r0:r0 + rc, :]

        @pl.when(pl.program_id(1) == 0)
        def _():
            tail_s[...] = buf_ref[0]
    else:
        nb = x_ref.shape[0]
        tm = rc = l * nb
        load = lambda ref, r0: jnp.concatenate([ref[:, t, :] for t in range(l)], axis=0)

    xs = []
    for r0 in range(0, tm, rc):
        mix = jnp.concatenate([load(p, r0).astype(BF16) for p in parts], axis=1)
        x = load(x_ref, r0) + jnp.dot(mix, wo_ref[...], preferred_element_type=F32)
        xs.append(x)
        h = _rmsnorm(x, nw_ref[...]).astype(BF16)
        for c0 in range(0, d_ff, FF_CHUNK):
            cs = slice(c0, c0 + FF_CHUNK)
            g = jnp.dot(h, wg_ref[:, cs], preferred_element_type=F32)
            up = jnp.dot(h, wu_ref[:, cs], preferred_element_type=F32)
            if long_mode:
                row = lax.broadcasted_iota(jnp.int32, g.shape, 0)
                p1 = jnp.where(row < 1, tail_s[1:2, cs], pltpu.roll(g, 1, 0))
                p2 = jnp.where(row < 2, jnp.where(row == 0, tail_s[0:1, cs], tail_s[1:2, cs]),
                               pltpu.roll(g, 2, 0))
                tail_s[:, cs] = g[rc - (CONV_W - 1):, :]
            else:
                b0, b1 = buf_ref[:, 0, cs], buf_ref[:, 1, cs]
                p1 = jnp.concatenate([b1, g[:tm - nb]], axis=0)
                p2 = jnp.concatenate([b0, b1, g[:tm - 2 * nb]], axis=0)
                bufo_ref[:, 0, cs] = g[tm - 2 * nb:tm - nb]
                bufo_ref[:, 1, cs] = g[tm - nb:]
            conv = cb_ref[:, cs] + cw_ref[0:1, cs] * p2
            conv = conv + cw_ref[1:2, cs] * p1
            conv = conv + cw_ref[2:3, cs] * g
            act_s[r0:r0 + rc, cs] = (jax.nn.silu(conv) * up).astype(BF16)
    y = jnp.concatenate(xs, axis=0) + jnp.dot(act_s[...], wd_ref[...], preferred_element_type=F32)
    if final_norm:
        y = _rmsnorm(y, fnw_ref[...])
    if long_mode:
        o_ref[0] = y
        bufo_ref[0] = tail_s[...]
    else:
        for t in range(l):
            o_ref[:, t, :] = y[t * nb:(t + 1) * nb]


def _ffn(x, parts, layer, wo, nw, wg, wu, conv_w, conv_b, wd, fnw, buf, n_seq, l, final_norm):
    assert CONV_W == 3 and l >= CONV_W - 1
    n, d = x.shape
    d_ff = wg.shape[2]
    weights = (wo, nw, wg, wu, conv_w, conv_b, wd, fnw)
    stacked = (wg, wu, wd)

    def wspec(w, single_buffer):
        kw = dict(pipeline_mode=pl.Buffered(1)) if single_buffer else {}
        if any(w is s for s in stacked):
            return pl.BlockSpec((None,) + w.shape[1:], lambda *_: (layer, 0, 0), **kw)
        return pl.BlockSpec(w.shape, lambda *_: (0,) * w.ndim, **kw)
    kern = functools.partial(_ffn_kernel, n_parts=len(parts), final_norm=final_norm, l=l)
    buf_shape = jax.ShapeDtypeStruct((n_seq, CONV_W - 1, d_ff), F32)
    rows3 = lambda a: a.reshape(n_seq, l, a.shape[1])
    if l % ROW_TILE == 0:
        tm = FFN_ROW_TILE
        assert l % tm == 0
        rspec = lambda w: pl.BlockSpec((1, tm, w), lambda b, i: (b, i, 0))
        bspec = pl.BlockSpec((1, CONV_W - 1, d_ff), lambda b, i: (b, 0, 0))
        y, buf_new = pl.pallas_call(
            functools.partial(kern, long_mode=True),
            grid=(n_seq, l // tm),
            in_specs=([rspec(d)] + [rspec(p.shape[1]) for p in parts]
                      + [wspec(w, True) for w in weights] + [bspec]),
            out_specs=[rspec(d), bspec],
            out_shape=[jax.ShapeDtypeStruct((n_seq, l, d), F32), buf_shape],
            scratch_shapes=[pltpu.VMEM((tm, d_ff), BF16), pltpu.VMEM((CONV_W - 1, d_ff), F32)],
            compiler_params=_cparams(2), name="ffn")(rows3(x), *map(rows3, parts), *weights, buf)
        return y.reshape(n, d), buf_new
    y, buf_new = pl.pallas_call(
        functools.partial(kern, long_mode=False),
        grid=(1,),
        in_specs=([_full((n_seq, l, d))] + [_full((n_seq, l, p.shape[1])) for p in parts]
                  + [wspec(w, False) for w in weights] + [_full(buf.shape)]),
        out_specs=[_full((n_seq, l, d)), _full(buf.shape)],
        out_shape=[jax.ShapeDtypeStruct((n_seq, l, d), F32), buf_shape],
        scratch_shapes=[pltpu.VMEM((n, d_ff), BF16)],
        compiler_params=_cparams(1), name="ffn")(
            rows3(x), *map(rows3, parts), *weights, buf)
    return y.reshape(n, d), buf_new


def _trunk(x3, pos, s5_re, s5_im, ret, hg, conv, p, s5_tl):
    b, l, d = x3.shape
    n = b * l
    x = x3.reshape(n, d)
    depth = p['norm_mix'].shape[0]
    new_re, new_im, new_ret, new_hg, new_conv = [], [], [], [], []
    for layer in range(depth):
        j = layer // 2
        nw = p['norm_mix'][layer].reshape(1, d)
        if layer % 2 == 0:
            s5w = p['s5'][j]
            width = s5w['d'].shape[1]
            y_s5, r, im = _s5_mixer(x.reshape(b, l, d), nw, p['w_in_ab'][j], s5w['bmat_re'], s5w['bmat_im'],
                                    s5w['cmat_re'], s5w['cmat_im'], s5w['d'], s5w['glu_w'], s5w['glu_b'],
                                    s5w['a_re'], s5w['a_im'], s5_re[j].reshape(b, -1), s5_im[j].reshape(b, -1),
                                    s5_tl)
            z = _proj(x, nw, p['w_in_ab'][j], col0=width)
            cc, ss = _rope_tables(pos, ret.shape[-1])
            y_ret, st = _retention(z, 0, cc, ss, ret[j], b, l)
            parts, w_out = [y_s5, y_ret], p['w_out_ab'][j]
            new_re.append(r.reshape(s5_re.shape[1:]))
            new_im.append(im.reshape(s5_im.shape[1:]))
            new_ret.append(st)
        else:
            z = _proj(x, nw, p['w_in_c'][j])
            y_hg, st = _hgrn(z, p['hg_lb_logits'], p['hg_norm_w'][j].reshape(1, -1), hg[j], b, l, layer)
            parts, w_out = [y_hg], p['w_out_c'][j]
            new_hg.append(st)
        x, buf = _ffn(x, parts, layer, w_out, p['norm_ffn'][layer].reshape(1, d), p['ffn_w_gate'],
                      p['ffn_w_up'], p['ffn_conv_w'][layer], p['ffn_conv_b'][layer].reshape(1, -1),
                      p['ffn_w_down'], p['norm_final'].reshape(1, d), conv[layer], b, l,
                      final_norm=(layer == depth - 1))
        new_conv.append(buf)
    return (x.reshape(b, l, d), jnp.stack(new_re), jnp.stack(new_im), jnp.stack(new_ret),
            jnp.stack(new_hg), jnp.stack(new_conv))


def kernel(x_prompt, x_sample, state_s5_re, state_s5_im, state_ret, state_hgrn, state_ffn_conv, pos_sample, norm_mix, norm_ffn, norm_final, w_in_ab, s5_lam_re, s5_lam_im, s5_log_dt, s5_b_re, s5_b_im, s5_c_re, s5_c_im, s5_d, s5_glu_w, s5_glu_b, w_out_ab, w_in_c, hg_lb_logits, hg_norm_w, w_out_c, ffn_w_gate, ffn_w_up, ffn_conv_w, ffn_conv_b, ffn_w_down):
    n_ab, n_grp, n_st = s5_lam_re.shape
    ch = s5_b_re.shape[-1]
    width = n_grp * ch
    grp_per_blk = MXU_DIM // ch
    n_blk = n_grp // grp_per_blk
    s5 = []
    for j in range(n_ab):
        a_re, a_im, bb_re, bb_im = _s5_prep(s5_lam_re[j], s5_lam_im[j], s5_log_dt[j], s5_b_re[j], s5_b_im[j])
        bmat = lambda m: jnp.stack([_block_diag(blk, grp_per_blk)
                                    for blk in m.reshape(n_blk, grp_per_blk * ch, n_st)]).astype(BF16)
        cmat = lambda c: jnp.stack([_block_diag(blk, grp_per_blk).T for blk in
                                    c.reshape(n_blk, grp_per_blk * ch, n_st)]).astype(BF16)
        s5.append(dict(bmat_re=bmat(bb_re), bmat_im=bmat(bb_im),
                       cmat_re=cmat(s5_c_re[j]), cmat_im=cmat(s5_c_im[j]),
                       d=s5_d[j].reshape(1, width), glu_w=s5_glu_w[j].astype(BF16),
                       glu_b=s5_glu_b[j].reshape(1, width),
                       a_re=a_re.reshape(1, n_grp * n_st), a_im=a_im.reshape(1, n_grp * n_st)))
    per_layer_bf16 = lambda w: [w[i].astype(BF16) for i in range(w.shape[0])]
    p = dict(norm_mix=norm_mix, norm_ffn=norm_ffn, norm_final=norm_final, s5=s5,
             w_in_ab=per_layer_bf16(w_in_ab), w_out_ab=per_layer_bf16(w_out_ab),
             w_in_c=per_layer_bf16(w_in_c), hg_lb_logits=hg_lb_logits, hg_norm_w=hg_norm_w,
             w_out_c=per_layer_bf16(w_out_c), ffn_w_gate=ffn_w_gate.astype(BF16),
             ffn_w_up=ffn_w_up.astype(BF16), ffn_conv_w=ffn_conv_w, ffn_conv_b=ffn_conv_b,
             ffn_w_down=ffn_w_down.astype(BF16))

    bp, lp, _ = x_prompt.shape
    z_s5 = jnp.zeros((n_ab, bp) + state_s5_re.shape[2:], F32)
    z_ret = jnp.zeros((n_ab, bp) + state_ret.shape[2:], F32)
    z_hg = jnp.zeros((state_hgrn.shape[0], bp) + state_hgrn.shape[2:], F32)
    z_conv = jnp.zeros((norm_mix.shape[0], bp) + state_ffn_conv.shape[2:], F32)
    outs_p = _trunk(x_prompt, jnp.arange(lp, dtype=jnp.int32), z_s5, z_s5, z_ret, z_hg, z_conv, p,
                    s5_tl=ROW_TILE // bp)
    bs, ls, _ = x_sample.shape
    pos_s = (pos_sample[:, None] + jnp.arange(ls, dtype=jnp.int32)[None, :]).reshape(-1)
    outs_s = _trunk(x_sample, pos_s, state_s5_re, state_s5_im, state_ret, state_hgrn, state_ffn_conv, p,
                    s5_tl=ls)
    return (outs_p[0], outs_s[0]) + outs_p[1:] + outs_s[1:]
```

```python
import functools

import jax
import jax.numpy as jnp
from jax import lax
from jax.experimental import pallas as pl
from jax.experimental.pallas import tpu as pltpu

F32 = jnp.float32
BF16 = jnp.bfloat16

NORM_EPS = 1e-6
ROPE_BASE = 10000.0
S5_GROUP_CH = 16
RET_HEADS = 4
RET_CHUNK = 128
HG_HEADS = 8
HG_BLOCK = 16
HG_CHUNK = 128
RET_HEADS_PER_STEP = 4
HG_HEADS_PER_STEP = 4
CONV_W = 3

LANES = 128
SUBLANES = 8
MXU_DIM = 256
VMEM_LIMIT_BYTES = 56 * 1024 * 1024
ROW_TILE = 512
FFN_ROW_TILE = 1024
PROJ_ROW_TILE = 1024
FF_CHUNK = 256
S5_SCAN_COLS = 512


def _cparams(n_grid_dims):
    return pltpu.CompilerParams(dimension_semantics=("arbitrary",) * n_grid_dims,
                                vmem_limit_bytes=VMEM_LIMIT_BYTES)


def _dot(a, b):
    return jnp.dot(a.astype(BF16), b.astype(BF16), preferred_element_type=F32)


def _dot_nt(a, b):
    return lax.dot_general(a.astype(BF16), b.astype(BF16), (((1,), (1,)), ((), ())),
                           preferred_element_type=F32)


def _rmsnorm(x, w):
    return x * lax.rsqrt(jnp.mean(x * x, axis=-1, keepdims=True) + NORM_EPS) * w


def _full(shape):
    nd = len(shape)
    return pl.BlockSpec(shape, lambda *_: (0,) * nd)


def _mix_dtype(l):
    return BF16 if l % ROW_TILE == 0 else F32


def _resident(shape):
    nd = len(shape)
    return pl.BlockSpec(shape, lambda *_: (0,) * nd, pipeline_mode=pl.Buffered(1))


def _proj_kernel(x_ref, nw_ref, w_ref, o_ref, *, segments, col_chunk, row_chunk):
    for r in range(0, x_ref.shape[0], row_chunk):
        rows = slice(r, r + row_chunk)
        h = _rmsnorm(x_ref[rows, :], nw_ref[...]).astype(BF16)
        out0 = 0
        for col0, width in segments:
            for c in range(0, width, col_chunk):
                o_ref[rows, out0 + c:out0 + c + col_chunk] = jnp.dot(
                    h, w_ref[:, col0 + c:col0 + c + col_chunk], preferred_element_type=F32)
            out0 += width


def _proj(x, nw, w_bf, segments=None):
    n, d = x.shape
    segments = tuple(segments or ((0, w_bf.shape[1]),))
    n_out = sum(width for _, width in segments)
    tm = min(PROJ_ROW_TILE, n)
    return pl.pallas_call(
        functools.partial(_proj_kernel, segments=segments, col_chunk=2 * MXU_DIM, row_chunk=ROW_TILE),
        grid=(n // tm,),
        in_specs=[pl.BlockSpec((tm, d), lambda i: (i, 0)), _full((1, d)), _resident(w_bf.shape)],
        out_specs=pl.BlockSpec((tm, n_out), lambda i: (i, 0)),
        out_shape=jax.ShapeDtypeStruct((n, n_out), F32),
        compiler_params=_cparams(1), name="proj")(x, nw, w_bf)


def _proj_t_kernel(x_ref, nw_ref, wt_ref, o_ref, *, c):
    h = _rmsnorm(x_ref[...], nw_ref[...]).astype(BF16)
    zt = _dot_nt(wt_ref[...], h)
    for j in range(o_ref.shape[0]):
        o_ref[j] = zt[:, j * c:(j + 1) * c]


def _proj_t(x, nw, wt_bf, c):
    n, d = x.shape
    width = wt_bf.shape[0]
    tm = min(PROJ_ROW_TILE, n)
    return pl.pallas_call(
        functools.partial(_proj_t_kernel, c=c),
        grid=(n // tm,),
        in_specs=[pl.BlockSpec((tm, d), lambda i: (i, 0)), _full((1, d)), _resident(wt_bf.shape)],
        out_specs=pl.BlockSpec((tm // c, width, c), lambda i: (i, 0, 0)),
        out_shape=jax.ShapeDtypeStruct((n // c, width, c), F32),
        compiler_params=_cparams(1), name="proj_t")(x, nw, wt_bf)


def _s5_prep_kernel(lre_ref, lim_ref, ldt_ref, lre16_ref, lim16_ref, ldt16_ref, bre_ref, bim_ref,
                    are_ref, aim_ref, bbre_ref, bbim_ref):
    def disc(lre, lim, ldt):
        dt = jnp.exp(ldt)
        mag = jnp.exp(lre * dt)
        ang = lim * dt
        return mag * jnp.cos(ang), mag * jnp.sin(ang)

    ab_re, ab_im = disc(lre_ref[...], lim_ref[...], ldt_ref[...])
    are_ref[...] = ab_re
    aim_ref[...] = ab_im
    lre, lim = lre16_ref[...], lim16_ref[...]
    ab_re, ab_im = disc(lre, lim, ldt16_ref[...])
    nr, ni = ab_re - 1.0, ab_im
    den = lre * lre + lim * lim
    f_re = (nr * lre + ni * lim) / den
    f_im = (ni * lre - nr * lim) / den
    b_re, b_im = bre_ref[...], bim_ref[...]
    bbre_ref[...] = f_re * b_re - f_im * b_im
    bbim_ref[...] = f_re * b_im + f_im * b_re


def _s5_prep(lam_re, lam_im, log_dt, b_re, b_im):
    g, p = lam_re.shape
    ch = b_re.shape[-1]
    rep = lambda a: jnp.repeat(a, ch, axis=0)
    ldt = log_dt.reshape(g, 1)
    bt = lambda b: jnp.swapaxes(b, 1, 2).reshape(g * ch, p)
    args = (lam_re, lam_im, ldt, rep(lam_re), rep(lam_im), rep(ldt), bt(b_re), bt(b_im))
    return pl.pallas_call(
        _s5_prep_kernel,
        in_specs=[_full(a.shape) for a in args],
        out_specs=[_full((g, p)), _full((g, p)), _full((g * ch, p)), _full((g * ch, p))],
        out_shape=[jax.ShapeDtypeStruct((g, p), F32)] * 2 + [jax.ShapeDtypeStruct((g * ch, p), F32)] * 2,
        name="s5_prep")(*args)


def _block_diag(m, n_blk):
    r, c = m.shape[0] // n_blk, m.shape[1]
    eye = jnp.eye(n_blk, dtype=m.dtype)
    return (m.reshape(n_blk, r, 1, c) * eye[:, None, :, None]).reshape(n_blk * r, n_blk * c)


def _s5_kernel(x_ref, nw_ref, wu_ref, bre_ref, bim_ref, cre_ref, cim_ref, d_ref, gw_ref, gb_ref,
               are_ref, aim_ref, s0re_ref, s0im_ref,
               y_ref, sre_ref, sim_ref, bure_s, buim_s, stre_s, stim_s, *, tb, tl):
    @pl.when(pl.program_id(0) == 0)
    def _():
        stre_s[...] = s0re_ref[...]
        stim_s[...] = s0im_ref[...]

    d = nw_ref.shape[-1]
    width = wu_ref.shape[1]
    x = jnp.concatenate([x_ref[:, t, :] for t in range(tl)], axis=0)
    u = jnp.dot(_rmsnorm(x, nw_ref[...]).astype(BF16), wu_ref[...], preferred_element_type=F32)
    ub = u.astype(BF16)

    n_kb, kw, cw = bre_ref.shape
    for kb in range(n_kb):
        uk = ub[:, kb * kw:(kb + 1) * kw]
        bure_s[:, kb * cw:(kb + 1) * cw] = jnp.dot(uk, bre_ref[kb], preferred_element_type=F32)
        buim_s[:, kb * cw:(kb + 1) * cw] = jnp.dot(uk, bim_ref[kb], preferred_element_type=F32)

    n_state = are_ref.shape[1]
    sc = min(S5_SCAN_COLS, n_state)
    for c0 in range(0, n_state, sc):
        cs = slice(c0, c0 + sc)
        a_re = jnp.broadcast_to(are_ref[:, cs], (tb, sc))
        a_im = jnp.broadcast_to(aim_ref[:, cs], (tb, sc))

        def step(t, carry, cs=cs, a_re=a_re, a_im=a_im):
            s_re, s_im = carry
            rows = pl.ds(pl.multiple_of(t * tb, tb), tb)
            n_re = a_re * s_re - a_im * s_im + bure_s[rows, cs]
            n_im = a_re * s_im + a_im * s_re + buim_s[rows, cs]
            bure_s[rows, cs] = n_re
            buim_s[rows, cs] = n_im
            return n_re, n_im

        s_re, s_im = lax.fori_loop(0, tl, step, (stre_s[:, cs], stim_s[:, cs]), unroll=True)
        stre_s[:, cs] = s_re
        stim_s[:, cs] = s_im

    n_ob, ckw, ocw = cre_ref.shape
    ys = []
    for ob in range(n_ob):
        ks = slice(ob * ckw, (ob + 1) * ckw)
        ys.append(_dot(bure_s[:, ks], cre_ref[ob]) - _dot(buim_s[:, ks], cim_ref[ob]))
    y = jnp.concatenate(ys, axis=1) + d_ref[...] * u
    y = jax.nn.gelu(y)
    out = y * jax.nn.sigmoid(_dot(y, gw_ref[...]) + gb_ref[...])
    for t in range(tl):
        y_ref[:, t, :] = out[t * tb:(t + 1) * tb, :]
    sre_ref[...] = stre_s[...]
    sim_ref[...] = stim_s[...]


def _s5_mixer(x3, nw, w_in_bf, bmat_re, bmat_im, cmat_re, cmat_im, d_row, glu_w_bf, glu_b, a_re, a_im,
              s0_re, s0_im, tl):
    b, l, d = x3.shape
    width = d_row.shape[1]
    n_state = a_re.shape[1]
    rows = b * tl
    args = (x3, nw, w_in_bf, bmat_re, bmat_im, cmat_re, cmat_im, d_row, glu_w_bf, glu_b,
            a_re, a_im, s0_re, s0_im)
    in_specs = ([pl.BlockSpec((b, tl, d), lambda i: (0, i, 0)), _full(nw.shape),
                 pl.BlockSpec((d, width), lambda i: (0, 0))] + [_full(a.shape) for a in args[3:]])
    y, s_re, s_im = pl.pallas_call(
        functools.partial(_s5_kernel, tb=b, tl=tl),
        grid=(l // tl,), in_specs=in_specs,
        out_specs=[pl.BlockSpec((b, tl, width), lambda i: (0, i, 0)), _full((b, n_state)), _full((b, n_state))],
        out_shape=[jax.ShapeDtypeStruct((b, l, width), F32),
                   jax.ShapeDtypeStruct((b, n_state), F32), jax.ShapeDtypeStruct((b, n_state), F32)],
        scratch_shapes=[pltpu.VMEM((rows, n_state), F32), pltpu.VMEM((rows, n_state), F32),
                        pltpu.VMEM((b, n_state), F32), pltpu.VMEM((b, n_state), F32)],
        compiler_params=_cparams(1), name="s5_mixer")(*args)
    return y.reshape(b * l, width), s_re, s_im


def _rope_kernel(pos_ref, inv_ref, cc_ref, ss_ref, *, dim_axis):
    ang = pos_ref[...] * inv_ref[...]
    idx = lax.broadcasted_iota(jnp.int32, ang.shape, dim_axis)
    cc_ref[...] = jnp.cos(ang)
    ss_ref[...] = jnp.where(idx < ang.shape[dim_axis] // 2, -jnp.sin(ang), jnp.sin(ang))


def _rope_tables(pos, dk, transposed=False):
    n = pos.shape[0]
    inv = 1.0 / (ROPE_BASE ** jnp.linspace(0.0, 1.0, dk // 2, dtype=F32))
    inv = jnp.concatenate([inv, inv])
    pos = pos.astype(F32)
    pos, inv, shape = (pos.reshape(1, n), inv.reshape(dk, 1), (dk, n)) if transposed else \
                      (pos.reshape(n, 1), inv.reshape(1, dk), (n, dk))
    return pl.pallas_call(
        functools.partial(_rope_kernel, dim_axis=0 if transposed else 1),
        in_specs=[_full(pos.shape), _full(inv.shape)],
        out_specs=[_full(shape), _full(shape)],
        out_shape=[jax.ShapeDtypeStruct(shape, F32)] * 2,
        name="rope_tables")(pos, inv)


def _rotate(t, cc, ss):
    return t * cc + pltpu.roll(t, t.shape[1] // 2, 1) * ss


def _rotate_t(t, cc, ss):
    half = t.shape[0] // 2
    return t * cc + jnp.concatenate([t[half:], t[:half]], axis=0) * ss


def _ret_consts(rows, c):
    lg = jnp.log(1.0 - 2.0 ** (-5.0 - jnp.arange(RET_HEADS, dtype=F32)))
    r = jnp.arange(rows)
    idx, blk = r % c, r // c
    diff = idx[:, None] - idx[None, :]
    same = blk[:, None] == blk[None, :]
    decay = jnp.where((same & (diff >= 0))[None],
                      jnp.exp(jnp.maximum(diff, 0)[None].astype(F32) * lg[:, None, None]), 0.0)
    wide = lambda v: jnp.broadcast_to(v[:, :, None], (RET_HEADS, rows, LANES))
    q_dec = wide(jnp.exp((idx + 1).astype(F32)[None, :] * lg[:, None]))
    k_dec = wide(jnp.exp((c - 1 - idx).astype(F32)[None, :] * lg[:, None]))
    chunk_dec = jnp.broadcast_to(jnp.exp(c * lg)[:, None, None], (RET_HEADS, SUBLANES, LANES))
    return decay, q_dec, k_dec, chunk_dec


def _ret_gate(o, g):
    o = o * lax.rsqrt(jnp.mean(o * o, axis=-1, keepdims=True) + NORM_EPS)
    return jax.nn.silu(g) * o


def _ret_long_kernel(q_ref, kt_ref, v_ref, g_ref, cc_ref, ss_ref, cct_ref, sst_ref, dec_ref, qd_ref, kdt_ref,
                     cd_ref, s0_ref, y_ref, so_ref, *, c, hp):
    dk = q_ref.shape[1] // hp
    scale = dk ** -0.5

    def chunk(i, states):
        rows = pl.ds(pl.multiple_of(i * c, c), c)
        cc, ss = cc_ref[rows, :], ss_ref[rows, :]
        cct, sst = cct_ref[i], sst_ref[i]
        new_states = []
        for j in range(hp):
            cols = slice(j * dk, (j + 1) * dk)
            q = _rotate(q_ref[rows, cols], cc, ss)
            kt = _rotate_t(kt_ref[i, cols, :], cct, sst) * scale
            v = v_ref[rows, cols]
            scores = _dot(q, kt) * dec_ref[j]
            o = _dot(scores, v) + _dot(q * qd_ref[j], states[j])
            new_states.append(cd_ref[j, 0:1, :] * states[j] + _dot(kt * kdt_ref[j], v))
            y_ref[rows, cols] = _ret_gate(o, g_ref[rows, cols]).astype(y_ref.dtype)
        return tuple(new_states)

    states = lax.fori_loop(0, q_ref.shape[0] // c, chunk, tuple(s0_ref[0, j] for j in range(hp)), unroll=2)
    for j in range(hp):
        so_ref[0, j] = states[j]


def _ret_short_kernel(q_ref, k_ref, v_ref, g_ref, cc_ref, ss_ref, dec_ref, qd_ref, kd_ref, cd_ref, s0_ref,
                      y_ref, so_ref, qd_s, oi_s, *, l):
    rows, dk = q_ref.shape
    scale = dk ** -0.5
    cc, ss = cc_ref[...], ss_ref[...]
    q = _rotate(q_ref[...], cc, ss)
    k = _rotate(k_ref[...], cc, ss) * scale
    v = v_ref[...]
    vb = v.astype(BF16)
    o_intra = _dot(_dot_nt(q, k) * dec_ref[0], vb)
    qd_s[...] = q * qd_ref[0]
    kdt = (k * kd_ref[0]).T
    cd = cd_ref[0, 0:1, :]
    win = 16
    per = win // l
    lane = lax.broadcasted_iota(jnp.int32, (dk, rows), 1)
    wrow = lax.broadcasted_iota(jnp.int32, (win, dk), 0)

    def window(w, carry):
        wr = pl.ds(pl.multiple_of(w * win, win), win)
        qw = qd_s[wr, :].astype(BF16)
        oi = jnp.zeros((win, dk), F32)
        for j in range(per):
            b = w * per + j
            s = s0_ref[b, 0]
            oi = jnp.where(wrow // l == j, jnp.dot(qw, s.astype(BF16), preferred_element_type=F32), oi)
            so_ref[b, 0] = cd * s + jnp.dot(jnp.where(lane // l == b, kdt, 0.0).astype(BF16), vb,
                                            preferred_element_type=F32)
        oi_s[wr, :] = oi
        return carry

    lax.fori_loop(0, rows // win, window, 0, unroll=True)
    y_ref[...] = _ret_gate(o_intra + oi_s[...], g_ref[...]).astype(y_ref.dtype)


def _retention(x, nw, w_in_bf, w_kt_bf, col0, pos, s0, n_seq, l):
    n = x.shape[0]
    dk = s0.shape[-1]
    width = RET_HEADS * dk
    out_shape = [jax.ShapeDtypeStruct((n, width), _mix_dtype(l)), jax.ShapeDtypeStruct(s0.shape, F32)]
    cspec = lambda a, hp: pl.BlockSpec((hp,) + a.shape[1:], lambda b, h: (h, 0, 0))
    cc, ss = _rope_tables(pos, dk)
    if l % RET_CHUNK == 0:
        c, hp = RET_CHUNK, RET_HEADS_PER_STEP
        n_hb = RET_HEADS // hp
        z = _proj(x, nw, w_in_bf, segments=((col0, width), (col0 + 2 * width, 2 * width)))
        kt = _proj_t(x, nw, w_kt_bf, c)
        chunked = lambda t: jnp.swapaxes(t.reshape(dk, l // c, c), 0, 1)
        cct, sst = map(chunked, _rope_tables(pos, dk, transposed=True))
        decay, q_dec, k_dec, chunk_dec = _ret_consts(c, c)
        k_dec_t = jnp.swapaxes(k_dec, 1, 2)
        zspec = lambda field: pl.BlockSpec((l, hp * dk), lambda b, h: (b, field * n_hb + h))
        sspec = pl.BlockSpec((1, hp, dk, dk), lambda b, h: (b, h, 0, 0))
        return pl.pallas_call(
            functools.partial(_ret_long_kernel, c=c, hp=hp), grid=(n_seq, n_hb),
            in_specs=[zspec(0), pl.BlockSpec((l // c, hp * dk, c), lambda b, h: (b, h, 0)), zspec(1), zspec(2),
                      _full(cc.shape), _full(ss.shape), _full(cct.shape), _full(sst.shape),
                      cspec(decay, hp), cspec(q_dec, hp), cspec(k_dec_t, hp), cspec(chunk_dec, hp), sspec],
            out_specs=[pl.BlockSpec((l, hp * dk), lambda b, h: (b, h)), sspec],
            out_shape=out_shape, compiler_params=_cparams(2), name="retention")(
                z, kt, z, z, cc, ss, cct, sst, decay, q_dec, k_dec_t, chunk_dec, s0)
    rows = LANES
    per_blk = rows // l
    z = _proj(x, nw, w_in_bf, segments=((col0, 4 * width),))
    decay, q_dec, k_dec, chunk_dec = _ret_consts(rows, l)
    zspec = lambda field: pl.BlockSpec((rows, dk), lambda b, h: (b, field * RET_HEADS + h))
    tspec = pl.BlockSpec((rows, dk), lambda b, h: (b, 0))
    sspec = pl.BlockSpec((per_blk, 1, dk, dk), lambda b, h: (b, h, 0, 0))
    return pl.pallas_call(
        functools.partial(_ret_short_kernel, l=l), grid=(n_seq // per_blk, RET_HEADS),
        in_specs=[zspec(0), zspec(1), zspec(2), zspec(3), tspec, tspec,
                  cspec(decay, 1), cspec(q_dec, 1), cspec(k_dec, 1), cspec(chunk_dec, 1), sspec],
        out_specs=[pl.BlockSpec((rows, dk), lambda b, h: (b, h)), sspec],
        out_shape=out_shape,
        scratch_shapes=[pltpu.VMEM((rows, dk), F32), pltpu.VMEM((rows, dk), F32)],
        compiler_params=_cparams(2), name="retention")(z, z, z, z, cc, ss, decay, q_dec, k_dec, chunk_dec, s0)


def _cumsum_rows(x, period):
    row = lax.broadcasted_iota(jnp.int32, x.shape, 0)
    s = 1
    while s < period:
        x = x + jnp.where(row % period >= s, pltpu.roll(x, s, 0), 0.0)
        s *= 2
    return x


def _row_of_block(x, period, offset):
    n, w = x.shape
    x3 = x.reshape(n // period, period, w)
    return jnp.broadcast_to(x3[:, offset:offset + 1, :], x3.shape).reshape(n, w)


def _hgrn_gates(fl, lg, layer):
    e = jnp.exp(lg - jnp.max(lg, axis=0, keepdims=True))
    soft = e / jnp.sum(e, axis=0, keepdims=True)
    cum = soft[0:1]
    for i in range(1, layer + 1):
        cum = cum + soft[i:i + 1]
    lb = cum - soft[0:1]
    forget = lb + (1.0 - lb) * jax.nn.sigmoid(fl)
    return 1.0 - forget, jnp.log2(forget)


def _hgrn_out(o, g, nw):
    o = o * lax.rsqrt(jnp.mean(o * o, axis=-1, keepdims=True) + NORM_EPS) * nw
    return o * jax.nn.silu(g)


def _hgrn_long_kernel(q_ref, f_ref, v_ref, g_ref, lbl_ref, nw_ref, s0_ref, y_ref, so_ref, *, c, layer, hp):
    dk = q_ref.shape[1] // hp
    nw = nw_ref[...]
    nv = c // SUBLANES
    bv = HG_BLOCK // SUBLANES
    levels = []
    half = HG_BLOCK
    while half < c:
        levels.append(half)
        half *= 2
    row = lax.broadcasted_iota(jnp.int32, (c, c), 0)
    col = lax.broadcasted_iota(jnp.int32, (c, c), 1)
    lvl = jnp.where((row // HG_BLOCK == col // HG_BLOCK) & (col <= row), 0, -1)
    for li, half in enumerate(levels, 1):
        lvl = jnp.where((row // (2 * half) == col // (2 * half)) & (row % (2 * half) >= half)
                        & (col % (2 * half) < half), li, lvl)
    sub = lax.broadcasted_iota(jnp.int32, (SUBLANES, dk), 0)
    zeros = jnp.zeros((SUBLANES, dk), F32)
    cat = lambda slabs: jnp.concatenate(slabs, axis=0)

    def one_head(rows, cols, s, in_level):
        q, v = q_ref[rows, cols], v_ref[rows, cols]
        k, lf = _hgrn_gates(f_ref[rows, cols], lbl_ref[:, cols], layer)
        slabs = lambda a: [a[j * SUBLANES:(j + 1) * SUBLANES, :] for j in range(nv)]
        qs, ks = slabs(q), slabs(k)
        loc, run = [], [jnp.zeros((1, dk), F32)]
        for x in slabs(lf):
            step = 1
            while step < SUBLANES:
                x = x + jnp.where(sub >= step, pltpu.roll(x, step, 0), 0.0)
                step *= 2
            loc.append(x)
            run.append(run[-1] + x[SUBLANES - 1:, :])
        d0 = [loc[j] + (run[j] - run[j - j % bv]) if j % bv else loc[j] for j in range(nv)]
        scores = jnp.where(in_level[0], _dot_nt(cat([qs[j] * jnp.exp2(d0[j]) for j in range(nv)]),
                                             cat([ks[j] * jnp.exp2(-d0[j]) for j in range(nv)])), 0.0)
        for li, half in enumerate(levels, 1):
            hv = half // SUBLANES
            q_side, k_side = [], []
            for j in range(nv):
                anchor = run[j - j % (2 * hv) + hv]
                if j % (2 * hv) >= hv:
                    q_side.append(qs[j] * jnp.exp2(loc[j] + (run[j] - anchor)))
                    k_side.append(zeros)
                else:
                    q_side.append(zeros)
                    k_side.append(ks[j] * jnp.exp2((anchor - run[j]) - loc[j]))
            scores = jnp.where(in_level[li], _dot_nt(cat(q_side), cat(k_side)), scores)
        o = _dot(scores, v) + _dot(cat([qs[j] * jnp.exp2(loc[j] + run[j]) for j in range(nv)]), s)
        k_hat = cat([ks[j] * jnp.exp2((run[nv] - run[j]) - loc[j]) for j in range(nv)])
        d_col = jnp.exp2(jnp.broadcast_to(run[nv], (dk, dk)).T)
        y_ref[rows, cols] = _hgrn_out(o, g_ref[rows, cols], nw).astype(y_ref.dtype)
        return d_col * s + _dot(k_hat.T, v)

    def chunk(i, states):
        rows = pl.ds(pl.multiple_of(i * c, c), c)
        in_level = [lvl == i for i in range(len(levels) + 1)]
        return tuple(one_head(rows, slice(j * dk, (j + 1) * dk), states[j], in_level) for j in range(hp))

    states = lax.fori_loop(0, q_ref.shape[0] // c, chunk, tuple(s0_ref[0, j] for j in range(hp)), unroll=2)
    for j in range(hp):
        so_ref[0, j] = states[j]


def _hgrn_short_kernel(q_ref, f_ref, v_ref, g_ref, lbl_ref, nw_ref, s0_ref, y_ref, so_ref, qd_s, oi_s,
                       *, l, layer):
    rows, dk = q_ref.shape
    q, v = q_ref[...], v_ref[...]
    vb = v.astype(BF16)
    k, lf = _hgrn_gates(f_ref[...], lbl_ref[...], layer)
    b = _cumsum_rows(lf, l)
    row = lax.broadcasted_iota(jnp.int32, (rows, rows), 0)
    col = lax.broadcasted_iota(jnp.int32, (rows, rows), 1)
    qe = q * jnp.exp2(b)
    scores = jnp.where((row // l == col // l) & (col <= row), _dot_nt(qe, k * jnp.exp2(-b)), 0.0)
    o_intra = _dot(scores, vb)
    qd_s[...] = qe
    b_last = _row_of_block(b, l, l - 1)
    kht = (k * jnp.exp2(b_last - b)).T
    dect = jnp.exp2(b_last).T
    win = 16
    per = win // l
    lane = lax.broadcasted_iota(jnp.int32, (dk, rows), 1)
    wrow = lax.broadcasted_iota(jnp.int32, (win, dk), 0)

    def window(w, carry):
        wr = pl.ds(pl.multiple_of(w * win, win), win)
        qw = qd_s[wr, :].astype(BF16)
        oi = jnp.zeros((win, dk), F32)
        for j in range(per):
            bi = w * per + j
            s = s0_ref[bi, 0]
            oi = jnp.where(wrow // l == j, jnp.dot(qw, s.astype(BF16), preferred_element_type=F32), oi)
            d_col = jnp.sum(jnp.where(lane == bi * l, dect, 0.0), axis=1, keepdims=True)
            so_ref[bi, 0] = d_col * s + jnp.dot(jnp.where(lane // l == bi, kht, 0.0).astype(BF16), vb,
                                                preferred_element_type=F32)
        oi_s[wr, :] = oi
        return carry

    lax.fori_loop(0, rows // win, window, 0, unroll=True)
    y_ref[...] = _hgrn_out(o_intra + oi_s[...], g_ref[...], nw_ref[...]).astype(y_ref.dtype)


def _hgrn(z, lb_logits, norm_w, s0, n_seq, l, layer):
    n = z.shape[0]
    dk = s0.shape[-1]
    depth = lb_logits.shape[0]
    if l % HG_CHUNK == 0:
        rows, n_blk, hp = l, n_seq, HG_HEADS_PER_STEP
        kern = functools.partial(_hgrn_long_kernel, c=HG_CHUNK, layer=layer, hp=hp)
        sspec = pl.BlockSpec((1, hp, dk, dk), lambda b, h: (b, h, 0, 0))
        scratch = []
    else:
        assert HG_BLOCK % l == 0
        rows, hp = LANES, 1
        per_blk = rows // l
        n_blk = n_seq // per_blk
        kern = functools.partial(_hgrn_short_kernel, l=l, layer=layer)
        sspec = pl.BlockSpec((per_blk, 1, dk, dk), lambda b, h: (b, h, 0, 0))
        scratch = [pltpu.VMEM((rows, dk), F32), pltpu.VMEM((rows, dk), F32)]
    n_hb = HG_HEADS // hp
    zspec = lambda rows, field: pl.BlockSpec((rows, hp * dk), lambda b, h: (b, field * n_hb + h))
    y, s_new = pl.pallas_call(
        kern, grid=(n_blk, n_hb),
        in_specs=[zspec(rows, 0), zspec(rows, 1), zspec(rows, 2), zspec(rows, 3),
                  pl.BlockSpec((depth, hp * dk), lambda b, h: (0, h)), _full((1, dk)), sspec],
        out_specs=[pl.BlockSpec((rows, hp * dk), lambda b, h: (b, h)), sspec],
        out_shape=[jax.ShapeDtypeStruct((n, HG_HEADS * dk), _mix_dtype(l)), jax.ShapeDtypeStruct(s0.shape, F32)],
        scratch_shapes=scratch,
        compiler_params=_cparams(2), name="hgrn")(z, z, z, z, lb_logits, norm_w, s0)
    return y, s_new


def _ffn_kernel(*refs, n_parts, long_mode, final_norm, l):
    x_ref, parts = refs[0], refs[1:1 + n_parts]
    (wo_ref, nw_ref, wg_ref, wu_ref, cw_ref, cb_ref, wd_ref, fnw_ref, buf_ref, o_ref, bufo_ref,
     act_s) = refs[1 + n_parts:13 + n_parts]
    d_ff = wg_ref.shape[1]
    if long_mode:
        tail_s = refs[13 + n_parts]
        tm = x_ref.shape[1]
        rc = min(ROW_TILE, tm)
        load = lambda ref, r0: ref[0, r0:r0 + rc, :]

        @pl.when(pl.program_id(1) == 0)
        def _():
            tail_s[...] = buf_ref[0]
    else:
        nb = x_ref.shape[0]
        tm = rc = l * nb
        load = lambda ref, r0: jnp.concatenate([ref[:, t, :] for t in range(l)], axis=0)

    xs = []
    for r0 in range(0, tm, rc):
        mix = jnp.concatenate([load(p, r0).astype(BF16) for p in parts], axis=1)
        x = load(x_ref, r0) + jnp.dot(mix, wo_ref[...], preferred_element_type=F32)
        xs.append(x)
        h = _rmsnorm(x, nw_ref[...]).astype(BF16)
        for c0 in range(0, d_ff, FF_CHUNK):
            cs = slice(c0, c0 + FF_CHUNK)
            g = jnp.dot(h, wg_ref[:, cs], preferred_element_type=F32)
            up = jnp.dot(h, wu_ref[:, cs], preferred_element_type=F32)
            if long_mode:
                row = lax.broadcasted_iota(jnp.int32, g.shape, 0)
                p1 = jnp.where(row < 1, tail_s[1:2, cs], pltpu.roll(g, 1, 0))
                p2 = jnp.where(row < 2, jnp.where(row == 0, tail_s[0:1, cs], tail_s[1:2, cs]),
                               pltpu.roll(g, 2, 0))
                tail_s[:, cs] = g[rc - (CONV_W - 1):, :]
            else:
                b0, b1 = buf_ref[:, 0, cs], buf_ref[:, 1, cs]
                p1 = jnp.concatenate([b1, g[:tm - nb]], axis=0)
                p2 = jnp.concatenate([b0, b1, g[:tm - 2 * nb]], axis=0)
                bufo_ref[:, 0, cs] = g[tm - 2 * nb:tm - nb]
                bufo_ref[:, 1, cs] = g[tm - nb:]
            conv = cb_ref[:, cs] + cw_ref[0:1, cs] * p2
            conv = conv + cw_ref[1:2, cs] * p1
            conv = conv + cw_ref[2:3, cs] * g
            act_s[r0:r0 + rc, cs] = (jax.nn.silu(conv) * up).astype(BF16)
    y = jnp.concatenate(xs, axis=0) + jnp.dot(act_s[...], wd_ref[...], preferred_element_type=F32)
    if final_norm:
        y = _rmsnorm(y, fnw_ref[...])
    if long_mode:
        o_ref[0] = y
        bufo_ref[0] = tail_s[...]
    else:
        for t in range(l):
            o_ref[:, t, :] = y[t * nb:(t + 1) * nb]


def _ffn(x, parts, layer, wo, nw, wg, wu, conv_w, conv_b, wd, fnw, buf, n_seq, l, final_norm):
    assert CONV_W == 3 and l >= CONV_W - 1
    n, d = x.shape
    d_ff = wg.shape[2]
    weights = (wo, nw, wg, wu, conv_w, conv_b, wd, fnw)
    stacked = (wg, wu, wd)

    def wspec(w, single_buffer):
        kw = dict(pipeline_mode=pl.Buffered(1)) if single_buffer else {}
        if any(w is s for s in stacked):
            return pl.BlockSpec((None,) + w.shape[1:], lambda *_: (layer, 0, 0), **kw)
        return pl.BlockSpec(w.shape, lambda *_: (0,) * w.ndim, **kw)
    kern = functools.partial(_ffn_kernel, n_parts=len(parts), final_norm=final_norm, l=l)
    buf_shape = jax.ShapeDtypeStruct((n_seq, CONV_W - 1, d_ff), F32)
    rows3 = lambda a: a.reshape(n_seq, l, a.shape[1])
    if l % ROW_TILE == 0:
        tm = FFN_ROW_TILE
        assert l % tm == 0
        rspec = lambda w: pl.BlockSpec((1, tm, w), lambda b, i: (b, i, 0))
        bspec = pl.BlockSpec((1, CONV_W - 1, d_ff), lambda b, i: (b, 0, 0))
        y, buf_new = pl.pallas_call(
            functools.partial(kern, long_mode=True),
            grid=(n_seq, l // tm),
            in_specs=([rspec(d)] + [rspec(p.shape[1]) for p in parts]
                      + [wspec(w, True) for w in weights] + [bspec]),
            out_specs=[rspec(d), bspec],
            out_shape=[jax.ShapeDtypeStruct((n_seq, l, d), F32), buf_shape],
            scratch_shapes=[pltpu.VMEM((tm, d_ff), BF16), pltpu.VMEM((CONV_W - 1, d_ff), F32)],
            compiler_params=_cparams(2), name="ffn")(rows3(x), *map(rows3, parts), *weights, buf)
        return y.reshape(n, d), buf_new
    y, buf_new = pl.pallas_call(
        functools.partial(kern, long_mode=False),
        grid=(1,),
        in_specs=([_full((n_seq, l, d))] + [_full((n_seq, l, p.shape[1])) for p in parts]
                  + [wspec(w, False) for w in weights] + [_full(buf.shape)]),
        out_specs=[_full((n_seq, l, d)), _full(buf.shape)],
        out_shape=[jax.ShapeDtypeStruct((n_seq, l, d), F32), buf_shape],
        scratch_shapes=[pltpu.VMEM((n, d_ff), BF16)],
        compiler_params=_cparams(1), name="ffn")(
            rows3(x), *map(rows3, parts), *weights, buf)
    return y.reshape(n, d), buf_new


def _trunk(x3, pos, s5_re, s5_im, ret, hg, conv, p, s5_tl):
    b, l, d = x3.shape
    n = b * l
    x = x3.reshape(n, d)
    depth = p['norm_mix'].shape[0]
    new_re, new_im, new_ret, new_hg, new_conv = [], [], [], [], []
    for layer in range(depth):
        j = layer // 2
        nw = p['norm_mix'][layer].reshape(1, d)
        if layer % 2 == 0:
            s5w = p['s5'][j]
            width = s5w['d'].shape[1]
            y_s5, r, im = _s5_mixer(x.reshape(b, l, d), nw, p['w_in_ab'][j], s5w['bmat_re'], s5w['bmat_im'],
                                    s5w['cmat_re'], s5w['cmat_im'], s5w['d'], s5w['glu_w'], s5w['glu_b'],
                                    s5w['a_re'], s5w['a_im'], s5_re[j].reshape(b, -1), s5_im[j].reshape(b, -1),
                                    s5_tl)
            y_ret, st = _retention(x, nw, p['w_in_ab'][j], p['w_kt'][j], width, pos, ret[j], b, l)
            parts, w_out = [y_s5, y_ret], p['w_out_ab'][j]
            new_re.append(r.reshape(s5_re.shape[1:]))
            new_im.append(im.reshape(s5_im.shape[1:]))
            new_ret.append(st)
        else:
            z = _proj(x, nw, p['w_in_c'][j])
            y_hg, st = _hgrn(z, p['hg_lb_logits'], p['hg_norm_w'][j].reshape(1, -1), hg[j], b, l, layer)
            parts, w_out = [y_hg], p['w_out_c'][j]
            new_hg.append(st)
        x, buf = _ffn(x, parts, layer, w_out, p['norm_ffn'][layer].reshape(1, d), p['ffn_w_gate'],
                      p['ffn_w_up'], p['ffn_conv_w'][layer], p['ffn_conv_b'][layer].reshape(1, -1),
                      p['ffn_w_down'], p['norm_final'].reshape(1, d), conv[layer], b, l,
                      final_norm=(layer == depth - 1))
        new_conv.append(buf)
    return (x.reshape(b, l, d), jnp.stack(new_re), jnp.stack(new_im), jnp.stack(new_ret),
            jnp.stack(new_hg), jnp.stack(new_conv))


def kernel(x_prompt, x_sample, state_s5_re, state_s5_im, state_ret, state_hgrn, state_ffn_conv, pos_sample, norm_mix, norm_ffn, norm_final, w_in_ab, s5_lam_re, s5_lam_im, s5_log_dt, s5_b_re, s5_b_im, s5_c_re, s5_c_im, s5_d, s5_glu_w, s5_glu_b, w_out_ab, w_in_c, hg_lb_logits, hg_norm_w, w_out_c, ffn_w_gate, ffn_w_up, ffn_conv_w, ffn_conv_b, ffn_w_down):
    n_ab, n_grp, n_st = s5_lam_re.shape
    ch = s5_b_re.shape[-1]
    width = n_grp * ch
    ret_width = state_ret.shape[2] * state_ret.shape[3]
    grp_per_blk = MXU_DIM // ch
    n_blk = n_grp // grp_per_blk
    s5 = []
    for j in range(n_ab):
        a_re, a_im, bb_re, bb_im = _s5_prep(s5_lam_re[j], s5_lam_im[j], s5_log_dt[j], s5_b_re[j], s5_b_im[j])
        bmat = lambda m: jnp.stack([_block_diag(blk, grp_per_blk)
                                    for blk in m.reshape(n_blk, grp_per_blk * ch, n_st)]).astype(BF16)
        cmat = lambda c: jnp.stack([_block_diag(blk, grp_per_blk).T for blk in
                                    c.reshape(n_blk, grp_per_blk * ch, n_st)]).astype(BF16)
        s5.append(dict(bmat_re=bmat(bb_re), bmat_im=bmat(bb_im),
                       cmat_re=cmat(s5_c_re[j]), cmat_im=cmat(s5_c_im[j]),
                       d=s5_d[j].reshape(1, width), glu_w=s5_glu_w[j].astype(BF16),
                       glu_b=s5_glu_b[j].reshape(1, width),
                       a_re=a_re.reshape(1, n_grp * n_st), a_im=a_im.reshape(1, n_grp * n_st)))
    per_layer_bf16 = lambda w: [w[i].astype(BF16) for i in range(w.shape[0])]
    p = dict(norm_mix=norm_mix, norm_ffn=norm_ffn, norm_final=norm_final, s5=s5,
             w_in_ab=per_layer_bf16(w_in_ab), w_out_ab=per_layer_bf16(w_out_ab),
             w_kt=[w_in_ab[j][:, width + ret_width:width + 2 * ret_width].T.astype(BF16) for j in range(n_ab)],
             w_in_c=per_layer_bf16(w_in_c), hg_lb_logits=hg_lb_logits, hg_norm_w=hg_norm_w,
             w_out_c=per_layer_bf16(w_out_c), ffn_w_gate=ffn_w_gate.astype(BF16),
             ffn_w_up=ffn_w_up.astype(BF16), ffn_conv_w=ffn_conv_w, ffn_conv_b=ffn_conv_b,
             ffn_w_down=ffn_w_down.astype(BF16))

    bp, lp, _ = x_prompt.shape
    z_s5 = jnp.zeros((n_ab, bp) + state_s5_re.shape[2:], F32)
    z_ret = jnp.zeros((n_ab, bp) + state_ret.shape[2:], F32)
    z_hg = jnp.zeros((state_hgrn.shape[0], bp) + state_hgrn.shape[2:], F32)
    z_conv = jnp.zeros((norm_mix.shape[0], bp) + state_ffn_conv.shape[2:], F32)
    outs_p = _trunk(x_prompt, jnp.arange(lp, dtype=jnp.int32), z_s5, z_s5, z_ret, z_hg, z_conv, p,
                    s5_tl=ROW_TILE // bp)
    bs, ls, _ = x_sample.shape
    pos_s = (pos_sample[:, None] + jnp.arange(ls, dtype=jnp.int32)[None, :]).reshape(-1)
    outs_s = _trunk(x_sample, pos_s, state_s5_re, state_s5_im, state_ret, state_hgrn, state_ffn_conv, p,
                    s5_tl=ls)
    return (outs_p[0], outs_s[0]) + outs_p[1:] + outs_s[1:]
```

```python
import functools

import jax
import jax.numpy as jnp
from jax import lax
from jax.experimental import pallas as pl
from jax.experimental.pallas import tpu as pltpu

F32 = jnp.float32
BF16 = jnp.bfloat16

NORM_EPS = 1e-6
ROPE_BASE = 10000.0
S5_GROUP_CH = 16
RET_HEADS = 4
RET_CHUNK = 128
HG_HEADS = 8
HG_BLOCK = 16
HG_CHUNK = 128
RET_HEADS_PER_STEP = 4
HG_HEADS_PER_STEP = 4
CONV_W = 3

LANES = 128
SUBLANES = 8
MXU_DIM = 256
VMEM_LIMIT_BYTES = 56 * 1024 * 1024
ROW_TILE = 512
FFN_ROW_TILE = 1024
PROJ_ROW_TILE = 1024
FF_CHUNK = 256
S5_SCAN_COLS = 512


def _cparams(n_grid_dims):
    return pltpu.CompilerParams(dimension_semantics=("arbitrary",) * n_grid_dims,
                                vmem_limit_bytes=VMEM_LIMIT_BYTES)


def _dot(a, b):
    return jnp.dot(a.astype(BF16), b.astype(BF16), preferred_element_type=F32)


def _dot_nt(a, b):
    return lax.dot_general(a.astype(BF16), b.astype(BF16), (((1,), (1,)), ((), ())),
                           preferred_element_type=F32)


def _rmsnorm(x, w):
    return x * lax.rsqrt(jnp.mean(x * x, axis=-1, keepdims=True) + NORM_EPS) * w


def _full(shape):
    nd = len(shape)
    return pl.BlockSpec(shape, lambda *_: (0,) * nd)


def _mix_dtype(l):
    return BF16 if l % ROW_TILE == 0 else F32


def _resident(shape):
    nd = len(shape)
    return pl.BlockSpec(shape, lambda *_: (0,) * nd, pipeline_mode=pl.Buffered(1))


def _proj_kernel(*refs, segments, col_chunk, row_chunk, t_chunk):
    if t_chunk:
        x_ref, nw_ref, w_ref, wt_ref, o_ref, ot_ref = refs
    else:
        x_ref, nw_ref, w_ref, o_ref = refs
    for r in range(0, x_ref.shape[0], row_chunk):
        rows = slice(r, r + row_chunk)
        h = _rmsnorm(x_ref[rows, :], nw_ref[...]).astype(BF16)
        out0 = 0
        for col0, width in segments:
            for c in range(0, width, col_chunk):
                o_ref[rows, out0 + c:out0 + c + col_chunk] = jnp.dot(
                    h, w_ref[:, col0 + c:col0 + c + col_chunk], preferred_element_type=F32)
            out0 += width
        if t_chunk:
            zt = _dot_nt(wt_ref[...], h)
            for j in range(row_chunk // t_chunk):
                ot_ref[r // t_chunk + j] = zt[:, j * t_chunk:(j + 1) * t_chunk]


def _proj(x, nw, w_bf, segments=None, wt_bf=None, t_chunk=0):
    n, d = x.shape
    segments = tuple(segments or ((0, w_bf.shape[1]),))
    n_out = sum(width for _, width in segments)
    tm = min(PROJ_ROW_TILE, n)
    in_specs = [pl.BlockSpec((tm, d), lambda i: (i, 0)), _full((1, d)), _resident(w_bf.shape)]
    out_specs = [pl.BlockSpec((tm, n_out), lambda i: (i, 0))]
    out_shape = [jax.ShapeDtypeStruct((n, n_out), F32)]
    args = [x, nw, w_bf]
    if t_chunk:
        t_width = wt_bf.shape[0]
        in_specs.append(_resident(wt_bf.shape))
        out_specs.append(pl.BlockSpec((tm // t_chunk, t_width, t_chunk), lambda i: (i, 0, 0)))
        out_shape.append(jax.ShapeDtypeStruct((n // t_chunk, t_width, t_chunk), F32))
        args.append(wt_bf)
    outs = pl.pallas_call(
        functools.partial(_proj_kernel, segments=segments, col_chunk=2 * MXU_DIM, row_chunk=min(ROW_TILE, tm),
                          t_chunk=t_chunk),
        grid=(n // tm,), in_specs=in_specs, out_specs=out_specs, out_shape=out_shape,
        compiler_params=_cparams(1), name="proj")(*args)
    return outs if t_chunk else outs[0]


def _s5_prep_kernel(lre_ref, lim_ref, ldt_ref, lre16_ref, lim16_ref, ldt16_ref, bre_ref, bim_ref,
                    are_ref, aim_ref, bbre_ref, bbim_ref):
    def disc(lre, lim, ldt):
        dt = jnp.exp(ldt)
        mag = jnp.exp(lre * dt)
        ang = lim * dt
        return mag * jnp.cos(ang), mag * jnp.sin(ang)

    ab_re, ab_im = disc(lre_ref[...], lim_ref[...], ldt_ref[...])
    are_ref[...] = ab_re
    aim_ref[...] = ab_im
    lre, lim = lre16_ref[...], lim16_ref[...]
    ab_re, ab_im = disc(lre, lim, ldt16_ref[...])
    nr, ni = ab_re - 1.0, ab_im
    den = lre * lre + lim * lim
    f_re = (nr * lre + ni * lim) / den
    f_im = (ni * lre - nr * lim) / den
    b_re, b_im = bre_ref[...], bim_ref[...]
    bbre_ref[...] = f_re * b_re - f_im * b_im
    bbim_ref[...] = f_re * b_im + f_im * b_re


def _s5_prep(lam_re, lam_im, log_dt, b_re, b_im):
    g, p = lam_re.shape
    ch = b_re.shape[-1]
    rep = lambda a: jnp.repeat(a, ch, axis=0)
    ldt = log_dt.reshape(g, 1)
    bt = lambda b: jnp.swapaxes(b, 1, 2).reshape(g * ch, p)
    args = (lam_re, lam_im, ldt, rep(lam_re), rep(lam_im), rep(ldt), bt(b_re), bt(b_im))
    return pl.pallas_call(
        _s5_prep_kernel,
        in_specs=[_full(a.shape) for a in args],
        out_specs=[_full((g, p)), _full((g, p)), _full((g * ch, p)), _full((g * ch, p))],
        out_shape=[jax.ShapeDtypeStruct((g, p), F32)] * 2 + [jax.ShapeDtypeStruct((g * ch, p), F32)] * 2,
        name="s5_prep")(*args)


def _block_diag(m, n_blk):
    r, c = m.shape[0] // n_blk, m.shape[1]
    eye = jnp.eye(n_blk, dtype=m.dtype)
    return (m.reshape(n_blk, r, 1, c) * eye[:, None, :, None]).reshape(n_blk * r, n_blk * c)


def _s5_kernel(x_ref, nw_ref, wu_ref, bre_ref, bim_ref, cre_ref, cim_ref, d_ref, gw_ref, gb_ref,
               are_ref, aim_ref, s0re_ref, s0im_ref,
               y_ref, sre_ref, sim_ref, bure_s, buim_s, stre_s, stim_s, *, tb, tl):
    @pl.when(pl.program_id(0) == 0)
    def _():
        stre_s[...] = s0re_ref[...]
        stim_s[...] = s0im_ref[...]

    x = jnp.concatenate([x_ref[:, t, :] for t in range(tl)], axis=0)
    u = jnp.dot(_rmsnorm(x, nw_ref[...]).astype(BF16), wu_ref[...], preferred_element_type=F32)
    ub = u.astype(BF16)

    n_kb, kw, cw = bre_ref.shape
    for kb in range(n_kb):
        uk = ub[:, kb * kw:(kb + 1) * kw]
        bure_s[:, kb * cw:(kb + 1) * cw] = jnp.dot(uk, bre_ref[kb], preferred_element_type=F32)
        buim_s[:, kb * cw:(kb + 1) * cw] = jnp.dot(uk, bim_ref[kb], preferred_element_type=F32)

    n_state = are_ref.shape[1]
    sc = min(S5_SCAN_COLS, n_state)
    for c0 in range(0, n_state, sc):
        cs = slice(c0, c0 + sc)
        a_re = jnp.broadcast_to(are_ref[:, cs], (tb, sc))
        a_im = jnp.broadcast_to(aim_ref[:, cs], (tb, sc))

        def step(t, carry, cs=cs, a_re=a_re, a_im=a_im):
            s_re, s_im = carry
            rows = pl.ds(pl.multiple_of(t * tb, tb), tb)
            n_re = a_re * s_re - a_im * s_im + bure_s[rows, cs]
            n_im = a_re * s_im + a_im * s_re + buim_s[rows, cs]
            bure_s[rows, cs] = n_re
            buim_s[rows, cs] = n_im
            return n_re, n_im

        s_re, s_im = lax.fori_loop(0, tl, step, (stre_s[:, cs], stim_s[:, cs]), unroll=True)
        stre_s[:, cs] = s_re
        stim_s[:, cs] = s_im

    n_ob, ckw, ocw = cre_ref.shape
    ys = []
    for ob in range(n_ob):
        ks = slice(ob * ckw, (ob + 1) * ckw)
        ys.append(_dot(bure_s[:, ks], cre_ref[ob]) - _dot(buim_s[:, ks], cim_ref[ob]))
    y = jnp.concatenate(ys, axis=1) + d_ref[...] * u
    y = jax.nn.gelu(y)
    out = y * jax.nn.sigmoid(_dot(y, gw_ref[...]) + gb_ref[...])
    for t in range(tl):
        y_ref[:, t, :] = out[t * tb:(t + 1) * tb, :]
    sre_ref[...] = stre_s[...]
    sim_ref[...] = stim_s[...]


def _s5_mixer(x3, nw, w_in_bf, bmat_re, bmat_im, cmat_re, cmat_im, d_row, glu_w_bf, glu_b, a_re, a_im,
              s0_re, s0_im, tl):
    b, l, d = x3.shape
    width = d_row.shape[1]
    n_state = a_re.shape[1]
    rows = b * tl
    args = (x3, nw, w_in_bf, bmat_re, bmat_im, cmat_re, cmat_im, d_row, glu_w_bf, glu_b,
            a_re, a_im, s0_re, s0_im)
    in_specs = ([pl.BlockSpec((b, tl, d), lambda i: (0, i, 0)), _full(nw.shape),
                 pl.BlockSpec((d, width), lambda i: (0, 0))] + [_full(a.shape) for a in args[3:]])
    y, s_re, s_im = pl.pallas_call(
        functools.partial(_s5_kernel, tb=b, tl=tl),
        grid=(l // tl,), in_specs=in_specs,
        out_specs=[pl.BlockSpec((b, tl, width), lambda i: (0, i, 0)), _full((b, n_state)), _full((b, n_state))],
        out_shape=[jax.ShapeDtypeStruct((b, l, width), F32),
                   jax.ShapeDtypeStruct((b, n_state), F32), jax.ShapeDtypeStruct((b, n_state), F32)],
        scratch_shapes=[pltpu.VMEM((rows, n_state), F32), pltpu.VMEM((rows, n_state), F32),
                        pltpu.VMEM((b, n_state), F32), pltpu.VMEM((b, n_state), F32)],
        compiler_params=_cparams(1), name="s5_mixer")(*args)
    return y.reshape(b * l, width), s_re, s_im


def _rope_kernel(pos_ref, inv_ref, cc_ref, ss_ref, *, dim_axis):
    ang = pos_ref[...] * inv_ref[...]
    idx = lax.broadcasted_iota(jnp.int32, ang.shape, dim_axis)
    cc_ref[...] = jnp.cos(ang)
    ss_ref[...] = jnp.where(idx < ang.shape[dim_axis] // 2, -jnp.sin(ang), jnp.sin(ang))


def _rope_tables(pos, dk, transposed=False):
    n = pos.shape[0]
    inv = 1.0 / (ROPE_BASE ** jnp.linspace(0.0, 1.0, dk // 2, dtype=F32))
    inv = jnp.concatenate([inv, inv])
    pos = pos.astype(F32)
    pos, inv, shape = (pos.reshape(1, n), inv.reshape(dk, 1), (dk, n)) if transposed else \
                      (pos.reshape(n, 1), inv.reshape(1, dk), (n, dk))
    return pl.pallas_call(
        functools.partial(_rope_kernel, dim_axis=0 if transposed else 1),
        in_specs=[_full(pos.shape), _full(inv.shape)],
        out_specs=[_full(shape), _full(shape)],
        out_shape=[jax.ShapeDtypeStruct(shape, F32)] * 2,
        name="rope_tables")(pos, inv)


def _rotate(t, cc, ss):
    return t * cc + pltpu.roll(t, t.shape[1] // 2, 1) * ss


def _rotate_t(t, cc, ss):
    half = t.shape[0] // 2
    return t * cc + jnp.concatenate([t[half:], t[:half]], axis=0) * ss


def _ret_consts(rows, c):
    lg = jnp.log(1.0 - 2.0 ** (-5.0 - jnp.arange(RET_HEADS, dtype=F32)))
    r = jnp.arange(rows)
    idx, blk = r % c, r // c
    diff = idx[:, None] - idx[None, :]
    same = blk[:, None] == blk[None, :]
    decay = jnp.where((same & (diff >= 0))[None],
                      jnp.exp(jnp.maximum(diff, 0)[None].astype(F32) * lg[:, None, None]), 0.0)
    wide = lambda v: jnp.broadcast_to(v[:, :, None], (RET_HEADS, rows, LANES))
    q_dec = wide(jnp.exp((idx + 1).astype(F32)[None, :] * lg[:, None]))
    k_dec = wide(jnp.exp((c - 1 - idx).astype(F32)[None, :] * lg[:, None]))
    chunk_dec = jnp.broadcast_to(jnp.exp(c * lg)[:, None, None], (RET_HEADS, SUBLANES, LANES))
    return decay, q_dec, k_dec, chunk_dec


def _ret_gate(o, g):
    o = o * lax.rsqrt(jnp.mean(o * o, axis=-1, keepdims=True) + NORM_EPS)
    return jax.nn.silu(g) * o


def _ret_long_kernel(q_ref, kt_ref, v_ref, g_ref, cc_ref, ss_ref, cct_ref, sst_ref, dec_ref, qd_ref, kdt_ref,
                     cd_ref, s0_ref, y_ref, so_ref, *, c, hp):
    dk = q_ref.shape[1] // hp
    scale = dk ** -0.5

    def chunk(i, states):
        rows = pl.ds(pl.multiple_of(i * c, c), c)
        cc, ss = cc_ref[rows, :], ss_ref[rows, :]
        cct, sst = cct_ref[i], sst_ref[i]
        new_states = []
        for j in range(hp):
            cols = slice(j * dk, (j + 1) * dk)
            q = _rotate(q_ref[rows, cols], cc, ss)
            kt = _rotate_t(kt_ref[i, cols, :], cct, sst) * scale
            v = v_ref[rows, cols]
            scores = _dot(q, kt) * dec_ref[j]
            o = _dot(scores, v) + _dot(q * qd_ref[j], states[j])
            new_states.append(cd_ref[j, 0:1, :] * states[j] + _dot(kt * kdt_ref[j], v))
            y_ref[rows, cols] = _ret_gate(o, g_ref[rows, cols]).astype(y_ref.dtype)
        return tuple(new_states)

    states = lax.fori_loop(0, q_ref.shape[0] // c, chunk, tuple(s0_ref[0, j] for j in range(hp)), unroll=4)
    for j in range(hp):
        so_ref[0, j] = states[j]


def _ret_short_kernel(q_ref, k_ref, v_ref, g_ref, cc_ref, ss_ref, dec_ref, qd_ref, kd_ref, cd_ref, s0_ref,
                      y_ref, so_ref, qd_s, oi_s, *, l):
    rows, dk = q_ref.shape
    scale = dk ** -0.5
    cc, ss = cc_ref[...], ss_ref[...]
    q = _rotate(q_ref[...], cc, ss)
    k = _rotate(k_ref[...], cc, ss) * scale
    v = v_ref[...]
    vb = v.astype(BF16)
    o_intra = _dot(_dot_nt(q, k) * dec_ref[0], vb)
    qd_s[...] = q * qd_ref[0]
    kdt = (k * kd_ref[0]).T
    cd = cd_ref[0, 0:1, :]
    win = 16
    per = win // l
    lane = lax.broadcasted_iota(jnp.int32, (dk, rows), 1)
    wrow = lax.broadcasted_iota(jnp.int32, (win, dk), 0)

    def window(w, carry):
        wr = pl.ds(pl.multiple_of(w * win, win), win)
        qw = qd_s[wr, :].astype(BF16)
        oi = jnp.zeros((win, dk), F32)
        for j in range(per):
            b = w * per + j
            s = s0_ref[b, 0]
            oi = jnp.where(wrow // l == j, jnp.dot(qw, s.astype(BF16), preferred_element_type=F32), oi)
            so_ref[b, 0] = cd * s + jnp.dot(jnp.where(lane // l == b, kdt, 0.0).astype(BF16), vb,
                                            preferred_element_type=F32)
        oi_s[wr, :] = oi
        return carry

    lax.fori_loop(0, rows // win, window, 0, unroll=True)
    y_ref[...] = _ret_gate(o_intra + oi_s[...], g_ref[...]).astype(y_ref.dtype)


def _retention(x, nw, w_in_bf, w_kt_bf, col0, pos, s0, n_seq, l):
    n = x.shape[0]
    dk = s0.shape[-1]
    width = RET_HEADS * dk
    out_shape = [jax.ShapeDtypeStruct((n, width), _mix_dtype(l)), jax.ShapeDtypeStruct(s0.shape, F32)]
    cspec = lambda a, hp: pl.BlockSpec((hp,) + a.shape[1:], lambda b, h: (h, 0, 0))
    cc, ss = _rope_tables(pos, dk)
    if l % RET_CHUNK == 0:
        c, hp = RET_CHUNK, RET_HEADS_PER_STEP
        n_hb = RET_HEADS // hp
        z, kt = _proj(x, nw, w_in_bf, ((col0, width), (col0 + 2 * width, 2 * width)), wt_bf=w_kt_bf, t_chunk=c)
        chunked = lambda t: jnp.swapaxes(t.reshape(dk, l // c, c), 0, 1)
        cct, sst = map(chunked, _rope_tables(pos, dk, transposed=True))
        decay, q_dec, k_dec, chunk_dec = _ret_consts(c, c)
        k_dec_t = jnp.swapaxes(k_dec, 1, 2)
        zspec = lambda field: pl.BlockSpec((l, hp * dk), lambda b, h: (b, field * n_hb + h))
        sspec = pl.BlockSpec((1, hp, dk, dk), lambda b, h: (b, h, 0, 0))
        return pl.pallas_call(
            functools.partial(_ret_long_kernel, c=c, hp=hp), grid=(n_seq, n_hb),
            in_specs=[zspec(0), pl.BlockSpec((l // c, hp * dk, c), lambda b, h: (b, h, 0)), zspec(1), zspec(2),
                      _full(cc.shape), _full(ss.shape), _full(cct.shape), _full(sst.shape),
                      cspec(decay, hp), cspec(q_dec, hp), cspec(k_dec_t, hp), cspec(chunk_dec, hp), sspec],
            out_specs=[pl.BlockSpec((l, hp * dk), lambda b, h: (b, h)), sspec],
            out_shape=out_shape, compiler_params=_cparams(2), name="retention")(
                z, kt, z, z, cc, ss, cct, sst, decay, q_dec, k_dec_t, chunk_dec, s0)
    rows = LANES
    per_blk = rows // l
    z = _proj(x, nw, w_in_bf, segments=((col0, 4 * width),))
    decay, q_dec, k_dec, chunk_dec = _ret_consts(rows, l)
    zspec = lambda field: pl.BlockSpec((rows, dk), lambda b, h: (b, field * RET_HEADS + h))
    tspec = pl.BlockSpec((rows, dk), lambda b, h: (b, 0))
    sspec = pl.BlockSpec((per_blk, 1, dk, dk), lambda b, h: (b, h, 0, 0))
    return pl.pallas_call(
        functools.partial(_ret_short_kernel, l=l), grid=(n_seq // per_blk, RET_HEADS),
        in_specs=[zspec(0), zspec(1), zspec(2), zspec(3), tspec, tspec,
                  cspec(decay, 1), cspec(q_dec, 1), cspec(k_dec, 1), cspec(chunk_dec, 1), sspec],
        out_specs=[pl.BlockSpec((rows, dk), lambda b, h: (b, h)), sspec],
        out_shape=out_shape,
        scratch_shapes=[pltpu.VMEM((rows, dk), F32), pltpu.VMEM((rows, dk), F32)],
        compiler_params=_cparams(2), name="retention")(z, z, z, z, cc, ss, decay, q_dec, k_dec, chunk_dec, s0)


def _cumsum_rows(x, period):
    row = lax.broadcasted_iota(jnp.int32, x.shape, 0)
    s = 1
    while s < period:
        x = x + jnp.where(row % period >= s, pltpu.roll(x, s, 0), 0.0)
        s *= 2
    return x


def _row_of_block(x, period, offset):
    n, w = x.shape
    x3 = x.reshape(n // period, period, w)
    return jnp.broadcast_to(x3[:, offset:offset + 1, :], x3.shape).reshape(n, w)


def _hgrn_gates(fl, lg, layer):
    e = jnp.exp(lg - jnp.max(lg, axis=0, keepdims=True))
    soft = e / jnp.sum(e, axis=0, keepdims=True)
    cum = soft[0:1]
    for i in range(1, layer + 1):
        cum = cum + soft[i:i + 1]
    lb = cum - soft[0:1]
    forget = lb + (1.0 - lb) * jax.nn.sigmoid(fl)
    return 1.0 - forget, jnp.log2(forget)


def _hgrn_out(o, g, nw):
    o = o * lax.rsqrt(jnp.mean(o * o, axis=-1, keepdims=True) + NORM_EPS) * nw
    return o * jax.nn.silu(g)


def _hgrn_levels(c):
    levels, half = [], HG_BLOCK
    while half < c:
        levels.append(half)
        half *= 2
    return levels


def _hgrn_long_kernel(q_ref, f_ref, v_ref, g_ref, lbl_ref, nw_ref, s0_ref, y_ref, so_ref, *, c, layer, hp):
    dk = q_ref.shape[1] // hp
    nw = nw_ref[...]
    nv = c // SUBLANES
    bv = HG_BLOCK // SUBLANES
    levels = _hgrn_levels(c)
    row = lax.broadcasted_iota(jnp.int32, (c, c), 0)
    col = lax.broadcasted_iota(jnp.int32, (c, c), 1)
    lvl = jnp.where((row // HG_BLOCK == col // HG_BLOCK) & (col <= row), 0, -1)
    for li, half in enumerate(levels, 1):
        lvl = jnp.where((row // (2 * half) == col // (2 * half)) & (row % (2 * half) >= half)
                        & (col % (2 * half) < half), li, lvl)
    sub = lax.broadcasted_iota(jnp.int32, (SUBLANES, dk), 0)
    zeros = jnp.zeros((SUBLANES, dk), F32)
    cat = lambda slabs: jnp.concatenate(slabs, axis=0)

    def one_head(rows, cols, s):
        q, v = q_ref[rows, cols], v_ref[rows, cols]
        k, lf = _hgrn_gates(f_ref[rows, cols], lbl_ref[:, cols], layer)
        slabs = lambda a: [a[j * SUBLANES:(j + 1) * SUBLANES, :] for j in range(nv)]
        qs, ks = slabs(q), slabs(k)
        loc, run = [], [jnp.zeros((1, dk), F32)]
        for x in slabs(lf):
            step = 1
            while step < SUBLANES:
                x = x + jnp.where(sub >= step, pltpu.roll(x, step, 0), 0.0)
                step *= 2
            loc.append(x)
            run.append(run[-1] + x[SUBLANES - 1:, :])
        d0 = [loc[j] + (run[j] - run[j - j % bv]) if j % bv else loc[j] for j in range(nv)]
        scores = jnp.where(lvl == 0, _dot_nt(cat([qs[j] * jnp.exp2(d0[j]) for j in range(nv)]),
                                             cat([ks[j] * jnp.exp2(-d0[j]) for j in range(nv)])), 0.0)
        for li, half in enumerate(levels, 1):
            hv = half // SUBLANES
            q_side, k_side = [], []
            for j in range(nv):
                anchor = run[j - j % (2 * hv) + hv]
                if j % (2 * hv) >= hv:
                    q_side.append(qs[j] * jnp.exp2(loc[j] + (run[j] - anchor)))
                    k_side.append(zeros)
                else:
                    q_side.append(zeros)
                    k_side.append(ks[j] * jnp.exp2((anchor - run[j]) - loc[j]))
            scores = jnp.where(lvl == li, _dot_nt(cat(q_side), cat(k_side)), scores)
        o = _dot(scores, v) + _dot(cat([qs[j] * jnp.exp2(loc[j] + run[j]) for j in range(nv)]), s)
        k_hat = cat([ks[j] * jnp.exp2((run[nv] - run[j]) - loc[j]) for j in range(nv)])
        d_col = jnp.exp2(jnp.broadcast_to(run[nv], (dk, dk)).T)
        y_ref[rows, cols] = _hgrn_out(o, g_ref[rows, cols], nw).astype(y_ref.dtype)
        return d_col * s + _dot(k_hat.T, v)

    def chunk(i, states):
        rows = pl.ds(pl.multiple_of(i * c, c), c)
        return tuple(one_head(rows, slice(hd * dk, (hd + 1) * dk), states[hd]) for hd in range(hp))

    states = lax.fori_loop(0, q_ref.shape[0] // c, chunk, tuple(s0_ref[0, j] for j in range(hp)), unroll=4)
    for j in range(hp):
        so_ref[0, j] = states[j]


def _hgrn_short_kernel(q_ref, f_ref, v_ref, g_ref, lbl_ref, nw_ref, s0_ref, y_ref, so_ref, qd_s, oi_s,
                       *, l, layer):
    rows, dk = q_ref.shape
    q, v = q_ref[...], v_ref[...]
    vb = v.astype(BF16)
    k, lf = _hgrn_gates(f_ref[...], lbl_ref[...], layer)
    b = _cumsum_rows(lf, l)
    row = lax.broadcasted_iota(jnp.int32, (rows, rows), 0)
    col = lax.broadcasted_iota(jnp.int32, (rows, rows), 1)
    qe = q * jnp.exp2(b)
    scores = jnp.where((row // l == col // l) & (col <= row), _dot_nt(qe, k * jnp.exp2(-b)), 0.0)
    o_intra = _dot(scores, vb)
    qd_s[...] = qe
    b_last = _row_of_block(b, l, l - 1)
    kht = (k * jnp.exp2(b_last - b)).T
    dect = jnp.exp2(b_last).T
    win = 16
    per = win // l
    lane = lax.broadcasted_iota(jnp.int32, (dk, rows), 1)
    wrow = lax.broadcasted_iota(jnp.int32, (win, dk), 0)

    def window(w, carry):
        wr = pl.ds(pl.multiple_of(w * win, win), win)
        qw = qd_s[wr, :].astype(BF16)
        oi = jnp.zeros((win, dk), F32)
        for j in range(per):
            bi = w * per + j
            s = s0_ref[bi, 0]
            oi = jnp.where(wrow // l == j, jnp.dot(qw, s.astype(BF16), preferred_element_type=F32), oi)
            d_col = jnp.sum(jnp.where(lane == bi * l, dect, 0.0), axis=1, keepdims=True)
            so_ref[bi, 0] = d_col * s + jnp.dot(jnp.where(lane // l == bi, kht, 0.0).astype(BF16), vb,
                                                preferred_element_type=F32)
        oi_s[wr, :] = oi
        return carry

    lax.fori_loop(0, rows // win, window, 0, unroll=True)
    y_ref[...] = _hgrn_out(o_intra + oi_s[...], g_ref[...], nw_ref[...]).astype(y_ref.dtype)


def _hgrn(z, lb_logits, norm_w, s0, n_seq, l, layer):
    n = z.shape[0]
    dk = s0.shape[-1]
    depth = lb_logits.shape[0]
    if l % HG_CHUNK == 0:
        rows, n_blk, hp = l, n_seq, HG_HEADS_PER_STEP
        kern = functools.partial(_hgrn_long_kernel, c=HG_CHUNK, layer=layer, hp=hp)
        sspec = pl.BlockSpec((1, hp, dk, dk), lambda b, h: (b, h, 0, 0))
        scratch = []
    else:
        assert HG_BLOCK % l == 0
        rows, hp = LANES, 1
        per_blk = rows // l
        n_blk = n_seq // per_blk
        kern = functools.partial(_hgrn_short_kernel, l=l, layer=layer)
        sspec = pl.BlockSpec((per_blk, 1, dk, dk), lambda b, h: (b, h, 0, 0))
        scratch = [pltpu.VMEM((rows, dk), F32), pltpu.VMEM((rows, dk), F32)]
    n_hb = HG_HEADS // hp
    zspec = lambda rows, field: pl.BlockSpec((rows, hp * dk), lambda b, h: (b, field * n_hb + h))
    y, s_new = pl.pallas_call(
        kern, grid=(n_blk, n_hb),
        in_specs=[zspec(rows, 0), zspec(rows, 1), zspec(rows, 2), zspec(rows, 3),
                  pl.BlockSpec((depth, hp * dk), lambda b, h: (0, h)), _full((1, dk)), sspec],
        out_specs=[pl.BlockSpec((rows, hp * dk), lambda b, h: (b, h)), sspec],
        out_shape=[jax.ShapeDtypeStruct((n, HG_HEADS * dk), _mix_dtype(l)), jax.ShapeDtypeStruct(s0.shape, F32)],
        scratch_shapes=scratch,
        compiler_params=_cparams(2), name="hgrn")(z, z, z, z, lb_logits, norm_w, s0)
    return y, s_new


def _ffn_kernel(*refs, n_parts, long_mode, final_norm, l):
    x_ref, parts = refs[0], refs[1:1 + n_parts]
    (wo_ref, nw_ref, wg_ref, wu_ref, cw_ref, cb_ref, wd_ref, fnw_ref, buf_ref, o_ref, bufo_ref,
     act_s) = refs[1 + n_parts:13 + n_parts]
    d_ff = wg_ref.shape[1]
    if long_mode:
        tail_s = refs[13 + n_parts]
        tm = x_ref.shape[1]
        rc = min(ROW_TILE, tm)
        load = lambda ref, r0: ref[0, r0:r0 + rc, :]

        @pl.when(pl.program_id(1) == 0)
        def _():
            tail_s[...] = buf_ref[0]
    else:
        nb = x_ref.shape[0]
        tm = rc = l * nb
        load = lambda ref, r0: jnp.concatenate([ref[:, t, :] for t in range(l)], axis=0)

    xs = []
    for r0 in range(0, tm, rc):
        mix = jnp.concatenate([load(p, r0).astype(BF16) for p in parts], axis=1)
        x = load(x_ref, r0) + jnp.dot(mix, wo_ref[...], preferred_element_type=F32)
        xs.append(x)
        h = _rmsnorm(x, nw_ref[...]).astype(BF16)
        for c0 in range(0, d_ff, FF_CHUNK):
            cs = slice(c0, c0 + FF_CHUNK)
            g = jnp.dot(h, wg_ref[:, cs], preferred_element_type=F32)
            up = jnp.dot(h, wu_ref[:, cs], preferred_element_type=F32)
            if long_mode:
                row = lax.broadcasted_iota(jnp.int32, g.shape, 0)
                p1 = jnp.where(row < 1, tail_s[1:2, cs], pltpu.roll(g, 1, 0))
                p2 = jnp.where(row < 2, jnp.where(row == 0, tail_s[0:1, cs], tail_s[1:2, cs]),
                               pltpu.roll(g, 2, 0))
                tail_s[:, cs] = g[rc - (CONV_W - 1):, :]
            else:
                b0, b1 = buf_ref[:, 0, cs], buf_ref[:, 1, cs]
                p1 = jnp.concatenate([b1, g[:tm - nb]], axis=0)
                p2 = jnp.concatenate([b0, b1, g[:tm - 2 * nb]], axis=0)
                bufo_ref[:, 0, cs] = g[tm - 2 * nb:tm - nb]
                bufo_ref[:, 1, cs] = g[tm - nb:]
            conv = cb_ref[:, cs] + cw_ref[0:1, cs] * p2
            conv = conv + cw_ref[1:2, cs] * p1
            conv = conv + cw_ref[2:3, cs] * g
            act_s[r0:r0 + rc, cs] = (jax.nn.silu(conv) * up).astype(BF16)
    y = jnp.concatenate(xs, axis=0) + jnp.dot(act_s[...], wd_ref[...], preferred_element_type=F32)
    if final_norm:
        y = _rmsnorm(y, fnw_ref[...])
    if long_mode:
        o_ref[0] = y
        bufo_ref[0] = tail_s[...]
    else:
        for t in range(l):
            o_ref[:, t, :] = y[t * nb:(t + 1) * nb]


def _ffn(x, parts, layer, wo, nw, wg, wu, conv_w, conv_b, wd, fnw, buf, n_seq, l, final_norm):
    assert CONV_W == 3 and l >= CONV_W - 1
    n, d = x.shape
    d_ff = wg.shape[2]
    weights = (wo, nw, wg, wu, conv_w, conv_b, wd, fnw)
    stacked = (wg, wu, wd)

    def wspec(w, single_buffer):
        kw = dict(pipeline_mode=pl.Buffered(1)) if single_buffer else {}
        if any(w is s for s in stacked):
            return pl.BlockSpec((None,) + w.shape[1:], lambda *_: (layer, 0, 0), **kw)
        return pl.BlockSpec(w.shape, lambda *_: (0,) * w.ndim, **kw)
    kern = functools.partial(_ffn_kernel, n_parts=len(parts), final_norm=final_norm, l=l)
    buf_shape = jax.ShapeDtypeStruct((n_seq, CONV_W - 1, d_ff), F32)
    rows3 = lambda a: a.reshape(n_seq, l, a.shape[1])
    if l % ROW_TILE == 0:
        tm = FFN_ROW_TILE
        assert l % tm == 0
        rspec = lambda w: pl.BlockSpec((1, tm, w), lambda b, i: (b, i, 0))
        bspec = pl.BlockSpec((1, CONV_W - 1, d_ff), lambda b, i: (b, 0, 0))
        y, buf_new = pl.pallas_call(
            functools.partial(kern, long_mode=True),
            grid=(n_seq, l // tm),
            in_specs=([rspec(d)] + [rspec(p.shape[1]) for p in parts]
                      + [wspec(w, True) for w in weights] + [bspec]),
            out_specs=[rspec(d), bspec],
            out_shape=[jax.ShapeDtypeStruct((n_seq, l, d), F32), buf_shape],
            scratch_shapes=[pltpu.VMEM((tm, d_ff), BF16), pltpu.VMEM((CONV_W - 1, d_ff), F32)],
            compiler_params=_cparams(2), name="ffn")(rows3(x), *map(rows3, parts), *weights, buf)
        return y.reshape(n, d), buf_new
    y, buf_new = pl.pallas_call(
        functools.partial(kern, long_mode=False),
        grid=(1,),
        in_specs=([_full((n_seq, l, d))] + [_full((n_seq, l, p.shape[1])) for p in parts]
                  + [wspec(w, False) for w in weights] + [_full(buf.shape)]),
        out_specs=[_full((n_seq, l, d)), _full(buf.shape)],
        out_shape=[jax.ShapeDtypeStruct((n_seq, l, d), F32), buf_shape],
        scratch_shapes=[pltpu.VMEM((n, d_ff), BF16)],
        compiler_params=_cparams(1), name="ffn")(
            rows3(x), *map(rows3, parts), *weights, buf)
    return y.reshape(n, d), buf_new


def _trunk(x3, pos, s5_re, s5_im, ret, hg, conv, p, s5_tl):
    b, l, d = x3.shape
    n = b * l
    x = x3.reshape(n, d)
    depth = p['norm_mix'].shape[0]
    new_re, new_im, new_ret, new_hg, new_conv = [], [], [], [], []
    for layer in range(depth):
        j = layer // 2
        nw = p['norm_mix'][layer].reshape(1, d)
        if layer % 2 == 0:
            s5w = p['s5'][j]
            width = s5w['d'].shape[1]
            y_s5, r, im = _s5_mixer(x.reshape(b, l, d), nw, p['w_in_ab'][j], s5w['bmat_re'], s5w['bmat_im'],
                                    s5w['cmat_re'], s5w['cmat_im'], s5w['d'], s5w['glu_w'], s5w['glu_b'],
                                    s5w['a_re'], s5w['a_im'], s5_re[j].reshape(b, -1), s5_im[j].reshape(b, -1),
                                    s5_tl)
            y_ret, st = _retention(x, nw, p['w_in_ab'][j], p['w_kt'][j], width, pos, ret[j], b, l)
            parts, w_out = [y_s5, y_ret], p['w_out_ab'][j]
            new_re.append(r.reshape(s5_re.shape[1:]))
            new_im.append(im.reshape(s5_im.shape[1:]))
            new_ret.append(st)
        else:
            z = _proj(x, nw, p['w_in_c'][j])
            y_hg, st = _hgrn(z, p['hg_lb_logits'], p['hg_norm_w'][j].reshape(1, -1), hg[j], b, l, layer)
            parts, w_out = [y_hg], p['w_out_c'][j]
            new_hg.append(st)
        x, buf = _ffn(x, parts, layer, w_out, p['norm_ffn'][layer].reshape(1, d), p['ffn_w_gate'],
                      p['ffn_w_up'], p['ffn_conv_w'][layer], p['ffn_conv_b'][layer].reshape(1, -1),
                      p['ffn_w_down'], p['norm_final'].reshape(1, d), conv[layer], b, l,
                      final_norm=(layer == depth - 1))
        new_conv.append(buf)
    return (x.reshape(b, l, d), jnp.stack(new_re), jnp.stack(new_im), jnp.stack(new_ret),
            jnp.stack(new_hg), jnp.stack(new_conv))


def kernel(x_prompt, x_sample, state_s5_re, state_s5_im, state_ret, state_hgrn, state_ffn_conv, pos_sample, norm_mix, norm_ffn, norm_final, w_in_ab, s5_lam_re, s5_lam_im, s5_log_dt, s5_b_re, s5_b_im, s5_c_re, s5_c_im, s5_d, s5_glu_w, s5_glu_b, w_out_ab, w_in_c, hg_lb_logits, hg_norm_w, w_out_c, ffn_w_gate, ffn_w_up, ffn_conv_w, ffn_conv_b, ffn_w_down):
    n_ab, n_grp, n_st = s5_lam_re.shape
    ch = s5_b_re.shape[-1]
    width = n_grp * ch
    ret_width = state_ret.shape[2] * state_ret.shape[3]
    grp_per_blk = MXU_DIM // ch
    n_blk = n_grp // grp_per_blk
    s5 = []
    for j in range(n_ab):
        a_re, a_im, bb_re, bb_im = _s5_prep(s5_lam_re[j], s5_lam_im[j], s5_log_dt[j], s5_b_re[j], s5_b_im[j])
        bmat = lambda m: jnp.stack([_block_diag(blk, grp_per_blk)
                                    for blk in m.reshape(n_blk, grp_per_blk * ch, n_st)]).astype(BF16)
        cmat = lambda c: jnp.stack([_block_diag(blk, grp_per_blk).T for blk in
                                    c.reshape(n_blk, grp_per_blk * ch, n_st)]).astype(BF16)
        s5.append(dict(bmat_re=bmat(bb_re), bmat_im=bmat(bb_im),
                       cmat_re=cmat(s5_c_re[j]), cmat_im=cmat(s5_c_im[j]),
                       d=s5_d[j].reshape(1, width), glu_w=s5_glu_w[j].astype(BF16),
                       glu_b=s5_glu_b[j].reshape(1, width),
                       a_re=a_re.reshape(1, n_grp * n_st), a_im=a_im.reshape(1, n_grp * n_st)))
    per_layer_bf16 = lambda w: [w[i].astype(BF16) for i in range(w.shape[0])]
    p = dict(norm_mix=norm_mix, norm_ffn=norm_ffn, norm_final=norm_final, s5=s5,
             w_in_ab=per_layer_bf16(w_in_ab), w_out_ab=per_layer_bf16(w_out_ab),
             w_kt=[w_in_ab[j][:, width + ret_width:width + 2 * ret_width].T.astype(BF16) for j in range(n_ab)],
             w_in_c=per_layer_bf16(w_in_c), hg_lb_logits=hg_lb_logits, hg_norm_w=hg_norm_w,
             w_out_c=per_layer_bf16(w_out_c), ffn_w_gate=ffn_w_gate.astype(BF16),
             ffn_w_up=ffn_w_up.astype(BF16), ffn_conv_w=ffn_conv_w, ffn_conv_b=ffn_conv_b,
             ffn_w_down=ffn_w_down.astype(BF16))

    bp, lp, _ = x_prompt.shape
    z_s5 = jnp.zeros((n_ab, bp) + state_s5_re.shape[2:], F32)
    z_ret = jnp.zeros((n_ab, bp) + state_ret.shape[2:], F32)
    z_hg = jnp.zeros((state_hgrn.shape[0], bp) + state_hgrn.shape[2:], F32)
    z_conv = jnp.zeros((norm_mix.shape[0], bp) + state_ffn_conv.shape[2:], F32)
    outs_p = _trunk(x_prompt, jnp.arange(lp, dtype=jnp.int32), z_s5, z_s5, z_ret, z_hg, z_conv, p,
                    s5_tl=ROW_TILE // bp)
    bs, ls, _ = x_sample.shape
    pos_s = (pos_sample[:, None] + jnp.arange(ls, dtype=jnp.int32)[None, :]).reshape(-1)
    outs_s = _trunk(x_sample, pos_s, state_s5_re, state_s5_im, state_ret, state_hgrn, state_ffn_conv, p,
                    s5_tl=ls)
    return (outs_p[0], outs_s[0]) + outs_p[1:] + outs_s[1:]
```

```python
import functools

import jax
import jax.numpy as jnp
from jax import lax
from jax.experimental import pallas as pl
from jax.experimental.pallas import tpu as pltpu

F32 = jnp.float32
BF16 = jnp.bfloat16

NORM_EPS = 1e-6
ROPE_BASE = 10000.0
S5_GROUP_CH = 16
RET_HEADS = 4
RET_CHUNK = 128
HG_HEADS = 8
HG_BLOCK = 16
HG_CHUNK = 128
RET_HEADS_PER_STEP = 4
HG_HEADS_PER_STEP = 4
SHORT_HEADS_PER_STEP = 2
CONV_W = 3

LANES = 128
SUBLANES = 8
MXU_DIM = 256
VMEM_LIMIT_BYTES = 56 * 1024 * 1024
ROW_TILE = 512
FFN_ROW_TILE = 1024
PROJ_ROW_TILE = 1024
S5_ROW_TILE = 1024
FF_CHUNK = 256
S5_SCAN_COLS = 512


def _cparams(n_grid_dims):
    return pltpu.CompilerParams(dimension_semantics=("arbitrary",) * n_grid_dims,
                                vmem_limit_bytes=VMEM_LIMIT_BYTES)


def _dot(a, b):
    return jnp.dot(a.astype(BF16), b.astype(BF16), preferred_element_type=F32)


def _dot_nt(a, b):
    return lax.dot_general(a.astype(BF16), b.astype(BF16), (((1,), (1,)), ((), ())),
                           preferred_element_type=F32)


def _rmsnorm(x, w):
    return x * lax.rsqrt(jnp.mean(x * x, axis=-1, keepdims=True) + NORM_EPS) * w


def _full(shape):
    nd = len(shape)
    return pl.BlockSpec(shape, lambda *_: (0,) * nd)


def _mix_dtype(l):
    return BF16 if l % ROW_TILE == 0 else F32


def _resident(shape):
    nd = len(shape)
    return pl.BlockSpec(shape, lambda *_: (0,) * nd, pipeline_mode=pl.Buffered(1))


def _proj_kernel(*refs, segments, col_chunk, row_chunk, t_chunk):
    if t_chunk:
        x_ref, nw_ref, w_ref, wt_ref, o_ref, ot_ref = refs
    else:
        x_ref, nw_ref, w_ref, o_ref = refs
    for r in range(0, x_ref.shape[0], row_chunk):
        rows = slice(r, r + row_chunk)
        h = _rmsnorm(x_ref[rows, :], nw_ref[...]).astype(BF16)
        out0 = 0
        for col0, width in segments:
            for c in range(0, width, col_chunk):
                o_ref[rows, out0 + c:out0 + c + col_chunk] = jnp.dot(
                    h, w_ref[:, col0 + c:col0 + c + col_chunk], preferred_element_type=F32)
            out0 += width
        if t_chunk:
            zt = _dot_nt(wt_ref[...], h)
            for j in range(row_chunk // t_chunk):
                ot_ref[r // t_chunk + j] = zt[:, j * t_chunk:(j + 1) * t_chunk]


def _proj(x, nw, w_bf, segments=None, wt_bf=None, t_chunk=0):
    n, d = x.shape
    segments = tuple(segments or ((0, w_bf.shape[1]),))
    n_out = sum(width for _, width in segments)
    tm = min(PROJ_ROW_TILE, n)
    in_specs = [pl.BlockSpec((tm, d), lambda i: (i, 0)), _full((1, d)), _resident(w_bf.shape)]
    out_specs = [pl.BlockSpec((tm, n_out), lambda i: (i, 0))]
    out_shape = [jax.ShapeDtypeStruct((n, n_out), F32)]
    args = [x, nw, w_bf]
    if t_chunk:
        t_width = wt_bf.shape[0]
        in_specs.append(_resident(wt_bf.shape))
        out_specs.append(pl.BlockSpec((tm // t_chunk, t_width, t_chunk), lambda i: (i, 0, 0)))
        out_shape.append(jax.ShapeDtypeStruct((n // t_chunk, t_width, t_chunk), F32))
        args.append(wt_bf)
    outs = pl.pallas_call(
        functools.partial(_proj_kernel, segments=segments, col_chunk=2 * MXU_DIM, row_chunk=min(ROW_TILE, tm),
                          t_chunk=t_chunk),
        grid=(n // tm,), in_specs=in_specs, out_specs=out_specs, out_shape=out_shape,
        compiler_params=_cparams(1), name="proj")(*args)
    return outs if t_chunk else outs[0]


def _s5_prep_kernel(lre_ref, lim_ref, ldt_ref, lre16_ref, lim16_ref, ldt16_ref, bre_ref, bim_ref,
                    are_ref, aim_ref, bbre_ref, bbim_ref):
    def disc(lre, lim, ldt):
        dt = jnp.exp(ldt)
        mag = jnp.exp(lre * dt)
        ang = lim * dt
        return mag * jnp.cos(ang), mag * jnp.sin(ang)

    ab_re, ab_im = disc(lre_ref[...], lim_ref[...], ldt_ref[...])
    are_ref[...] = ab_re
    aim_ref[...] = ab_im
    lre, lim = lre16_ref[...], lim16_ref[...]
    ab_re, ab_im = disc(lre, lim, ldt16_ref[...])
    nr, ni = ab_re - 1.0, ab_im
    den = lre * lre + lim * lim
    f_re = (nr * lre + ni * lim) / den
    f_im = (ni * lre - nr * lim) / den
    b_re, b_im = bre_ref[...], bim_ref[...]
    bbre_ref[...] = f_re * b_re - f_im * b_im
    bbim_ref[...] = f_re * b_im + f_im * b_re


def _s5_prep(lam_re, lam_im, log_dt, b_re, b_im):
    g, p = lam_re.shape
    ch = b_re.shape[-1]
    rep = lambda a: jnp.repeat(a, ch, axis=0)
    ldt = log_dt.reshape(g, 1)
    bt = lambda b: jnp.swapaxes(b, 1, 2).reshape(g * ch, p)
    args = (lam_re, lam_im, ldt, rep(lam_re), rep(lam_im), rep(ldt), bt(b_re), bt(b_im))
    return pl.pallas_call(
        _s5_prep_kernel,
        in_specs=[_full(a.shape) for a in args],
        out_specs=[_full((g, p)), _full((g, p)), _full((g * ch, p)), _full((g * ch, p))],
        out_shape=[jax.ShapeDtypeStruct((g, p), F32)] * 2 + [jax.ShapeDtypeStruct((g * ch, p), F32)] * 2,
        name="s5_prep")(*args)


def _block_diag(m, n_blk):
    r, c = m.shape[0] // n_blk, m.shape[1]
    eye = jnp.eye(n_blk, dtype=m.dtype)
    return (m.reshape(n_blk, r, 1, c) * eye[:, None, :, None]).reshape(n_blk * r, n_blk * c)


def _s5_kernel(x_ref, nw_ref, wu_ref, bre_ref, bim_ref, cre_ref, cim_ref, d_ref, gw_ref, gb_ref,
               are_ref, aim_ref, s0re_ref, s0im_ref,
               y_ref, sre_ref, sim_ref, bure_s, buim_s, stre_s, stim_s, *, tb, tl):
    @pl.when(pl.program_id(0) == 0)
    def _():
        stre_s[...] = s0re_ref[...]
        stim_s[...] = s0im_ref[...]

    x = jnp.concatenate([x_ref[:, t, :] for t in range(tl)], axis=0)
    u = jnp.dot(_rmsnorm(x, nw_ref[...]).astype(BF16), wu_ref[...], preferred_element_type=F32)
    ub = u.astype(BF16)

    n_kb, kw, cw = bre_ref.shape
    for kb in range(n_kb):
        uk = ub[:, kb * kw:(kb + 1) * kw]
        bure_s[:, kb * cw:(kb + 1) * cw] = jnp.dot(uk, bre_ref[kb], preferred_element_type=F32)
        buim_s[:, kb * cw:(kb + 1) * cw] = jnp.dot(uk, bim_ref[kb], preferred_element_type=F32)

    n_state = are_ref.shape[1]
    sc = min(S5_SCAN_COLS, n_state)
    for c0 in range(0, n_state, sc):
        cs = slice(c0, c0 + sc)
        a_re = jnp.broadcast_to(are_ref[:, cs], (tb, sc))
        a_im = jnp.broadcast_to(aim_ref[:, cs], (tb, sc))

        def step(t, carry, cs=cs, a_re=a_re, a_im=a_im):
            s_re, s_im = carry
            rows = pl.ds(pl.multiple_of(t * tb, tb), tb)
            n_re = a_re * s_re - a_im * s_im + bure_s[rows, cs]
            n_im = a_re * s_im + a_im * s_re + buim_s[rows, cs]
            bure_s[rows, cs] = n_re
            buim_s[rows, cs] = n_im
            return n_re, n_im

        s_re, s_im = lax.fori_loop(0, tl, step, (stre_s[:, cs], stim_s[:, cs]), unroll=True)
        stre_s[:, cs] = s_re
        stim_s[:, cs] = s_im

    n_ob, ckw, ocw = cre_ref.shape
    ys = []
    for ob in range(n_ob):
        ks = slice(ob * ckw, (ob + 1) * ckw)
        ys.append(_dot(bure_s[:, ks], cre_ref[ob]) - _dot(buim_s[:, ks], cim_ref[ob]))
    y = jnp.concatenate(ys, axis=1) + d_ref[...] * u
    y = jax.nn.gelu(y)
    out = y * jax.nn.sigmoid(_dot(y, gw_ref[...]) + gb_ref[...])
    for t in range(tl):
        y_ref[:, t, :] = out[t * tb:(t + 1) * tb, :]
    sre_ref[...] = stre_s[...]
    sim_ref[...] = stim_s[...]


def _s5_mixer(x3, nw, w_in_bf, bmat_re, bmat_im, cmat_re, cmat_im, d_row, glu_w_bf, glu_b, a_re, a_im,
              s0_re, s0_im, tl):
    b, l, d = x3.shape
    width = d_row.shape[1]
    n_state = a_re.shape[1]
    rows = b * tl
    args = (x3, nw, w_in_bf, bmat_re, bmat_im, cmat_re, cmat_im, d_row, glu_w_bf, glu_b,
            a_re, a_im, s0_re, s0_im)
    in_specs = ([pl.BlockSpec((b, tl, d), lambda i: (0, i, 0)), _full(nw.shape),
                 pl.BlockSpec((d, width), lambda i: (0, 0))] + [_full(a.shape) for a in args[3:]])
    y, s_re, s_im = pl.pallas_call(
        functools.partial(_s5_kernel, tb=b, tl=tl),
        grid=(l // tl,), in_specs=in_specs,
        out_specs=[pl.BlockSpec((b, tl, width), lambda i: (0, i, 0)), _full((b, n_state)), _full((b, n_state))],
        out_shape=[jax.ShapeDtypeStruct((b, l, width), F32),
                   jax.ShapeDtypeStruct((b, n_state), F32), jax.ShapeDtypeStruct((b, n_state), F32)],
        scratch_shapes=[pltpu.VMEM((rows, n_state), F32), pltpu.VMEM((rows, n_state), F32),
                        pltpu.VMEM((b, n_state), F32), pltpu.VMEM((b, n_state), F32)],
        compiler_params=_cparams(1), name="s5_mixer")(*args)
    return y.reshape(b * l, width), s_re, s_im


def _rope_kernel(pos_ref, inv_ref, cc_ref, ss_ref, *, dim_axis):
    ang = pos_ref[...] * inv_ref[...]
    idx = lax.broadcasted_iota(jnp.int32, ang.shape, dim_axis)
    cc_ref[...] = jnp.cos(ang)
    ss_ref[...] = jnp.where(idx < ang.shape[dim_axis] // 2, -jnp.sin(ang), jnp.sin(ang))


def _rope_tables(pos, dk, transposed=False):
    n = pos.shape[0]
    inv = 1.0 / (ROPE_BASE ** jnp.linspace(0.0, 1.0, dk // 2, dtype=F32))
    inv = jnp.concatenate([inv, inv])
    pos = pos.astype(F32)
    pos, inv, shape = (pos.reshape(1, n), inv.reshape(dk, 1), (dk, n)) if transposed else \
                      (pos.reshape(n, 1), inv.reshape(1, dk), (n, dk))
    return pl.pallas_call(
        functools.partial(_rope_kernel, dim_axis=0 if transposed else 1),
        in_specs=[_full(pos.shape), _full(inv.shape)],
        out_specs=[_full(shape), _full(shape)],
        out_shape=[jax.ShapeDtypeStruct(shape, F32)] * 2,
        name="rope_tables")(pos, inv)


def _rotate(t, cc, ss):
    return t * cc + pltpu.roll(t, t.shape[1] // 2, 1) * ss


def _rotate_t(t, cc, ss):
    half = t.shape[0] // 2
    return t * cc + jnp.concatenate([t[half:], t[:half]], axis=0) * ss


def _ret_consts(rows, c):
    lg = jnp.log(1.0 - 2.0 ** (-5.0 - jnp.arange(RET_HEADS, dtype=F32)))
    r = jnp.arange(rows)
    idx, blk = r % c, r // c
    diff = idx[:, None] - idx[None, :]
    same = blk[:, None] == blk[None, :]
    decay = jnp.where((same & (diff >= 0))[None],
                      jnp.exp(jnp.maximum(diff, 0)[None].astype(F32) * lg[:, None, None]), 0.0)
    wide = lambda v: jnp.broadcast_to(v[:, :, None], (RET_HEADS, rows, LANES))
    q_dec = wide(jnp.exp((idx + 1).astype(F32)[None, :] * lg[:, None]))
    k_dec = wide(jnp.exp((c - 1 - idx).astype(F32)[None, :] * lg[:, None]))
    chunk_dec = jnp.broadcast_to(jnp.exp(c * lg)[:, None, None], (RET_HEADS, SUBLANES, LANES))
    return decay, q_dec, k_dec, chunk_dec


def _ret_gate(o, g):
    o = o * lax.rsqrt(jnp.mean(o * o, axis=-1, keepdims=True) + NORM_EPS)
    return jax.nn.silu(g) * o


def _ret_long_kernel(q_ref, kt_ref, v_ref, g_ref, cc_ref, ss_ref, cct_ref, sst_ref, dec_ref, qd_ref, kdt_ref,
                     cd_ref, s0_ref, y_ref, so_ref, *, c, hp):
    dk = q_ref.shape[1] // hp
    scale = dk ** -0.5

    def chunk(i, states):
        rows = pl.ds(pl.multiple_of(i * c, c), c)
        cc, ss = cc_ref[rows, :], ss_ref[rows, :]
        cct, sst = cct_ref[i], sst_ref[i]
        new_states = []
        for j in range(hp):
            cols = slice(j * dk, (j + 1) * dk)
            q = _rotate(q_ref[rows, cols], cc, ss)
            kt = _rotate_t(kt_ref[i, cols, :], cct, sst) * scale
            v = v_ref[rows, cols]
            scores = _dot(q, kt) * dec_ref[j]
            o = _dot(scores, v) + _dot(q * qd_ref[j], states[j])
            new_states.append(cd_ref[j, 0:1, :] * states[j] + _dot(kt * kdt_ref[j], v))
            y_ref[rows, cols] = _ret_gate(o, g_ref[rows, cols]).astype(y_ref.dtype)
        return tuple(new_states)

    states = lax.fori_loop(0, q_ref.shape[0] // c, chunk, tuple(s0_ref[0, j] for j in range(hp)), unroll=4)
    for j in range(hp):
        so_ref[0, j] = states[j]


def _short_state_pass(qd, kt, v, s0_ref, so_ref, hd, decay_of, l):
    rows, dk = v.shape
    row = lax.broadcasted_iota(jnp.int32, (rows, dk), 0)
    v_cols = jnp.concatenate([jnp.where(row // l == b, v, 0.0).astype(BF16) for b in range(rows // l)], axis=1)
    kv = jnp.dot(kt.astype(BF16), v_cols, preferred_element_type=F32)
    win = 16
    per = win // l
    wrow = lax.broadcasted_iota(jnp.int32, (win, dk), 0)
    outs = []
    for w in range(rows // win):
        qw = qd[w * win:(w + 1) * win].astype(BF16)
        oi = jnp.zeros((win, dk), F32)
        for j in range(per):
            b = w * per + j
            s = s0_ref[b, hd]
            oi = jnp.where(wrow // l == j, jnp.dot(qw, s.astype(BF16), preferred_element_type=F32), oi)
            so_ref[b, hd] = decay_of(b) * s + kv[:, b * dk:(b + 1) * dk]
        outs.append(oi)
    return jnp.concatenate(outs, axis=0)


def _ret_short_kernel(q_ref, k_ref, v_ref, g_ref, cc_ref, ss_ref, dec_ref, qd_ref, kd_ref, cd_ref, s0_ref,
                      y_ref, so_ref, *, l, hp):
    dk = q_ref.shape[1] // hp
    scale = dk ** -0.5
    cc, ss = cc_ref[...], ss_ref[...]
    for hd in range(hp):
        cols = slice(hd * dk, (hd + 1) * dk)
        q = _rotate(q_ref[:, cols], cc, ss)
        k = _rotate(k_ref[:, cols], cc, ss) * scale
        v = v_ref[:, cols]
        o_intra = _dot(_dot_nt(q, k) * dec_ref[hd], v)
        cd = cd_ref[hd, 0:1, :]
        o_inter = _short_state_pass(q * qd_ref[hd], (k * kd_ref[hd]).T, v, s0_ref, so_ref, hd, lambda b: cd, l)
        y_ref[:, cols] = _ret_gate(o_intra + o_inter, g_ref[:, cols]).astype(y_ref.dtype)


def _retention(x, nw, w_in_bf, w_kt_bf, col0, pos, s0, n_seq, l):
    n = x.shape[0]
    dk = s0.shape[-1]
    width = RET_HEADS * dk
    out_shape = [jax.ShapeDtypeStruct((n, width), _mix_dtype(l)), jax.ShapeDtypeStruct(s0.shape, F32)]
    cspec = lambda a, hp: pl.BlockSpec((hp,) + a.shape[1:], lambda b, h: (h, 0, 0))
    cc, ss = _rope_tables(pos, dk)
    if l % RET_CHUNK == 0:
        c, hp = RET_CHUNK, RET_HEADS_PER_STEP
        n_hb = RET_HEADS // hp
        z, kt = _proj(x, nw, w_in_bf, ((col0, width), (col0 + 2 * width, 2 * width)), wt_bf=w_kt_bf, t_chunk=c)
        chunked = lambda t: jnp.swapaxes(t.reshape(dk, l // c, c), 0, 1)
        cct, sst = map(chunked, _rope_tables(pos, dk, transposed=True))
        decay, q_dec, k_dec, chunk_dec = _ret_consts(c, c)
        k_dec_t = jnp.swapaxes(k_dec, 1, 2)
        zspec = lambda field: pl.BlockSpec((l, hp * dk), lambda b, h: (b, field * n_hb + h))
        sspec = pl.BlockSpec((1, hp, dk, dk), lambda b, h: (b, h, 0, 0))
        return pl.pallas_call(
            functools.partial(_ret_long_kernel, c=c, hp=hp), grid=(n_seq, n_hb),
            in_specs=[zspec(0), pl.BlockSpec((l // c, hp * dk, c), lambda b, h: (b, h, 0)), zspec(1), zspec(2),
                      _full(cc.shape), _full(ss.shape), _full(cct.shape), _full(sst.shape),
                      cspec(decay, hp), cspec(q_dec, hp), cspec(k_dec_t, hp), cspec(chunk_dec, hp), sspec],
            out_specs=[pl.BlockSpec((l, hp * dk), lambda b, h: (b, h)), sspec],
            out_shape=out_shape, compiler_params=_cparams(2), name="retention")(
                z, kt, z, z, cc, ss, cct, sst, decay, q_dec, k_dec_t, chunk_dec, s0)
    rows, hp = LANES, SHORT_HEADS_PER_STEP
    per_blk = rows // l
    n_hb = RET_HEADS // hp
    z = _proj(x, nw, w_in_bf, segments=((col0, 4 * width),))
    decay, q_dec, k_dec, chunk_dec = _ret_consts(rows, l)
    zspec = lambda field: pl.BlockSpec((rows, hp * dk), lambda b, h: (b, field * n_hb + h))
    tspec = pl.BlockSpec((rows, dk), lambda b, h: (b, 0))
    sspec = pl.BlockSpec((per_blk, hp, dk, dk), lambda b, h: (b, h, 0, 0))
    return pl.pallas_call(
        functools.partial(_ret_short_kernel, l=l, hp=hp), grid=(n_seq // per_blk, n_hb),
        in_specs=[zspec(0), zspec(1), zspec(2), zspec(3), tspec, tspec,
                  cspec(decay, hp), cspec(q_dec, hp), cspec(k_dec, hp), cspec(chunk_dec, hp), sspec],
        out_specs=[pl.BlockSpec((rows, hp * dk), lambda b, h: (b, h)), sspec],
        out_shape=out_shape,
        compiler_params=_cparams(2), name="retention")(z, z, z, z, cc, ss, decay, q_dec, k_dec, chunk_dec, s0)


def _cumsum_rows(x, period):
    row = lax.broadcasted_iota(jnp.int32, x.shape, 0)
    s = 1
    while s < period:
        x = x + jnp.where(row % period >= s, pltpu.roll(x, s, 0), 0.0)
        s *= 2
    return x


def _row_of_block(x, period, offset):
    n, w = x.shape
    x3 = x.reshape(n // period, period, w)
    return jnp.broadcast_to(x3[:, offset:offset + 1, :], x3.shape).reshape(n, w)


def _hgrn_gates(fl, lg, layer):
    e = jnp.exp(lg - jnp.max(lg, axis=0, keepdims=True))
    soft = e / jnp.sum(e, axis=0, keepdims=True)
    cum = soft[0:1]
    for i in range(1, layer + 1):
        cum = cum + soft[i:i + 1]
    lb = cum - soft[0:1]
    forget = lb + (1.0 - lb) * jax.nn.sigmoid(fl)
    return 1.0 - forget, jnp.log2(forget)


def _hgrn_out(o, g, nw):
    o = o * lax.rsqrt(jnp.mean(o * o, axis=-1, keepdims=True) + NORM_EPS) * nw
    return o * jax.nn.silu(g)


def _hgrn_levels(c):
    levels, half = [], HG_BLOCK
    while half < c:
        levels.append(half)
        half *= 2
    return levels


def _hgrn_long_kernel(q_ref, f_ref, v_ref, g_ref, lbl_ref, nw_ref, s0_ref, y_ref, so_ref, *, c, layer, hp):
    dk = q_ref.shape[1] // hp
    nw = nw_ref[...]
    nv = c // SUBLANES
    bv = HG_BLOCK // SUBLANES
    levels = _hgrn_levels(c)
    row = lax.broadcasted_iota(jnp.int32, (c, c), 0)
    col = lax.broadcasted_iota(jnp.int32, (c, c), 1)
    lvl = jnp.where((row // HG_BLOCK == col // HG_BLOCK) & (col <= row), 0, -1)
    for li, half in enumerate(levels, 1):
        lvl = jnp.where((row // (2 * half) == col // (2 * half)) & (row % (2 * half) >= half)
                        & (col % (2 * half) < half), li, lvl)
    sub = lax.broadcasted_iota(jnp.int32, (SUBLANES, dk), 0)
    zeros = jnp.zeros((SUBLANES, dk), F32)
    cat = lambda slabs: jnp.concatenate(slabs, axis=0)

    def one_head(rows, cols, s):
        q, v = q_ref[rows, cols], v_ref[rows, cols]
        k, lf = _hgrn_gates(f_ref[rows, cols], lbl_ref[:, cols], layer)
        slabs = lambda a: [a[j * SUBLANES:(j + 1) * SUBLANES, :] for j in range(nv)]
        qs, ks = slabs(q), slabs(k)
        loc, run = [], [jnp.zeros((1, dk), F32)]
        for x in slabs(lf):
            step = 1
            while step < SUBLANES:
                x = x + jnp.where(sub >= step, pltpu.roll(x, step, 0), 0.0)
                step *= 2
            loc.append(x)
            run.append(run[-1] + x[SUBLANES - 1:, :])
        d0 = [loc[j] + (run[j] - run[j - j % bv]) if j % bv else loc[j] for j in range(nv)]
        scores = jnp.where(lvl == 0, _dot_nt(cat([qs[j] * jnp.exp2(d0[j]) for j in range(nv)]),
                                             cat([ks[j] * jnp.exp2(-d0[j]) for j in range(nv)])), 0.0)
        for li, half in enumerate(levels, 1):
            hv = half // SUBLANES
            q_side, k_side = [], []
            for j in range(nv):
                anchor = run[j - j % (2 * hv) + hv]
                if j % (2 * hv) >= hv:
                    q_side.append(qs[j] * jnp.exp2(loc[j] + (run[j] - anchor)))
                    k_side.append(zeros)
                else:
                    q_side.append(zeros)
                    k_side.append(ks[j] * jnp.exp2((anchor - run[j]) - loc[j]))
            scores = jnp.where(lvl == li, _dot_nt(cat(q_side), cat(k_side)), scores)
        o = _dot(scores, v) + _dot(cat([qs[j] * jnp.exp2(loc[j] + run[j]) for j in range(nv)]), s)
        k_hat = cat([ks[j] * jnp.exp2((run[nv] - run[j]) - loc[j]) for j in range(nv)])
        d_col = jnp.exp2(jnp.broadcast_to(run[nv], (dk, dk)).T)
        y_ref[rows, cols] = _hgrn_out(o, g_ref[rows, cols], nw).astype(y_ref.dtype)
        return d_col * s + _dot(k_hat.T, v)

    def chunk(i, states):
        rows = pl.ds(pl.multiple_of(i * c, c), c)
        return tuple(one_head(rows, slice(hd * dk, (hd + 1) * dk), states[hd]) for hd in range(hp))

    states = lax.fori_loop(0, q_ref.shape[0] // c, chunk, tuple(s0_ref[0, j] for j in range(hp)), unroll=4)
    for j in range(hp):
        so_ref[0, j] = states[j]


def _hgrn_short_kernel(q_ref, f_ref, v_ref, g_ref, lbl_ref, nw_ref, s0_ref, y_ref, so_ref, *, l, layer, hp):
    rows = q_ref.shape[0]
    dk = q_ref.shape[1] // hp
    row = lax.broadcasted_iota(jnp.int32, (rows, rows), 0)
    col = lax.broadcasted_iota(jnp.int32, (rows, rows), 1)
    causal = (row // l == col // l) & (col <= row)
    for hd in range(hp):
        cols = slice(hd * dk, (hd + 1) * dk)
        q, v = q_ref[:, cols], v_ref[:, cols]
        k, lf = _hgrn_gates(f_ref[:, cols], lbl_ref[:, cols], layer)
        b = _cumsum_rows(lf, l)
        qe = q * jnp.exp2(b)
        o_intra = _dot(jnp.where(causal, _dot_nt(qe, k * jnp.exp2(-b)), 0.0), v)
        b_last = _row_of_block(b, l, l - 1)
        dect = jnp.exp2(b_last).T
        o_inter = _short_state_pass(qe, (k * jnp.exp2(b_last - b)).T, v, s0_ref, so_ref, hd,
                                    lambda s, dect=dect: dect[:, s * l:s * l + 1], l)
        y_ref[:, cols] = _hgrn_out(o_intra + o_inter, g_ref[:, cols], nw_ref[...]).astype(y_ref.dtype)


def _hgrn(z, lb_logits, norm_w, s0, n_seq, l, layer):
    n = z.shape[0]
    dk = s0.shape[-1]
    depth = lb_logits.shape[0]
    if l % HG_CHUNK == 0:
        rows, n_blk, hp = l, n_seq, HG_HEADS_PER_STEP
        kern = functools.partial(_hgrn_long_kernel, c=HG_CHUNK, layer=layer, hp=hp)
        sspec = pl.BlockSpec((1, hp, dk, dk), lambda b, h: (b, h, 0, 0))
    else:
        assert HG_BLOCK % l == 0
        rows, hp = LANES, SHORT_HEADS_PER_STEP
        per_blk = rows // l
        n_blk = n_seq // per_blk
        kern = functools.partial(_hgrn_short_kernel, l=l, layer=layer, hp=hp)
        sspec = pl.BlockSpec((per_blk, hp, dk, dk), lambda b, h: (b, h, 0, 0))
    n_hb = HG_HEADS // hp
    zspec = lambda rows, field: pl.BlockSpec((rows, hp * dk), lambda b, h: (b, field * n_hb + h))
    y, s_new = pl.pallas_call(
        kern, grid=(n_blk, n_hb),
        in_specs=[zspec(rows, 0), zspec(rows, 1), zspec(rows, 2), zspec(rows, 3),
                  pl.BlockSpec((depth, hp * dk), lambda b, h: (0, h)), _full((1, dk)), sspec],
        out_specs=[pl.BlockSpec((rows, hp * dk), lambda b, h: (b, h)), sspec],
        out_shape=[jax.ShapeDtypeStruct((n, HG_HEADS * dk), _mix_dtype(l)), jax.ShapeDtypeStruct(s0.shape, F32)],
        compiler_params=_cparams(2), name="hgrn")(z, z, z, z, lb_logits, norm_w, s0)
    return y, s_new


def _ffn_kernel(*refs, n_parts, long_mode, final_norm, l):
    x_ref, parts = refs[0], refs[1:1 + n_parts]
    (wo_ref, nw_ref, wg_ref, wu_ref, cw_ref, cb_ref, wd_ref, fnw_ref, buf_ref, o_ref, bufo_ref,
     act_s) = refs[1 + n_parts:13 + n_parts]
    d_ff = wg_ref.shape[1]
    if long_mode:
        tail_s = refs[13 + n_parts]
        tm = x_ref.shape[1]
        rc = min(ROW_TILE, tm)
        load = lambda ref, r0: ref[0, r0:r0 + rc, :]

        @pl.when(pl.program_id(1) == 0)
        def _():
            tail_s[...] = buf_ref[0]
    else:
        nb = x_ref.shape[0]
        tm = rc = l * nb
        load = lambda ref, r0: jnp.concatenate([ref[:, t, :] for t in range(l)], axis=0)

    xs = []
    for r0 in range(0, tm, rc):
        mix = jnp.concatenate([load(p, r0).astype(BF16) for p in parts], axis=1)
        x = load(x_ref, r0) + jnp.dot(mix, wo_ref[...], preferred_element_type=F32)
        xs.append(x)
        h = _rmsnorm(x, nw_ref[...]).astype(BF16)
        for c0 in range(0, d_ff, FF_CHUNK):
            cs = slice(c0, c0 + FF_CHUNK)
            g = jnp.dot(h, wg_ref[:, cs], preferred_element_type=F32)
            up = jnp.dot(h, wu_ref[:, cs], preferred_element_type=F32)
            if long_mode:
                row = lax.broadcasted_iota(jnp.int32, g.shape, 0)
                p1 = jnp.where(row < 1, tail_s[1:2, cs], pltpu.roll(g, 1, 0))
                p2 = jnp.where(row < 2, jnp.where(row == 0, tail_s[0:1, cs], tail_s[1:2, cs]),
                               pltpu.roll(g, 2, 0))
                tail_s[:, cs] = g[rc - (CONV_W - 1):, :]
            else:
                b0, b1 = buf_ref[:, 0, cs], buf_ref[:, 1, cs]
                p1 = jnp.concatenate([b1, g[:tm - nb]], axis=0)
                p2 = jnp.concatenate([b0, b1, g[:tm - 2 * nb]], axis=0)
                bufo_ref[:, 0, cs] = g[tm - 2 * nb:tm - nb]
                bufo_ref[:, 1, cs] = g[tm - nb:]
            conv = cb_ref[:, cs] + cw_ref[0:1, cs] * p2
            conv = conv + cw_ref[1:2, cs] * p1
            conv = conv + cw_ref[2:3, cs] * g
            act_s[r0:r0 + rc, cs] = (jax.nn.silu(conv) * up).astype(BF16)
    y = jnp.concatenate(xs, axis=0) + jnp.dot(act_s[...], wd_ref[...], preferred_element_type=F32)
    if final_norm:
        y = _rmsnorm(y, fnw_ref[...])
    if long_mode:
        o_ref[0] = y
        bufo_ref[0] = tail_s[...]
    else:
        for t in range(l):
            o_ref[:, t, :] = y[t * nb:(t + 1) * nb]


def _ffn(x, parts, layer, wo, nw, wg, wu, conv_w, conv_b, wd, fnw, buf, n_seq, l, final_norm):
    assert CONV_W == 3 and l >= CONV_W - 1
    n, d = x.shape
    d_ff = wg.shape[2]
    weights = (wo, nw, wg, wu, conv_w, conv_b, wd, fnw)
    stacked = (wg, wu, wd)

    def wspec(w, single_buffer):
        kw = dict(pipeline_mode=pl.Buffered(1)) if single_buffer else {}
        if any(w is s for s in stacked):
            return pl.BlockSpec((None,) + w.shape[1:], lambda *_: (layer, 0, 0), **kw)
        return pl.BlockSpec(w.shape, lambda *_: (0,) * w.ndim, **kw)
    kern = functools.partial(_ffn_kernel, n_parts=len(parts), final_norm=final_norm, l=l)
    buf_shape = jax.ShapeDtypeStruct((n_seq, CONV_W - 1, d_ff), F32)
    rows3 = lambda a: a.reshape(n_seq, l, a.shape[1])
    if l % ROW_TILE == 0:
        tm = FFN_ROW_TILE
        assert l % tm == 0
        rspec = lambda w: pl.BlockSpec((1, tm, w), lambda b, i: (b, i, 0))
        bspec = pl.BlockSpec((1, CONV_W - 1, d_ff), lambda b, i: (b, 0, 0))
        y, buf_new = pl.pallas_call(
            functools.partial(kern, long_mode=True),
            grid=(n_seq, l // tm),
            in_specs=([rspec(d)] + [rspec(p.shape[1]) for p in parts]
                      + [wspec(w, True) for w in weights]
                      + [pl.BlockSpec((None, 1, CONV_W - 1, d_ff), lambda b, i: (layer, b, 0, 0))]),
            out_specs=[rspec(d), bspec],
            out_shape=[jax.ShapeDtypeStruct((n_seq, l, d), F32), buf_shape],
            scratch_shapes=[pltpu.VMEM((tm, d_ff), BF16), pltpu.VMEM((CONV_W - 1, d_ff), F32)],
            compiler_params=_cparams(2), name="ffn")(rows3(x), *map(rows3, parts), *weights, buf)
        return y.reshape(n, d), buf_new
    y, buf_new = pl.pallas_call(
        functools.partial(kern, long_mode=False),
        grid=(1,),
        in_specs=([_full((n_seq, l, d))] + [_full((n_seq, l, p.shape[1])) for p in parts]
                  + [wspec(w, False) for w in weights]
                  + [pl.BlockSpec((None,) + buf.shape[1:], lambda *_: (layer, 0, 0, 0))]),
        out_specs=[_full((n_seq, l, d)), _full(buf.shape[1:])],
        out_shape=[jax.ShapeDtypeStruct((n_seq, l, d), F32), buf_shape],
        scratch_shapes=[pltpu.VMEM((n, d_ff), BF16)],
        compiler_params=_cparams(1), name="ffn")(
            rows3(x), *map(rows3, parts), *weights, buf)
    return y.reshape(n, d), buf_new


def _trunk(x3, pos, s5_re, s5_im, ret, hg, conv, p, s5_tl):
    b, l, d = x3.shape
    n = b * l
    x = x3.reshape(n, d)
    depth = p['norm_mix'].shape[0]
    new_re, new_im, new_ret, new_hg, new_conv = [], [], [], [], []
    for layer in range(depth):
        j = layer // 2
        nw = p['norm_mix'][layer].reshape(1, d)
        if layer % 2 == 0:
            s5w = p['s5'][j]
            width = s5w['d'].shape[1]
            y_s5, r, im = _s5_mixer(x.reshape(b, l, d), nw, p['w_in_ab'][j], s5w['bmat_re'], s5w['bmat_im'],
                                    s5w['cmat_re'], s5w['cmat_im'], s5w['d'], s5w['glu_w'], s5w['glu_b'],
                                    s5w['a_re'], s5w['a_im'], s5_re[j].reshape(b, -1), s5_im[j].reshape(b, -1),
                                    s5_tl)
            y_ret, st = _retention(x, nw, p['w_in_ab'][j], p['w_kt'][j], width, pos, ret[j], b, l)
            parts, w_out = [y_s5, y_ret], p['w_out_ab'][j]
            new_re.append(r.reshape(s5_re.shape[1:]))
            new_im.append(im.reshape(s5_im.shape[1:]))
            new_ret.append(st)
        else:
            z = _proj(x, nw, p['w_in_c'][j])
            y_hg, st = _hgrn(z, p['hg_lb_logits'], p['hg_norm_w'][j].reshape(1, -1), hg[j], b, l, layer)
            parts, w_out = [y_hg], p['w_out_c'][j]
            new_hg.append(st)
        x, buf = _ffn(x, parts, layer, w_out, p['norm_ffn'][layer].reshape(1, d), p['ffn_w_gate'],
                      p['ffn_w_up'], p['ffn_conv_w'][layer], p['ffn_conv_b'][layer].reshape(1, -1),
                      p['ffn_w_down'], p['norm_final'].reshape(1, d), conv, b, l,
                      final_norm=(layer == depth - 1))
        new_conv.append(buf)
    return (x.reshape(b, l, d), jnp.stack(new_re), jnp.stack(new_im), jnp.stack(new_ret),
            jnp.stack(new_hg), jnp.stack(new_conv))


def kernel(x_prompt, x_sample, state_s5_re, state_s5_im, state_ret, state_hgrn, state_ffn_conv, pos_sample, norm_mix, norm_ffn, norm_final, w_in_ab, s5_lam_re, s5_lam_im, s5_log_dt, s5_b_re, s5_b_im, s5_c_re, s5_c_im, s5_d, s5_glu_w, s5_glu_b, w_out_ab, w_in_c, hg_lb_logits, hg_norm_w, w_out_c, ffn_w_gate, ffn_w_up, ffn_conv_w, ffn_conv_b, ffn_w_down):
    n_ab, n_grp, n_st = s5_lam_re.shape
    ch = s5_b_re.shape[-1]
    width = n_grp * ch
    ret_width = state_ret.shape[2] * state_ret.shape[3]
    grp_per_blk = MXU_DIM // ch
    n_blk = n_grp // grp_per_blk
    s5 = []
    for j in range(n_ab):
        a_re, a_im, bb_re, bb_im = _s5_prep(s5_lam_re[j], s5_lam_im[j], s5_log_dt[j], s5_b_re[j], s5_b_im[j])
        bmat = lambda m: jnp.stack([_block_diag(blk, grp_per_blk)
                                    for blk in m.reshape(n_blk, grp_per_blk * ch, n_st)]).astype(BF16)
        cmat = lambda c: jnp.stack([_block_diag(blk, grp_per_blk).T for blk in
                                    c.reshape(n_blk, grp_per_blk * ch, n_st)]).astype(BF16)
        s5.append(dict(bmat_re=bmat(bb_re), bmat_im=bmat(bb_im),
                       cmat_re=cmat(s5_c_re[j]), cmat_im=cmat(s5_c_im[j]),
                       d=s5_d[j].reshape(1, width), glu_w=s5_glu_w[j].astype(BF16),
                       glu_b=s5_glu_b[j].reshape(1, width),
                       a_re=a_re.reshape(1, n_grp * n_st), a_im=a_im.reshape(1, n_grp * n_st)))
    per_layer_bf16 = lambda w: [w[i].astype(BF16) for i in range(w.shape[0])]
    p = dict(norm_mix=norm_mix, norm_ffn=norm_ffn, norm_final=norm_final, s5=s5,
             w_in_ab=per_layer_bf16(w_in_ab), w_out_ab=per_layer_bf16(w_out_ab),
             w_kt=[w_in_ab[j][:, width + ret_width:width + 2 * ret_width].T.astype(BF16) for j in range(n_ab)],
             w_in_c=per_layer_bf16(w_in_c), hg_lb_logits=hg_lb_logits, hg_norm_w=hg_norm_w,
             w_out_c=per_layer_bf16(w_out_c), ffn_w_gate=ffn_w_gate.astype(BF16),
             ffn_w_up=ffn_w_up.astype(BF16), ffn_conv_w=ffn_conv_w, ffn_conv_b=ffn_conv_b,
             ffn_w_down=ffn_w_down.astype(BF16))

    bp, lp, _ = x_prompt.shape
    z_s5 = jnp.zeros((n_ab, bp) + state_s5_re.shape[2:], F32)
    z_ret = jnp.zeros((n_ab, bp) + state_ret.shape[2:], F32)
    z_hg = jnp.zeros((state_hgrn.shape[0], bp) + state_hgrn.shape[2:], F32)
    z_conv = jnp.zeros((norm_mix.shape[0], bp) + state_ffn_conv.shape[2:], F32)
    outs_p = _trunk(x_prompt, jnp.arange(lp, dtype=jnp.int32), z_s5, z_s5, z_ret, z_hg, z_conv, p,
                    s5_tl=S5_ROW_TILE // bp)
    bs, ls, _ = x_sample.shape
    pos_s = (pos_sample[:, None] + jnp.arange(ls, dtype=jnp.int32)[None, :]).reshape(-1)
    outs_s = _trunk(x_sample, pos_s, state_s5_re, state_s5_im, state_ret, state_hgrn, state_ffn_conv, p,
                    s5_tl=ls)
    return (outs_p[0], outs_s[0]) + outs_p[1:] + outs_s[1:]
```

```python
import functools

import jax
import jax.numpy as jnp
from jax import lax
from jax.experimental import pallas as pl
from jax.experimental.pallas import tpu as pltpu

F32 = jnp.float32
BF16 = jnp.bfloat16

NORM_EPS = 1e-6
ROPE_BASE = 10000.0
S5_GROUP_CH = 16
RET_HEADS = 4
RET_CHUNK = 128
HG_HEADS = 8
HG_BLOCK = 16
HG_CHUNK = 128
RET_HEADS_PER_STEP = 4
HG_HEADS_PER_STEP = 4
SHORT_HEADS_PER_STEP = 2
CONV_W = 3

LANES = 128
SUBLANES = 8
MXU_DIM = 256
VMEM_LIMIT_BYTES = 56 * 1024 * 1024
ROW_TILE = 512
FFN_ROW_TILE = 1024
PROJ_ROW_TILE = 1024
S5_ROW_TILE = 1024
FF_CHUNK = 256
S5_SCAN_COLS = 512


def _cparams(n_grid_dims):
    return pltpu.CompilerParams(dimension_semantics=("arbitrary",) * n_grid_dims,
                                vmem_limit_bytes=VMEM_LIMIT_BYTES)


def _dot(a, b):
    return jnp.dot(a.astype(BF16), b.astype(BF16), preferred_element_type=F32)


def _dot_nt(a, b):
    return lax.dot_general(a.astype(BF16), b.astype(BF16), (((1,), (1,)), ((), ())),
                           preferred_element_type=F32)


def _rmsnorm(x, w):
    return x * lax.rsqrt(jnp.mean(x * x, axis=-1, keepdims=True) + NORM_EPS) * w


def _full(shape):
    nd = len(shape)
    return pl.BlockSpec(shape, lambda *_: (0,) * nd)


def _mix_dtype(l):
    return BF16 if l % ROW_TILE == 0 else F32


def _resident(shape):
    nd = len(shape)
    return pl.BlockSpec(shape, lambda *_: (0,) * nd, pipeline_mode=pl.Buffered(1))


def _proj_kernel(*refs, segments, col_chunk, row_chunk, t_chunk, gate_layer):
    refs = list(refs)
    x_ref, nw_ref, w_ref = refs[:3]
    rest = refs[3:]
    lg_ref = rest.pop(0) if gate_layer is not None else None
    wt_ref = rest.pop(0) if t_chunk else None
    o_ref = rest.pop(0)
    ot_ref = rest.pop(0) if t_chunk else None
    for r in range(0, x_ref.shape[0], row_chunk):
        rows = slice(r, r + row_chunk)
        h = _rmsnorm(x_ref[rows, :], nw_ref[...]).astype(BF16)
        out0 = 0
        for col0, width, kind in segments:
            for c in range(0, width, col_chunk):
                val = jnp.dot(h, w_ref[:, col0 + c:col0 + c + col_chunk], preferred_element_type=F32)
                if kind == "silu":
                    val = jax.nn.silu(val)
                elif kind == "log2_forget":
                    val = _hgrn_log2_forget(val, lg_ref[:, c:c + col_chunk], gate_layer)
                o_ref[rows, out0 + c:out0 + c + col_chunk] = val
            out0 += width
        if t_chunk:
            zt = _dot_nt(wt_ref[...], h)
            for j in range(row_chunk // t_chunk):
                ot_ref[r // t_chunk + j] = zt[:, j * t_chunk:(j + 1) * t_chunk]


def _proj(x, nw, w_bf, segments=None, wt_bf=None, t_chunk=0, gate_logits=None, gate_layer=None):
    n, d = x.shape
    segments = tuple((tuple(s) + (None,))[:3] for s in (segments or ((0, w_bf.shape[1]),)))
    n_out = sum(width for _, width, _ in segments)
    tm = min(PROJ_ROW_TILE, n)
    in_specs = [pl.BlockSpec((tm, d), lambda i: (i, 0)), _full((1, d)), _resident(w_bf.shape)]
    out_specs = [pl.BlockSpec((tm, n_out), lambda i: (i, 0))]
    out_shape = [jax.ShapeDtypeStruct((n, n_out), F32)]
    args = [x, nw, w_bf]
    if gate_layer is not None:
        in_specs.append(_full(gate_logits.shape))
        args.append(gate_logits)
    if t_chunk:
        t_width = wt_bf.shape[0]
        in_specs.append(_resident(wt_bf.shape))
        out_specs.append(pl.BlockSpec((tm // t_chunk, t_width, t_chunk), lambda i: (i, 0, 0)))
        out_shape.append(jax.ShapeDtypeStruct((n // t_chunk, t_width, t_chunk), F32))
        args.append(wt_bf)
    outs = pl.pallas_call(
        functools.partial(_proj_kernel, segments=segments, col_chunk=2 * MXU_DIM, row_chunk=min(ROW_TILE, tm),
                          t_chunk=t_chunk, gate_layer=gate_layer),
        grid=(n // tm,), in_specs=in_specs, out_specs=out_specs, out_shape=out_shape,
        compiler_params=_cparams(1), name="proj")(*args)
    return outs if t_chunk else outs[0]


def _s5_prep_kernel(lre_ref, lim_ref, ldt_ref, lre16_ref, lim16_ref, ldt16_ref, bre_ref, bim_ref,
                    are_ref, aim_ref, bbre_ref, bbim_ref):
    def disc(lre, lim, ldt):
        dt = jnp.exp(ldt)
        mag = jnp.exp(lre * dt)
        ang = lim * dt
        return mag * jnp.cos(ang), mag * jnp.sin(ang)

    ab_re, ab_im = disc(lre_ref[...], lim_ref[...], ldt_ref[...])
    are_ref[...] = ab_re
    aim_ref[...] = ab_im
    lre, lim = lre16_ref[...], lim16_ref[...]
    ab_re, ab_im = disc(lre, lim, ldt16_ref[...])
    nr, ni = ab_re - 1.0, ab_im
    den = lre * lre + lim * lim
    f_re = (nr * lre + ni * lim) / den
    f_im = (ni * lre - nr * lim) / den
    b_re, b_im = bre_ref[...], bim_ref[...]
    bbre_ref[...] = f_re * b_re - f_im * b_im
    bbim_ref[...] = f_re * b_im + f_im * b_re


def _s5_prep(lam_re, lam_im, log_dt, b_re, b_im):
    g, p = lam_re.shape
    ch = b_re.shape[-1]
    rep = lambda a: jnp.repeat(a, ch, axis=0)
    ldt = log_dt.reshape(g, 1)
    bt = lambda b: jnp.swapaxes(b, 1, 2).reshape(g * ch, p)
    args = (lam_re, lam_im, ldt, rep(lam_re), rep(lam_im), rep(ldt), bt(b_re), bt(b_im))
    return pl.pallas_call(
        _s5_prep_kernel,
        in_specs=[_full(a.shape) for a in args],
        out_specs=[_full((g, p)), _full((g, p)), _full((g * ch, p)), _full((g * ch, p))],
        out_shape=[jax.ShapeDtypeStruct((g, p), F32)] * 2 + [jax.ShapeDtypeStruct((g * ch, p), F32)] * 2,
        name="s5_prep")(*args)


def _block_diag(m, n_blk):
    r, c = m.shape[0] // n_blk, m.shape[1]
    eye = jnp.eye(n_blk, dtype=m.dtype)
    return (m.reshape(n_blk, r, 1, c) * eye[:, None, :, None]).reshape(n_blk * r, n_blk * c)


def _s5_kernel(x_ref, nw_ref, wu_ref, bre_ref, bim_ref, cre_ref, cim_ref, d_ref, gw_ref, gb_ref,
               are_ref, aim_ref, s0re_ref, s0im_ref,
               y_ref, sre_ref, sim_ref, bure_s, buim_s, stre_s, stim_s, *, tb, tl):
    @pl.when(pl.program_id(0) == 0)
    def _():
        stre_s[...] = s0re_ref[...]
        stim_s[...] = s0im_ref[...]

    x = jnp.concatenate([x_ref[:, t, :] for t in range(tl)], axis=0)
    u = jnp.dot(_rmsnorm(x, nw_ref[...]).astype(BF16), wu_ref[...], preferred_element_type=F32)
    ub = u.astype(BF16)

    n_kb, kw, cw = bre_ref.shape
    for kb in range(n_kb):
        uk = ub[:, kb * kw:(kb + 1) * kw]
        bure_s[:, kb * cw:(kb + 1) * cw] = jnp.dot(uk, bre_ref[kb], preferred_element_type=F32)
        buim_s[:, kb * cw:(kb + 1) * cw] = jnp.dot(uk, bim_ref[kb], preferred_element_type=F32)

    n_state = are_ref.shape[1]
    sc = min(S5_SCAN_COLS, n_state)
    for c0 in range(0, n_state, sc):
        cs = slice(c0, c0 + sc)
        a_re = jnp.broadcast_to(are_ref[:, cs], (tb, sc))
        a_im = jnp.broadcast_to(aim_ref[:, cs], (tb, sc))

        def step(t, carry, cs=cs, a_re=a_re, a_im=a_im):
            s_re, s_im = carry
            rows = pl.ds(pl.multiple_of(t * tb, tb), tb)
            n_re = a_re * s_re - a_im * s_im + bure_s[rows, cs]
            n_im = a_re * s_im + a_im * s_re + buim_s[rows, cs]
            bure_s[rows, cs] = n_re
            buim_s[rows, cs] = n_im
            return n_re, n_im

        s_re, s_im = lax.fori_loop(0, tl, step, (stre_s[:, cs], stim_s[:, cs]), unroll=True)
        stre_s[:, cs] = s_re
        stim_s[:, cs] = s_im

    n_ob, ckw, ocw = cre_ref.shape
    ys = []
    for ob in range(n_ob):
        ks = slice(ob * ckw, (ob + 1) * ckw)
        ys.append(_dot(bure_s[:, ks], cre_ref[ob]) - _dot(buim_s[:, ks], cim_ref[ob]))
    y = jnp.concatenate(ys, axis=1) + d_ref[...] * u
    y = jax.nn.gelu(y)
    out = y * jax.nn.sigmoid(_dot(y, gw_ref[...]) + gb_ref[...])
    for t in range(tl):
        y_ref[:, t, :] = out[t * tb:(t + 1) * tb, :]
    sre_ref[...] = stre_s[...]
    sim_ref[...] = stim_s[...]


def _s5_mixer(x3, nw, w_in_bf, bmat_re, bmat_im, cmat_re, cmat_im, d_row, glu_w_bf, glu_b, a_re, a_im,
              s0_re, s0_im, tl):
    b, l, d = x3.shape
    width = d_row.shape[1]
    n_state = a_re.shape[1]
    rows = b * tl
    args = (x3, nw, w_in_bf, bmat_re, bmat_im, cmat_re, cmat_im, d_row, glu_w_bf, glu_b,
            a_re, a_im, s0_re, s0_im)
    in_specs = ([pl.BlockSpec((b, tl, d), lambda i: (0, i, 0)), _full(nw.shape),
                 pl.BlockSpec((d, width), lambda i: (0, 0))] + [_full(a.shape) for a in args[3:]])
    y, s_re, s_im = pl.pallas_call(
        functools.partial(_s5_kernel, tb=b, tl=tl),
        grid=(l // tl,), in_specs=in_specs,
        out_specs=[pl.BlockSpec((b, tl, width), lambda i: (0, i, 0)), _full((b, n_state)), _full((b, n_state))],
        out_shape=[jax.ShapeDtypeStruct((b, l, width), F32),
                   jax.ShapeDtypeStruct((b, n_state), F32), jax.ShapeDtypeStruct((b, n_state), F32)],
        scratch_shapes=[pltpu.VMEM((rows, n_state), F32), pltpu.VMEM((rows, n_state), F32),
                        pltpu.VMEM((b, n_state), F32), pltpu.VMEM((b, n_state), F32)],
        compiler_params=_cparams(1), name="s5_mixer")(*args)
    return y.reshape(b * l, width), s_re, s_im


def _rope_kernel(pos_ref, inv_ref, cc_ref, ss_ref, *, dim_axis):
    ang = pos_ref[...] * inv_ref[...]
    idx = lax.broadcasted_iota(jnp.int32, ang.shape, dim_axis)
    cc_ref[...] = jnp.cos(ang)
    ss_ref[...] = jnp.where(idx < ang.shape[dim_axis] // 2, -jnp.sin(ang), jnp.sin(ang))


def _rope_tables(pos, dk, transposed=False):
    n = pos.shape[0]
    inv = 1.0 / (ROPE_BASE ** jnp.linspace(0.0, 1.0, dk // 2, dtype=F32))
    inv = jnp.concatenate([inv, inv])
    pos = pos.astype(F32)
    pos, inv, shape = (pos.reshape(1, n), inv.reshape(dk, 1), (dk, n)) if transposed else \
                      (pos.reshape(n, 1), inv.reshape(1, dk), (n, dk))
    return pl.pallas_call(
        functools.partial(_rope_kernel, dim_axis=0 if transposed else 1),
        in_specs=[_full(pos.shape), _full(inv.shape)],
        out_specs=[_full(shape), _full(shape)],
        out_shape=[jax.ShapeDtypeStruct(shape, F32)] * 2,
        name="rope_tables")(pos, inv)


def _rotate(t, cc, ss):
    return t * cc + pltpu.roll(t, t.shape[1] // 2, 1) * ss


def _rotate_t(t, cc, ss):
    half = t.shape[0] // 2
    return t * cc + jnp.concatenate([t[half:], t[:half]], axis=0) * ss


def _ret_consts(rows, c):
    lg = jnp.log(1.0 - 2.0 ** (-5.0 - jnp.arange(RET_HEADS, dtype=F32)))
    r = jnp.arange(rows)
    idx, blk = r % c, r // c
    diff = idx[:, None] - idx[None, :]
    same = blk[:, None] == blk[None, :]
    decay = jnp.where((same & (diff >= 0))[None],
                      jnp.exp(jnp.maximum(diff, 0)[None].astype(F32) * lg[:, None, None]), 0.0)
    wide = lambda v: jnp.broadcast_to(v[:, :, None], (RET_HEADS, rows, LANES))
    q_dec = wide(jnp.exp((idx + 1).astype(F32)[None, :] * lg[:, None]))
    k_dec = wide(jnp.exp((c - 1 - idx).astype(F32)[None, :] * lg[:, None]))
    chunk_dec = jnp.broadcast_to(jnp.exp(c * lg)[:, None, None], (RET_HEADS, SUBLANES, LANES))
    return decay, q_dec, k_dec, chunk_dec


def _ret_gate(o, g):
    o = o * lax.rsqrt(jnp.mean(o * o, axis=-1, keepdims=True) + NORM_EPS)
    return jax.nn.silu(g) * o


def _ret_long_kernel(q_ref, kt_ref, v_ref, g_ref, cc_ref, ss_ref, cct_ref, sst_ref, dec_ref, qd_ref, kdt_ref,
                     cd_ref, s0_ref, y_ref, so_ref, *, c, hp):
    dk = q_ref.shape[1] // hp
    scale = dk ** -0.5

    def chunk(i, states):
        rows = pl.ds(pl.multiple_of(i * c, c), c)
        cc, ss = cc_ref[rows, :], ss_ref[rows, :]
        cct, sst = cct_ref[i], sst_ref[i]
        new_states = []
        for j in range(hp):
            cols = slice(j * dk, (j + 1) * dk)
            q = _rotate(q_ref[rows, cols], cc, ss)
            kt = _rotate_t(kt_ref[i, cols, :], cct, sst) * scale
            v = v_ref[rows, cols]
            scores = _dot(q, kt) * dec_ref[j]
            o = _dot(scores, v) + _dot(q * qd_ref[j], states[j])
            new_states.append(cd_ref[j, 0:1, :] * states[j] + _dot(kt * kdt_ref[j], v))
            y_ref[rows, cols] = _ret_gate(o, g_ref[rows, cols]).astype(y_ref.dtype)
        return tuple(new_states)

    states = lax.fori_loop(0, q_ref.shape[0] // c, chunk, tuple(s0_ref[0, j] for j in range(hp)), unroll=4)
    for j in range(hp):
        so_ref[0, j] = states[j]


def _short_state_pass(qd, kt, v, s0_ref, so_ref, hd, decay_of, l):
    rows, dk = v.shape
    row = lax.broadcasted_iota(jnp.int32, (rows, dk), 0)
    v_cols = jnp.concatenate([jnp.where(row // l == b, v, 0.0).astype(BF16) for b in range(rows // l)], axis=1)
    kv = jnp.dot(kt.astype(BF16), v_cols, preferred_element_type=F32)
    win = 16
    per = win // l
    wrow = lax.broadcasted_iota(jnp.int32, (win, dk), 0)
    outs = []
    for w in range(rows // win):
        qw = qd[w * win:(w + 1) * win].astype(BF16)
        oi = jnp.zeros((win, dk), F32)
        for j in range(per):
            b = w * per + j
            s = s0_ref[b, hd]
            oi = jnp.where(wrow // l == j, jnp.dot(qw, s.astype(BF16), preferred_element_type=F32), oi)
            so_ref[b, hd] = decay_of(b) * s + kv[:, b * dk:(b + 1) * dk]
        outs.append(oi)
    return jnp.concatenate(outs, axis=0)


def _ret_short_kernel(q_ref, k_ref, v_ref, g_ref, cc_ref, ss_ref, dec_ref, qd_ref, kd_ref, cd_ref, s0_ref,
                      y_ref, so_ref, *, l, hp):
    dk = q_ref.shape[1] // hp
    scale = dk ** -0.5
    cc, ss = cc_ref[...], ss_ref[...]
    for hd in range(hp):
        cols = slice(hd * dk, (hd + 1) * dk)
        q = _rotate(q_ref[:, cols], cc, ss)
        k = _rotate(k_ref[:, cols], cc, ss) * scale
        v = v_ref[:, cols]
        o_intra = _dot(_dot_nt(q, k) * dec_ref[hd], v)
        cd = cd_ref[hd, 0:1, :]
        o_inter = _short_state_pass(q * qd_ref[hd], (k * kd_ref[hd]).T, v, s0_ref, so_ref, hd, lambda b: cd, l)
        y_ref[:, cols] = _ret_gate(o_intra + o_inter, g_ref[:, cols]).astype(y_ref.dtype)


def _retention(x, nw, w_in_bf, w_kt_bf, col0, pos, s0, n_seq, l):
    n = x.shape[0]
    dk = s0.shape[-1]
    width = RET_HEADS * dk
    out_shape = [jax.ShapeDtypeStruct((n, width), _mix_dtype(l)), jax.ShapeDtypeStruct(s0.shape, F32)]
    cspec = lambda a, hp: pl.BlockSpec((hp,) + a.shape[1:], lambda b, h: (h, 0, 0))
    cc, ss = _rope_tables(pos, dk)
    if l % RET_CHUNK == 0:
        c, hp = RET_CHUNK, RET_HEADS_PER_STEP
        n_hb = RET_HEADS // hp
        z, kt = _proj(x, nw, w_in_bf, ((col0, width), (col0 + 2 * width, 2 * width)), wt_bf=w_kt_bf, t_chunk=c)
        chunked = lambda t: jnp.swapaxes(t.reshape(dk, l // c, c), 0, 1)
        cct, sst = map(chunked, _rope_tables(pos, dk, transposed=True))
        decay, q_dec, k_dec, chunk_dec = _ret_consts(c, c)
        k_dec_t = jnp.swapaxes(k_dec, 1, 2)
        zspec = lambda field: pl.BlockSpec((l, hp * dk), lambda b, h: (b, field * n_hb + h))
        sspec = pl.BlockSpec((1, hp, dk, dk), lambda b, h: (b, h, 0, 0))
        return pl.pallas_call(
            functools.partial(_ret_long_kernel, c=c, hp=hp), grid=(n_seq, n_hb),
            in_specs=[zspec(0), pl.BlockSpec((l // c, hp * dk, c), lambda b, h: (b, h, 0)), zspec(1), zspec(2),
                      _full(cc.shape), _full(ss.shape), _full(cct.shape), _full(sst.shape),
                      cspec(decay, hp), cspec(q_dec, hp), cspec(k_dec_t, hp), cspec(chunk_dec, hp), sspec],
            out_specs=[pl.BlockSpec((l, hp * dk), lambda b, h: (b, h)), sspec],
            out_shape=out_shape, compiler_params=_cparams(2), name="retention")(
                z, kt, z, z, cc, ss, cct, sst, decay, q_dec, k_dec_t, chunk_dec, s0)
    rows, hp = LANES, SHORT_HEADS_PER_STEP
    per_blk = rows // l
    n_hb = RET_HEADS // hp
    z = _proj(x, nw, w_in_bf, segments=((col0, 4 * width),))
    decay, q_dec, k_dec, chunk_dec = _ret_consts(rows, l)
    zspec = lambda field: pl.BlockSpec((rows, hp * dk), lambda b, h: (b, field * n_hb + h))
    tspec = pl.BlockSpec((rows, dk), lambda b, h: (b, 0))
    sspec = pl.BlockSpec((per_blk, hp, dk, dk), lambda b, h: (b, h, 0, 0))
    return pl.pallas_call(
        functools.partial(_ret_short_kernel, l=l, hp=hp), grid=(n_seq // per_blk, n_hb),
        in_specs=[zspec(0), zspec(1), zspec(2), zspec(3), tspec, tspec,
                  cspec(decay, hp), cspec(q_dec, hp), cspec(k_dec, hp), cspec(chunk_dec, hp), sspec],
        out_specs=[pl.BlockSpec((rows, hp * dk), lambda b, h: (b, h)), sspec],
        out_shape=out_shape,
        compiler_params=_cparams(2), name="retention")(z, z, z, z, cc, ss, decay, q_dec, k_dec, chunk_dec, s0)


def _cumsum_rows(x, period):
    row = lax.broadcasted_iota(jnp.int32, x.shape, 0)
    s = 1
    while s < period:
        x = x + jnp.where(row % period >= s, pltpu.roll(x, s, 0), 0.0)
        s *= 2
    return x


def _row_of_block(x, period, offset):
    n, w = x.shape
    x3 = x.reshape(n // period, period, w)
    return jnp.broadcast_to(x3[:, offset:offset + 1, :], x3.shape).reshape(n, w)


def _hgrn_log2_forget(fl, lg, layer):
    e = jnp.exp(lg - jnp.max(lg, axis=0, keepdims=True))
    soft = e / jnp.sum(e, axis=0, keepdims=True)
    cum = soft[0:1]
    for i in range(1, layer + 1):
        cum = cum + soft[i:i + 1]
    lb = cum - soft[0:1]
    return jnp.log2(lb + (1.0 - lb) * jax.nn.sigmoid(fl))


def _hgrn_out(o, gate, nw):
    o = o * lax.rsqrt(jnp.mean(o * o, axis=-1, keepdims=True) + NORM_EPS) * nw
    return o * gate


def _hgrn_levels(c):
    levels, half = [], HG_BLOCK
    while half < c:
        levels.append(half)
        half *= 2
    return levels


def _hgrn_long_kernel(q_ref, f_ref, v_ref, g_ref, nw_ref, s0_ref, y_ref, so_ref, *, c, hp):
    dk = q_ref.shape[1] // hp
    nw = nw_ref[...]
    nv = c // SUBLANES
    bv = HG_BLOCK // SUBLANES
    levels = _hgrn_levels(c)
    row = lax.broadcasted_iota(jnp.int32, (c, c), 0)
    col = lax.broadcasted_iota(jnp.int32, (c, c), 1)
    lvl = jnp.where((row // HG_BLOCK == col // HG_BLOCK) & (col <= row), 0, -1)
    for li, half in enumerate(levels, 1):
        lvl = jnp.where((row // (2 * half) == col // (2 * half)) & (row % (2 * half) >= half)
                        & (col % (2 * half) < half), li, lvl)
    sub = lax.broadcasted_iota(jnp.int32, (SUBLANES, dk), 0)
    zeros = jnp.zeros((SUBLANES, dk), F32)
    cat = lambda slabs: jnp.concatenate(slabs, axis=0)

    def one_head(rows, cols, s):
        q, v = q_ref[rows, cols], v_ref[rows, cols]
        lf = f_ref[rows, cols]
        k = 1.0 - jnp.exp2(lf)
        slabs = lambda a: [a[j * SUBLANES:(j + 1) * SUBLANES, :] for j in range(nv)]
        qs, ks = slabs(q), slabs(k)
        loc, run = [], [jnp.zeros((1, dk), F32)]
        for x in slabs(lf):
            step = 1
            while step < SUBLANES:
                x = x + jnp.where(sub >= step, pltpu.roll(x, step, 0), 0.0)
                step *= 2
            loc.append(x)
            run.append(run[-1] + x[SUBLANES - 1:, :])
        d0 = [loc[j] + (run[j] - run[j - j % bv]) if j % bv else loc[j] for j in range(nv)]
        scores = jnp.where(lvl == 0, _dot_nt(cat([qs[j] * jnp.exp2(d0[j]) for j in range(nv)]),
                                             cat([ks[j] * jnp.exp2(-d0[j]) for j in range(nv)])), 0.0)
        for li, half in enumerate(levels, 1):
            hv = half // SUBLANES
            q_side, k_side = [], []
            for j in range(nv):
                anchor = run[j - j % (2 * hv) + hv]
                if j % (2 * hv) >= hv:
                    q_side.append(qs[j] * jnp.exp2(loc[j] + (run[j] - anchor)))
                    k_side.append(zeros)
                else:
                    q_side.append(zeros)
                    k_side.append(ks[j] * jnp.exp2((anchor - run[j]) - loc[j]))
            scores = jnp.where(lvl == li, _dot_nt(cat(q_side), cat(k_side)), scores)
        o = _dot(scores, v) + _dot(cat([qs[j] * jnp.exp2(loc[j] + run[j]) for j in range(nv)]), s)
        k_hat = cat([ks[j] * jnp.exp2((run[nv] - run[j]) - loc[j]) for j in range(nv)])
        d_col = jnp.exp2(jnp.broadcast_to(run[nv], (dk, dk)).T)
        y_ref[rows, cols] = _hgrn_out(o, g_ref[rows, cols], nw).astype(y_ref.dtype)
        return d_col * s + _dot(k_hat.T, v)

    def chunk(i, states):
        rows = pl.ds(pl.multiple_of(i * c, c), c)
        return tuple(one_head(rows, slice(hd * dk, (hd + 1) * dk), states[hd]) for hd in range(hp))

    states = lax.fori_loop(0, q_ref.shape[0] // c, chunk, tuple(s0_ref[0, j] for j in range(hp)), unroll=4)
    for j in range(hp):
        so_ref[0, j] = states[j]


def _hgrn_short_kernel(q_ref, f_ref, v_ref, g_ref, nw_ref, s0_ref, y_ref, so_ref, *, l, hp):
    rows = q_ref.shape[0]
    dk = q_ref.shape[1] // hp
    row = lax.broadcasted_iota(jnp.int32, (rows, rows), 0)
    col = lax.broadcasted_iota(jnp.int32, (rows, rows), 1)
    causal = (row // l == col // l) & (col <= row)
    for hd in range(hp):
        cols = slice(hd * dk, (hd + 1) * dk)
        q, v = q_ref[:, cols], v_ref[:, cols]
        lf = f_ref[:, cols]
        k = 1.0 - jnp.exp2(lf)
        b = _cumsum_rows(lf, l)
        qe = q * jnp.exp2(b)
        o_intra = _dot(jnp.where(causal, _dot_nt(qe, k * jnp.exp2(-b)), 0.0), v)
        b_last = _row_of_block(b, l, l - 1)
        dect = jnp.exp2(b_last).T
        o_inter = _short_state_pass(qe, (k * jnp.exp2(b_last - b)).T, v, s0_ref, so_ref, hd,
                                    lambda s, dect=dect: dect[:, s * l:s * l + 1], l)
        y_ref[:, cols] = _hgrn_out(o_intra + o_inter, g_ref[:, cols], nw_ref[...]).astype(y_ref.dtype)


def _hgrn(z, norm_w, s0, n_seq, l):
    n = z.shape[0]
    dk = s0.shape[-1]
    if l % HG_CHUNK == 0:
        rows, n_blk, hp = l, n_seq, HG_HEADS_PER_STEP
        kern = functools.partial(_hgrn_long_kernel, c=HG_CHUNK, hp=hp)
        sspec = pl.BlockSpec((1, hp, dk, dk), lambda b, h: (b, h, 0, 0))
    else:
        assert HG_BLOCK % l == 0
        rows, hp = LANES, SHORT_HEADS_PER_STEP
        per_blk = rows // l
        n_blk = n_seq // per_blk
        kern = functools.partial(_hgrn_short_kernel, l=l, hp=hp)
        sspec = pl.BlockSpec((per_blk, hp, dk, dk), lambda b, h: (b, h, 0, 0))
    n_hb = HG_HEADS // hp
    zspec = lambda rows, field: pl.BlockSpec((rows, hp * dk), lambda b, h: (b, field * n_hb + h))
    y, s_new = pl.pallas_call(
        kern, grid=(n_blk, n_hb),
        in_specs=[zspec(rows, 2), zspec(rows, 0), zspec(rows, 3), zspec(rows, 1), _full((1, dk)), sspec],
        out_specs=[pl.BlockSpec((rows, hp * dk), lambda b, h: (b, h)), sspec],
        out_shape=[jax.ShapeDtypeStruct((n, HG_HEADS * dk), _mix_dtype(l)), jax.ShapeDtypeStruct(s0.shape, F32)],
        compiler_params=_cparams(2), name="hgrn")(z, z, z, z, norm_w, s0)
    return y, s_new


def _ffn_kernel(*refs, n_parts, long_mode, final_norm, l):
    x_ref, parts = refs[0], refs[1:1 + n_parts]
    (wo_ref, nw_ref, wg_ref, wu_ref, cw_ref, cb_ref, wd_ref, fnw_ref, buf_ref, o_ref, bufo_ref,
     act_s) = refs[1 + n_parts:13 + n_parts]
    d_ff = wg_ref.shape[1]
    if long_mode:
        tail_s = refs[13 + n_parts]
        tm = x_ref.shape[1]
        rc = min(ROW_TILE, tm)
        load = lambda ref, r0: ref[0, r0:r0 + rc, :]

        @pl.when(pl.program_id(1) == 0)
        def _():
            tail_s[...] = buf_ref[0]
    else:
        nb = x_ref.shape[0]
        tm = rc = l * nb
        load = lambda ref, r0: jnp.concatenate([ref[:, t, :] for t in range(l)], axis=0)

    xs = []
    for r0 in range(0, tm, rc):
        mix = jnp.concatenate([load(p, r0).astype(BF16) for p in parts], axis=1)
        x = load(x_ref, r0) + jnp.dot(mix, wo_ref[...], preferred_element_type=F32)
        xs.append(x)
        h = _rmsnorm(x, nw_ref[...]).astype(BF16)
        for c0 in range(0, d_ff, FF_CHUNK):
            cs = slice(c0, c0 + FF_CHUNK)
            g = jnp.dot(h, wg_ref[:, cs], preferred_element_type=F32)
            up = jnp.dot(h, wu_ref[:, cs], preferred_element_type=F32)
            if long_mode:
                row = lax.broadcasted_iota(jnp.int32, g.shape, 0)
                p1 = jnp.where(row < 1, tail_s[1:2, cs], pltpu.roll(g, 1, 0))
                p2 = jnp.where(row < 2, jnp.where(row == 0, tail_s[0:1, cs], tail_s[1:2, cs]),
                               pltpu.roll(g, 2, 0))
                tail_s[:, cs] = g[rc - (CONV_W - 1):, :]
            else:
                b0, b1 = buf_ref[:, 0, cs], buf_ref[:, 1, cs]
                p1 = jnp.concatenate([b1, g[:tm - nb]], axis=0)
                p2 = jnp.concatenate([b0, b1, g[:tm - 2 * nb]], axis=0)
                bufo_ref[:, 0, cs] = g[tm - 2 * nb:tm - nb]
                bufo_ref[:, 1, cs] = g[tm - nb:]
            conv = cb_ref[:, cs] + cw_ref[0:1, cs] * p2
            conv = conv + cw_ref[1:2, cs] * p1
            conv = conv + cw_ref[2:3, cs] * g
            act_s[r0:r0 + rc, cs] = (jax.nn.silu(conv) * up).astype(BF16)
    y = jnp.concatenate(xs, axis=0) + jnp.dot(act_s[...], wd_ref[...], preferred_element_type=F32)
    if final_norm:
        y = _rmsnorm(y, fnw_ref[...])
    if long_mode:
        o_ref[0] = y
        bufo_ref[0] = tail_s[...]
    else:
        for t in range(l):
            o_ref[:, t, :] = y[t * nb:(t + 1) * nb]


def _ffn(x, parts, layer, wo, nw, wg, wu, conv_w, conv_b, wd, fnw, buf, n_seq, l, final_norm):
    assert CONV_W == 3 and l >= CONV_W - 1
    n, d = x.shape
    d_ff = wg.shape[2]
    weights = (wo, nw, wg, wu, conv_w, conv_b, wd, fnw)
    stacked = (wg, wu, wd)

    def wspec(w, single_buffer):
        kw = dict(pipeline_mode=pl.Buffered(1)) if single_buffer else {}
        if any(w is s for s in stacked):
            return pl.BlockSpec((None,) + w.shape[1:], lambda *_: (layer, 0, 0), **kw)
        return pl.BlockSpec(w.shape, lambda *_: (0,) * w.ndim, **kw)
    kern = functools.partial(_ffn_kernel, n_parts=len(parts), final_norm=final_norm, l=l)
    buf_shape = jax.ShapeDtypeStruct((n_seq, CONV_W - 1, d_ff), F32)
    rows3 = lambda a: a.reshape(n_seq, l, a.shape[1])
    if l % ROW_TILE == 0:
        tm = FFN_ROW_TILE
        assert l % tm == 0
        rspec = lambda w: pl.BlockSpec((1, tm, w), lambda b, i: (b, i, 0))
        bspec = pl.BlockSpec((1, CONV_W - 1, d_ff), lambda b, i: (b, 0, 0))
        y, buf_new = pl.pallas_call(
            functools.partial(kern, long_mode=True),
            grid=(n_seq, l // tm),
            in_specs=([rspec(d)] + [rspec(p.shape[1]) for p in parts]
                      + [wspec(w, True) for w in weights]
                      + [pl.BlockSpec((None, 1, CONV_W - 1, d_ff), lambda b, i: (layer, b, 0, 0))]),
            out_specs=[rspec(d), bspec],
            out_shape=[jax.ShapeDtypeStruct((n_seq, l, d), F32), buf_shape],
            scratch_shapes=[pltpu.VMEM((tm, d_ff), BF16), pltpu.VMEM((CONV_W - 1, d_ff), F32)],
            compiler_params=_cparams(2), name="ffn")(rows3(x), *map(rows3, parts), *weights, buf)
        return y.reshape(n, d), buf_new
    y, buf_new = pl.pallas_call(
        functools.partial(kern, long_mode=False),
        grid=(1,),
        in_specs=([_full((n_seq, l, d))] + [_full((n_seq, l, p.shape[1])) for p in parts]
                  + [wspec(w, False) for w in weights]
                  + [pl.BlockSpec((None,) + buf.shape[1:], lambda *_: (layer, 0, 0, 0))]),
        out_specs=[_full((n_seq, l, d)), _full(buf.shape[1:])],
        out_shape=[jax.ShapeDtypeStruct((n_seq, l, d), F32), buf_shape],
        scratch_shapes=[pltpu.VMEM((n, d_ff), BF16)],
        compiler_params=_cparams(1), name="ffn")(
            rows3(x), *map(rows3, parts), *weights, buf)
    return y.reshape(n, d), buf_new


def _trunk(x3, pos, s5_re, s5_im, ret, hg, conv, p, s5_tl):
    b, l, d = x3.shape
    n = b * l
    x = x3.reshape(n, d)
    depth = p['norm_mix'].shape[0]
    new_re, new_im, new_ret, new_hg, new_conv = [], [], [], [], []
    for layer in range(depth):
        j = layer // 2
        nw = p['norm_mix'][layer].reshape(1, d)
        if layer % 2 == 0:
            s5w = p['s5'][j]
            width = s5w['d'].shape[1]
            y_s5, r, im = _s5_mixer(x.reshape(b, l, d), nw, p['w_in_ab'][j], s5w['bmat_re'], s5w['bmat_im'],
                                    s5w['cmat_re'], s5w['cmat_im'], s5w['d'], s5w['glu_w'], s5w['glu_b'],
                                    s5w['a_re'], s5w['a_im'], s5_re[j].reshape(b, -1), s5_im[j].reshape(b, -1),
                                    s5_tl)
            y_ret, st = _retention(x, nw, p['w_in_ab'][j], p['w_kt'][j], width, pos, ret[j], b, l)
            parts, w_out = [y_s5, y_ret], p['w_out_ab'][j]
            new_re.append(r.reshape(s5_re.shape[1:]))
            new_im.append(im.reshape(s5_im.shape[1:]))
            new_ret.append(st)
        else:
            hw = p['hg_lb_logits'].shape[1]
            z = _proj(x, nw, p['w_in_c'][j],
                      segments=((hw, hw, "log2_forget"), (3 * hw, hw, "silu"), (0, hw), (2 * hw, hw)),
                      gate_logits=p['hg_lb_logits'], gate_layer=layer)
            y_hg, st = _hgrn(z, p['hg_norm_w'][j].reshape(1, -1), hg[j], b, l)
            parts, w_out = [y_hg], p['w_out_c'][j]
            new_hg.append(st)
        x, buf = _ffn(x, parts, layer, w_out, p['norm_ffn'][layer].reshape(1, d), p['ffn_w_gate'],
                      p['ffn_w_up'], p['ffn_conv_w'][layer], p['ffn_conv_b'][layer].reshape(1, -1),
                      p['ffn_w_down'], p['norm_final'].reshape(1, d), conv, b, l,
                      final_norm=(layer == depth - 1))
        new_conv.append(buf)
    return (x.reshape(b, l, d), jnp.stack(new_re), jnp.stack(new_im), jnp.stack(new_ret),
            jnp.stack(new_hg), jnp.stack(new_conv))


def kernel(x_prompt, x_sample, state_s5_re, state_s5_im, state_ret, state_hgrn, state_ffn_conv, pos_sample, norm_mix, norm_ffn, norm_final, w_in_ab, s5_lam_re, s5_lam_im, s5_log_dt, s5_b_re, s5_b_im, s5_c_re, s5_c_im, s5_d, s5_glu_w, s5_glu_b, w_out_ab, w_in_c, hg_lb_logits, hg_norm_w, w_out_c, ffn_w_gate, ffn_w_up, ffn_conv_w, ffn_conv_b, ffn_w_down):
    n_ab, n_grp, n_st = s5_lam_re.shape
    ch = s5_b_re.shape[-1]
    width = n_grp * ch
    ret_width = state_ret.shape[2] * state_ret.shape[3]
    grp_per_blk = MXU_DIM // ch
    n_blk = n_grp // grp_per_blk
    s5 = []
    for j in range(n_ab):
        a_re, a_im, bb_re, bb_im = _s5_prep(s5_lam_re[j], s5_lam_im[j], s5_log_dt[j], s5_b_re[j], s5_b_im[j])
        bmat = lambda m: jnp.stack([_block_diag(blk, grp_per_blk)
                                    for blk in m.reshape(n_blk, grp_per_blk * ch, n_st)]).astype(BF16)
        cmat = lambda c: jnp.stack([_block_diag(blk, grp_per_blk).T for blk in
                                    c.reshape(n_blk, grp_per_blk * ch, n_st)]).astype(BF16)
        s5.append(dict(bmat_re=bmat(bb_re), bmat_im=bmat(bb_im),
                       cmat_re=cmat(s5_c_re[j]), cmat_im=cmat(s5_c_im[j]),
                       d=s5_d[j].reshape(1, width), glu_w=s5_glu_w[j].astype(BF16),
                       glu_b=s5_glu_b[j].reshape(1, width),
                       a_re=a_re.reshape(1, n_grp * n_st), a_im=a_im.reshape(1, n_grp * n_st)))
    per_layer_bf16 = lambda w: [w[i].astype(BF16) for i in range(w.shape[0])]
    p = dict(norm_mix=norm_mix, norm_ffn=norm_ffn, norm_final=norm_final, s5=s5,
             w_in_ab=per_layer_bf16(w_in_ab), w_out_ab=per_layer_bf16(w_out_ab),
             w_kt=[w_in_ab[j][:, width + ret_width:width + 2 * ret_width].T.astype(BF16) for j in range(n_ab)],
             w_in_c=per_layer_bf16(w_in_c), hg_lb_logits=hg_lb_logits, hg_norm_w=hg_norm_w,
             w_out_c=per_layer_bf16(w_out_c), ffn_w_gate=ffn_w_gate.astype(BF16),
             ffn_w_up=ffn_w_up.astype(BF16), ffn_conv_w=ffn_conv_w, ffn_conv_b=ffn_conv_b,
             ffn_w_down=ffn_w_down.astype(BF16))

    bp, lp, _ = x_prompt.shape
    z_s5 = jnp.zeros((n_ab, bp) + state_s5_re.shape[2:], F32)
    z_ret = jnp.zeros((n_ab, bp) + state_ret.shape[2:], F32)
    z_hg = jnp.zeros((state_hgrn.shape[0], bp) + state_hgrn.shape[2:], F32)
    z_conv = jnp.zeros((norm_mix.shape[0], bp) + state_ffn_conv.shape[2:], F32)
    outs_p = _trunk(x_prompt, jnp.arange(lp, dtype=jnp.int32), z_s5, z_s5, z_ret, z_hg, z_conv, p,
                    s5_tl=S5_ROW_TILE // bp)
    bs, ls, _ = x_sample.shape
    pos_s = (pos_sample[:, None] + jnp.arange(ls, dtype=jnp.int32)[None, :]).reshape(-1)
    outs_s = _trunk(x_sample, pos_s, state_s5_re, state_s5_im, state_ret, state_hgrn, state_ffn_conv, p,
                    s5_tl=ls)
    return (outs_p[0], outs_s[0]) + outs_p[1:] + outs_s[1:]
```

```python
import functools

import jax
import jax.numpy as jnp
from jax import lax
from jax.experimental import pallas as pl
from jax.experimental.pallas import tpu as pltpu

F32 = jnp.float32
BF16 = jnp.bfloat16

NORM_EPS = 1e-6
ROPE_BASE = 10000.0
S5_GROUP_CH = 16
RET_HEADS = 4
RET_CHUNK = 128
HG_HEADS = 8
HG_BLOCK = 16
HG_CHUNK = 128
RET_HEADS_PER_STEP = 4
HG_HEADS_PER_STEP = 4
SHORT_HEADS_PER_STEP = 2
CONV_W = 3

LANES = 128
SUBLANES = 8
MXU_DIM = 256
VMEM_LIMIT_BYTES = 56 * 1024 * 1024
ROW_TILE = 512
FFN_ROW_TILE = 1024
PROJ_ROW_TILE = 1024
S5_ROW_TILE = 1024
FF_CHUNK = 256
S5_SCAN_COLS = 512


def _cparams(n_grid_dims):
    return pltpu.CompilerParams(dimension_semantics=("arbitrary",) * n_grid_dims,
                                vmem_limit_bytes=VMEM_LIMIT_BYTES)


def _dot(a, b):
    return jnp.dot(a.astype(BF16), b.astype(BF16), preferred_element_type=F32)


def _dot_nt(a, b):
    return lax.dot_general(a.astype(BF16), b.astype(BF16), (((1,), (1,)), ((), ())),
                           preferred_element_type=F32)


def _rmsnorm(x, w):
    return x * lax.rsqrt(jnp.mean(x * x, axis=-1, keepdims=True) + NORM_EPS) * w


def _sigmoid(x):
    return 0.5 * jnp.tanh(0.5 * x) + 0.5


def _full(shape):
    nd = len(shape)
    return pl.BlockSpec(shape, lambda *_: (0,) * nd)


def _mix_dtype(l):
    return BF16 if l % ROW_TILE == 0 else F32


def _resident(shape):
    nd = len(shape)
    return pl.BlockSpec(shape, lambda *_: (0,) * nd, pipeline_mode=pl.Buffered(1))


def _proj_kernel(*refs, segments, col_chunk, row_chunk, t_chunk, gate_layer):
    refs = list(refs)
    x_ref, nw_ref, w_ref = refs[:3]
    rest = refs[3:]
    lg_ref = rest.pop(0) if gate_layer is not None else None
    wt_ref = rest.pop(0) if t_chunk else None
    o_ref = rest.pop(0)
    ot_ref = rest.pop(0) if t_chunk else None
    for r in range(0, x_ref.shape[0], row_chunk):
        rows = slice(r, r + row_chunk)
        h = _rmsnorm(x_ref[rows, :], nw_ref[...]).astype(BF16)
        out0 = 0
        for col0, width, kind in segments:
            for c in range(0, width, col_chunk):
                val = jnp.dot(h, w_ref[:, col0 + c:col0 + c + col_chunk], preferred_element_type=F32)
                if kind == "silu":
                    val = val * _sigmoid(val)
                elif kind == "log2_forget":
                    val = _hgrn_log2_forget(val, lg_ref[:, c:c + col_chunk], gate_layer)
                o_ref[rows, out0 + c:out0 + c + col_chunk] = val
            out0 += width
        if t_chunk:
            zt = _dot_nt(wt_ref[...], h)
            for j in range(row_chunk // t_chunk):
                ot_ref[r // t_chunk + j] = zt[:, j * t_chunk:(j + 1) * t_chunk]


def _proj(x, nw, w_bf, segments=None, wt_bf=None, t_chunk=0, gate_logits=None, gate_layer=None):
    n, d = x.shape
    segments = tuple((tuple(s) + (None,))[:3] for s in (segments or ((0, w_bf.shape[1]),)))
    n_out = sum(width for _, width, _ in segments)
    tm = min(PROJ_ROW_TILE, n)
    in_specs = [pl.BlockSpec((tm, d), lambda i: (i, 0)), _full((1, d)), _resident(w_bf.shape)]
    out_specs = [pl.BlockSpec((tm, n_out), lambda i: (i, 0))]
    out_shape = [jax.ShapeDtypeStruct((n, n_out), F32)]
    args = [x, nw, w_bf]
    if gate_layer is not None:
        in_specs.append(_full(gate_logits.shape))
        args.append(gate_logits)
    if t_chunk:
        t_width = wt_bf.shape[0]
        in_specs.append(_resident(wt_bf.shape))
        out_specs.append(pl.BlockSpec((tm // t_chunk, t_width, t_chunk), lambda i: (i, 0, 0)))
        out_shape.append(jax.ShapeDtypeStruct((n // t_chunk, t_width, t_chunk), F32))
        args.append(wt_bf)
    outs = pl.pallas_call(
        functools.partial(_proj_kernel, segments=segments, col_chunk=2 * MXU_DIM, row_chunk=min(ROW_TILE, tm),
                          t_chunk=t_chunk, gate_layer=gate_layer),
        grid=(n // tm,), in_specs=in_specs, out_specs=out_specs, out_shape=out_shape,
        compiler_params=_cparams(1), name="proj")(*args)
    return outs if t_chunk else outs[0]


def _s5_prep_kernel(lre_ref, lim_ref, ldt_ref, lre16_ref, lim16_ref, ldt16_ref, bre_ref, bim_ref,
                    are_ref, aim_ref, bbre_ref, bbim_ref):
    def disc(lre, lim, ldt):
        dt = jnp.exp(ldt)
        mag = jnp.exp(lre * dt)
        ang = lim * dt
        return mag * jnp.cos(ang), mag * jnp.sin(ang)

    ab_re, ab_im = disc(lre_ref[...], lim_ref[...], ldt_ref[...])
    are_ref[...] = ab_re
    aim_ref[...] = ab_im
    lre, lim = lre16_ref[...], lim16_ref[...]
    ab_re, ab_im = disc(lre, lim, ldt16_ref[...])
    nr, ni = ab_re - 1.0, ab_im
    den = lre * lre + lim * lim
    f_re = (nr * lre + ni * lim) / den
    f_im = (ni * lre - nr * lim) / den
    b_re, b_im = bre_ref[...], bim_ref[...]
    bbre_ref[...] = f_re * b_re - f_im * b_im
    bbim_ref[...] = f_re * b_im + f_im * b_re


def _s5_prep(lam_re, lam_im, log_dt, b_re, b_im):
    g, p = lam_re.shape
    ch = b_re.shape[-1]
    rep = lambda a: jnp.repeat(a, ch, axis=0)
    ldt = log_dt.reshape(g, 1)
    bt = lambda b: jnp.swapaxes(b, 1, 2).reshape(g * ch, p)
    args = (lam_re, lam_im, ldt, rep(lam_re), rep(lam_im), rep(ldt), bt(b_re), bt(b_im))
    return pl.pallas_call(
        _s5_prep_kernel,
        in_specs=[_full(a.shape) for a in args],
        out_specs=[_full((g, p)), _full((g, p)), _full((g * ch, p)), _full((g * ch, p))],
        out_shape=[jax.ShapeDtypeStruct((g, p), F32)] * 2 + [jax.ShapeDtypeStruct((g * ch, p), F32)] * 2,
        name="s5_prep")(*args)


def _block_diag(m, n_blk):
    r, c = m.shape[0] // n_blk, m.shape[1]
    eye = jnp.eye(n_blk, dtype=m.dtype)
    return (m.reshape(n_blk, r, 1, c) * eye[:, None, :, None]).reshape(n_blk * r, n_blk * c)


def _s5_kernel(x_ref, nw_ref, wu_ref, bre_ref, bim_ref, cre_ref, cim_ref, d_ref, gw_ref, gb_ref,
               are_ref, aim_ref, s0re_ref, s0im_ref,
               y_ref, sre_ref, sim_ref, bure_s, buim_s, stre_s, stim_s, *, tb, tl):
    @pl.when(pl.program_id(0) == 0)
    def _():
        stre_s[...] = s0re_ref[...]
        stim_s[...] = s0im_ref[...]

    x = jnp.concatenate([x_ref[:, t, :] for t in range(tl)], axis=0)
    u = jnp.dot(_rmsnorm(x, nw_ref[...]).astype(BF16), wu_ref[...], preferred_element_type=F32)
    ub = u.astype(BF16)

    n_kb, kw, cw = bre_ref.shape
    for kb in range(n_kb):
        uk = ub[:, kb * kw:(kb + 1) * kw]
        bure_s[:, kb * cw:(kb + 1) * cw] = jnp.dot(uk, bre_ref[kb], preferred_element_type=F32)
        buim_s[:, kb * cw:(kb + 1) * cw] = jnp.dot(uk, bim_ref[kb], preferred_element_type=F32)

    n_state = are_ref.shape[1]
    sc = min(S5_SCAN_COLS, n_state)
    for c0 in range(0, n_state, sc):
        cs = slice(c0, c0 + sc)
        a_re = jnp.broadcast_to(are_ref[:, cs], (tb, sc))
        a_im = jnp.broadcast_to(aim_ref[:, cs], (tb, sc))

        def step(t, carry, cs=cs, a_re=a_re, a_im=a_im):
            s_re, s_im = carry
            rows = pl.ds(pl.multiple_of(t * tb, tb), tb)
            n_re = a_re * s_re - a_im * s_im + bure_s[rows, cs]
            n_im = a_re * s_im + a_im * s_re + buim_s[rows, cs]
            bure_s[rows, cs] = n_re
            buim_s[rows, cs] = n_im
            return n_re, n_im

        s_re, s_im = lax.fori_loop(0, tl, step, (stre_s[:, cs], stim_s[:, cs]), unroll=True)
        stre_s[:, cs] = s_re
        stim_s[:, cs] = s_im

    n_ob, ckw, ocw = cre_ref.shape
    ys = []
    for ob in range(n_ob):
        ks = slice(ob * ckw, (ob + 1) * ckw)
        ys.append(_dot(bure_s[:, ks], cre_ref[ob]) - _dot(buim_s[:, ks], cim_ref[ob]))
    y = jnp.concatenate(ys, axis=1) + d_ref[...] * u
    y = jax.nn.gelu(y)
    out = y * jax.nn.sigmoid(_dot(y, gw_ref[...]) + gb_ref[...])
    for t in range(tl):
        y_ref[:, t, :] = out[t * tb:(t + 1) * tb, :]
    sre_ref[...] = stre_s[...]
    sim_ref[...] = stim_s[...]


def _s5_mixer(x3, nw, w_in_bf, bmat_re, bmat_im, cmat_re, cmat_im, d_row, glu_w_bf, glu_b, a_re, a_im,
              s0_re, s0_im, tl):
    b, l, d = x3.shape
    width = d_row.shape[1]
    n_state = a_re.shape[1]
    rows = b * tl
    args = (x3, nw, w_in_bf, bmat_re, bmat_im, cmat_re, cmat_im, d_row, glu_w_bf, glu_b,
            a_re, a_im, s0_re, s0_im)
    in_specs = ([pl.BlockSpec((b, tl, d), lambda i: (0, i, 0)), _full(nw.shape),
                 pl.BlockSpec((d, width), lambda i: (0, 0))] + [_full(a.shape) for a in args[3:]])
    y, s_re, s_im = pl.pallas_call(
        functools.partial(_s5_kernel, tb=b, tl=tl),
        grid=(l // tl,), in_specs=in_specs,
        out_specs=[pl.BlockSpec((b, tl, width), lambda i: (0, i, 0)), _full((b, n_state)), _full((b, n_state))],
        out_shape=[jax.ShapeDtypeStruct((b, l, width), F32),
                   jax.ShapeDtypeStruct((b, n_state), F32), jax.ShapeDtypeStruct((b, n_state), F32)],
        scratch_shapes=[pltpu.VMEM((rows, n_state), F32), pltpu.VMEM((rows, n_state), F32),
                        pltpu.VMEM((b, n_state), F32), pltpu.VMEM((b, n_state), F32)],
        compiler_params=_cparams(1), name="s5_mixer")(*args)
    return y.reshape(b * l, width), s_re, s_im


def _rope_kernel(pos_ref, inv_ref, cc_ref, ss_ref, *, dim_axis):
    ang = pos_ref[...] * inv_ref[...]
    idx = lax.broadcasted_iota(jnp.int32, ang.shape, dim_axis)
    cc_ref[...] = jnp.cos(ang)
    ss_ref[...] = jnp.where(idx < ang.shape[dim_axis] // 2, -jnp.sin(ang), jnp.sin(ang))


def _rope_tables(pos, dk, transposed=False):
    n = pos.shape[0]
    inv = 1.0 / (ROPE_BASE ** jnp.linspace(0.0, 1.0, dk // 2, dtype=F32))
    inv = jnp.concatenate([inv, inv])
    pos = pos.astype(F32)
    pos, inv, shape = (pos.reshape(1, n), inv.reshape(dk, 1), (dk, n)) if transposed else \
                      (pos.reshape(n, 1), inv.reshape(1, dk), (n, dk))
    return pl.pallas_call(
        functools.partial(_rope_kernel, dim_axis=0 if transposed else 1),
        in_specs=[_full(pos.shape), _full(inv.shape)],
        out_specs=[_full(shape), _full(shape)],
        out_shape=[jax.ShapeDtypeStruct(shape, F32)] * 2,
        name="rope_tables")(pos, inv)


def _rotate(t, cc, ss):
    return t * cc + pltpu.roll(t, t.shape[1] // 2, 1) * ss


def _rotate_t(t, cc, ss):
    half = t.shape[0] // 2
    return t * cc + jnp.concatenate([t[half:], t[:half]], axis=0) * ss


def _ret_consts(rows, c):
    lg = jnp.log(1.0 - 2.0 ** (-5.0 - jnp.arange(RET_HEADS, dtype=F32)))
    r = jnp.arange(rows)
    idx, blk = r % c, r // c
    diff = idx[:, None] - idx[None, :]
    same = blk[:, None] == blk[None, :]
    decay = jnp.where((same & (diff >= 0))[None],
                      jnp.exp(jnp.maximum(diff, 0)[None].astype(F32) * lg[:, None, None]), 0.0)
    wide = lambda v: jnp.broadcast_to(v[:, :, None], (RET_HEADS, rows, LANES))
    q_dec = wide(jnp.exp((idx + 1).astype(F32)[None, :] * lg[:, None]))
    k_dec = wide(jnp.exp((c - 1 - idx).astype(F32)[None, :] * lg[:, None]))
    chunk_dec = jnp.broadcast_to(jnp.exp(c * lg)[:, None, None], (RET_HEADS, SUBLANES, LANES))
    return decay, q_dec, k_dec, chunk_dec


def _ret_gate(o, g):
    o = o * lax.rsqrt(jnp.mean(o * o, axis=-1, keepdims=True) + NORM_EPS)
    return jax.nn.silu(g) * o


def _ret_long_kernel(q_ref, kt_ref, v_ref, g_ref, cc_ref, ss_ref, cct_ref, sst_ref, dec_ref, qd_ref, kdt_ref,
                     cd_ref, s0_ref, y_ref, so_ref, *, c, hp):
    dk = q_ref.shape[1] // hp
    scale = dk ** -0.5

    def chunk(i, states):
        rows = pl.ds(pl.multiple_of(i * c, c), c)
        cc, ss = cc_ref[rows, :], ss_ref[rows, :]
        cct, sst = cct_ref[i], sst_ref[i]
        new_states = []
        for j in range(hp):
            cols = slice(j * dk, (j + 1) * dk)
            q = _rotate(q_ref[rows, cols], cc, ss)
            kt = _rotate_t(kt_ref[i, cols, :], cct, sst) * scale
            v = v_ref[rows, cols]
            scores = _dot(q, kt) * dec_ref[j]
            o = _dot(scores, v) + _dot(q * qd_ref[j], states[j])
            new_states.append(cd_ref[j, 0:1, :] * states[j] + _dot(kt * kdt_ref[j], v))
            y_ref[rows, cols] = _ret_gate(o, g_ref[rows, cols]).astype(y_ref.dtype)
        return tuple(new_states)

    states = lax.fori_loop(0, q_ref.shape[0] // c, chunk, tuple(s0_ref[0, j] for j in range(hp)), unroll=4)
    for j in range(hp):
        so_ref[0, j] = states[j]


def _short_state_pass(qd, kt, v, s0_ref, so_ref, hd, decay_of, l):
    rows, dk = v.shape
    row = lax.broadcasted_iota(jnp.int32, (rows, dk), 0)
    v_cols = jnp.concatenate([jnp.where(row // l == b, v, 0.0).astype(BF16) for b in range(rows // l)], axis=1)
    kv = jnp.dot(kt.astype(BF16), v_cols, preferred_element_type=F32)
    win = 16
    per = win // l
    wrow = lax.broadcasted_iota(jnp.int32, (win, dk), 0)
    outs = []
    for w in range(rows // win):
        qw = qd[w * win:(w + 1) * win].astype(BF16)
        oi = jnp.zeros((win, dk), F32)
        for j in range(per):
            b = w * per + j
            s = s0_ref[b, hd]
            oi = jnp.where(wrow // l == j, jnp.dot(qw, s.astype(BF16), preferred_element_type=F32), oi)
            so_ref[b, hd] = decay_of(b) * s + kv[:, b * dk:(b + 1) * dk]
        outs.append(oi)
    return jnp.concatenate(outs, axis=0)


def _ret_short_kernel(q_ref, k_ref, v_ref, g_ref, cc_ref, ss_ref, dec_ref, qd_ref, kd_ref, cd_ref, s0_ref,
                      y_ref, so_ref, *, l, hp):
    dk = q_ref.shape[1] // hp
    scale = dk ** -0.5
    cc, ss = cc_ref[...], ss_ref[...]
    for hd in range(hp):
        cols = slice(hd * dk, (hd + 1) * dk)
        q = _rotate(q_ref[:, cols], cc, ss)
        k = _rotate(k_ref[:, cols], cc, ss) * scale
        v = v_ref[:, cols]
        o_intra = _dot(_dot_nt(q, k) * dec_ref[hd], v)
        cd = cd_ref[hd, 0:1, :]
        o_inter = _short_state_pass(q * qd_ref[hd], (k * kd_ref[hd]).T, v, s0_ref, so_ref, hd, lambda b: cd, l)
        y_ref[:, cols] = _ret_gate(o_intra + o_inter, g_ref[:, cols]).astype(y_ref.dtype)


def _retention(x, nw, w_in_bf, w_kt_bf, col0, pos, s0, n_seq, l):
    n = x.shape[0]
    dk = s0.shape[-1]
    width = RET_HEADS * dk
    out_shape = [jax.ShapeDtypeStruct((n, width), _mix_dtype(l)), jax.ShapeDtypeStruct(s0.shape, F32)]
    cspec = lambda a, hp: pl.BlockSpec((hp,) + a.shape[1:], lambda b, h: (h, 0, 0))
    cc, ss = _rope_tables(pos, dk)
    if l % RET_CHUNK == 0:
        c, hp = RET_CHUNK, RET_HEADS_PER_STEP
        n_hb = RET_HEADS // hp
        z, kt = _proj(x, nw, w_in_bf, ((col0, width), (col0 + 2 * width, 2 * width)), wt_bf=w_kt_bf, t_chunk=c)
        chunked = lambda t: jnp.swapaxes(t.reshape(dk, l // c, c), 0, 1)
        cct, sst = map(chunked, _rope_tables(pos, dk, transposed=True))
        decay, q_dec, k_dec, chunk_dec = _ret_consts(c, c)
        k_dec_t = jnp.swapaxes(k_dec, 1, 2)
        zspec = lambda field: pl.BlockSpec((l, hp * dk), lambda b, h: (b, field * n_hb + h))
        sspec = pl.BlockSpec((1, hp, dk, dk), lambda b, h: (b, h, 0, 0))
        return pl.pallas_call(
            functools.partial(_ret_long_kernel, c=c, hp=hp), grid=(n_seq, n_hb),
            in_specs=[zspec(0), pl.BlockSpec((l // c, hp * dk, c), lambda b, h: (b, h, 0)), zspec(1), zspec(2),
                      _full(cc.shape), _full(ss.shape), _full(cct.shape), _full(sst.shape),
                      cspec(decay, hp), cspec(q_dec, hp), cspec(k_dec_t, hp), cspec(chunk_dec, hp), sspec],
            out_specs=[pl.BlockSpec((l, hp * dk), lambda b, h: (b, h)), sspec],
            out_shape=out_shape, compiler_params=_cparams(2), name="retention")(
                z, kt, z, z, cc, ss, cct, sst, decay, q_dec, k_dec_t, chunk_dec, s0)
    rows, hp = LANES, SHORT_HEADS_PER_STEP
    per_blk = rows // l
    n_hb = RET_HEADS // hp
    z = _proj(x, nw, w_in_bf, segments=((col0, 4 * width),))
    decay, q_dec, k_dec, chunk_dec = _ret_consts(rows, l)
    zspec = lambda field: pl.BlockSpec((rows, hp * dk), lambda b, h: (b, field * n_hb + h))
    tspec = pl.BlockSpec((rows, dk), lambda b, h: (b, 0))
    sspec = pl.BlockSpec((per_blk, hp, dk, dk), lambda b, h: (b, h, 0, 0))
    return pl.pallas_call(
        functools.partial(_ret_short_kernel, l=l, hp=hp), grid=(n_seq // per_blk, n_hb),
        in_specs=[zspec(0), zspec(1), zspec(2), zspec(3), tspec, tspec,
                  cspec(decay, hp), cspec(q_dec, hp), cspec(k_dec, hp), cspec(chunk_dec, hp), sspec],
        out_specs=[pl.BlockSpec((rows, hp * dk), lambda b, h: (b, h)), sspec],
        out_shape=out_shape,
        compiler_params=_cparams(2), name="retention")(z, z, z, z, cc, ss, decay, q_dec, k_dec, chunk_dec, s0)


def _cumsum_rows(x, period):
    row = lax.broadcasted_iota(jnp.int32, x.shape, 0)
    s = 1
    while s < period:
        x = x + jnp.where(row % period >= s, pltpu.roll(x, s, 0), 0.0)
        s *= 2
    return x


def _row_of_block(x, period, offset):
    n, w = x.shape
    x3 = x.reshape(n // period, period, w)
    return jnp.broadcast_to(x3[:, offset:offset + 1, :], x3.shape).reshape(n, w)


def _hgrn_log2_forget(fl, lg, layer):
    e = jnp.exp(lg - jnp.max(lg, axis=0, keepdims=True))
    soft = e / jnp.sum(e, axis=0, keepdims=True)
    cum = soft[0:1]
    for i in range(1, layer + 1):
        cum = cum + soft[i:i + 1]
    lb = cum - soft[0:1]
    return jnp.log2(lb + (1.0 - lb) * _sigmoid(fl))


def _hgrn_out(o, gate, nw):
    o = o * lax.rsqrt(jnp.mean(o * o, axis=-1, keepdims=True) + NORM_EPS) * nw
    return o * gate


def _hgrn_levels(c):
    levels, half = [], HG_BLOCK
    while half < c:
        levels.append(half)
        half *= 2
    return levels


def _hgrn_long_kernel(q_ref, f_ref, v_ref, g_ref, nw_ref, s0_ref, y_ref, so_ref, *, c, hp):
    dk = q_ref.shape[1] // hp
    nw = nw_ref[...]
    nv = c // SUBLANES
    bv = HG_BLOCK // SUBLANES
    levels = _hgrn_levels(c)
    row = lax.broadcasted_iota(jnp.int32, (c, c), 0)
    col = lax.broadcasted_iota(jnp.int32, (c, c), 1)
    lvl = jnp.where((row // HG_BLOCK == col // HG_BLOCK) & (col <= row), 0, -1)
    for li, half in enumerate(levels, 1):
        lvl = jnp.where((row // (2 * half) == col // (2 * half)) & (row % (2 * half) >= half)
                        & (col % (2 * half) < half), li, lvl)
    sub = lax.broadcasted_iota(jnp.int32, (SUBLANES, dk), 0)
    zeros = jnp.zeros((SUBLANES, dk), F32)
    cat = lambda slabs: jnp.concatenate(slabs, axis=0)

    def one_head(rows, cols, s):
        q, v = q_ref[rows, cols], v_ref[rows, cols]
        lf = f_ref[rows, cols]
        k = 1.0 - jnp.exp2(lf)
        slabs = lambda a: [a[j * SUBLANES:(j + 1) * SUBLANES, :] for j in range(nv)]
        qs, ks = slabs(q), slabs(k)
        loc, run = [], [jnp.zeros((1, dk), F32)]
        for x in slabs(lf):
            step = 1
            while step < SUBLANES:
                x = x + jnp.where(sub >= step, pltpu.roll(x, step, 0), 0.0)
                step *= 2
            loc.append(x)
            run.append(run[-1] + x[SUBLANES - 1:, :])
        d0 = [loc[j] + (run[j] - run[j - j % bv]) if j % bv else loc[j] for j in range(nv)]
        scores = jnp.where(lvl == 0, _dot_nt(cat([qs[j] * jnp.exp2(d0[j]) for j in range(nv)]),
                                             cat([ks[j] * jnp.exp2(-d0[j]) for j in range(nv)])), 0.0)
        for li, half in enumerate(levels, 1):
            hv = half // SUBLANES
            q_side, k_side = [], []
            for j in range(nv):
                anchor = run[j - j % (2 * hv) + hv]
                if j % (2 * hv) >= hv:
                    q_side.append(qs[j] * jnp.exp2(loc[j] + (run[j] - anchor)))
                    k_side.append(zeros)
                else:
                    q_side.append(zeros)
                    k_side.append(ks[j] * jnp.exp2((anchor - run[j]) - loc[j]))
            scores = jnp.where(lvl == li, _dot_nt(cat(q_side), cat(k_side)), scores)
        o = _dot(scores, v) + _dot(cat([qs[j] * jnp.exp2(loc[j] + run[j]) for j in range(nv)]), s)
        k_hat = cat([ks[j] * jnp.exp2((run[nv] - run[j]) - loc[j]) for j in range(nv)])
        d_col = jnp.exp2(jnp.broadcast_to(run[nv], (dk, dk)).T)
        y_ref[rows, cols] = _hgrn_out(o, g_ref[rows, cols], nw).astype(y_ref.dtype)
        return d_col * s + _dot(k_hat.T, v)

    def chunk(i, states):
        rows = pl.ds(pl.multiple_of(i * c, c), c)
        return tuple(one_head(rows, slice(hd * dk, (hd + 1) * dk), states[hd]) for hd in range(hp))

    states = lax.fori_loop(0, q_ref.shape[0] // c, chunk, tuple(s0_ref[0, j] for j in range(hp)), unroll=4)
    for j in range(hp):
        so_ref[0, j] = states[j]


def _hgrn_short_kernel(q_ref, f_ref, v_ref, g_ref, nw_ref, s0_ref, y_ref, so_ref, *, l, hp):
    rows = q_ref.shape[0]
    dk = q_ref.shape[1] // hp
    row = lax.broadcasted_iota(jnp.int32, (rows, rows), 0)
    col = lax.broadcasted_iota(jnp.int32, (rows, rows), 1)
    causal = (row // l == col // l) & (col <= row)
    for hd in range(hp):
        cols = slice(hd * dk, (hd + 1) * dk)
        q, v = q_ref[:, cols], v_ref[:, cols]
        lf = f_ref[:, cols]
        k = 1.0 - jnp.exp2(lf)
        b = _cumsum_rows(lf, l)
        qe = q * jnp.exp2(b)
        o_intra = _dot(jnp.where(causal, _dot_nt(qe, k * jnp.exp2(-b)), 0.0), v)
        b_last = _row_of_block(b, l, l - 1)
        dect = jnp.exp2(b_last).T
        o_inter = _short_state_pass(qe, (k * jnp.exp2(b_last - b)).T, v, s0_ref, so_ref, hd,
                                    lambda s, dect=dect: dect[:, s * l:s * l + 1], l)
        y_ref[:, cols] = _hgrn_out(o_intra + o_inter, g_ref[:, cols], nw_ref[...]).astype(y_ref.dtype)


def _hgrn(z, norm_w, s0, n_seq, l):
    n = z.shape[0]
    dk = s0.shape[-1]
    if l % HG_CHUNK == 0:
        rows, n_blk, hp = l, n_seq, HG_HEADS_PER_STEP
        kern = functools.partial(_hgrn_long_kernel, c=HG_CHUNK, hp=hp)
        sspec = pl.BlockSpec((1, hp, dk, dk), lambda b, h: (b, h, 0, 0))
    else:
        assert HG_BLOCK % l == 0
        rows, hp = LANES, SHORT_HEADS_PER_STEP
        per_blk = rows // l
        n_blk = n_seq // per_blk
        kern = functools.partial(_hgrn_short_kernel, l=l, hp=hp)
        sspec = pl.BlockSpec((per_blk, hp, dk, dk), lambda b, h: (b, h, 0, 0))
    n_hb = HG_HEADS // hp
    zspec = lambda rows, field: pl.BlockSpec((rows, hp * dk), lambda b, h: (b, field * n_hb + h))
    y, s_new = pl.pallas_call(
        kern, grid=(n_blk, n_hb),
        in_specs=[zspec(rows, 2), zspec(rows, 0), zspec(rows, 3), zspec(rows, 1), _full((1, dk)), sspec],
        out_specs=[pl.BlockSpec((rows, hp * dk), lambda b, h: (b, h)), sspec],
        out_shape=[jax.ShapeDtypeStruct((n, HG_HEADS * dk), _mix_dtype(l)), jax.ShapeDtypeStruct(s0.shape, F32)],
        compiler_params=_cparams(2), name="hgrn")(z, z, z, z, norm_w, s0)
    return y, s_new


def _ffn_kernel(*refs, n_parts, long_mode, final_norm, l):
    x_ref, parts = refs[0], refs[1:1 + n_parts]
    (wo_ref, nw_ref, wg_ref, wu_ref, cw_ref, cb_ref, wd_ref, fnw_ref, buf_ref, o_ref, bufo_ref,
     act_s) = refs[1 + n_parts:13 + n_parts]
    d_ff = wg_ref.shape[1]
    if long_mode:
        tail_s = refs[13 + n_parts]
        tm = x_ref.shape[1]
        rc = min(ROW_TILE, tm)
        load = lambda ref, r0: ref[0, r0:r0 + rc, :]

        @pl.when(pl.program_id(1) == 0)
        def _():
            tail_s[...] = buf_ref[0]
    else:
        nb = x_ref.shape[0]
        tm = rc = l * nb
        load = lambda ref, r0: jnp.concatenate([ref[:, t, :] for t in range(l)], axis=0)

    xs = []
    for r0 in range(0, tm, rc):
        mix = jnp.concatenate([load(p, r0).astype(BF16) for p in parts], axis=1)
        x = load(x_ref, r0) + jnp.dot(mix, wo_ref[...], preferred_element_type=F32)
        xs.append(x)
        h = _rmsnorm(x, nw_ref[...]).astype(BF16)
        for c0 in range(0, d_ff, FF_CHUNK):
            cs = slice(c0, c0 + FF_CHUNK)
            g = jnp.dot(h, wg_ref[:, cs], preferred_element_type=F32)
            up = jnp.dot(h, wu_ref[:, cs], preferred_element_type=F32)
            if long_mode:
                row = lax.broadcasted_iota(jnp.int32, g.shape, 0)
                p1 = jnp.where(row < 1, tail_s[1:2, cs], pltpu.roll(g, 1, 0))
                p2 = jnp.where(row < 2, jnp.where(row == 0, tail_s[0:1, cs], tail_s[1:2, cs]),
                               pltpu.roll(g, 2, 0))
                tail_s[:, cs] = g[rc - (CONV_W - 1):, :]
            else:
                b0, b1 = buf_ref[:, 0, cs], buf_ref[:, 1, cs]
                p1 = jnp.concatenate([b1, g[:tm - nb]], axis=0)
                p2 = jnp.concatenate([b0, b1, g[:tm - 2 * nb]], axis=0)
                bufo_ref[:, 0, cs] = g[tm - 2 * nb:tm - nb]
                bufo_ref[:, 1, cs] = g[tm - nb:]
            conv = cb_ref[:, cs] + cw_ref[0:1, cs] * p2
            conv = conv + cw_ref[1:2, cs] * p1
            conv = conv + cw_ref[2:3, cs] * g
            act_s[r0:r0 + rc, cs] = (jax.nn.silu(conv) * up).astype(BF16)
    y = jnp.concatenate(xs, axis=0) + jnp.dot(act_s[...], wd_ref[...], preferred_element_type=F32)
    if final_norm:
        y = _rmsnorm(y, fnw_ref[...])
    if long_mode:
        o_ref[0] = y
        bufo_ref[0] = tail_s[...]
    else:
        for t in range(l):
            o_ref[:, t, :] = y[t * nb:(t + 1) * nb]


def _ffn(x, parts, layer, wo, nw, wg, wu, conv_w, conv_b, wd, fnw, buf, n_seq, l, final_norm):
    assert CONV_W == 3 and l >= CONV_W - 1
    n, d = x.shape
    d_ff = wg.shape[2]
    weights = (wo, nw, wg, wu, conv_w, conv_b, wd, fnw)
    stacked = (wg, wu, wd)

    def wspec(w, single_buffer):
        kw = dict(pipeline_mode=pl.Buffered(1)) if single_buffer else {}
        if any(w is s for s in stacked):
            return pl.BlockSpec((None,) + w.shape[1:], lambda *_: (layer, 0, 0), **kw)
        return pl.BlockSpec(w.shape, lambda *_: (0,) * w.ndim, **kw)
    kern = functools.partial(_ffn_kernel, n_parts=len(parts), final_norm=final_norm, l=l)
    buf_shape = jax.ShapeDtypeStruct((n_seq, CONV_W - 1, d_ff), F32)
    rows3 = lambda a: a.reshape(n_seq, l, a.shape[1])
    if l % ROW_TILE == 0:
        tm = FFN_ROW_TILE
        assert l % tm == 0
        rspec = lambda w: pl.BlockSpec((1, tm, w), lambda b, i: (b, i, 0))
        bspec = pl.BlockSpec((1, CONV_W - 1, d_ff), lambda b, i: (b, 0, 0))
        y, buf_new = pl.pallas_call(
            functools.partial(kern, long_mode=True),
            grid=(n_seq, l // tm),
            in_specs=([rspec(d)] + [rspec(p.shape[1]) for p in parts]
                      + [wspec(w, True) for w in weights]
                      + [pl.BlockSpec((None, 1, CONV_W - 1, d_ff), lambda b, i: (layer, b, 0, 0))]),
            out_specs=[rspec(d), bspec],
            out_shape=[jax.ShapeDtypeStruct((n_seq, l, d), F32), buf_shape],
            scratch_shapes=[pltpu.VMEM((tm, d_ff), BF16), pltpu.VMEM((CONV_W - 1, d_ff), F32)],
            compiler_params=_cparams(2), name="ffn")(rows3(x), *map(rows3, parts), *weights, buf)
        return y.reshape(n, d), buf_new
    y, buf_new = pl.pallas_call(
        functools.partial(kern, long_mode=False),
        grid=(1,),
        in_specs=([_full((n_seq, l, d))] + [_full((n_seq, l, p.shape[1])) for p in parts]
                  + [wspec(w, False) for w in weights]
                  + [pl.BlockSpec((None,) + buf.shape[1:], lambda *_: (layer, 0, 0, 0))]),
        out_specs=[_full((n_seq, l, d)), _full(buf.shape[1:])],
        out_shape=[jax.ShapeDtypeStruct((n_seq, l, d), F32), buf_shape],
        scratch_shapes=[pltpu.VMEM((n, d_ff), BF16)],
        compiler_params=_cparams(1), name="ffn")(
            rows3(x), *map(rows3, parts), *weights, buf)
    return y.reshape(n, d), buf_new


def _trunk(x3, pos, s5_re, s5_im, ret, hg, conv, p, s5_tl):
    b, l, d = x3.shape
    n = b * l
    x = x3.reshape(n, d)
    depth = p['norm_mix'].shape[0]
    new_re, new_im, new_ret, new_hg, new_conv = [], [], [], [], []
    for layer in range(depth):
        j = layer // 2
        nw = p['norm_mix'][layer].reshape(1, d)
        if layer % 2 == 0:
            s5w = p['s5'][j]
            width = s5w['d'].shape[1]
            y_s5, r, im = _s5_mixer(x.reshape(b, l, d), nw, p['w_in_ab'][j], s5w['bmat_re'], s5w['bmat_im'],
                                    s5w['cmat_re'], s5w['cmat_im'], s5w['d'], s5w['glu_w'], s5w['glu_b'],
                                    s5w['a_re'], s5w['a_im'], s5_re[j].reshape(b, -1), s5_im[j].reshape(b, -1),
                                    s5_tl)
            y_ret, st = _retention(x, nw, p['w_in_ab'][j], p['w_kt'][j], width, pos, ret[j], b, l)
            parts, w_out = [y_s5, y_ret], p['w_out_ab'][j]
            new_re.append(r.reshape(s5_re.shape[1:]))
            new_im.append(im.reshape(s5_im.shape[1:]))
            new_ret.append(st)
        else:
            hw = p['hg_lb_logits'].shape[1]
            z = _proj(x, nw, p['w_in_c'][j],
                      segments=((hw, hw, "log2_forget"), (3 * hw, hw, "silu"), (0, hw), (2 * hw, hw)),
                      gate_logits=p['hg_lb_logits'], gate_layer=layer)
            y_hg, st = _hgrn(z, p['hg_norm_w'][j].reshape(1, -1), hg[j], b, l)
            parts, w_out = [y_hg], p['w_out_c'][j]
            new_hg.append(st)
        x, buf = _ffn(x, parts, layer, w_out, p['norm_ffn'][layer].reshape(1, d), p['ffn_w_gate'],
                      p['ffn_w_up'], p['ffn_conv_w'][layer], p['ffn_conv_b'][layer].reshape(1, -1),
                      p['ffn_w_down'], p['norm_final'].reshape(1, d), conv, b, l,
                      final_norm=(layer == depth - 1))
        new_conv.append(buf)
    return (x.reshape(b, l, d), jnp.stack(new_re), jnp.stack(new_im), jnp.stack(new_ret),
            jnp.stack(new_hg), jnp.stack(new_conv))


def kernel(x_prompt, x_sample, state_s5_re, state_s5_im, state_ret, state_hgrn, state_ffn_conv, pos_sample, norm_mix, norm_ffn, norm_final, w_in_ab, s5_lam_re, s5_lam_im, s5_log_dt, s5_b_re, s5_b_im, s5_c_re, s5_c_im, s5_d, s5_glu_w, s5_glu_b, w_out_ab, w_in_c, hg_lb_logits, hg_norm_w, w_out_c, ffn_w_gate, ffn_w_up, ffn_conv_w, ffn_conv_b, ffn_w_down):
    n_ab, n_grp, n_st = s5_lam_re.shape
    ch = s5_b_re.shape[-1]
    width = n_grp * ch
    ret_width = state_ret.shape[2] * state_ret.shape[3]
    grp_per_blk = MXU_DIM // ch
    n_blk = n_grp // grp_per_blk
    s5 = []
    for j in range(n_ab):
        a_re, a_im, bb_re, bb_im = _s5_prep(s5_lam_re[j], s5_lam_im[j], s5_log_dt[j], s5_b_re[j], s5_b_im[j])
        bmat = lambda m: jnp.stack([_block_diag(blk, grp_per_blk)
                                    for blk in m.reshape(n_blk, grp_per_blk * ch, n_st)]).astype(BF16)
        cmat = lambda c: jnp.stack([_block_diag(blk, grp_per_blk).T for blk in
                                    c.reshape(n_blk, grp_per_blk * ch, n_st)]).astype(BF16)
        s5.append(dict(bmat_re=bmat(bb_re), bmat_im=bmat(bb_im),
                       cmat_re=cmat(s5_c_re[j]), cmat_im=cmat(s5_c_im[j]),
                       d=s5_d[j].reshape(1, width), glu_w=s5_glu_w[j].astype(BF16),
                       glu_b=s5_glu_b[j].reshape(1, width),
                       a_re=a_re.reshape(1, n_grp * n_st), a_im=a_im.reshape(1, n_grp * n_st)))
    per_layer_bf16 = lambda w: [w[i].astype(BF16) for i in range(w.shape[0])]
    p = dict(norm_mix=norm_mix, norm_ffn=norm_ffn, norm_final=norm_final, s5=s5,
             w_in_ab=per_layer_bf16(w_in_ab), w_out_ab=per_layer_bf16(w_out_ab),
             w_kt=[w_in_ab[j][:, width + ret_width:width + 2 * ret_width].T.astype(BF16) for j in range(n_ab)],
             w_in_c=per_layer_bf16(w_in_c), hg_lb_logits=hg_lb_logits, hg_norm_w=hg_norm_w,
             w_out_c=per_layer_bf16(w_out_c), ffn_w_gate=ffn_w_gate.astype(BF16),
             ffn_w_up=ffn_w_up.astype(BF16), ffn_conv_w=ffn_conv_w, ffn_conv_b=ffn_conv_b,
             ffn_w_down=ffn_w_down.astype(BF16))

    bp, lp, _ = x_prompt.shape
    z_s5 = jnp.zeros((n_ab, bp) + state_s5_re.shape[2:], F32)
    z_ret = jnp.zeros((n_ab, bp) + state_ret.shape[2:], F32)
    z_hg = jnp.zeros((state_hgrn.shape[0], bp) + state_hgrn.shape[2:], F32)
    z_conv = jnp.zeros((norm_mix.shape[0], bp) + state_ffn_conv.shape[2:], F32)
    outs_p = _trunk(x_prompt, jnp.arange(lp, dtype=jnp.int32), z_s5, z_s5, z_ret, z_hg, z_conv, p,
                    s5_tl=S5_ROW_TILE // bp)
    bs, ls, _ = x_sample.shape
    pos_s = (pos_sample[:, None] + jnp.arange(ls, dtype=jnp.int32)[None, :]).reshape(-1)
    outs_s = _trunk(x_sample, pos_s, state_s5_re, state_s5_im, state_ret, state_hgrn, state_ffn_conv, p,
                    s5_tl=ls)
    return (outs_p[0], outs_s[0]) + outs_p[1:] + outs_s[1:]
```

```python
import functools

import jax
import jax.numpy as jnp
from jax import lax
from jax.experimental import pallas as pl
from jax.experimental.pallas import tpu as pltpu

F32 = jnp.float32
BF16 = jnp.bfloat16

NORM_EPS = 1e-6
ROPE_BASE = 10000.0
S5_GROUP_CH = 16
RET_HEADS = 4
RET_CHUNK = 128
HG_HEADS = 8
HG_BLOCK = 16
HG_CHUNK = 128
RET_HEADS_PER_STEP = 4
HG_HEADS_PER_STEP = 4
SHORT_HEADS_PER_STEP = 2
CONV_W = 3

LANES = 128
SUBLANES = 8
MXU_DIM = 256
VMEM_LIMIT_BYTES = 56 * 1024 * 1024
ROW_TILE = 512
FFN_ROW_TILE = 1024
PROJ_ROW_TILE = 1024
S5_ROW_TILE = 1024
FF_CHUNK = 256
S5_SCAN_COLS = 512


def _cparams(n_grid_dims):
    return pltpu.CompilerParams(dimension_semantics=("arbitrary",) * n_grid_dims,
                                vmem_limit_bytes=VMEM_LIMIT_BYTES)


def _dot(a, b):
    return jnp.dot(a.astype(BF16), b.astype(BF16), preferred_element_type=F32)


def _dot_nt(a, b):
    return lax.dot_general(a.astype(BF16), b.astype(BF16), (((1,), (1,)), ((), ())),
                           preferred_element_type=F32)


def _rmsnorm(x, w):
    return x * lax.rsqrt(jnp.mean(x * x, axis=-1, keepdims=True) + NORM_EPS) * w


def _sigmoid(x):
    return 0.5 * jnp.tanh(0.5 * x) + 0.5


def _full(shape):
    nd = len(shape)
    return pl.BlockSpec(shape, lambda *_: (0,) * nd)


def _mix_dtype(l):
    return BF16 if l % ROW_TILE == 0 else F32


def _resident(shape):
    nd = len(shape)
    return pl.BlockSpec(shape, lambda *_: (0,) * nd, pipeline_mode=pl.Buffered(1))


def _proj_kernel(*refs, segments, col_chunk, row_chunk, t_chunk, gate_layer):
    refs = list(refs)
    x_ref, nw_ref, w_ref = refs[:3]
    rest = refs[3:]
    lg_ref = rest.pop(0) if gate_layer is not None else None
    wt_ref = rest.pop(0) if t_chunk else None
    o_ref = rest.pop(0)
    ot_ref = rest.pop(0) if t_chunk else None
    for r in range(0, x_ref.shape[0], row_chunk):
        rows = slice(r, r + row_chunk)
        h = _rmsnorm(x_ref[rows, :], nw_ref[...]).astype(BF16)
        out0 = 0
        for col0, width, kind in segments:
            for c in range(0, width, col_chunk):
                val = jnp.dot(h, w_ref[:, col0 + c:col0 + c + col_chunk], preferred_element_type=F32)
                if kind == "silu":
                    val = val * _sigmoid(val)
                elif kind == "log2_forget":
                    val = _hgrn_log2_forget(val, lg_ref[:, c:c + col_chunk], gate_layer)
                o_ref[rows, out0 + c:out0 + c + col_chunk] = val
            out0 += width
        if t_chunk:
            zt = _dot_nt(wt_ref[...], h)
            for j in range(row_chunk // t_chunk):
                ot_ref[r // t_chunk + j] = zt[:, j * t_chunk:(j + 1) * t_chunk]


def _proj(x, nw, w_bf, segments=None, wt_bf=None, t_chunk=0, gate_logits=None, gate_layer=None):
    n, d = x.shape
    segments = tuple((tuple(s) + (None,))[:3] for s in (segments or ((0, w_bf.shape[1]),)))
    n_out = sum(width for _, width, _ in segments)
    tm = min(PROJ_ROW_TILE, n)
    in_specs = [pl.BlockSpec((tm, d), lambda i: (i, 0)), _full((1, d)), _resident(w_bf.shape)]
    out_specs = [pl.BlockSpec((tm, n_out), lambda i: (i, 0))]
    out_shape = [jax.ShapeDtypeStruct((n, n_out), F32)]
    args = [x, nw, w_bf]
    if gate_layer is not None:
        in_specs.append(_full(gate_logits.shape))
        args.append(gate_logits)
    if t_chunk:
        t_width = wt_bf.shape[0]
        in_specs.append(_resident(wt_bf.shape))
        out_specs.append(pl.BlockSpec((tm // t_chunk, t_width, t_chunk), lambda i: (i, 0, 0)))
        out_shape.append(jax.ShapeDtypeStruct((n // t_chunk, t_width, t_chunk), F32))
        args.append(wt_bf)
    outs = pl.pallas_call(
        functools.partial(_proj_kernel, segments=segments, col_chunk=2 * MXU_DIM, row_chunk=min(ROW_TILE, tm),
                          t_chunk=t_chunk, gate_layer=gate_layer),
        grid=(n // tm,), in_specs=in_specs, out_specs=out_specs, out_shape=out_shape,
        compiler_params=_cparams(1), name="proj")(*args)
    return outs if t_chunk else outs[0]


def _s5_prep_kernel(lre_ref, lim_ref, ldt_ref, lre16_ref, lim16_ref, ldt16_ref, bre_ref, bim_ref,
                    are_ref, aim_ref, bbre_ref, bbim_ref):
    def disc(lre, lim, ldt):
        dt = jnp.exp(ldt)
        mag = jnp.exp(lre * dt)
        ang = lim * dt
        return mag * jnp.cos(ang), mag * jnp.sin(ang)

    ab_re, ab_im = disc(lre_ref[...], lim_ref[...], ldt_ref[...])
    are_ref[...] = ab_re
    aim_ref[...] = ab_im
    lre, lim = lre16_ref[...], lim16_ref[...]
    ab_re, ab_im = disc(lre, lim, ldt16_ref[...])
    nr, ni = ab_re - 1.0, ab_im
    den = lre * lre + lim * lim
    f_re = (nr * lre + ni * lim) / den
    f_im = (ni * lre - nr * lim) / den
    b_re, b_im = bre_ref[...], bim_ref[...]
    bbre_ref[...] = f_re * b_re - f_im * b_im
    bbim_ref[...] = f_re * b_im + f_im * b_re


def _s5_prep(lam_re, lam_im, log_dt, b_re, b_im):
    g, p = lam_re.shape
    ch = b_re.shape[-1]
    rep = lambda a: jnp.repeat(a, ch, axis=0)
    ldt = log_dt.reshape(g, 1)
    bt = lambda b: jnp.swapaxes(b, 1, 2).reshape(g * ch, p)
    args = (lam_re, lam_im, ldt, rep(lam_re), rep(lam_im), rep(ldt), bt(b_re), bt(b_im))
    return pl.pallas_call(
        _s5_prep_kernel,
        in_specs=[_full(a.shape) for a in args],
        out_specs=[_full((g, p)), _full((g, p)), _full((g * ch, p)), _full((g * ch, p))],
        out_shape=[jax.ShapeDtypeStruct((g, p), F32)] * 2 + [jax.ShapeDtypeStruct((g * ch, p), F32)] * 2,
        name="s5_prep")(*args)


def _block_diag(m, n_blk):
    r, c = m.shape[0] // n_blk, m.shape[1]
    eye = jnp.eye(n_blk, dtype=m.dtype)
    return (m.reshape(n_blk, r, 1, c) * eye[:, None, :, None]).reshape(n_blk * r, n_blk * c)


def _s5_kernel(x_ref, nw_ref, wu_ref, bre_ref, bim_ref, cre_ref, cim_ref, d_ref, gw_ref, gb_ref,
               are_ref, aim_ref, s0re_ref, s0im_ref,
               y_ref, sre_ref, sim_ref, bure_s, buim_s, stre_s, stim_s, *, tb, tl):
    @pl.when(pl.program_id(0) == 0)
    def _():
        stre_s[...] = s0re_ref[...]
        stim_s[...] = s0im_ref[...]

    x = jnp.concatenate([x_ref[:, t, :] for t in range(tl)], axis=0)
    u = jnp.dot(_rmsnorm(x, nw_ref[...]).astype(BF16), wu_ref[...], preferred_element_type=F32)
    ub = u.astype(BF16)

    n_kb, kw, cw = bre_ref.shape
    for kb in range(n_kb):
        uk = ub[:, kb * kw:(kb + 1) * kw]
        bure_s[:, kb * cw:(kb + 1) * cw] = jnp.dot(uk, bre_ref[kb], preferred_element_type=F32)
        buim_s[:, kb * cw:(kb + 1) * cw] = jnp.dot(uk, bim_ref[kb], preferred_element_type=F32)

    n_state = are_ref.shape[1]
    sc = min(S5_SCAN_COLS, n_state)
    for c0 in range(0, n_state, sc):
        cs = slice(c0, c0 + sc)
        a_re = jnp.broadcast_to(are_ref[:, cs], (tb, sc))
        a_im = jnp.broadcast_to(aim_ref[:, cs], (tb, sc))

        def step(t, carry, cs=cs, a_re=a_re, a_im=a_im):
            s_re, s_im = carry
            rows = pl.ds(pl.multiple_of(t * tb, tb), tb)
            n_re = a_re * s_re - a_im * s_im + bure_s[rows, cs]
            n_im = a_re * s_im + a_im * s_re + buim_s[rows, cs]
            bure_s[rows, cs] = n_re
            buim_s[rows, cs] = n_im
            return n_re, n_im

        s_re, s_im = lax.fori_loop(0, tl, step, (stre_s[:, cs], stim_s[:, cs]), unroll=True)
        stre_s[:, cs] = s_re
        stim_s[:, cs] = s_im

    n_ob, ckw, ocw = cre_ref.shape
    ys = []
    for ob in range(n_ob):
        ks = slice(ob * ckw, (ob + 1) * ckw)
        ys.append(_dot(bure_s[:, ks], cre_ref[ob]) - _dot(buim_s[:, ks], cim_ref[ob]))
    y = jnp.concatenate(ys, axis=1) + d_ref[...] * u
    y = jax.nn.gelu(y)
    out = y * jax.nn.sigmoid(_dot(y, gw_ref[...]) + gb_ref[...])
    for t in range(tl):
        y_ref[:, t, :] = out[t * tb:(t + 1) * tb, :]
    sre_ref[...] = stre_s[...]
    sim_ref[...] = stim_s[...]


def _s5_mixer(x3, nw, w_in_bf, bmat_re, bmat_im, cmat_re, cmat_im, d_row, glu_w_bf, glu_b, a_re, a_im,
              s0_re, s0_im, tl):
    b, l, d = x3.shape
    width = d_row.shape[1]
    n_state = a_re.shape[1]
    rows = b * tl
    args = (x3, nw, w_in_bf, bmat_re, bmat_im, cmat_re, cmat_im, d_row, glu_w_bf, glu_b,
            a_re, a_im, s0_re, s0_im)
    in_specs = ([pl.BlockSpec((b, tl, d), lambda i: (0, i, 0)), _full(nw.shape),
                 pl.BlockSpec((d, width), lambda i: (0, 0))] + [_full(a.shape) for a in args[3:]])
    y, s_re, s_im = pl.pallas_call(
        functools.partial(_s5_kernel, tb=b, tl=tl),
        grid=(l // tl,), in_specs=in_specs,
        out_specs=[pl.BlockSpec((b, tl, width), lambda i: (0, i, 0)), _full((b, n_state)), _full((b, n_state))],
        out_shape=[jax.ShapeDtypeStruct((b, l, width), F32),
                   jax.ShapeDtypeStruct((b, n_state), F32), jax.ShapeDtypeStruct((b, n_state), F32)],
        scratch_shapes=[pltpu.VMEM((rows, n_state), F32), pltpu.VMEM((rows, n_state), F32),
                        pltpu.VMEM((b, n_state), F32), pltpu.VMEM((b, n_state), F32)],
        compiler_params=_cparams(1), name="s5_mixer")(*args)
    return y.reshape(b * l, width), s_re, s_im


def _rope_kernel(pos_ref, inv_ref, cc_ref, ss_ref, *t_refs, t_chunk):
    ang = pos_ref[...] * inv_ref[...]
    lane = lax.broadcasted_iota(jnp.int32, ang.shape, 1)
    cc = jnp.cos(ang)
    ss = jnp.where(lane < ang.shape[1] // 2, -jnp.sin(ang), jnp.sin(ang))
    cc_ref[...] = cc
    ss_ref[...] = ss
    if t_chunk:
        cct_ref, sst_ref = t_refs
        for i in range(cct_ref.shape[0]):
            cct_ref[i] = cc[i * t_chunk:(i + 1) * t_chunk].T
            sst_ref[i] = ss[i * t_chunk:(i + 1) * t_chunk].T


def _rope_tables(pos, dk, t_chunk=0):
    n = pos.shape[0]
    inv = 1.0 / (ROPE_BASE ** jnp.linspace(0.0, 1.0, dk // 2, dtype=F32))
    inv = jnp.concatenate([inv, inv]).reshape(1, dk)
    shapes = [(n, dk)] * 2 + ([(n // t_chunk, dk, t_chunk)] * 2 if t_chunk else [])
    return pl.pallas_call(
        functools.partial(_rope_kernel, t_chunk=t_chunk),
        in_specs=[_full((n, 1)), _full((1, dk))],
        out_specs=[_full(s) for s in shapes],
        out_shape=[jax.ShapeDtypeStruct(s, F32) for s in shapes],
        name="rope_tables")(pos.astype(F32).reshape(n, 1), inv)


def _rotate(t, cc, ss):
    return t * cc + pltpu.roll(t, t.shape[1] // 2, 1) * ss


def _rotate_t(t, cc, ss):
    half = t.shape[0] // 2
    return t * cc + jnp.concatenate([t[half:], t[:half]], axis=0) * ss


def _ret_consts(rows, c):
    lg = jnp.log(1.0 - 2.0 ** (-5.0 - jnp.arange(RET_HEADS, dtype=F32)))
    r = jnp.arange(rows)
    idx, blk = r % c, r // c
    diff = idx[:, None] - idx[None, :]
    same = blk[:, None] == blk[None, :]
    decay = jnp.where((same & (diff >= 0))[None],
                      jnp.exp(jnp.maximum(diff, 0)[None].astype(F32) * lg[:, None, None]), 0.0)
    wide = lambda v: jnp.broadcast_to(v[:, :, None], (RET_HEADS, rows, LANES))
    q_dec = wide(jnp.exp((idx + 1).astype(F32)[None, :] * lg[:, None]))
    k_dec = wide(jnp.exp((c - 1 - idx).astype(F32)[None, :] * lg[:, None]))
    chunk_dec = jnp.broadcast_to(jnp.exp(c * lg)[:, None, None], (RET_HEADS, SUBLANES, LANES))
    return decay, q_dec, k_dec, chunk_dec


def _ret_gate(o, g):
    o = o * lax.rsqrt(jnp.mean(o * o, axis=-1, keepdims=True) + NORM_EPS)
    return jax.nn.silu(g) * o


def _ret_long_kernel(q_ref, kt_ref, v_ref, g_ref, cc_ref, ss_ref, cct_ref, sst_ref, dec_ref, qd_ref, kdt_ref,
                     cd_ref, s0_ref, y_ref, so_ref, *, c, hp):
    dk = q_ref.shape[1] // hp
    scale = dk ** -0.5

    def chunk(i, states):
        rows = pl.ds(pl.multiple_of(i * c, c), c)
        cc, ss = cc_ref[rows, :], ss_ref[rows, :]
        cct, sst = cct_ref[i], sst_ref[i]
        new_states = []
        for j in range(hp):
            cols = slice(j * dk, (j + 1) * dk)
            q = _rotate(q_ref[rows, cols], cc, ss)
            kt = _rotate_t(kt_ref[i, cols, :], cct, sst) * scale
            v = v_ref[rows, cols]
            scores = _dot(q, kt) * dec_ref[j]
            o = _dot(scores, v) + _dot(q * qd_ref[j], states[j])
            new_states.append(cd_ref[j, 0:1, :] * states[j] + _dot(kt * kdt_ref[j], v))
            y_ref[rows, cols] = _ret_gate(o, g_ref[rows, cols]).astype(y_ref.dtype)
        return tuple(new_states)

    states = lax.fori_loop(0, q_ref.shape[0] // c, chunk, tuple(s0_ref[0, j] for j in range(hp)), unroll=4)
    for j in range(hp):
        so_ref[0, j] = states[j]


def _short_state_pass(qd, kt, v, s0_ref, so_ref, hd, decay_of, l):
    rows, dk = v.shape
    row = lax.broadcasted_iota(jnp.int32, (rows, dk), 0)
    v_cols = jnp.concatenate([jnp.where(row // l == b, v, 0.0).astype(BF16) for b in range(rows // l)], axis=1)
    kv = jnp.dot(kt.astype(BF16), v_cols, preferred_element_type=F32)
    win = 16
    per = win // l
    wrow = lax.broadcasted_iota(jnp.int32, (win, dk), 0)
    outs = []
    for w in range(rows // win):
        qw = qd[w * win:(w + 1) * win].astype(BF16)
        oi = jnp.zeros((win, dk), F32)
        for j in range(per):
            b = w * per + j
            s = s0_ref[b, hd]
            oi = jnp.where(wrow // l == j, jnp.dot(qw, s.astype(BF16), preferred_element_type=F32), oi)
            so_ref[b, hd] = decay_of(b) * s + kv[:, b * dk:(b + 1) * dk]
        outs.append(oi)
    return jnp.concatenate(outs, axis=0)


def _ret_short_kernel(q_ref, k_ref, v_ref, g_ref, cc_ref, ss_ref, dec_ref, qd_ref, kd_ref, cd_ref, s0_ref,
                      y_ref, so_ref, *, l, hp):
    dk = q_ref.shape[1] // hp
    scale = dk ** -0.5
    cc, ss = cc_ref[...], ss_ref[...]
    for hd in range(hp):
        cols = slice(hd * dk, (hd + 1) * dk)
        q = _rotate(q_ref[:, cols], cc, ss)
        k = _rotate(k_ref[:, cols], cc, ss) * scale
        v = v_ref[:, cols]
        o_intra = _dot(_dot_nt(q, k) * dec_ref[hd], v)
        cd = cd_ref[hd, 0:1, :]
        o_inter = _short_state_pass(q * qd_ref[hd], (k * kd_ref[hd]).T, v, s0_ref, so_ref, hd, lambda b: cd, l)
        y_ref[:, cols] = _ret_gate(o_intra + o_inter, g_ref[:, cols]).astype(y_ref.dtype)


def _retention(x, nw, w_in_bf, w_kt_bf, col0, pos, s0, n_seq, l):
    n = x.shape[0]
    dk = s0.shape[-1]
    width = RET_HEADS * dk
    out_shape = [jax.ShapeDtypeStruct((n, width), _mix_dtype(l)), jax.ShapeDtypeStruct(s0.shape, F32)]
    cspec = lambda a, hp: pl.BlockSpec((hp,) + a.shape[1:], lambda b, h: (h, 0, 0))
    if l % RET_CHUNK == 0:
        c, hp = RET_CHUNK, RET_HEADS_PER_STEP
        n_hb = RET_HEADS // hp
        z, kt = _proj(x, nw, w_in_bf, ((col0, width), (col0 + 2 * width, 2 * width)), wt_bf=w_kt_bf, t_chunk=c)
        cc, ss, cct, sst = _rope_tables(pos, dk, t_chunk=c)
        decay, q_dec, k_dec, chunk_dec = _ret_consts(c, c)
        k_dec_t = jnp.swapaxes(k_dec, 1, 2)
        zspec = lambda field: pl.BlockSpec((l, hp * dk), lambda b, h: (b, field * n_hb + h))
        sspec = pl.BlockSpec((1, hp, dk, dk), lambda b, h: (b, h, 0, 0))
        return pl.pallas_call(
            functools.partial(_ret_long_kernel, c=c, hp=hp), grid=(n_seq, n_hb),
            in_specs=[zspec(0), pl.BlockSpec((l // c, hp * dk, c), lambda b, h: (b, h, 0)), zspec(1), zspec(2),
                      _full(cc.shape), _full(ss.shape), _full(cct.shape), _full(sst.shape),
                      cspec(decay, hp), cspec(q_dec, hp), cspec(k_dec_t, hp), cspec(chunk_dec, hp), sspec],
            out_specs=[pl.BlockSpec((l, hp * dk), lambda b, h: (b, h)), sspec],
            out_shape=out_shape, compiler_params=_cparams(2), name="retention")(
                z, kt, z, z, cc, ss, cct, sst, decay, q_dec, k_dec_t, chunk_dec, s0)
    rows, hp = LANES, SHORT_HEADS_PER_STEP
    per_blk = rows // l
    n_hb = RET_HEADS // hp
    z = _proj(x, nw, w_in_bf, segments=((col0, 4 * width),))
    cc, ss = _rope_tables(pos, dk)
    decay, q_dec, k_dec, chunk_dec = _ret_consts(rows, l)
    zspec = lambda field: pl.BlockSpec((rows, hp * dk), lambda b, h: (b, field * n_hb + h))
    tspec = pl.BlockSpec((rows, dk), lambda b, h: (b, 0))
    sspec = pl.BlockSpec((per_blk, hp, dk, dk), lambda b, h: (b, h, 0, 0))
    return pl.pallas_call(
        functools.partial(_ret_short_kernel, l=l, hp=hp), grid=(n_seq // per_blk, n_hb),
        in_specs=[zspec(0), zspec(1), zspec(2), zspec(3), tspec, tspec,
                  cspec(decay, hp), cspec(q_dec, hp), cspec(k_dec, hp), cspec(chunk_dec, hp), sspec],
        out_specs=[pl.BlockSpec((rows, hp * dk), lambda b, h: (b, h)), sspec],
        out_shape=out_shape,
        compiler_params=_cparams(2), name="retention")(z, z, z, z, cc, ss, decay, q_dec, k_dec, chunk_dec, s0)


def _cumsum_rows(x, period):
    row = lax.broadcasted_iota(jnp.int32, x.shape, 0)
    s = 1
    while s < period:
        x = x + jnp.where(row % period >= s, pltpu.roll(x, s, 0), 0.0)
        s *= 2
    return x


def _row_of_block(x, period, offset):
    n, w = x.shape
    x3 = x.reshape(n // period, period, w)
    return jnp.broadcast_to(x3[:, offset:offset + 1, :], x3.shape).reshape(n, w)


def _hgrn_log2_forget(fl, lg, layer):
    e = jnp.exp(lg - jnp.max(lg, axis=0, keepdims=True))
    soft = e / jnp.sum(e, axis=0, keepdims=True)
    cum = soft[0:1]
    for i in range(1, layer + 1):
        cum = cum + soft[i:i + 1]
    lb = cum - soft[0:1]
    return jnp.log2(lb + (1.0 - lb) * _sigmoid(fl))


def _hgrn_out(o, gate, nw):
    o = o * lax.rsqrt(jnp.mean(o * o, axis=-1, keepdims=True) + NORM_EPS) * nw
    return o * gate


def _hgrn_levels(c):
    levels, half = [], HG_BLOCK
    while half < c:
        levels.append(half)
        half *= 2
    return levels


def _hgrn_long_kernel(q_ref, f_ref, v_ref, g_ref, nw_ref, s0_ref, y_ref, so_ref, *, c, hp):
    dk = q_ref.shape[1] // hp
    nw = nw_ref[...]
    nv = c // SUBLANES
    bv = HG_BLOCK // SUBLANES
    levels = _hgrn_levels(c)
    row = lax.broadcasted_iota(jnp.int32, (c, c), 0)
    col = lax.broadcasted_iota(jnp.int32, (c, c), 1)
    lvl = jnp.where((row // HG_BLOCK == col // HG_BLOCK) & (col <= row), 0, -1)
    for li, half in enumerate(levels, 1):
        lvl = jnp.where((row // (2 * half) == col // (2 * half)) & (row % (2 * half) >= half)
                        & (col % (2 * half) < half), li, lvl)
    sub = lax.broadcasted_iota(jnp.int32, (SUBLANES, dk), 0)
    zeros = jnp.zeros((SUBLANES, dk), F32)
    cat = lambda slabs: jnp.concatenate(slabs, axis=0)

    def one_head(rows, cols, s):
        q, v = q_ref[rows, cols], v_ref[rows, cols]
        lf = f_ref[rows, cols]
        k = 1.0 - jnp.exp2(lf)
        slabs = lambda a: [a[j * SUBLANES:(j + 1) * SUBLANES, :] for j in range(nv)]
        qs, ks = slabs(q), slabs(k)
        loc, run = [], [jnp.zeros((1, dk), F32)]
        for x in slabs(lf):
            step = 1
            while step < SUBLANES:
                x = x + jnp.where(sub >= step, pltpu.roll(x, step, 0), 0.0)
                step *= 2
            loc.append(x)
            run.append(run[-1] + x[SUBLANES - 1:, :])
        d0 = [loc[j] + (run[j] - run[j - j % bv]) if j % bv else loc[j] for j in range(nv)]
        scores = jnp.where(lvl == 0, _dot_nt(cat([qs[j] * jnp.exp2(d0[j]) for j in range(nv)]),
                                             cat([ks[j] * jnp.exp2(-d0[j]) for j in range(nv)])), 0.0)
        for li, half in enumerate(levels, 1):
            hv = half // SUBLANES
            q_side, k_side = [], []
            for j in range(nv):
                anchor = run[j - j % (2 * hv) + hv]
                if j % (2 * hv) >= hv:
                    q_side.append(qs[j] * jnp.exp2(loc[j] + (run[j] - anchor)))
                    k_side.append(zeros)
                else:
                    q_side.append(zeros)
                    k_side.append(ks[j] * jnp.exp2((anchor - run[j]) - loc[j]))
            scores = jnp.where(lvl == li, _dot_nt(cat(q_side), cat(k_side)), scores)
        o = _dot(scores, v) + _dot(cat([qs[j] * jnp.exp2(loc[j] + run[j]) for j in range(nv)]), s)
        k_hat = cat([ks[j] * jnp.exp2((run[nv] - run[j]) - loc[j]) for j in range(nv)])
        d_col = jnp.exp2(jnp.broadcast_to(run[nv], (dk, dk)).T)
        y_ref[rows, cols] = _hgrn_out(o, g_ref[rows, cols], nw).astype(y_ref.dtype)
        return d_col * s + _dot(k_hat.T, v)

    def chunk(i, states):
        rows = pl.ds(pl.multiple_of(i * c, c), c)
        return tuple(one_head(rows, slice(hd * dk, (hd + 1) * dk), states[hd]) for hd in range(hp))

    states = lax.fori_loop(0, q_ref.shape[0] // c, chunk, tuple(s0_ref[0, j] for j in range(hp)), unroll=4)
    for j in range(hp):
        so_ref[0, j] = states[j]


def _hgrn_short_kernel(q_ref, f_ref, v_ref, g_ref, nw_ref, s0_ref, y_ref, so_ref, *, l, hp):
    rows = q_ref.shape[0]
    dk = q_ref.shape[1] // hp
    row = lax.broadcasted_iota(jnp.int32, (rows, rows), 0)
    col = lax.broadcasted_iota(jnp.int32, (rows, rows), 1)
    causal = (row // l == col // l) & (col <= row)
    for hd in range(hp):
        cols = slice(hd * dk, (hd + 1) * dk)
        q, v = q_ref[:, cols], v_ref[:, cols]
        lf = f_ref[:, cols]
        k = 1.0 - jnp.exp2(lf)
        b = _cumsum_rows(lf, l)
        qe = q * jnp.exp2(b)
        o_intra = _dot(jnp.where(causal, _dot_nt(qe, k * jnp.exp2(-b)), 0.0), v)
        b_last = _row_of_block(b, l, l - 1)
        dect = jnp.exp2(b_last).T
        o_inter = _short_state_pass(qe, (k * jnp.exp2(b_last - b)).T, v, s0_ref, so_ref, hd,
                                    lambda s, dect=dect: dect[:, s * l:s * l + 1], l)
        y_ref[:, cols] = _hgrn_out(o_intra + o_inter, g_ref[:, cols], nw_ref[...]).astype(y_ref.dtype)


def _hgrn(z, norm_w, s0, n_seq, l):
    n = z.shape[0]
    dk = s0.shape[-1]
    if l % HG_CHUNK == 0:
        rows, n_blk, hp = l, n_seq, HG_HEADS_PER_STEP
        kern = functools.partial(_hgrn_long_kernel, c=HG_CHUNK, hp=hp)
        sspec = pl.BlockSpec((1, hp, dk, dk), lambda b, h: (b, h, 0, 0))
    else:
        assert HG_BLOCK % l == 0
        rows, hp = LANES, SHORT_HEADS_PER_STEP
        per_blk = rows // l
        n_blk = n_seq // per_blk
        kern = functools.partial(_hgrn_short_kernel, l=l, hp=hp)
        sspec = pl.BlockSpec((per_blk, hp, dk, dk), lambda b, h: (b, h, 0, 0))
    n_hb = HG_HEADS // hp
    zspec = lambda rows, field: pl.BlockSpec((rows, hp * dk), lambda b, h: (b, field * n_hb + h))
    y, s_new = pl.pallas_call(
        kern, grid=(n_blk, n_hb),
        in_specs=[zspec(rows, 2), zspec(rows, 0), zspec(rows, 3), zspec(rows, 1), _full((1, dk)), sspec],
        out_specs=[pl.BlockSpec((rows, hp * dk), lambda b, h: (b, h)), sspec],
        out_shape=[jax.ShapeDtypeStruct((n, HG_HEADS * dk), _mix_dtype(l)), jax.ShapeDtypeStruct(s0.shape, F32)],
        compiler_params=_cparams(2), name="hgrn")(z, z, z, z, norm_w, s0)
    return y, s_new


def _ffn_kernel(*refs, n_parts, long_mode, final_norm, l):
    x_ref, parts = refs[0], refs[1:1 + n_parts]
    (wo_ref, nw_ref, wg_ref, wu_ref, cw_ref, cb_ref, wd_ref, fnw_ref, buf_ref, o_ref, bufo_ref,
     act_s) = refs[1 + n_parts:13 + n_parts]
    d_ff = wg_ref.shape[1]
    if long_mode:
        tail_s = refs[13 + n_parts]
        tm = x_ref.shape[1]
        rc = min(ROW_TILE, tm)
        load = lambda ref, r0: ref[0, r0:r0 + rc, :]

        @pl.when(pl.program_id(1) == 0)
        def _():
            tail_s[...] = buf_ref[0]
    else:
        nb = x_ref.shape[0]
        tm = rc = l * nb
        load = lambda ref, r0: jnp.concatenate([ref[:, t, :] for t in range(l)], axis=0)

    xs = []
    for r0 in range(0, tm, rc):
        mix = jnp.concatenate([load(p, r0).astype(BF16) for p in parts], axis=1)
        x = load(x_ref, r0) + jnp.dot(mix, wo_ref[...], preferred_element_type=F32)
        xs.append(x)
        h = _rmsnorm(x, nw_ref[...]).astype(BF16)
        for c0 in range(0, d_ff, FF_CHUNK):
            cs = slice(c0, c0 + FF_CHUNK)
            g = jnp.dot(h, wg_ref[:, cs], preferred_element_type=F32)
            up = jnp.dot(h, wu_ref[:, cs], preferred_element_type=F32)
            if long_mode:
                row = lax.broadcasted_iota(jnp.int32, g.shape, 0)
                p1 = jnp.where(row < 1, tail_s[1:2, cs], pltpu.roll(g, 1, 0))
                p2 = jnp.where(row < 2, jnp.where(row == 0, tail_s[0:1, cs], tail_s[1:2, cs]),
                               pltpu.roll(g, 2, 0))
                tail_s[:, cs] = g[rc - (CONV_W - 1):, :]
            else:
                b0, b1 = buf_ref[:, 0, cs], buf_ref[:, 1, cs]
                p1 = jnp.concatenate([b1, g[:tm - nb]], axis=0)
                p2 = jnp.concatenate([b0, b1, g[:tm - 2 * nb]], axis=0)
                bufo_ref[:, 0, cs] = g[tm - 2 * nb:tm - nb]
                bufo_ref[:, 1, cs] = g[tm - nb:]
            conv = cb_ref[:, cs] + cw_ref[0:1, cs] * p2
            conv = conv + cw_ref[1:2, cs] * p1
            conv = conv + cw_ref[2:3, cs] * g
            act_s[r0:r0 + rc, cs] = (jax.nn.silu(conv) * up).astype(BF16)
    y = jnp.concatenate(xs, axis=0) + jnp.dot(act_s[...], wd_ref[...], preferred_element_type=F32)
    if final_norm:
        y = _rmsnorm(y, fnw_ref[...])
    if long_mode:
        o_ref[0] = y
        bufo_ref[0] = tail_s[...]
    else:
        for t in range(l):
            o_ref[:, t, :] = y[t * nb:(t + 1) * nb]


def _ffn(x, parts, layer, wo, nw, wg, wu, conv_w, conv_b, wd, fnw, buf, n_seq, l, final_norm):
    assert CONV_W == 3 and l >= CONV_W - 1
    n, d = x.shape
    d_ff = wg.shape[2]
    weights = (wo, nw, wg, wu, conv_w, conv_b, wd, fnw)
    stacked = (wg, wu, wd)

    def wspec(w, single_buffer):
        kw = dict(pipeline_mode=pl.Buffered(1)) if single_buffer else {}
        if any(w is s for s in stacked):
            return pl.BlockSpec((None,) + w.shape[1:], lambda *_: (layer, 0, 0), **kw)
        return pl.BlockSpec(w.shape, lambda *_: (0,) * w.ndim, **kw)
    kern = functools.partial(_ffn_kernel, n_parts=len(parts), final_norm=final_norm, l=l)
    buf_shape = jax.ShapeDtypeStruct((n_seq, CONV_W - 1, d_ff), F32)
    rows3 = lambda a: a.reshape(n_seq, l, a.shape[1])
    if l % ROW_TILE == 0:
        tm = FFN_ROW_TILE
        assert l % tm == 0
        rspec = lambda w: pl.BlockSpec((1, tm, w), lambda b, i: (b, i, 0))
        bspec = pl.BlockSpec((1, CONV_W - 1, d_ff), lambda b, i: (b, 0, 0))
        y, buf_new = pl.pallas_call(
            functools.partial(kern, long_mode=True),
            grid=(n_seq, l // tm),
            in_specs=([rspec(d)] + [rspec(p.shape[1]) for p in parts]
                      + [wspec(w, True) for w in weights]
                      + [pl.BlockSpec((None, 1, CONV_W - 1, d_ff), lambda b, i: (layer, b, 0, 0))]),
            out_specs=[rspec(d), bspec],
            out_shape=[jax.ShapeDtypeStruct((n_seq, l, d), F32), buf_shape],
            scratch_shapes=[pltpu.VMEM((tm, d_ff), BF16), pltpu.VMEM((CONV_W - 1, d_ff), F32)],
            compiler_params=_cparams(2), name="ffn")(rows3(x), *map(rows3, parts), *weights, buf)
        return y.reshape(n, d), buf_new
    y, buf_new = pl.pallas_call(
        functools.partial(kern, long_mode=False),
        grid=(1,),
        in_specs=([_full((n_seq, l, d))] + [_full((n_seq, l, p.shape[1])) for p in parts]
                  + [wspec(w, False) for w in weights]
                  + [pl.BlockSpec((None,) + buf.shape[1:], lambda *_: (layer, 0, 0, 0))]),
        out_specs=[_full((n_seq, l, d)), _full(buf.shape[1:])],
        out_shape=[jax.ShapeDtypeStruct((n_seq, l, d), F32), buf_shape],
        scratch_shapes=[pltpu.VMEM((n, d_ff), BF16)],
        compiler_params=_cparams(1), name="ffn")(
            rows3(x), *map(rows3, parts), *weights, buf)
    return y.reshape(n, d), buf_new


def _trunk(x3, pos, s5_re, s5_im, ret, hg, conv, p, s5_tl):
    b, l, d = x3.shape
    n = b * l
    x = x3.reshape(n, d)
    depth = p['norm_mix'].shape[0]
    new_re, new_im, new_ret, new_hg, new_conv = [], [], [], [], []
    for layer in range(depth):
        j = layer // 2
        nw = p['norm_mix'][layer].reshape(1, d)
        if layer % 2 == 0:
            s5w = p['s5'][j]
            width = s5w['d'].shape[1]
            y_s5, r, im = _s5_mixer(x.reshape(b, l, d), nw, p['w_in_ab'][j], s5w['bmat_re'], s5w['bmat_im'],
                                    s5w['cmat_re'], s5w['cmat_im'], s5w['d'], s5w['glu_w'], s5w['glu_b'],
                                    s5w['a_re'], s5w['a_im'], s5_re[j].reshape(b, -1), s5_im[j].reshape(b, -1),
                                    s5_tl)
            y_ret, st = _retention(x, nw, p['w_in_ab'][j], p['w_kt'][j], width, pos, ret[j], b, l)
            parts, w_out = [y_s5, y_ret], p['w_out_ab'][j]
            new_re.append(r.reshape(s5_re.shape[1:]))
            new_im.append(im.reshape(s5_im.shape[1:]))
            new_ret.append(st)
        else:
            hw = p['hg_lb_logits'].shape[1]
            z = _proj(x, nw, p['w_in_c'][j],
                      segments=((hw, hw, "log2_forget"), (3 * hw, hw, "silu"), (0, hw), (2 * hw, hw)),
                      gate_logits=p['hg_lb_logits'], gate_layer=layer)
            y_hg, st = _hgrn(z, p['hg_norm_w'][j].reshape(1, -1), hg[j], b, l)
            parts, w_out = [y_hg], p['w_out_c'][j]
            new_hg.append(st)
        x, buf = _ffn(x, parts, layer, w_out, p['norm_ffn'][layer].reshape(1, d), p['ffn_w_gate'],
                      p['ffn_w_up'], p['ffn_conv_w'][layer], p['ffn_conv_b'][layer].reshape(1, -1),
                      p['ffn_w_down'], p['norm_final'].reshape(1, d), conv, b, l,
                      final_norm=(layer == depth - 1))
        new_conv.append(buf)
    return (x.reshape(b, l, d), jnp.stack(new_re), jnp.stack(new_im), jnp.stack(new_ret),
            jnp.stack(new_hg), jnp.stack(new_conv))


def kernel(x_prompt, x_sample, state_s5_re, state_s5_im, state_ret, state_hgrn, state_ffn_conv, pos_sample, norm_mix, norm_ffn, norm_final, w_in_ab, s5_lam_re, s5_lam_im, s5_log_dt, s5_b_re, s5_b_im, s5_c_re, s5_c_im, s5_d, s5_glu_w, s5_glu_b, w_out_ab, w_in_c, hg_lb_logits, hg_norm_w, w_out_c, ffn_w_gate, ffn_w_up, ffn_conv_w, ffn_conv_b, ffn_w_down):
    n_ab, n_grp, n_st = s5_lam_re.shape
    ch = s5_b_re.shape[-1]
    width = n_grp * ch
    ret_width = state_ret.shape[2] * state_ret.shape[3]
    grp_per_blk = MXU_DIM // ch
    n_blk = n_grp // grp_per_blk
    s5 = []
    for j in range(n_ab):
        a_re, a_im, bb_re, bb_im = _s5_prep(s5_lam_re[j], s5_lam_im[j], s5_log_dt[j], s5_b_re[j], s5_b_im[j])
        bmat = lambda m: jnp.stack([_block_diag(blk, grp_per_blk)
                                    for blk in m.reshape(n_blk, grp_per_blk * ch, n_st)]).astype(BF16)
        cmat = lambda c: jnp.stack([_block_diag(blk, grp_per_blk).T for blk in
                                    c.reshape(n_blk, grp_per_blk * ch, n_st)]).astype(BF16)
        s5.append(dict(bmat_re=bmat(bb_re), bmat_im=bmat(bb_im),
                       cmat_re=cmat(s5_c_re[j]), cmat_im=cmat(s5_c_im[j]),
                       d=s5_d[j].reshape(1, width), glu_w=s5_glu_w[j].astype(BF16),
                       glu_b=s5_glu_b[j].reshape(1, width),
                       a_re=a_re.reshape(1, n_grp * n_st), a_im=a_im.reshape(1, n_grp * n_st)))
    per_layer_bf16 = lambda w: [w[i].astype(BF16) for i in range(w.shape[0])]
    p = dict(norm_mix=norm_mix, norm_ffn=norm_ffn, norm_final=norm_final, s5=s5,
             w_in_ab=per_layer_bf16(w_in_ab), w_out_ab=per_layer_bf16(w_out_ab),
             w_kt=[w_in_ab[j][:, width + ret_width:width + 2 * ret_width].T.astype(BF16) for j in range(n_ab)],
             w_in_c=per_layer_bf16(w_in_c), hg_lb_logits=hg_lb_logits, hg_norm_w=hg_norm_w,
             w_out_c=per_layer_bf16(w_out_c), ffn_w_gate=ffn_w_gate.astype(BF16),
             ffn_w_up=ffn_w_up.astype(BF16), ffn_conv_w=ffn_conv_w, ffn_conv_b=ffn_conv_b,
             ffn_w_down=ffn_w_down.astype(BF16))

    bp, lp, _ = x_prompt.shape
    z_s5 = jnp.zeros((n_ab, bp) + state_s5_re.shape[2:], F32)
    z_ret = jnp.zeros((n_ab, bp) + state_ret.shape[2:], F32)
    z_hg = jnp.zeros((state_hgrn.shape[0], bp) + state_hgrn.shape[2:], F32)
    z_conv = jnp.zeros((norm_mix.shape[0], bp) + state_ffn_conv.shape[2:], F32)
    outs_p = _trunk(x_prompt, jnp.arange(lp, dtype=jnp.int32), z_s5, z_s5, z_ret, z_hg, z_conv, p,
                    s5_tl=S5_ROW_TILE // bp)
    bs, ls, _ = x_sample.shape
    pos_s = (pos_sample[:, None] + jnp.arange(ls, dtype=jnp.int32)[None, :]).reshape(-1)
    outs_s = _trunk(x_sample, pos_s, state_s5_re, state_s5_im, state_ret, state_hgrn, state_ffn_conv, p,
                    s5_tl=ls)
    return (outs_p[0], outs_s[0]) + outs_p[1:] + outs_s[1:]
```

```python
import functools

import jax
import jax.numpy as jnp
from jax import lax
from jax.experimental import pallas as pl
from jax.experimental.pallas import tpu as pltpu

F32 = jnp.float32
BF16 = jnp.bfloat16

NORM_EPS = 1e-6
ROPE_BASE = 10000.0
S5_GROUP_CH = 16
RET_HEADS = 4
RET_CHUNK = 128
HG_HEADS = 8
HG_BLOCK = 16
HG_CHUNK = 128
RET_HEADS_PER_STEP = 4
HG_HEADS_PER_STEP = 4
SHORT_HEADS_PER_STEP = 2
CONV_W = 3

LANES = 128
SUBLANES = 8
MXU_DIM = 256
VMEM_LIMIT_BYTES = 56 * 1024 * 1024
ROW_TILE = 512
FFN_ROW_TILE = 1024
PROJ_ROW_TILE = 1024
S5_ROW_TILE = 1024
FF_CHUNK = 256
S5_SCAN_COLS = 512


def _cparams(n_grid_dims):
    return pltpu.CompilerParams(dimension_semantics=("arbitrary",) * n_grid_dims,
                                vmem_limit_bytes=VMEM_LIMIT_BYTES)


def _dot(a, b):
    return jnp.dot(a.astype(BF16), b.astype(BF16), preferred_element_type=F32)


def _dot_nt(a, b):
    return lax.dot_general(a.astype(BF16), b.astype(BF16), (((1,), (1,)), ((), ())),
                           preferred_element_type=F32)


def _rmsnorm(x, w):
    return x * lax.rsqrt(jnp.mean(x * x, axis=-1, keepdims=True) + NORM_EPS) * w


def _sigmoid(x):
    return 0.5 * jnp.tanh(0.5 * x) + 0.5


def _full(shape):
    nd = len(shape)
    return pl.BlockSpec(shape, lambda *_: (0,) * nd)


def _mix_dtype(l):
    return BF16 if l % ROW_TILE == 0 else F32


def _resident(shape):
    nd = len(shape)
    return pl.BlockSpec(shape, lambda *_: (0,) * nd, pipeline_mode=pl.Buffered(1))


def _proj_kernel(*refs, segments, col_chunk, row_chunk, t_chunk, gate_layer):
    refs = list(refs)
    x_ref, nw_ref, w_ref = refs[:3]
    rest = refs[3:]
    lg_ref = rest.pop(0) if gate_layer is not None else None
    wt_ref = rest.pop(0) if t_chunk else None
    o_ref = rest.pop(0)
    ot_ref = rest.pop(0) if t_chunk else None
    for r in range(0, x_ref.shape[0], row_chunk):
        rows = slice(r, r + row_chunk)
        h = _rmsnorm(x_ref[rows, :], nw_ref[...]).astype(BF16)
        out0 = 0
        for col0, width, kind in segments:
            for c in range(0, width, col_chunk):
                val = jnp.dot(h, w_ref[:, col0 + c:col0 + c + col_chunk], preferred_element_type=F32)
                if kind == "silu":
                    val = val * _sigmoid(val)
                elif kind == "log2_forget":
                    val = _hgrn_log2_forget(val, lg_ref[:, c:c + col_chunk], gate_layer)
                o_ref[rows, out0 + c:out0 + c + col_chunk] = val
            out0 += width
        if t_chunk:
            zt = _dot_nt(wt_ref[...], h)
            for j in range(row_chunk // t_chunk):
                ot_ref[r // t_chunk + j] = zt[:, j * t_chunk:(j + 1) * t_chunk]


def _proj(x, nw, w_bf, segments=None, wt_bf=None, t_chunk=0, gate_logits=None, gate_layer=None):
    n, d = x.shape
    segments = tuple((tuple(s) + (None,))[:3] for s in (segments or ((0, w_bf.shape[1]),)))
    n_out = sum(width for _, width, _ in segments)
    tm = min(PROJ_ROW_TILE, n)
    in_specs = [pl.BlockSpec((tm, d), lambda i: (i, 0)), _full((1, d)), _resident(w_bf.shape)]
    out_specs = [pl.BlockSpec((tm, n_out), lambda i: (i, 0))]
    out_shape = [jax.ShapeDtypeStruct((n, n_out), F32)]
    args = [x, nw, w_bf]
    if gate_layer is not None:
        in_specs.append(_full(gate_logits.shape))
        args.append(gate_logits)
    if t_chunk:
        t_width = wt_bf.shape[0]
        in_specs.append(_resident(wt_bf.shape))
        out_specs.append(pl.BlockSpec((tm // t_chunk, t_width, t_chunk), lambda i: (i, 0, 0)))
        out_shape.append(jax.ShapeDtypeStruct((n // t_chunk, t_width, t_chunk), F32))
        args.append(wt_bf)
    outs = pl.pallas_call(
        functools.partial(_proj_kernel, segments=segments, col_chunk=2 * MXU_DIM, row_chunk=min(ROW_TILE, tm),
                          t_chunk=t_chunk, gate_layer=gate_layer),
        grid=(n // tm,), in_specs=in_specs, out_specs=out_specs, out_shape=out_shape,
        compiler_params=_cparams(1), name="proj")(*args)
    return outs if t_chunk else outs[0]


def _s5_prep_kernel(lre_ref, lim_ref, ldt_ref, lre16_ref, lim16_ref, ldt16_ref, bre_ref, bim_ref, cre_ref, cim_ref,
                    are_ref, aim_ref, bmre_ref, bmim_ref, cmre_ref, cmim_ref, *, ch):
    def block_diag(m):
        rows, p = m.shape
        tiled = jnp.concatenate([m] * (rows // ch), axis=1)
        row = lax.broadcasted_iota(jnp.int32, tiled.shape, 0)
        col = lax.broadcasted_iota(jnp.int32, tiled.shape, 1)
        return jnp.where(col // p == row // ch, tiled, 0.0)

    def disc(lre, lim, ldt):
        dt = jnp.exp(ldt)
        mag = jnp.exp(lre * dt)
        ang = lim * dt
        return mag * jnp.cos(ang), mag * jnp.sin(ang)

    ab_re, ab_im = disc(lre_ref[...], lim_ref[...], ldt_ref[...])
    are_ref[...] = ab_re
    aim_ref[...] = ab_im
    lre, lim = lre16_ref[...], lim16_ref[...]
    ab_re, ab_im = disc(lre, lim, ldt16_ref[...])
    nr, ni = ab_re - 1.0, ab_im
    den = lre * lre + lim * lim
    f_re = (nr * lre + ni * lim) / den
    f_im = (ni * lre - nr * lim) / den
    b_re, b_im = bre_ref[...], bim_ref[...]
    bb_re = f_re * b_re - f_im * b_im
    bb_im = f_re * b_im + f_im * b_re
    kw = bmre_ref.shape[1]
    for kb in range(bmre_ref.shape[0]):
        rows = slice(kb * kw, (kb + 1) * kw)
        bmre_ref[kb] = block_diag(bb_re[rows]).astype(bmre_ref.dtype)
        bmim_ref[kb] = block_diag(bb_im[rows]).astype(bmim_ref.dtype)
        cmre_ref[kb] = block_diag(cre_ref[rows, :]).T.astype(cmre_ref.dtype)
        cmim_ref[kb] = block_diag(cim_ref[rows, :]).T.astype(cmim_ref.dtype)


def _s5_prep(lam_re, lam_im, log_dt, b_re, b_im, c_re, c_im, grp_per_blk):
    g, p = lam_re.shape
    ch = b_re.shape[-1]
    rep = lambda a: jnp.repeat(a, ch, axis=0)
    ldt = log_dt.reshape(g, 1)
    bt = lambda b: jnp.swapaxes(b, 1, 2).reshape(g * ch, p)
    args = (lam_re, lam_im, ldt, rep(lam_re), rep(lam_im), rep(ldt), bt(b_re), bt(b_im),
            c_re.reshape(g * ch, p), c_im.reshape(g * ch, p))
    n_blk, kw, cw = g // grp_per_blk, grp_per_blk * ch, grp_per_blk * p
    shapes = [(g, p)] * 2 + [(n_blk, kw, cw)] * 2 + [(n_blk, cw, kw)] * 2
    dtypes = [F32] * 2 + [BF16] * 4
    return pl.pallas_call(
        functools.partial(_s5_prep_kernel, ch=ch),
        in_specs=[_full(a.shape) for a in args],
        out_specs=[_full(s) for s in shapes],
        out_shape=[jax.ShapeDtypeStruct(s, t) for s, t in zip(shapes, dtypes)],
        name="s5_prep")(*args)


def _s5_kernel(x_ref, nw_ref, wu_ref, bre_ref, bim_ref, cre_ref, cim_ref, d_ref, gw_ref, gb_ref,
               are_ref, aim_ref, s0re_ref, s0im_ref,
               y_ref, sre_ref, sim_ref, bure_s, buim_s, stre_s, stim_s, *, tb, tl):
    @pl.when(pl.program_id(0) == 0)
    def _():
        stre_s[...] = s0re_ref[...]
        stim_s[...] = s0im_ref[...]

    x = jnp.concatenate([x_ref[:, t, :] for t in range(tl)], axis=0)
    u = jnp.dot(_rmsnorm(x, nw_ref[...]).astype(BF16), wu_ref[...], preferred_element_type=F32)
    ub = u.astype(BF16)

    n_kb, kw, cw = bre_ref.shape
    for kb in range(n_kb):
        uk = ub[:, kb * kw:(kb + 1) * kw]
        bure_s[:, kb * cw:(kb + 1) * cw] = jnp.dot(uk, bre_ref[kb], preferred_element_type=F32)
        buim_s[:, kb * cw:(kb + 1) * cw] = jnp.dot(uk, bim_ref[kb], preferred_element_type=F32)

    n_state = are_ref.shape[1]
    sc = min(S5_SCAN_COLS, n_state)
    for c0 in range(0, n_state, sc):
        cs = slice(c0, c0 + sc)
        a_re = jnp.broadcast_to(are_ref[:, cs], (tb, sc))
        a_im = jnp.broadcast_to(aim_ref[:, cs], (tb, sc))

        def step(t, carry, cs=cs, a_re=a_re, a_im=a_im):
            s_re, s_im = carry
            rows = pl.ds(pl.multiple_of(t * tb, tb), tb)
            n_re = a_re * s_re - a_im * s_im + bure_s[rows, cs]
            n_im = a_re * s_im + a_im * s_re + buim_s[rows, cs]
            bure_s[rows, cs] = n_re
            buim_s[rows, cs] = n_im
            return n_re, n_im

        s_re, s_im = lax.fori_loop(0, tl, step, (stre_s[:, cs], stim_s[:, cs]), unroll=True)
        stre_s[:, cs] = s_re
        stim_s[:, cs] = s_im

    n_ob, ckw, ocw = cre_ref.shape
    ys = []
    for ob in range(n_ob):
        ks = slice(ob * ckw, (ob + 1) * ckw)
        ys.append(_dot(bure_s[:, ks], cre_ref[ob]) - _dot(buim_s[:, ks], cim_ref[ob]))
    y = jnp.concatenate(ys, axis=1) + d_ref[...] * u
    y = jax.nn.gelu(y)
    out = y * jax.nn.sigmoid(_dot(y, gw_ref[...]) + gb_ref[...])
    for t in range(tl):
        y_ref[:, t, :] = out[t * tb:(t + 1) * tb, :]
    sre_ref[...] = stre_s[...]
    sim_ref[...] = stim_s[...]


def _s5_mixer(x3, nw, w_in_bf, bmat_re, bmat_im, cmat_re, cmat_im, d_row, glu_w_bf, glu_b, a_re, a_im,
              s0_re, s0_im, tl):
    b, l, d = x3.shape
    width = d_row.shape[1]
    n_state = a_re.shape[1]
    rows = b * tl
    args = (x3, nw, w_in_bf, bmat_re, bmat_im, cmat_re, cmat_im, d_row, glu_w_bf, glu_b,
            a_re, a_im, s0_re, s0_im)
    in_specs = ([pl.BlockSpec((b, tl, d), lambda i: (0, i, 0)), _full(nw.shape),
                 pl.BlockSpec((d, width), lambda i: (0, 0))] + [_full(a.shape) for a in args[3:]])
    y, s_re, s_im = pl.pallas_call(
        functools.partial(_s5_kernel, tb=b, tl=tl),
        grid=(l // tl,), in_specs=in_specs,
        out_specs=[pl.BlockSpec((b, tl, width), lambda i: (0, i, 0)), _full((b, n_state)), _full((b, n_state))],
        out_shape=[jax.ShapeDtypeStruct((b, l, width), F32),
                   jax.ShapeDtypeStruct((b, n_state), F32), jax.ShapeDtypeStruct((b, n_state), F32)],
        scratch_shapes=[pltpu.VMEM((rows, n_state), F32), pltpu.VMEM((rows, n_state), F32),
                        pltpu.VMEM((b, n_state), F32), pltpu.VMEM((b, n_state), F32)],
        compiler_params=_cparams(1), name="s5_mixer")(*args)
    return y.reshape(b * l, width), s_re, s_im


def _rope_kernel(pos_ref, inv_ref, cc_ref, ss_ref, *t_refs, t_chunk):
    ang = pos_ref[...] * inv_ref[...]
    lane = lax.broadcasted_iota(jnp.int32, ang.shape, 1)
    cc = jnp.cos(ang)
    ss = jnp.where(lane < ang.shape[1] // 2, -jnp.sin(ang), jnp.sin(ang))
    cc_ref[...] = cc
    ss_ref[...] = ss
    if t_chunk:
        cct_ref, sst_ref = t_refs
        for i in range(cct_ref.shape[0]):
            cct_ref[i] = cc[i * t_chunk:(i + 1) * t_chunk].T
            sst_ref[i] = ss[i * t_chunk:(i + 1) * t_chunk].T


def _rope_tables(pos, dk, t_chunk=0):
    n = pos.shape[0]
    inv = 1.0 / (ROPE_BASE ** jnp.linspace(0.0, 1.0, dk // 2, dtype=F32))
    inv = jnp.concatenate([inv, inv]).reshape(1, dk)
    shapes = [(n, dk)] * 2 + ([(n // t_chunk, dk, t_chunk)] * 2 if t_chunk else [])
    return pl.pallas_call(
        functools.partial(_rope_kernel, t_chunk=t_chunk),
        in_specs=[_full((n, 1)), _full((1, dk))],
        out_specs=[_full(s) for s in shapes],
        out_shape=[jax.ShapeDtypeStruct(s, F32) for s in shapes],
        name="rope_tables")(pos.astype(F32).reshape(n, 1), inv)


def _rotate(t, cc, ss):
    return t * cc + pltpu.roll(t, t.shape[1] // 2, 1) * ss


def _rotate_t(t, cc, ss):
    half = t.shape[0] // 2
    return t * cc + jnp.concatenate([t[half:], t[:half]], axis=0) * ss


def _ret_consts(rows, c):
    lg = jnp.log(1.0 - 2.0 ** (-5.0 - jnp.arange(RET_HEADS, dtype=F32)))
    r = jnp.arange(rows)
    idx, blk = r % c, r // c
    diff = idx[:, None] - idx[None, :]
    same = blk[:, None] == blk[None, :]
    decay = jnp.where((same & (diff >= 0))[None],
                      jnp.exp(jnp.maximum(diff, 0)[None].astype(F32) * lg[:, None, None]), 0.0)
    wide = lambda v: jnp.broadcast_to(v[:, :, None], (RET_HEADS, rows, LANES))
    q_dec = wide(jnp.exp((idx + 1).astype(F32)[None, :] * lg[:, None]))
    k_dec = wide(jnp.exp((c - 1 - idx).astype(F32)[None, :] * lg[:, None]))
    chunk_dec = jnp.broadcast_to(jnp.exp(c * lg)[:, None, None], (RET_HEADS, SUBLANES, LANES))
    return decay, q_dec, k_dec, chunk_dec


def _ret_gate(o, g):
    o = o * lax.rsqrt(jnp.mean(o * o, axis=-1, keepdims=True) + NORM_EPS)
    return jax.nn.silu(g) * o


def _ret_long_kernel(q_ref, kt_ref, v_ref, g_ref, cc_ref, ss_ref, cct_ref, sst_ref, dec_ref, qd_ref, kdt_ref,
                     cd_ref, s0_ref, y_ref, so_ref, *, c, hp):
    dk = q_ref.shape[1] // hp
    scale = dk ** -0.5

    def chunk(i, states):
        rows = pl.ds(pl.multiple_of(i * c, c), c)
        cc, ss = cc_ref[rows, :], ss_ref[rows, :]
        cct, sst = cct_ref[i], sst_ref[i]
        new_states = []
        for j in range(hp):
            cols = slice(j * dk, (j + 1) * dk)
            q = _rotate(q_ref[rows, cols], cc, ss)
            kt = _rotate_t(kt_ref[i, cols, :], cct, sst) * scale
            v = v_ref[rows, cols]
            scores = _dot(q, kt) * dec_ref[j]
            o = _dot(scores, v) + _dot(q * qd_ref[j], states[j])
            new_states.append(cd_ref[j, 0:1, :] * states[j] + _dot(kt * kdt_ref[j], v))
            y_ref[rows, cols] = _ret_gate(o, g_ref[rows, cols]).astype(y_ref.dtype)
        return tuple(new_states)

    states = lax.fori_loop(0, q_ref.shape[0] // c, chunk, tuple(s0_ref[0, j] for j in range(hp)), unroll=4)
    for j in range(hp):
        so_ref[0, j] = states[j]


def _short_state_pass(qd, kt, v, s0_ref, so_ref, hd, decay_of, l):
    rows, dk = v.shape
    row = lax.broadcasted_iota(jnp.int32, (rows, dk), 0)
    v_cols = jnp.concatenate([jnp.where(row // l == b, v, 0.0).astype(BF16) for b in range(rows // l)], axis=1)
    kv = jnp.dot(kt.astype(BF16), v_cols, preferred_element_type=F32)
    win = 16
    per = win // l
    wrow = lax.broadcasted_iota(jnp.int32, (win, dk), 0)
    outs = []
    for w in range(rows // win):
        qw = qd[w * win:(w + 1) * win].astype(BF16)
        oi = jnp.zeros((win, dk), F32)
        for j in range(per):
            b = w * per + j
            s = s0_ref[b, hd]
            oi = jnp.where(wrow // l == j, jnp.dot(qw, s.astype(BF16), preferred_element_type=F32), oi)
            so_ref[b, hd] = decay_of(b) * s + kv[:, b * dk:(b + 1) * dk]
        outs.append(oi)
    return jnp.concatenate(outs, axis=0)


def _ret_short_kernel(q_ref, k_ref, v_ref, g_ref, cc_ref, ss_ref, dec_ref, qd_ref, kd_ref, cd_ref, s0_ref,
                      y_ref, so_ref, *, l, hp):
    dk = q_ref.shape[1] // hp
    scale = dk ** -0.5
    cc, ss = cc_ref[...], ss_ref[...]
    for hd in range(hp):
        cols = slice(hd * dk, (hd + 1) * dk)
        q = _rotate(q_ref[:, cols], cc, ss)
        k = _rotate(k_ref[:, cols], cc, ss) * scale
        v = v_ref[:, cols]
        o_intra = _dot(_dot_nt(q, k) * dec_ref[hd], v)
        cd = cd_ref[hd, 0:1, :]
        o_inter = _short_state_pass(q * qd_ref[hd], (k * kd_ref[hd]).T, v, s0_ref, so_ref, hd, lambda b: cd, l)
        y_ref[:, cols] = _ret_gate(o_intra + o_inter, g_ref[:, cols]).astype(y_ref.dtype)


def _retention(x, nw, w_in_bf, w_kt_bf, col0, pos, s0, n_seq, l):
    n = x.shape[0]
    dk = s0.shape[-1]
    width = RET_HEADS * dk
    out_shape = [jax.ShapeDtypeStruct((n, width), _mix_dtype(l)), jax.ShapeDtypeStruct(s0.shape, F32)]
    cspec = lambda a, hp: pl.BlockSpec((hp,) + a.shape[1:], lambda b, h: (h, 0, 0))
    if l % RET_CHUNK == 0:
        c, hp = RET_CHUNK, RET_HEADS_PER_STEP
        n_hb = RET_HEADS // hp
        z, kt = _proj(x, nw, w_in_bf, ((col0, width), (col0 + 2 * width, 2 * width)), wt_bf=w_kt_bf, t_chunk=c)
        cc, ss, cct, sst = _rope_tables(pos, dk, t_chunk=c)
        decay, q_dec, k_dec, chunk_dec = _ret_consts(c, c)
        k_dec_t = jnp.swapaxes(k_dec, 1, 2)
        zspec = lambda field: pl.BlockSpec((l, hp * dk), lambda b, h: (b, field * n_hb + h))
        sspec = pl.BlockSpec((1, hp, dk, dk), lambda b, h: (b, h, 0, 0))
        return pl.pallas_call(
            functools.partial(_ret_long_kernel, c=c, hp=hp), grid=(n_seq, n_hb),
            in_specs=[zspec(0), pl.BlockSpec((l // c, hp * dk, c), lambda b, h: (b, h, 0)), zspec(1), zspec(2),
                      _full(cc.shape), _full(ss.shape), _full(cct.shape), _full(sst.shape),
                      cspec(decay, hp), cspec(q_dec, hp), cspec(k_dec_t, hp), cspec(chunk_dec, hp), sspec],
            out_specs=[pl.BlockSpec((l, hp * dk), lambda b, h: (b, h)), sspec],
            out_shape=out_shape, compiler_params=_cparams(2), name="retention")(
                z, kt, z, z, cc, ss, cct, sst, decay, q_dec, k_dec_t, chunk_dec, s0)
    rows, hp = LANES, SHORT_HEADS_PER_STEP
    per_blk = rows // l
    n_hb = RET_HEADS // hp
    z = _proj(x, nw, w_in_bf, segments=((col0, 4 * width),))
    cc, ss = _rope_tables(pos, dk)
    decay, q_dec, k_dec, chunk_dec = _ret_consts(rows, l)
    zspec = lambda field: pl.BlockSpec((rows, hp * dk), lambda b, h: (b, field * n_hb + h))
    tspec = pl.BlockSpec((rows, dk), lambda b, h: (b, 0))
    sspec = pl.BlockSpec((per_blk, hp, dk, dk), lambda b, h: (b, h, 0, 0))
    return pl.pallas_call(
        functools.partial(_ret_short_kernel, l=l, hp=hp), grid=(n_seq // per_blk, n_hb),
        in_specs=[zspec(0), zspec(1), zspec(2), zspec(3), tspec, tspec,
                  cspec(decay, hp), cspec(q_dec, hp), cspec(k_dec, hp), cspec(chunk_dec, hp), sspec],
        out_specs=[pl.BlockSpec((rows, hp * dk), lambda b, h: (b, h)), sspec],
        out_shape=out_shape,
        compiler_params=_cparams(2), name="retention")(z, z, z, z, cc, ss, decay, q_dec, k_dec, chunk_dec, s0)


def _cumsum_rows(x, period):
    row = lax.broadcasted_iota(jnp.int32, x.shape, 0)
    s = 1
    while s < period:
        x = x + jnp.where(row % period >= s, pltpu.roll(x, s, 0), 0.0)
        s *= 2
    return x


def _row_of_block(x, period, offset):
    n, w = x.shape
    x3 = x.reshape(n // period, period, w)
    return jnp.broadcast_to(x3[:, offset:offset + 1, :], x3.shape).reshape(n, w)


def _hgrn_log2_forget(fl, lg, layer):
    e = jnp.exp(lg - jnp.max(lg, axis=0, keepdims=True))
    soft = e / jnp.sum(e, axis=0, keepdims=True)
    cum = soft[0:1]
    for i in range(1, layer + 1):
        cum = cum + soft[i:i + 1]
    lb = cum - soft[0:1]
    return jnp.log2(lb + (1.0 - lb) * _sigmoid(fl))


def _hgrn_out(o, gate, nw):
    o = o * lax.rsqrt(jnp.mean(o * o, axis=-1, keepdims=True) + NORM_EPS) * nw
    return o * gate


def _hgrn_levels(c):
    levels, half = [], HG_BLOCK
    while half < c:
        levels.append(half)
        half *= 2
    return levels


def _hgrn_long_kernel(q_ref, f_ref, v_ref, g_ref, nw_ref, s0_ref, y_ref, so_ref, *, c, hp):
    dk = q_ref.shape[1] // hp
    nw = nw_ref[...]
    nv = c // SUBLANES
    bv = HG_BLOCK // SUBLANES
    levels = _hgrn_levels(c)
    row = lax.broadcasted_iota(jnp.int32, (c, c), 0)
    col = lax.broadcasted_iota(jnp.int32, (c, c), 1)
    lvl = jnp.where((row // HG_BLOCK == col // HG_BLOCK) & (col <= row), 0, -1)
    for li, half in enumerate(levels, 1):
        lvl = jnp.where((row // (2 * half) == col // (2 * half)) & (row % (2 * half) >= half)
                        & (col % (2 * half) < half), li, lvl)
    sub = lax.broadcasted_iota(jnp.int32, (SUBLANES, dk), 0)
    zeros = jnp.zeros((SUBLANES, dk), F32)
    cat = lambda slabs: jnp.concatenate(slabs, axis=0)

    def one_head(rows, cols, s):
        q, v = q_ref[rows, cols], v_ref[rows, cols]
        lf = f_ref[rows, cols]
        k = 1.0 - jnp.exp2(lf)
        slabs = lambda a: [a[j * SUBLANES:(j + 1) * SUBLANES, :] for j in range(nv)]
        qs, ks = slabs(q), slabs(k)
        loc, run = [], [jnp.zeros((1, dk), F32)]
        for x in slabs(lf):
            step = 1
            while step < SUBLANES:
                x = x + jnp.where(sub >= step, pltpu.roll(x, step, 0), 0.0)
                step *= 2
            loc.append(x)
            run.append(run[-1] + x[SUBLANES - 1:, :])
        d0 = [loc[j] + (run[j] - run[j - j % bv]) if j % bv else loc[j] for j in range(nv)]
        scores = jnp.where(lvl == 0, _dot_nt(cat([qs[j] * jnp.exp2(d0[j]) for j in range(nv)]),
                                             cat([ks[j] * jnp.exp2(-d0[j]) for j in range(nv)])), 0.0)
        for li, half in enumerate(levels, 1):
            hv = half // SUBLANES
            q_side, k_side = [], []
            for j in range(nv):
                anchor = run[j - j % (2 * hv) + hv]
                if j % (2 * hv) >= hv:
                    q_side.append(qs[j] * jnp.exp2(loc[j] + (run[j] - anchor)))
                    k_side.append(zeros)
                else:
                    q_side.append(zeros)
                    k_side.append(ks[j] * jnp.exp2((anchor - run[j]) - loc[j]))
            scores = jnp.where(lvl == li, _dot_nt(cat(q_side), cat(k_side)), scores)
        o = _dot(scores, v) + _dot(cat([qs[j] * jnp.exp2(loc[j] + run[j]) for j in range(nv)]), s)
        k_hat = cat([ks[j] * jnp.exp2((run[nv] - run[j]) - loc[j]) for j in range(nv)])
        d_col = jnp.exp2(jnp.broadcast_to(run[nv], (dk, dk)).T)
        y_ref[rows, cols] = _hgrn_out(o, g_ref[rows, cols], nw).astype(y_ref.dtype)
        return d_col * s + _dot(k_hat.T, v)

    def chunk(i, states):
        rows = pl.ds(pl.multiple_of(i * c, c), c)
        return tuple(one_head(rows, slice(hd * dk, (hd + 1) * dk), states[hd]) for hd in range(hp))

    states = lax.fori_loop(0, q_ref.shape[0] // c, chunk, tuple(s0_ref[0, j] for j in range(hp)), unroll=4)
    for j in range(hp):
        so_ref[0, j] = states[j]


def _hgrn_short_kernel(q_ref, f_ref, v_ref, g_ref, nw_ref, s0_ref, y_ref, so_ref, *, l, hp):
    rows = q_ref.shape[0]
    dk = q_ref.shape[1] // hp
    row = lax.broadcasted_iota(jnp.int32, (rows, rows), 0)
    col = lax.broadcasted_iota(jnp.int32, (rows, rows), 1)
    causal = (row // l == col // l) & (col <= row)
    for hd in range(hp):
        cols = slice(hd * dk, (hd + 1) * dk)
        q, v = q_ref[:, cols], v_ref[:, cols]
        lf = f_ref[:, cols]
        k = 1.0 - jnp.exp2(lf)
        b = _cumsum_rows(lf, l)
        qe = q * jnp.exp2(b)
        o_intra = _dot(jnp.where(causal, _dot_nt(qe, k * jnp.exp2(-b)), 0.0), v)
        b_last = _row_of_block(b, l, l - 1)
        dect = jnp.exp2(b_last).T
        o_inter = _short_state_pass(qe, (k * jnp.exp2(b_last - b)).T, v, s0_ref, so_ref, hd,
                                    lambda s, dect=dect: dect[:, s * l:s * l + 1], l)
        y_ref[:, cols] = _hgrn_out(o_intra + o_inter, g_ref[:, cols], nw_ref[...]).astype(y_ref.dtype)


def _hgrn(z, norm_w, s0, n_seq, l):
    n = z.shape[0]
    dk = s0.shape[-1]
    if l % HG_CHUNK == 0:
        rows, n_blk, hp = l, n_seq, HG_HEADS_PER_STEP
        kern = functools.partial(_hgrn_long_kernel, c=HG_CHUNK, hp=hp)
        sspec = pl.BlockSpec((1, hp, dk, dk), lambda b, h: (b, h, 0, 0))
    else:
        assert HG_BLOCK % l == 0
        rows, hp = LANES, SHORT_HEADS_PER_STEP
        per_blk = rows // l
        n_blk = n_seq // per_blk
        kern = functools.partial(_hgrn_short_kernel, l=l, hp=hp)
        sspec = pl.BlockSpec((per_blk, hp, dk, dk), lambda b, h: (b, h, 0, 0))
    n_hb = HG_HEADS // hp
    zspec = lambda rows, field: pl.BlockSpec((rows, hp * dk), lambda b, h: (b, field * n_hb + h))
    y, s_new = pl.pallas_call(
        kern, grid=(n_blk, n_hb),
        in_specs=[zspec(rows, 2), zspec(rows, 0), zspec(rows, 3), zspec(rows, 1), _full((1, dk)), sspec],
        out_specs=[pl.BlockSpec((rows, hp * dk), lambda b, h: (b, h)), sspec],
        out_shape=[jax.ShapeDtypeStruct((n, HG_HEADS * dk), _mix_dtype(l)), jax.ShapeDtypeStruct(s0.shape, F32)],
        compiler_params=_cparams(2), name="hgrn")(z, z, z, z, norm_w, s0)
    return y, s_new


def _ffn_kernel(*refs, n_parts, long_mode, final_norm, l):
    x_ref, parts = refs[0], refs[1:1 + n_parts]
    (wo_ref, nw_ref, wg_ref, wu_ref, cw_ref, cb_ref, wd_ref, fnw_ref, buf_ref, o_ref, bufo_ref,
     act_s) = refs[1 + n_parts:13 + n_parts]
    d_ff = wg_ref.shape[1]
    if long_mode:
        tail_s = refs[13 + n_parts]
        tm = x_ref.shape[1]
        rc = min(ROW_TILE, tm)
        load = lambda ref, r0: ref[0, r0:r0 + rc, :]

        @pl.when(pl.program_id(1) == 0)
        def _():
            tail_s[...] = buf_ref[0]
    else:
        nb = x_ref.shape[0]
        tm = rc = l * nb
        load = lambda ref, r0: jnp.concatenate([ref[:, t, :] for t in range(l)], axis=0)

    xs = []
    for r0 in range(0, tm, rc):
        mix = jnp.concatenate([load(p, r0).astype(BF16) for p in parts], axis=1)
        x = load(x_ref, r0) + jnp.dot(mix, wo_ref[...], preferred_element_type=F32)
        xs.append(x)
        h = _rmsnorm(x, nw_ref[...]).astype(BF16)
        for c0 in range(0, d_ff, FF_CHUNK):
            cs = slice(c0, c0 + FF_CHUNK)
            g = jnp.dot(h, wg_ref[:, cs], preferred_element_type=F32)
            up = jnp.dot(h, wu_ref[:, cs], preferred_element_type=F32)
            if long_mode:
                row = lax.broadcasted_iota(jnp.int32, g.shape, 0)
                p1 = jnp.where(row < 1, tail_s[1:2, cs], pltpu.roll(g, 1, 0))
                p2 = jnp.where(row < 2, jnp.where(row == 0, tail_s[0:1, cs], tail_s[1:2, cs]),
                               pltpu.roll(g, 2, 0))
                tail_s[:, cs] = g[rc - (CONV_W - 1):, :]
            else:
                b0, b1 = buf_ref[:, 0, cs], buf_ref[:, 1, cs]
                p1 = jnp.concatenate([b1, g[:tm - nb]], axis=0)
                p2 = jnp.concatenate([b0, b1, g[:tm - 2 * nb]], axis=0)
                bufo_ref[:, 0, cs] = g[tm - 2 * nb:tm - nb]
                bufo_ref[:, 1, cs] = g[tm - nb:]
            conv = cb_ref[:, cs] + cw_ref[0:1, cs] * p2
            conv = conv + cw_ref[1:2, cs] * p1
            conv = conv + cw_ref[2:3, cs] * g
            act_s[r0:r0 + rc, cs] = (jax.nn.silu(conv) * up).astype(BF16)
    y = jnp.concatenate(xs, axis=0) + jnp.dot(act_s[...], wd_ref[...], preferred_element_type=F32)
    if final_norm:
        y = _rmsnorm(y, fnw_ref[...])
    if long_mode:
        o_ref[0] = y
        bufo_ref[0] = tail_s[...]
    else:
        for t in range(l):
            o_ref[:, t, :] = y[t * nb:(t + 1) * nb]


def _ffn(x, parts, layer, wo, nw, wg, wu, conv_w, conv_b, wd, fnw, buf, n_seq, l, final_norm):
    assert CONV_W == 3 and l >= CONV_W - 1
    n, d = x.shape
    d_ff = wg.shape[2]
    weights = (wo, nw, wg, wu, conv_w, conv_b, wd, fnw)
    stacked = (wg, wu, wd)

    def wspec(w, single_buffer):
        kw = dict(pipeline_mode=pl.Buffered(1)) if single_buffer else {}
        if any(w is s for s in stacked):
            return pl.BlockSpec((None,) + w.shape[1:], lambda *_: (layer, 0, 0), **kw)
        return pl.BlockSpec(w.shape, lambda *_: (0,) * w.ndim, **kw)
    kern = functools.partial(_ffn_kernel, n_parts=len(parts), final_norm=final_norm, l=l)
    buf_shape = jax.ShapeDtypeStruct((n_seq, CONV_W - 1, d_ff), F32)
    rows3 = lambda a: a.reshape(n_seq, l, a.shape[1])
    if l % ROW_TILE == 0:
        tm = FFN_ROW_TILE
        assert l % tm == 0
        rspec = lambda w: pl.BlockSpec((1, tm, w), lambda b, i: (b, i, 0))
        bspec = pl.BlockSpec((1, CONV_W - 1, d_ff), lambda b, i: (b, 0, 0))
        y, buf_new = pl.pallas_call(
            functools.partial(kern, long_mode=True),
            grid=(n_seq, l // tm),
            in_specs=([rspec(d)] + [rspec(p.shape[1]) for p in parts]
                      + [wspec(w, True) for w in weights]
                      + [pl.BlockSpec((None, 1, CONV_W - 1, d_ff), lambda b, i: (layer, b, 0, 0))]),
            out_specs=[rspec(d), bspec],
            out_shape=[jax.ShapeDtypeStruct((n_seq, l, d), F32), buf_shape],
            scratch_shapes=[pltpu.VMEM((tm, d_ff), BF16), pltpu.VMEM((CONV_W - 1, d_ff), F32)],
            compiler_params=_cparams(2), name="ffn")(rows3(x), *map(rows3, parts), *weights, buf)
        return y.reshape(n, d), buf_new
    y, buf_new = pl.pallas_call(
        functools.partial(kern, long_mode=False),
        grid=(1,),
        in_specs=([_full((n_seq, l, d))] + [_full((n_seq, l, p.shape[1])) for p in parts]
                  + [wspec(w, False) for w in weights]
                  + [pl.BlockSpec((None,) + buf.shape[1:], lambda *_: (layer, 0, 0, 0))]),
        out_specs=[_full((n_seq, l, d)), _full(buf.shape[1:])],
        out_shape=[jax.ShapeDtypeStruct((n_seq, l, d), F32), buf_shape],
        scratch_shapes=[pltpu.VMEM((n, d_ff), BF16)],
        compiler_params=_cparams(1), name="ffn")(
            rows3(x), *map(rows3, parts), *weights, buf)
    return y.reshape(n, d), buf_new


def _trunk(x3, pos, s5_re, s5_im, ret, hg, conv, p, s5_tl):
    b, l, d = x3.shape
    n = b * l
    x = x3.reshape(n, d)
    depth = p['norm_mix'].shape[0]
    new_re, new_im, new_ret, new_hg, new_conv = [], [], [], [], []
    for layer in range(depth):
        j = layer // 2
        nw = p['norm_mix'][layer].reshape(1, d)
        if layer % 2 == 0:
            s5w = p['s5'][j]
            width = s5w['d'].shape[1]
            y_s5, r, im = _s5_mixer(x.reshape(b, l, d), nw, p['w_in_ab'][j], s5w['bmat_re'], s5w['bmat_im'],
                                    s5w['cmat_re'], s5w['cmat_im'], s5w['d'], s5w['glu_w'], s5w['glu_b'],
                                    s5w['a_re'], s5w['a_im'], s5_re[j].reshape(b, -1), s5_im[j].reshape(b, -1),
                                    s5_tl)
            y_ret, st = _retention(x, nw, p['w_in_ab'][j], p['w_kt'][j], width, pos, ret[j], b, l)
            parts, w_out = [y_s5, y_ret], p['w_out_ab'][j]
            new_re.append(r.reshape(s5_re.shape[1:]))
            new_im.append(im.reshape(s5_im.shape[1:]))
            new_ret.append(st)
        else:
            hw = p['hg_lb_logits'].shape[1]
            z = _proj(x, nw, p['w_in_c'][j],
                      segments=((hw, hw, "log2_forget"), (3 * hw, hw, "silu"), (0, hw), (2 * hw, hw)),
                      gate_logits=p['hg_lb_logits'], gate_layer=layer)
            y_hg, st = _hgrn(z, p['hg_norm_w'][j].reshape(1, -1), hg[j], b, l)
            parts, w_out = [y_hg], p['w_out_c'][j]
            new_hg.append(st)
        x, buf = _ffn(x, parts, layer, w_out, p['norm_ffn'][layer].reshape(1, d), p['ffn_w_gate'],
                      p['ffn_w_up'], p['ffn_conv_w'][layer], p['ffn_conv_b'][layer].reshape(1, -1),
                      p['ffn_w_down'], p['norm_final'].reshape(1, d), conv, b, l,
                      final_norm=(layer == depth - 1))
        new_conv.append(buf)
    return (x.reshape(b, l, d), jnp.stack(new_re), jnp.stack(new_im), jnp.stack(new_ret),
            jnp.stack(new_hg), jnp.stack(new_conv))


def kernel(x_prompt, x_sample, state_s5_re, state_s5_im, state_ret, state_hgrn, state_ffn_conv, pos_sample, norm_mix, norm_ffn, norm_final, w_in_ab, s5_lam_re, s5_lam_im, s5_log_dt, s5_b_re, s5_b_im, s5_c_re, s5_c_im, s5_d, s5_glu_w, s5_glu_b, w_out_ab, w_in_c, hg_lb_logits, hg_norm_w, w_out_c, ffn_w_gate, ffn_w_up, ffn_conv_w, ffn_conv_b, ffn_w_down):
    n_ab, n_grp, n_st = s5_lam_re.shape
    ch = s5_b_re.shape[-1]
    width = n_grp * ch
    ret_width = state_ret.shape[2] * state_ret.shape[3]
    grp_per_blk = MXU_DIM // ch
    s5 = []
    for j in range(n_ab):
        a_re, a_im, bmat_re, bmat_im, cmat_re, cmat_im = _s5_prep(
            s5_lam_re[j], s5_lam_im[j], s5_log_dt[j], s5_b_re[j], s5_b_im[j], s5_c_re[j], s5_c_im[j], grp_per_blk)
        s5.append(dict(bmat_re=bmat_re, bmat_im=bmat_im, cmat_re=cmat_re, cmat_im=cmat_im,
                       d=s5_d[j].reshape(1, width), glu_w=s5_glu_w[j].astype(BF16),
                       glu_b=s5_glu_b[j].reshape(1, width),
                       a_re=a_re.reshape(1, n_grp * n_st), a_im=a_im.reshape(1, n_grp * n_st)))
    per_layer_bf16 = lambda w: [w[i].astype(BF16) for i in range(w.shape[0])]
    p = dict(norm_mix=norm_mix, norm_ffn=norm_ffn, norm_final=norm_final, s5=s5,
             w_in_ab=per_layer_bf16(w_in_ab), w_out_ab=per_layer_bf16(w_out_ab),
             w_kt=[w_in_ab[j][:, width + ret_width:width + 2 * ret_width].T.astype(BF16) for j in range(n_ab)],
             w_in_c=per_layer_bf16(w_in_c), hg_lb_logits=hg_lb_logits, hg_norm_w=hg_norm_w,
             w_out_c=per_layer_bf16(w_out_c), ffn_w_gate=ffn_w_gate.astype(BF16),
             ffn_w_up=ffn_w_up.astype(BF16), ffn_conv_w=ffn_conv_w, ffn_conv_b=ffn_conv_b,
             ffn_w_down=ffn_w_down.astype(BF16))

    bp, lp, _ = x_prompt.shape
    z_s5 = jnp.zeros((n_ab, bp) + state_s5_re.shape[2:], F32)
    z_ret = jnp.zeros((n_ab, bp) + state_ret.shape[2:], F32)
    z_hg = jnp.zeros((state_hgrn.shape[0], bp) + state_hgrn.shape[2:], F32)
    z_conv = jnp.zeros((norm_mix.shape[0], bp) + state_ffn_conv.shape[2:], F32)
    outs_p = _trunk(x_prompt, jnp.arange(lp, dtype=jnp.int32), z_s5, z_s5, z_ret, z_hg, z_conv, p,
                    s5_tl=S5_ROW_TILE // bp)
    bs, ls, _ = x_sample.shape
    pos_s = (pos_sample[:, None] + jnp.arange(ls, dtype=jnp.int32)[None, :]).reshape(-1)
    outs_s = _trunk(x_sample, pos_s, state_s5_re, state_s5_im, state_ret, state_hgrn, state_ffn_conv, p,
                    s5_tl=ls)
    return (outs_p[0], outs_s[0]) + outs_p[1:] + outs_s[1:]
```

```python
import functools

import jax
import jax.numpy as jnp
import numpy as np
from jax import lax
from jax.experimental import pallas as pl
from jax.experimental.pallas import tpu as pltpu

F32 = jnp.float32
BF16 = jnp.bfloat16

NORM_EPS = 1e-6
ROPE_BASE = 10000.0
S5_GROUP_CH = 16
RET_HEADS = 4
RET_CHUNK = 128
HG_HEADS = 8
HG_BLOCK = 16
HG_CHUNK = 128
RET_HEADS_PER_STEP = 4
HG_HEADS_PER_STEP = 4
SHORT_HEADS_PER_STEP = 2
CONV_W = 3

LANES = 128
SUBLANES = 8
MXU_DIM = 256
VMEM_LIMIT_BYTES = 56 * 1024 * 1024
ROW_TILE = 512
FFN_ROW_TILE = 1024
PROJ_ROW_TILE = 1024
S5_ROW_TILE = 1024
FF_CHUNK = 256
S5_SCAN_COLS = 512


def _cparams(n_grid_dims):
    return pltpu.CompilerParams(dimension_semantics=("arbitrary",) * n_grid_dims,
                                vmem_limit_bytes=VMEM_LIMIT_BYTES)


def _dot(a, b):
    return jnp.dot(a.astype(BF16), b.astype(BF16), preferred_element_type=F32)


def _dot_nt(a, b):
    return lax.dot_general(a.astype(BF16), b.astype(BF16), (((1,), (1,)), ((), ())),
                           preferred_element_type=F32)


def _rmsnorm(x, w):
    return x * lax.rsqrt(jnp.mean(x * x, axis=-1, keepdims=True) + NORM_EPS) * w


def _sigmoid(x):
    return 0.5 * jnp.tanh(0.5 * x) + 0.5


def _full(shape):
    nd = len(shape)
    return pl.BlockSpec(shape, lambda *_: (0,) * nd)


def _mix_dtype(l):
    return BF16 if l % ROW_TILE == 0 else F32


def _resident(shape):
    nd = len(shape)
    return pl.BlockSpec(shape, lambda *_: (0,) * nd, pipeline_mode=pl.Buffered(1))


def _proj_kernel(*refs, segments, col_chunk, row_chunk, t_chunk, gate_layer):
    refs = list(refs)
    x_ref, nw_ref, w_ref = refs[:3]
    rest = refs[3:]
    lg_ref = rest.pop(0) if gate_layer is not None else None
    wt_ref = rest.pop(0) if t_chunk else None
    o_ref = rest.pop(0)
    ot_ref = rest.pop(0) if t_chunk else None
    for r in range(0, x_ref.shape[0], row_chunk):
        rows = slice(r, r + row_chunk)
        h = _rmsnorm(x_ref[rows, :], nw_ref[...]).astype(BF16)
        out0 = 0
        for col0, width, kind in segments:
            for c in range(0, width, col_chunk):
                val = jnp.dot(h, w_ref[:, col0 + c:col0 + c + col_chunk], preferred_element_type=F32)
                if kind == "silu":
                    val = val * _sigmoid(val)
                elif kind == "log2_forget":
                    val = _hgrn_log2_forget(val, lg_ref[:, c:c + col_chunk], gate_layer)
                o_ref[rows, out0 + c:out0 + c + col_chunk] = val
            out0 += width
        if t_chunk:
            zt = _dot_nt(wt_ref[...], h)
            for j in range(row_chunk // t_chunk):
                ot_ref[r // t_chunk + j] = zt[:, j * t_chunk:(j + 1) * t_chunk]


def _proj(x, nw, w_bf, segments=None, wt_bf=None, t_chunk=0, gate_logits=None, gate_layer=None):
    n, d = x.shape
    segments = tuple((tuple(s) + (None,))[:3] for s in (segments or ((0, w_bf.shape[1]),)))
    n_out = sum(width for _, width, _ in segments)
    tm = min(PROJ_ROW_TILE, n)
    in_specs = [pl.BlockSpec((tm, d), lambda i: (i, 0)), _full((1, d)), _resident(w_bf.shape)]
    out_specs = [pl.BlockSpec((tm, n_out), lambda i: (i, 0))]
    out_shape = [jax.ShapeDtypeStruct((n, n_out), F32)]
    args = [x, nw, w_bf]
    if gate_layer is not None:
        in_specs.append(_full(gate_logits.shape))
        args.append(gate_logits)
    if t_chunk:
        t_width = wt_bf.shape[0]
        in_specs.append(_resident(wt_bf.shape))
        out_specs.append(pl.BlockSpec((tm // t_chunk, t_width, t_chunk), lambda i: (i, 0, 0)))
        out_shape.append(jax.ShapeDtypeStruct((n // t_chunk, t_width, t_chunk), F32))
        args.append(wt_bf)
    outs = pl.pallas_call(
        functools.partial(_proj_kernel, segments=segments, col_chunk=2 * MXU_DIM, row_chunk=min(ROW_TILE, tm),
                          t_chunk=t_chunk, gate_layer=gate_layer),
        grid=(n // tm,), in_specs=in_specs, out_specs=out_specs, out_shape=out_shape,
        compiler_params=_cparams(1), name="proj")(*args)
    return outs if t_chunk else outs[0]


def _s5_prep_kernel(lre_ref, lim_ref, ldt_ref, lre16_ref, lim16_ref, ldt16_ref, bre_ref, bim_ref, cre_ref, cim_ref,
                    are_ref, aim_ref, bmre_ref, bmim_ref, cmre_ref, cmim_ref, *, ch):
    def block_diag(m):
        rows, p = m.shape
        tiled = jnp.concatenate([m] * (rows // ch), axis=1)
        row = lax.broadcasted_iota(jnp.int32, tiled.shape, 0)
        col = lax.broadcasted_iota(jnp.int32, tiled.shape, 1)
        return jnp.where(col // p == row // ch, tiled, 0.0)

    def disc(lre, lim, ldt):
        dt = jnp.exp(ldt)
        mag = jnp.exp(lre * dt)
        ang = lim * dt
        return mag * jnp.cos(ang), mag * jnp.sin(ang)

    ab_re, ab_im = disc(lre_ref[...], lim_ref[...], ldt_ref[...])
    are_ref[...] = ab_re
    aim_ref[...] = ab_im
    lre, lim = lre16_ref[...], lim16_ref[...]
    ab_re, ab_im = disc(lre, lim, ldt16_ref[...])
    nr, ni = ab_re - 1.0, ab_im
    den = lre * lre + lim * lim
    f_re = (nr * lre + ni * lim) / den
    f_im = (ni * lre - nr * lim) / den
    b_re, b_im = bre_ref[...], bim_ref[...]
    bb_re = f_re * b_re - f_im * b_im
    bb_im = f_re * b_im + f_im * b_re
    kw = bmre_ref.shape[1]
    for kb in range(bmre_ref.shape[0]):
        rows = slice(kb * kw, (kb + 1) * kw)
        bmre_ref[kb] = block_diag(bb_re[rows]).astype(bmre_ref.dtype)
        bmim_ref[kb] = block_diag(bb_im[rows]).astype(bmim_ref.dtype)
        cmre_ref[kb] = block_diag(cre_ref[rows, :]).T.astype(cmre_ref.dtype)
        cmim_ref[kb] = block_diag(cim_ref[rows, :]).T.astype(cmim_ref.dtype)


def _s5_prep(lam_re, lam_im, log_dt, b_re, b_im, c_re, c_im, grp_per_blk):
    g, p = lam_re.shape
    ch = b_re.shape[-1]
    rep = lambda a: jnp.repeat(a, ch, axis=0)
    ldt = log_dt.reshape(g, 1)
    bt = lambda b: jnp.swapaxes(b, 1, 2).reshape(g * ch, p)
    args = (lam_re, lam_im, ldt, rep(lam_re), rep(lam_im), rep(ldt), bt(b_re), bt(b_im),
            c_re.reshape(g * ch, p), c_im.reshape(g * ch, p))
    n_blk, kw, cw = g // grp_per_blk, grp_per_blk * ch, grp_per_blk * p
    shapes = [(g, p)] * 2 + [(n_blk, kw, cw)] * 2 + [(n_blk, cw, kw)] * 2
    dtypes = [F32] * 2 + [BF16] * 4
    return pl.pallas_call(
        functools.partial(_s5_prep_kernel, ch=ch),
        in_specs=[_full(a.shape) for a in args],
        out_specs=[_full(s) for s in shapes],
        out_shape=[jax.ShapeDtypeStruct(s, t) for s, t in zip(shapes, dtypes)],
        name="s5_prep")(*args)


def _s5_kernel(x_ref, nw_ref, wu_ref, bre_ref, bim_ref, cre_ref, cim_ref, d_ref, gw_ref, gb_ref,
               are_ref, aim_ref, s0re_ref, s0im_ref,
               y_ref, sre_ref, sim_ref, bure_s, buim_s, stre_s, stim_s, *, tb, tl):
    @pl.when(pl.program_id(0) == 0)
    def _():
        stre_s[...] = s0re_ref[...]
        stim_s[...] = s0im_ref[...]

    x = jnp.concatenate([x_ref[:, t, :] for t in range(tl)], axis=0)
    u = jnp.dot(_rmsnorm(x, nw_ref[...]).astype(BF16), wu_ref[...], preferred_element_type=F32)
    ub = u.astype(BF16)

    n_kb, kw, cw = bre_ref.shape
    for kb in range(n_kb):
        uk = ub[:, kb * kw:(kb + 1) * kw]
        bure_s[:, kb * cw:(kb + 1) * cw] = jnp.dot(uk, bre_ref[kb], preferred_element_type=F32)
        buim_s[:, kb * cw:(kb + 1) * cw] = jnp.dot(uk, bim_ref[kb], preferred_element_type=F32)

    n_state = are_ref.shape[1]
    sc = min(S5_SCAN_COLS, n_state)
    for c0 in range(0, n_state, sc):
        cs = slice(c0, c0 + sc)
        a_re = jnp.broadcast_to(are_ref[:, cs], (tb, sc))
        a_im = jnp.broadcast_to(aim_ref[:, cs], (tb, sc))

        def step(t, carry, cs=cs, a_re=a_re, a_im=a_im):
            s_re, s_im = carry
            rows = pl.ds(pl.multiple_of(t * tb, tb), tb)
            n_re = a_re * s_re - a_im * s_im + bure_s[rows, cs]
            n_im = a_re * s_im + a_im * s_re + buim_s[rows, cs]
            bure_s[rows, cs] = n_re
            buim_s[rows, cs] = n_im
            return n_re, n_im

        s_re, s_im = lax.fori_loop(0, tl, step, (stre_s[:, cs], stim_s[:, cs]), unroll=True)
        stre_s[:, cs] = s_re
        stim_s[:, cs] = s_im

    n_ob, ckw, ocw = cre_ref.shape
    ys = []
    for ob in range(n_ob):
        ks = slice(ob * ckw, (ob + 1) * ckw)
        ys.append(_dot(bure_s[:, ks], cre_ref[ob]) - _dot(buim_s[:, ks], cim_ref[ob]))
    y = jnp.concatenate(ys, axis=1) + d_ref[...] * u
    y = jax.nn.gelu(y)
    out = y * jax.nn.sigmoid(_dot(y, gw_ref[...]) + gb_ref[...])
    for t in range(tl):
        y_ref[:, t, :] = out[t * tb:(t + 1) * tb, :]
    sre_ref[...] = stre_s[...]
    sim_ref[...] = stim_s[...]


def _s5_mixer(x3, nw, w_in_bf, bmat_re, bmat_im, cmat_re, cmat_im, d_row, glu_w_bf, glu_b, a_re, a_im,
              s0_re, s0_im, tl):
    b, l, d = x3.shape
    width = d_row.shape[1]
    n_state = a_re.shape[1]
    rows = b * tl
    args = (x3, nw, w_in_bf, bmat_re, bmat_im, cmat_re, cmat_im, d_row, glu_w_bf, glu_b,
            a_re, a_im, s0_re, s0_im)
    in_specs = ([pl.BlockSpec((b, tl, d), lambda i: (0, i, 0)), _full(nw.shape),
                 pl.BlockSpec((d, width), lambda i: (0, 0))] + [_full(a.shape) for a in args[3:]])
    y, s_re, s_im = pl.pallas_call(
        functools.partial(_s5_kernel, tb=b, tl=tl),
        grid=(l // tl,), in_specs=in_specs,
        out_specs=[pl.BlockSpec((b, tl, width), lambda i: (0, i, 0)), _full((b, n_state)), _full((b, n_state))],
        out_shape=[jax.ShapeDtypeStruct((b, l, width), F32),
                   jax.ShapeDtypeStruct((b, n_state), F32), jax.ShapeDtypeStruct((b, n_state), F32)],
        scratch_shapes=[pltpu.VMEM((rows, n_state), F32), pltpu.VMEM((rows, n_state), F32),
                        pltpu.VMEM((b, n_state), F32), pltpu.VMEM((b, n_state), F32)],
        compiler_params=_cparams(1), name="s5_mixer")(*args)
    return y.reshape(b * l, width), s_re, s_im


def _rope_kernel(pos_ref, inv_ref, cc_ref, ss_ref, *t_refs, t_chunk):
    ang = pos_ref[...] * inv_ref[...]
    lane = lax.broadcasted_iota(jnp.int32, ang.shape, 1)
    cc = jnp.cos(ang)
    ss = jnp.where(lane < ang.shape[1] // 2, -jnp.sin(ang), jnp.sin(ang))
    cc_ref[...] = cc
    ss_ref[...] = ss
    if t_chunk:
        cct_ref, sst_ref = t_refs
        for i in range(cct_ref.shape[0]):
            cct_ref[i] = cc[i * t_chunk:(i + 1) * t_chunk].T
            sst_ref[i] = ss[i * t_chunk:(i + 1) * t_chunk].T


def _rope_tables(pos, dk, t_chunk=0):
    n = pos.shape[0]
    inv = 1.0 / (ROPE_BASE ** jnp.linspace(0.0, 1.0, dk // 2, dtype=F32))
    inv = jnp.concatenate([inv, inv]).reshape(1, dk)
    shapes = [(n, dk)] * 2 + ([(n // t_chunk, dk, t_chunk)] * 2 if t_chunk else [])
    return pl.pallas_call(
        functools.partial(_rope_kernel, t_chunk=t_chunk),
        in_specs=[_full((n, 1)), _full((1, dk))],
        out_specs=[_full(s) for s in shapes],
        out_shape=[jax.ShapeDtypeStruct(s, F32) for s in shapes],
        name="rope_tables")(pos.astype(F32).reshape(n, 1), inv)


def _rotate(t, cc, ss):
    return t * cc + pltpu.roll(t, t.shape[1] // 2, 1) * ss


def _rotate_t(t, cc, ss):
    half = t.shape[0] // 2
    return t * cc + jnp.concatenate([t[half:], t[:half]], axis=0) * ss


def _ret_consts(rows, c):
    f32 = np.float32
    lg = np.log(f32(1.0) - f32(2.0) ** (f32(-5.0) - np.arange(RET_HEADS, dtype=f32)))
    r = np.arange(rows)
    idx, blk = r % c, r // c
    diff = idx[:, None] - idx[None, :]
    same = blk[:, None] == blk[None, :]
    decay = np.where((same & (diff >= 0))[None],
                     np.exp(np.maximum(diff, 0)[None].astype(f32) * lg[:, None, None]), f32(0.0))
    wide = lambda v: np.broadcast_to(v[:, :, None], (RET_HEADS, rows, LANES))
    q_dec = wide(np.exp((idx + 1).astype(f32)[None, :] * lg[:, None]))
    k_dec = wide(np.exp((c - 1 - idx).astype(f32)[None, :] * lg[:, None]))
    chunk_dec = np.broadcast_to(np.exp(f32(c) * lg)[:, None, None], (RET_HEADS, SUBLANES, LANES))
    return tuple(jnp.asarray(np.ascontiguousarray(a), F32) for a in (decay, q_dec, k_dec, chunk_dec))


def _ret_gate(o, g):
    o = o * lax.rsqrt(jnp.mean(o * o, axis=-1, keepdims=True) + NORM_EPS)
    return jax.nn.silu(g) * o


def _ret_long_kernel(q_ref, kt_ref, v_ref, g_ref, cc_ref, ss_ref, cct_ref, sst_ref, dec_ref, qd_ref, kdt_ref,
                     cd_ref, s0_ref, y_ref, so_ref, *, c, hp):
    dk = q_ref.shape[1] // hp
    scale = dk ** -0.5

    def chunk(i, states):
        rows = pl.ds(pl.multiple_of(i * c, c), c)
        cc, ss = cc_ref[rows, :], ss_ref[rows, :]
        cct, sst = cct_ref[i], sst_ref[i]
        new_states = []
        for j in range(hp):
            cols = slice(j * dk, (j + 1) * dk)
            q = _rotate(q_ref[rows, cols], cc, ss)
            kt = _rotate_t(kt_ref[i, cols, :], cct, sst) * scale
            v = v_ref[rows, cols]
            scores = _dot(q, kt) * dec_ref[j]
            o = _dot(scores, v) + _dot(q * qd_ref[j], states[j])
            new_states.append(cd_ref[j, 0:1, :] * states[j] + _dot(kt * kdt_ref[j], v))
            y_ref[rows, cols] = _ret_gate(o, g_ref[rows, cols]).astype(y_ref.dtype)
        return tuple(new_states)

    states = lax.fori_loop(0, q_ref.shape[0] // c, chunk, tuple(s0_ref[0, j] for j in range(hp)), unroll=4)
    for j in range(hp):
        so_ref[0, j] = states[j]


def _short_state_pass(qd, kt, v, s0_ref, so_ref, hd, decay_of, l):
    rows, dk = v.shape
    row = lax.broadcasted_iota(jnp.int32, (rows, dk), 0)
    v_cols = jnp.concatenate([jnp.where(row // l == b, v, 0.0).astype(BF16) for b in range(rows // l)], axis=1)
    kv = jnp.dot(kt.astype(BF16), v_cols, preferred_element_type=F32)
    win = 16
    per = win // l
    wrow = lax.broadcasted_iota(jnp.int32, (win, dk), 0)
    outs = []
    for w in range(rows // win):
        qw = qd[w * win:(w + 1) * win].astype(BF16)
        oi = jnp.zeros((win, dk), F32)
        for j in range(per):
            b = w * per + j
            s = s0_ref[b, hd]
            oi = jnp.where(wrow // l == j, jnp.dot(qw, s.astype(BF16), preferred_element_type=F32), oi)
            so_ref[b, hd] = decay_of(b) * s + kv[:, b * dk:(b + 1) * dk]
        outs.append(oi)
    return jnp.concatenate(outs, axis=0)


def _ret_short_kernel(q_ref, k_ref, v_ref, g_ref, cc_ref, ss_ref, dec_ref, qd_ref, kd_ref, cd_ref, s0_ref,
                      y_ref, so_ref, *, l, hp):
    dk = q_ref.shape[1] // hp
    scale = dk ** -0.5
    cc, ss = cc_ref[...], ss_ref[...]
    for hd in range(hp):
        cols = slice(hd * dk, (hd + 1) * dk)
        q = _rotate(q_ref[:, cols], cc, ss)
        k = _rotate(k_ref[:, cols], cc, ss) * scale
        v = v_ref[:, cols]
        o_intra = _dot(_dot_nt(q, k) * dec_ref[hd], v)
        cd = cd_ref[hd, 0:1, :]
        o_inter = _short_state_pass(q * qd_ref[hd], (k * kd_ref[hd]).T, v, s0_ref, so_ref, hd, lambda b: cd, l)
        y_ref[:, cols] = _ret_gate(o_intra + o_inter, g_ref[:, cols]).astype(y_ref.dtype)


def _retention(x, nw, w_in_bf, w_kt_bf, col0, pos, s0, n_seq, l):
    n = x.shape[0]
    dk = s0.shape[-1]
    width = RET_HEADS * dk
    out_shape = [jax.ShapeDtypeStruct((n, width), _mix_dtype(l)), jax.ShapeDtypeStruct(s0.shape, F32)]
    cspec = lambda a, hp: pl.BlockSpec((hp,) + a.shape[1:], lambda b, h: (h, 0, 0))
    if l % RET_CHUNK == 0:
        c, hp = RET_CHUNK, RET_HEADS_PER_STEP
        n_hb = RET_HEADS // hp
        z, kt = _proj(x, nw, w_in_bf, ((col0, width), (col0 + 2 * width, 2 * width)), wt_bf=w_kt_bf, t_chunk=c)
        cc, ss, cct, sst = _rope_tables(pos, dk, t_chunk=c)
        decay, q_dec, k_dec, chunk_dec = _ret_consts(c, c)
        k_dec_t = jnp.swapaxes(k_dec, 1, 2)
        zspec = lambda field: pl.BlockSpec((l, hp * dk), lambda b, h: (b, field * n_hb + h))
        sspec = pl.BlockSpec((1, hp, dk, dk), lambda b, h: (b, h, 0, 0))
        return pl.pallas_call(
            functools.partial(_ret_long_kernel, c=c, hp=hp), grid=(n_seq, n_hb),
            in_specs=[zspec(0), pl.BlockSpec((l // c, hp * dk, c), lambda b, h: (b, h, 0)), zspec(1), zspec(2),
                      _full(cc.shape), _full(ss.shape), _full(cct.shape), _full(sst.shape),
                      cspec(decay, hp), cspec(q_dec, hp), cspec(k_dec_t, hp), cspec(chunk_dec, hp), sspec],
            out_specs=[pl.BlockSpec((l, hp * dk), lambda b, h: (b, h)), sspec],
            out_shape=out_shape, compiler_params=_cparams(2), name="retention")(
                z, kt, z, z, cc, ss, cct, sst, decay, q_dec, k_dec_t, chunk_dec, s0)
    rows, hp = LANES, SHORT_HEADS_PER_STEP
    per_blk = rows // l
    n_hb = RET_HEADS // hp
    z = _proj(x, nw, w_in_bf, segments=((col0, 4 * width),))
    cc, ss = _rope_tables(pos, dk)
    decay, q_dec, k_dec, chunk_dec = _ret_consts(rows, l)
    zspec = lambda field: pl.BlockSpec((rows, hp * dk), lambda b, h: (b, field * n_hb + h))
    tspec = pl.BlockSpec((rows, dk), lambda b, h: (b, 0))
    sspec = pl.BlockSpec((per_blk, hp, dk, dk), lambda b, h: (b, h, 0, 0))
    return pl.pallas_call(
        functools.partial(_ret_short_kernel, l=l, hp=hp), grid=(n_seq // per_blk, n_hb),
        in_specs=[zspec(0), zspec(1), zspec(2), zspec(3), tspec, tspec,
                  cspec(decay, hp), cspec(q_dec, hp), cspec(k_dec, hp), cspec(chunk_dec, hp), sspec],
        out_specs=[pl.BlockSpec((rows, hp * dk), lambda b, h: (b, h)), sspec],
        out_shape=out_shape,
        compiler_params=_cparams(2), name="retention")(z, z, z, z, cc, ss, decay, q_dec, k_dec, chunk_dec, s0)


def _cumsum_rows(x, period):
    row = lax.broadcasted_iota(jnp.int32, x.shape, 0)
    s = 1
    while s < period:
        x = x + jnp.where(row % period >= s, pltpu.roll(x, s, 0), 0.0)
        s *= 2
    return x


def _row_of_block(x, period, offset):
    n, w = x.shape
    x3 = x.reshape(n // period, period, w)
    return jnp.broadcast_to(x3[:, offset:offset + 1, :], x3.shape).reshape(n, w)


def _hgrn_log2_forget(fl, lg, layer):
    e = jnp.exp(lg - jnp.max(lg, axis=0, keepdims=True))
    soft = e / jnp.sum(e, axis=0, keepdims=True)
    cum = soft[0:1]
    for i in range(1, layer + 1):
        cum = cum + soft[i:i + 1]
    lb = cum - soft[0:1]
    return jnp.log2(lb + (1.0 - lb) * _sigmoid(fl))


def _hgrn_out(o, gate, nw):
    o = o * lax.rsqrt(jnp.mean(o * o, axis=-1, keepdims=True) + NORM_EPS) * nw
    return o * gate


def _hgrn_levels(c):
    levels, half = [], HG_BLOCK
    while half < c:
        levels.append(half)
        half *= 2
    return levels


def _hgrn_long_kernel(q_ref, f_ref, v_ref, g_ref, nw_ref, s0_ref, y_ref, so_ref, *, c, hp):
    dk = q_ref.shape[1] // hp
    nw = nw_ref[...]
    nv = c // SUBLANES
    bv = HG_BLOCK // SUBLANES
    levels = _hgrn_levels(c)
    row = lax.broadcasted_iota(jnp.int32, (c, c), 0)
    col = lax.broadcasted_iota(jnp.int32, (c, c), 1)
    lvl = jnp.where((row // HG_BLOCK == col // HG_BLOCK) & (col <= row), 0, -1)
    for li, half in enumerate(levels, 1):
        lvl = jnp.where((row // (2 * half) == col // (2 * half)) & (row % (2 * half) >= half)
                        & (col % (2 * half) < half), li, lvl)
    sub = lax.broadcasted_iota(jnp.int32, (SUBLANES, dk), 0)
    zeros = jnp.zeros((SUBLANES, dk), F32)
    cat = lambda slabs: jnp.concatenate(slabs, axis=0)

    def one_head(rows, cols, s):
        q, v = q_ref[rows, cols], v_ref[rows, cols]
        lf = f_ref[rows, cols]
        k = 1.0 - jnp.exp2(lf)
        slabs = lambda a: [a[j * SUBLANES:(j + 1) * SUBLANES, :] for j in range(nv)]
        qs, ks = slabs(q), slabs(k)
        loc, run = [], [jnp.zeros((1, dk), F32)]
        for x in slabs(lf):
            step = 1
            while step < SUBLANES:
                x = x + jnp.where(sub >= step, pltpu.roll(x, step, 0), 0.0)
                step *= 2
            loc.append(x)
            run.append(run[-1] + x[SUBLANES - 1:, :])
        d0 = [loc[j] + (run[j] - run[j - j % bv]) if j % bv else loc[j] for j in range(nv)]
        scores = jnp.where(lvl == 0, _dot_nt(cat([qs[j] * jnp.exp2(d0[j]) for j in range(nv)]),
                                             cat([ks[j] * jnp.exp2(-d0[j]) for j in range(nv)])), 0.0)
        for li, half in enumerate(levels, 1):
            hv = half // SUBLANES
            q_side, k_side = [], []
            for j in range(nv):
                anchor = run[j - j % (2 * hv) + hv]
                if j % (2 * hv) >= hv:
                    q_side.append(qs[j] * jnp.exp2(loc[j] + (run[j] - anchor)))
                    k_side.append(zeros)
                else:
                    q_side.append(zeros)
                    k_side.append(ks[j] * jnp.exp2((anchor - run[j]) - loc[j]))
            scores = jnp.where(lvl == li, _dot_nt(cat(q_side), cat(k_side)), scores)
        o = _dot(scores, v) + _dot(cat([qs[j] * jnp.exp2(loc[j] + run[j]) for j in range(nv)]), s)
        k_hat = cat([ks[j] * jnp.exp2((run[nv] - run[j]) - loc[j]) for j in range(nv)])
        d_col = jnp.exp2(jnp.broadcast_to(run[nv], (dk, dk)).T)
        y_ref[rows, cols] = _hgrn_out(o, g_ref[rows, cols], nw).astype(y_ref.dtype)
        return d_col * s + _dot(k_hat.T, v)

    def chunk(i, states):
        rows = pl.ds(pl.multiple_of(i * c, c), c)
        return tuple(one_head(rows, slice(hd * dk, (hd + 1) * dk), states[hd]) for hd in range(hp))

    states = lax.fori_loop(0, q_ref.shape[0] // c, chunk, tuple(s0_ref[0, j] for j in range(hp)), unroll=4)
    for j in range(hp):
        so_ref[0, j] = states[j]


def _hgrn_short_kernel(q_ref, f_ref, v_ref, g_ref, nw_ref, s0_ref, y_ref, so_ref, *, l, hp):
    rows = q_ref.shape[0]
    dk = q_ref.shape[1] // hp
    row = lax.broadcasted_iota(jnp.int32, (rows, rows), 0)
    col = lax.broadcasted_iota(jnp.int32, (rows, rows), 1)
    causal = (row // l == col // l) & (col <= row)
    for hd in range(hp):
        cols = slice(hd * dk, (hd + 1) * dk)
        q, v = q_ref[:, cols], v_ref[:, cols]
        lf = f_ref[:, cols]
        k = 1.0 - jnp.exp2(lf)
        b = _cumsum_rows(lf, l)
        qe = q * jnp.exp2(b)
        o_intra = _dot(jnp.where(causal, _dot_nt(qe, k * jnp.exp2(-b)), 0.0), v)
        b_last = _row_of_block(b, l, l - 1)
        dect = jnp.exp2(b_last).T
        o_inter = _short_state_pass(qe, (k * jnp.exp2(b_last - b)).T, v, s0_ref, so_ref, hd,
                                    lambda s, dect=dect: dect[:, s * l:s * l + 1], l)
        y_ref[:, cols] = _hgrn_out(o_intra + o_inter, g_ref[:, cols], nw_ref[...]).astype(y_ref.dtype)


def _hgrn(z, norm_w, s0, n_seq, l):
    n = z.shape[0]
    dk = s0.shape[-1]
    if l % HG_CHUNK == 0:
        rows, n_blk, hp = l, n_seq, HG_HEADS_PER_STEP
        kern = functools.partial(_hgrn_long_kernel, c=HG_CHUNK, hp=hp)
        sspec = pl.BlockSpec((1, hp, dk, dk), lambda b, h: (b, h, 0, 0))
    else:
        assert HG_BLOCK % l == 0
        rows, hp = LANES, SHORT_HEADS_PER_STEP
        per_blk = rows // l
        n_blk = n_seq // per_blk
        kern = functools.partial(_hgrn_short_kernel, l=l, hp=hp)
        sspec = pl.BlockSpec((per_blk, hp, dk, dk), lambda b, h: (b, h, 0, 0))
    n_hb = HG_HEADS // hp
    zspec = lambda rows, field: pl.BlockSpec((rows, hp * dk), lambda b, h: (b, field * n_hb + h))
    y, s_new = pl.pallas_call(
        kern, grid=(n_blk, n_hb),
        in_specs=[zspec(rows, 2), zspec(rows, 0), zspec(rows, 3), zspec(rows, 1), _full((1, dk)), sspec],
        out_specs=[pl.BlockSpec((rows, hp * dk), lambda b, h: (b, h)), sspec],
        out_shape=[jax.ShapeDtypeStruct((n, HG_HEADS * dk), _mix_dtype(l)), jax.ShapeDtypeStruct(s0.shape, F32)],
        compiler_params=_cparams(2), name="hgrn")(z, z, z, z, norm_w, s0)
    return y, s_new


def _ffn_kernel(*refs, n_parts, long_mode, final_norm, l):
    x_ref, parts = refs[0], refs[1:1 + n_parts]
    (wo_ref, nw_ref, wg_ref, wu_ref, cw_ref, cb_ref, wd_ref, fnw_ref, buf_ref, o_ref, bufo_ref,
     act_s) = refs[1 + n_parts:13 + n_parts]
    d_ff = wg_ref.shape[1]
    if long_mode:
        tail_s = refs[13 + n_parts]
        tm = x_ref.shape[1]
        rc = min(ROW_TILE, tm)
        load = lambda ref, r0: ref[0, r0:r0 + rc, :]

        @pl.when(pl.program_id(1) == 0)
        def _():
            tail_s[...] = buf_ref[0]
    else:
        nb = x_ref.shape[0]
        tm = rc = l * nb
        load = lambda ref, r0: jnp.concatenate([ref[:, t, :] for t in range(l)], axis=0)

    xs = []
    for r0 in range(0, tm, rc):
        mix = jnp.concatenate([load(p, r0).astype(BF16) for p in parts], axis=1)
        x = load(x_ref, r0) + jnp.dot(mix, wo_ref[...], preferred_element_type=F32)
        xs.append(x)
        h = _rmsnorm(x, nw_ref[...]).astype(BF16)
        for c0 in range(0, d_ff, FF_CHUNK):
            cs = slice(c0, c0 + FF_CHUNK)
            g = jnp.dot(h, wg_ref[:, cs], preferred_element_type=F32)
            up = jnp.dot(h, wu_ref[:, cs], preferred_element_type=F32)
            if long_mode:
                row = lax.broadcasted_iota(jnp.int32, g.shape, 0)
                p1 = jnp.where(row < 1, tail_s[1:2, cs], pltpu.roll(g, 1, 0))
                p2 = jnp.where(row < 2, jnp.where(row == 0, tail_s[0:1, cs], tail_s[1:2, cs]),
                               pltpu.roll(g, 2, 0))
                tail_s[:, cs] = g[rc - (CONV_W - 1):, :]
            else:
                b0, b1 = buf_ref[:, 0, cs], buf_ref[:, 1, cs]
                p1 = jnp.concatenate([b1, g[:tm - nb]], axis=0)
                p2 = jnp.concatenate([b0, b1, g[:tm - 2 * nb]], axis=0)
                bufo_ref[:, 0, cs] = g[tm - 2 * nb:tm - nb]
                bufo_ref[:, 1, cs] = g[tm - nb:]
            conv = cb_ref[:, cs] + cw_ref[0:1, cs] * p2
            conv = conv + cw_ref[1:2, cs] * p1
            conv = conv + cw_ref[2:3, cs] * g
            act_s[r0:r0 + rc, cs] = (jax.nn.silu(conv) * up).astype(BF16)
    y = jnp.concatenate(xs, axis=0) + jnp.dot(act_s[...], wd_ref[...], preferred_element_type=F32)
    if final_norm:
        y = _rmsnorm(y, fnw_ref[...])
    if long_mode:
        o_ref[0] = y
        bufo_ref[0] = tail_s[...]
    else:
        for t in range(l):
            o_ref[:, t, :] = y[t * nb:(t + 1) * nb]


def _ffn(x, parts, layer, wo, nw, wg, wu, conv_w, conv_b, wd, fnw, buf, n_seq, l, final_norm):
    assert CONV_W == 3 and l >= CONV_W - 1
    n, d = x.shape
    d_ff = wg.shape[2]
    weights = (wo, nw, wg, wu, conv_w, conv_b, wd, fnw)
    stacked = (wg, wu, wd)

    def wspec(w, single_buffer):
        kw = dict(pipeline_mode=pl.Buffered(1)) if single_buffer else {}
        if any(w is s for s in stacked):
            return pl.BlockSpec((None,) + w.shape[1:], lambda *_: (layer, 0, 0), **kw)
        return pl.BlockSpec(w.shape, lambda *_: (0,) * w.ndim, **kw)
    kern = functools.partial(_ffn_kernel, n_parts=len(parts), final_norm=final_norm, l=l)
    buf_shape = jax.ShapeDtypeStruct((n_seq, CONV_W - 1, d_ff), F32)
    rows3 = lambda a: a.reshape(n_seq, l, a.shape[1])
    if l % ROW_TILE == 0:
        tm = FFN_ROW_TILE
        assert l % tm == 0
        rspec = lambda w: pl.BlockSpec((1, tm, w), lambda b, i: (b, i, 0))
        bspec = pl.BlockSpec((1, CONV_W - 1, d_ff), lambda b, i: (b, 0, 0))
        y, buf_new = pl.pallas_call(
            functools.partial(kern, long_mode=True),
            grid=(n_seq, l // tm),
            in_specs=([rspec(d)] + [rspec(p.shape[1]) for p in parts]
                      + [wspec(w, True) for w in weights]
                      + [pl.BlockSpec((None, 1, CONV_W - 1, d_ff), lambda b, i: (layer, b, 0, 0))]),
            out_specs=[rspec(d), bspec],
            out_shape=[jax.ShapeDtypeStruct((n_seq, l, d), F32), buf_shape],
            scratch_shapes=[pltpu.VMEM((tm, d_ff), BF16), pltpu.VMEM((CONV_W - 1, d_ff), F32)],
            compiler_params=_cparams(2), name="ffn")(rows3(x), *map(rows3, parts), *weights, buf)
        return y.reshape(n, d), buf_new
    y, buf_new = pl.pallas_call(
        functools.partial(kern, long_mode=False),
        grid=(1,),
        in_specs=([_full((n_seq, l, d))] + [_full((n_seq, l, p.shape[1])) for p in parts]
                  + [wspec(w, False) for w in weights]
                  + [pl.BlockSpec((None,) + buf.shape[1:], lambda *_: (layer, 0, 0, 0))]),
        out_specs=[_full((n_seq, l, d)), _full(buf.shape[1:])],
        out_shape=[jax.ShapeDtypeStruct((n_seq, l, d), F32), buf_shape],
        scratch_shapes=[pltpu.VMEM((n, d_ff), BF16)],
        compiler_params=_cparams(1), name="ffn")(
            rows3(x), *map(rows3, parts), *weights, buf)
    return y.reshape(n, d), buf_new


def _trunk(x3, pos, s5_re, s5_im, ret, hg, conv, p, s5_tl):
    b, l, d = x3.shape
    n = b * l
    x = x3.reshape(n, d)
    depth = p['norm_mix'].shape[0]
    new_re, new_im, new_ret, new_hg, new_conv = [], [], [], [], []
    for layer in range(depth):
        j = layer // 2
        nw = p['norm_mix'][layer].reshape(1, d)
        if layer % 2 == 0:
            s5w = p['s5'][j]
            width = s5w['d'].shape[1]
            y_s5, r, im = _s5_mixer(x.reshape(b, l, d), nw, p['w_in_ab'][j], s5w['bmat_re'], s5w['bmat_im'],
                                    s5w['cmat_re'], s5w['cmat_im'], s5w['d'], s5w['glu_w'], s5w['glu_b'],
                                    s5w['a_re'], s5w['a_im'], s5_re[j].reshape(b, -1), s5_im[j].reshape(b, -1),
                                    s5_tl)
            y_ret, st = _retention(x, nw, p['w_in_ab'][j], p['w_kt'][j], width, pos, ret[j], b, l)
            parts, w_out = [y_s5, y_ret], p['w_out_ab'][j]
            new_re.append(r.reshape(s5_re.shape[1:]))
            new_im.append(im.reshape(s5_im.shape[1:]))
            new_ret.append(st)
        else:
            hw = p['hg_lb_logits'].shape[1]
            z = _proj(x, nw, p['w_in_c'][j],
                      segments=((hw, hw, "log2_forget"), (3 * hw, hw, "silu"), (0, hw), (2 * hw, hw)),
                      gate_logits=p['hg_lb_logits'], gate_layer=layer)
            y_hg, st = _hgrn(z, p['hg_norm_w'][j].reshape(1, -1), hg[j], b, l)
            parts, w_out = [y_hg], p['w_out_c'][j]
            new_hg.append(st)
        x, buf = _ffn(x, parts, layer, w_out, p['norm_ffn'][layer].reshape(1, d), p['ffn_w_gate'],
                      p['ffn_w_up'], p['ffn_conv_w'][layer], p['ffn_conv_b'][layer].reshape(1, -1),
                      p['ffn_w_down'], p['norm_final'].reshape(1, d), conv, b, l,
                      final_norm=(layer == depth - 1))
        new_conv.append(buf)
    return (x.reshape(b, l, d), jnp.stack(new_re), jnp.stack(new_im), jnp.stack(new_ret),
            jnp.stack(new_hg), jnp.stack(new_conv))


def kernel(x_prompt, x_sample, state_s5_re, state_s5_im, state_ret, state_hgrn, state_ffn_conv, pos_sample, norm_mix, norm_ffn, norm_final, w_in_ab, s5_lam_re, s5_lam_im, s5_log_dt, s5_b_re, s5_b_im, s5_c_re, s5_c_im, s5_d, s5_glu_w, s5_glu_b, w_out_ab, w_in_c, hg_lb_logits, hg_norm_w, w_out_c, ffn_w_gate, ffn_w_up, ffn_conv_w, ffn_conv_b, ffn_w_down):
    n_ab, n_grp, n_st = s5_lam_re.shape
    ch = s5_b_re.shape[-1]
    width = n_grp * ch
    ret_width = state_ret.shape[2] * state_ret.shape[3]
    grp_per_blk = MXU_DIM // ch
    s5 = []
    for j in range(n_ab):
        a_re, a_im, bmat_re, bmat_im, cmat_re, cmat_im = _s5_prep(
            s5_lam_re[j], s5_lam_im[j], s5_log_dt[j], s5_b_re[j], s5_b_im[j], s5_c_re[j], s5_c_im[j], grp_per_blk)
        s5.append(dict(bmat_re=bmat_re, bmat_im=bmat_im, cmat_re=cmat_re, cmat_im=cmat_im,
                       d=s5_d[j].reshape(1, width), glu_w=s5_glu_w[j].astype(BF16),
                       glu_b=s5_glu_b[j].reshape(1, width),
                       a_re=a_re.reshape(1, n_grp * n_st), a_im=a_im.reshape(1, n_grp * n_st)))
    per_layer_bf16 = lambda w: [w[i].astype(BF16) for i in range(w.shape[0])]
    p = dict(norm_mix=norm_mix, norm_ffn=norm_ffn, norm_final=norm_final, s5=s5,
             w_in_ab=per_layer_bf16(w_in_ab), w_out_ab=per_layer_bf16(w_out_ab),
             w_kt=[w_in_ab[j][:, width + ret_width:width + 2 * ret_width].T.astype(BF16) for j in range(n_ab)],
             w_in_c=per_layer_bf16(w_in_c), hg_lb_logits=hg_lb_logits, hg_norm_w=hg_norm_w,
             w_out_c=per_layer_bf16(w_out_c), ffn_w_gate=ffn_w_gate.astype(BF16),
             ffn_w_up=ffn_w_up.astype(BF16), ffn_conv_w=ffn_conv_w, ffn_conv_b=ffn_conv_b,
             ffn_w_down=ffn_w_down.astype(BF16))

    bp, lp, _ = x_prompt.shape
    z_s5 = jnp.zeros((n_ab, bp) + state_s5_re.shape[2:], F32)
    z_ret = jnp.zeros((n_ab, bp) + state_ret.shape[2:], F32)
    z_hg = jnp.zeros((state_hgrn.shape[0], bp) + state_hgrn.shape[2:], F32)
    z_conv = jnp.zeros((norm_mix.shape[0], bp) + state_ffn_conv.shape[2:], F32)
    outs_p = _trunk(x_prompt, jnp.arange(lp, dtype=jnp.int32), z_s5, z_s5, z_ret, z_hg, z_conv, p,
                    s5_tl=S5_ROW_TILE // bp)
    bs, ls, _ = x_sample.shape
    pos_s = (pos_sample[:, None] + jnp.arange(ls, dtype=jnp.int32)[None, :]).reshape(-1)
    outs_s = _trunk(x_sample, pos_s, state_s5_re, state_s5_im, state_ret, state_hgrn, state_ffn_conv, p,
                    s5_tl=ls)
    return (outs_p[0], outs_s[0]) + outs_p[1:] + outs_s[1:]
```

```python
import functools

import jax
import jax.numpy as jnp
import numpy as np
from jax import lax
from jax.experimental import pallas as pl
from jax.experimental.pallas import tpu as pltpu

F32 = jnp.float32
BF16 = jnp.bfloat16

NORM_EPS = 1e-6
ROPE_BASE = 10000.0
RET_HEADS = 4
RET_CHUNK = 128
HG_HEADS = 8
HG_BLOCK = 16
HG_CHUNK = 128
RET_HEADS_PER_STEP = 4
HG_HEADS_PER_STEP = 4
RET_SHORT_HEADS_PER_STEP = 4
HG_SHORT_HEADS_PER_STEP = 2
CONV_W = 3

LANES = 128
SUBLANES = 8
MXU_DIM = 256
VMEM_LIMIT_BYTES = 56 * 1024 * 1024
ROW_TILE = 512
FFN_ROW_TILE = 1024
PROJ_ROW_TILE = 1024
S5_ROW_TILE = 1024
FF_CHUNK = 256
S5_SCAN_COLS = 512


def _cparams(n_grid_dims):
    return pltpu.CompilerParams(dimension_semantics=("arbitrary",) * n_grid_dims,
                                vmem_limit_bytes=VMEM_LIMIT_BYTES)


def _dot(a, b):
    return jnp.dot(a.astype(BF16), b.astype(BF16), preferred_element_type=F32)


def _dot_nt(a, b):
    return lax.dot_general(a.astype(BF16), b.astype(BF16), (((1,), (1,)), ((), ())),
                           preferred_element_type=F32)


def _rmsnorm(x, w):
    return x * lax.rsqrt(jnp.mean(x * x, axis=-1, keepdims=True) + NORM_EPS) * w


def _sigmoid(x):
    return 0.5 * jnp.tanh(0.5 * x) + 0.5


def _full(shape):
    nd = len(shape)
    return pl.BlockSpec(shape, lambda *_: (0,) * nd)


def _mix_dtype(l):
    return BF16 if l % ROW_TILE == 0 else F32


def _resident(shape):
    nd = len(shape)
    return pl.BlockSpec(shape, lambda *_: (0,) * nd, pipeline_mode=pl.Buffered(1))


def _proj_kernel(*refs, segments, col_chunk, row_chunk, t_chunk, gate_layer):
    refs = list(refs)
    x_ref, nw_ref, w_ref = refs[:3]
    rest = refs[3:]
    lg_ref = rest.pop(0) if gate_layer is not None else None
    wt_ref = rest.pop(0) if t_chunk else None
    o_ref = rest.pop(0)
    ot_ref = rest.pop(0) if t_chunk else None
    for r in range(0, x_ref.shape[0], row_chunk):
        rows = slice(r, r + row_chunk)
        h = _rmsnorm(x_ref[rows, :], nw_ref[...]).astype(BF16)
        out0 = 0
        for col0, width, kind in segments:
            for c in range(0, width, col_chunk):
                val = jnp.dot(h, w_ref[:, col0 + c:col0 + c + col_chunk], preferred_element_type=F32)
                if kind == "silu":
                    val = val * _sigmoid(val)
                elif kind == "log2_forget":
                    val = _hgrn_log2_forget(val, lg_ref[:, c:c + col_chunk], gate_layer)
                o_ref[rows, out0 + c:out0 + c + col_chunk] = val
            out0 += width
        if t_chunk:
            zt = _dot_nt(wt_ref[...], h)
            for j in range(row_chunk // t_chunk):
                ot_ref[r // t_chunk + j] = zt[:, j * t_chunk:(j + 1) * t_chunk]


def _proj(x, nw, w_bf, segments=None, wt_bf=None, t_chunk=0, gate_logits=None, gate_layer=None):
    n, d = x.shape
    segments = tuple((tuple(s) + (None,))[:3] for s in (segments or ((0, w_bf.shape[1]),)))
    n_out = sum(width for _, width, _ in segments)
    tm = min(PROJ_ROW_TILE, n)
    in_specs = [pl.BlockSpec((tm, d), lambda i: (i, 0)), _full((1, d)), _resident(w_bf.shape)]
    out_specs = [pl.BlockSpec((tm, n_out), lambda i: (i, 0))]
    out_shape = [jax.ShapeDtypeStruct((n, n_out), F32)]
    args = [x, nw, w_bf]
    if gate_layer is not None:
        in_specs.append(_full(gate_logits.shape))
        args.append(gate_logits)
    if t_chunk:
        t_width = wt_bf.shape[0]
        in_specs.append(_resident(wt_bf.shape))
        out_specs.append(pl.BlockSpec((tm // t_chunk, t_width, t_chunk), lambda i: (i, 0, 0)))
        out_shape.append(jax.ShapeDtypeStruct((n // t_chunk, t_width, t_chunk), F32))
        args.append(wt_bf)
    outs = pl.pallas_call(
        functools.partial(_proj_kernel, segments=segments, col_chunk=2 * MXU_DIM, row_chunk=min(ROW_TILE, tm),
                          t_chunk=t_chunk, gate_layer=gate_layer),
        grid=(n // tm,), in_specs=in_specs, out_specs=out_specs, out_shape=out_shape,
        compiler_params=_cparams(1), name="proj")(*args)
    return outs if t_chunk else outs[0]


def _s5_prep_kernel(lre_ref, lim_ref, ldt_ref, lre16_ref, lim16_ref, ldt16_ref, bre_ref, bim_ref, cre_ref, cim_ref,
                    are_ref, aim_ref, bmre_ref, bmim_ref, cmre_ref, cmim_ref, *, ch):
    def block_diag(m):
        rows, p = m.shape
        tiled = jnp.concatenate([m] * (rows // ch), axis=1)
        row = lax.broadcasted_iota(jnp.int32, tiled.shape, 0)
        col = lax.broadcasted_iota(jnp.int32, tiled.shape, 1)
        return jnp.where(col // p == row // ch, tiled, 0.0)

    def disc(lre, lim, ldt):
        dt = jnp.exp(ldt)
        mag = jnp.exp(lre * dt)
        ang = lim * dt
        return mag * jnp.cos(ang), mag * jnp.sin(ang)

    ab_re, ab_im = disc(lre_ref[...], lim_ref[...], ldt_ref[...])
    are_ref[...] = ab_re
    aim_ref[...] = ab_im
    lre, lim = lre16_ref[...], lim16_ref[...]
    ab_re, ab_im = disc(lre, lim, ldt16_ref[...])
    nr, ni = ab_re - 1.0, ab_im
    den = lre * lre + lim * lim
    f_re = (nr * lre + ni * lim) / den
    f_im = (ni * lre - nr * lim) / den
    b_re, b_im = bre_ref[...], bim_ref[...]
    bb_re = f_re * b_re - f_im * b_im
    bb_im = f_re * b_im + f_im * b_re
    kw = bmre_ref.shape[1]
    for kb in range(bmre_ref.shape[0]):
        rows = slice(kb * kw, (kb + 1) * kw)
        bmre_ref[kb] = block_diag(bb_re[rows]).astype(bmre_ref.dtype)
        bmim_ref[kb] = block_diag(bb_im[rows]).astype(bmim_ref.dtype)
        cmre_ref[kb] = block_diag(cre_ref[rows, :]).T.astype(cmre_ref.dtype)
        cmim_ref[kb] = block_diag(cim_ref[rows, :]).T.astype(cmim_ref.dtype)


def _s5_prep(lam_re, lam_im, log_dt, b_re, b_im, c_re, c_im, grp_per_blk):
    g, p = lam_re.shape
    ch = b_re.shape[-1]
    rep = lambda a: jnp.repeat(a, ch, axis=0)
    ldt = log_dt.reshape(g, 1)
    bt = lambda b: jnp.swapaxes(b, 1, 2).reshape(g * ch, p)
    args = (lam_re, lam_im, ldt, rep(lam_re), rep(lam_im), rep(ldt), bt(b_re), bt(b_im),
            c_re.reshape(g * ch, p), c_im.reshape(g * ch, p))
    n_blk, kw, cw = g // grp_per_blk, grp_per_blk * ch, grp_per_blk * p
    shapes = [(g, p)] * 2 + [(n_blk, kw, cw)] * 2 + [(n_blk, cw, kw)] * 2
    dtypes = [F32] * 2 + [BF16] * 4
    return pl.pallas_call(
        functools.partial(_s5_prep_kernel, ch=ch),
        in_specs=[_full(a.shape) for a in args],
        out_specs=[_full(s) for s in shapes],
        out_shape=[jax.ShapeDtypeStruct(s, t) for s, t in zip(shapes, dtypes)],
        name="s5_prep")(*args)


def _s5_kernel(x_ref, nw_ref, wu_ref, bre_ref, bim_ref, cre_ref, cim_ref, d_ref, gw_ref, gb_ref,
               are_ref, aim_ref, s0re_ref, s0im_ref,
               y_ref, sre_ref, sim_ref, bure_s, buim_s, stre_s, stim_s, *, tb, tl):
    @pl.when(pl.program_id(0) == 0)
    def _():
        stre_s[...] = s0re_ref[...]
        stim_s[...] = s0im_ref[...]

    x = jnp.concatenate([x_ref[:, t, :] for t in range(tl)], axis=0)
    u = jnp.dot(_rmsnorm(x, nw_ref[...]).astype(BF16), wu_ref[...], preferred_element_type=F32)
    ub = u.astype(BF16)

    n_kb, kw, cw = bre_ref.shape
    for kb in range(n_kb):
        uk = ub[:, kb * kw:(kb + 1) * kw]
        bure_s[:, kb * cw:(kb + 1) * cw] = jnp.dot(uk, bre_ref[kb], preferred_element_type=F32)
        buim_s[:, kb * cw:(kb + 1) * cw] = jnp.dot(uk, bim_ref[kb], preferred_element_type=F32)

    n_state = are_ref.shape[1]
    sc = min(S5_SCAN_COLS, n_state)
    for c0 in range(0, n_state, sc):
        cs = slice(c0, c0 + sc)
        a_re = jnp.broadcast_to(are_ref[:, cs], (tb, sc))
        a_im = jnp.broadcast_to(aim_ref[:, cs], (tb, sc))

        def step(t, carry, cs=cs, a_re=a_re, a_im=a_im):
            s_re, s_im = carry
            rows = pl.ds(pl.multiple_of(t * tb, tb), tb)
            n_re = a_re * s_re - a_im * s_im + bure_s[rows, cs]
            n_im = a_re * s_im + a_im * s_re + buim_s[rows, cs]
            bure_s[rows, cs] = n_re
            buim_s[rows, cs] = n_im
            return n_re, n_im

        s_re, s_im = lax.fori_loop(0, tl, step, (stre_s[:, cs], stim_s[:, cs]), unroll=True)
        stre_s[:, cs] = s_re
        stim_s[:, cs] = s_im

    n_ob, ckw, ocw = cre_ref.shape
    ys = []
    for ob in range(n_ob):
        ks = slice(ob * ckw, (ob + 1) * ckw)
        ys.append(_dot(bure_s[:, ks], cre_ref[ob]) - _dot(buim_s[:, ks], cim_ref[ob]))
    y = jnp.concatenate(ys, axis=1) + d_ref[...] * u
    y = jax.nn.gelu(y)
    out = y * jax.nn.sigmoid(_dot(y, gw_ref[...]) + gb_ref[...])
    for t in range(tl):
        y_ref[:, t, :] = out[t * tb:(t + 1) * tb, :]
    sre_ref[...] = stre_s[...]
    sim_ref[...] = stim_s[...]


def _s5_mixer(x3, nw, w_in_bf, bmat_re, bmat_im, cmat_re, cmat_im, d_row, glu_w_bf, glu_b, a_re, a_im,
              s0_re, s0_im, tl):
    b, l, d = x3.shape
    width = d_row.shape[1]
    n_state = a_re.shape[1]
    rows = b * tl
    args = (x3, nw, w_in_bf, bmat_re, bmat_im, cmat_re, cmat_im, d_row, glu_w_bf, glu_b,
            a_re, a_im, s0_re, s0_im)
    in_specs = ([pl.BlockSpec((b, tl, d), lambda i: (0, i, 0)), _full(nw.shape),
                 pl.BlockSpec((d, width), lambda i: (0, 0))] + [_full(a.shape) for a in args[3:]])
    y, s_re, s_im = pl.pallas_call(
        functools.partial(_s5_kernel, tb=b, tl=tl),
        grid=(l // tl,), in_specs=in_specs,
        out_specs=[pl.BlockSpec((b, tl, width), lambda i: (0, i, 0)), _full((b, n_state)), _full((b, n_state))],
        out_shape=[jax.ShapeDtypeStruct((b, l, width), F32),
                   jax.ShapeDtypeStruct((b, n_state), F32), jax.ShapeDtypeStruct((b, n_state), F32)],
        scratch_shapes=[pltpu.VMEM((rows, n_state), F32), pltpu.VMEM((rows, n_state), F32),
                        pltpu.VMEM((b, n_state), F32), pltpu.VMEM((b, n_state), F32)],
        compiler_params=_cparams(1), name="s5_mixer")(*args)
    return y.reshape(b * l, width), s_re, s_im


def _rope_kernel(pos_ref, inv_ref, cc_ref, ss_ref, *t_refs, t_chunk):
    ang = pos_ref[...] * inv_ref[...]
    lane = lax.broadcasted_iota(jnp.int32, ang.shape, 1)
    cc = jnp.cos(ang)
    ss = jnp.where(lane < ang.shape[1] // 2, -jnp.sin(ang), jnp.sin(ang))
    cc_ref[...] = cc
    ss_ref[...] = ss
    if t_chunk:
        cct_ref, sst_ref = t_refs
        for i in range(cct_ref.shape[0]):
            cct_ref[i] = cc[i * t_chunk:(i + 1) * t_chunk].T
            sst_ref[i] = ss[i * t_chunk:(i + 1) * t_chunk].T


def _rope_tables(pos, dk, t_chunk=0):
    n = pos.shape[0]
    inv = 1.0 / (ROPE_BASE ** jnp.linspace(0.0, 1.0, dk // 2, dtype=F32))
    inv = jnp.concatenate([inv, inv]).reshape(1, dk)
    shapes = [(n, dk)] * 2 + ([(n // t_chunk, dk, t_chunk)] * 2 if t_chunk else [])
    return pl.pallas_call(
        functools.partial(_rope_kernel, t_chunk=t_chunk),
        in_specs=[_full((n, 1)), _full((1, dk))],
        out_specs=[_full(s) for s in shapes],
        out_shape=[jax.ShapeDtypeStruct(s, F32) for s in shapes],
        name="rope_tables")(pos.astype(F32).reshape(n, 1), inv)


def _rotate(t, cc, ss):
    return t * cc + pltpu.roll(t, t.shape[1] // 2, 1) * ss


def _rotate_t(t, cc, ss):
    half = t.shape[0] // 2
    return t * cc + jnp.concatenate([t[half:], t[:half]], axis=0) * ss


def _ret_consts(rows, c):
    f32 = np.float32
    lg = np.log(f32(1.0) - f32(2.0) ** (f32(-5.0) - np.arange(RET_HEADS, dtype=f32)))
    r = np.arange(rows)
    idx, blk = r % c, r // c
    diff = idx[:, None] - idx[None, :]
    same = blk[:, None] == blk[None, :]
    decay = np.where((same & (diff >= 0))[None],
                     np.exp(np.maximum(diff, 0)[None].astype(f32) * lg[:, None, None]), f32(0.0))
    wide = lambda v: np.broadcast_to(v[:, :, None], (RET_HEADS, rows, LANES))
    q_dec = wide(np.exp((idx + 1).astype(f32)[None, :] * lg[:, None]))
    k_dec = wide(np.exp((c - 1 - idx).astype(f32)[None, :] * lg[:, None]))
    chunk_dec = np.broadcast_to(np.exp(f32(c) * lg)[:, None, None], (RET_HEADS, SUBLANES, LANES))
    return tuple(jnp.asarray(np.ascontiguousarray(a), F32) for a in (decay, q_dec, k_dec, chunk_dec))


def _ret_gate(o, g):
    o = o * lax.rsqrt(jnp.mean(o * o, axis=-1, keepdims=True) + NORM_EPS)
    return jax.nn.silu(g) * o


def _ret_long_kernel(q_ref, kt_ref, v_ref, g_ref, cc_ref, ss_ref, cct_ref, sst_ref, dec_ref, qd_ref, kdt_ref,
                     cd_ref, s0_ref, y_ref, so_ref, *, c, hp):
    dk = q_ref.shape[1] // hp
    scale = dk ** -0.5

    def chunk(i, states):
        rows = pl.ds(pl.multiple_of(i * c, c), c)
        cc, ss = cc_ref[rows, :], ss_ref[rows, :]
        cct, sst = cct_ref[i], sst_ref[i]
        new_states = []
        for j in range(hp):
            cols = slice(j * dk, (j + 1) * dk)
            q = _rotate(q_ref[rows, cols], cc, ss)
            kt = _rotate_t(kt_ref[i, cols, :], cct, sst) * scale
            v = v_ref[rows, cols]
            scores = _dot(q, kt) * dec_ref[j]
            o = _dot(scores, v) + _dot(q * qd_ref[j], states[j])
            new_states.append(cd_ref[j, 0:1, :] * states[j] + _dot(kt * kdt_ref[j], v))
            y_ref[rows, cols] = _ret_gate(o, g_ref[rows, cols]).astype(y_ref.dtype)
        return tuple(new_states)

    states = lax.fori_loop(0, q_ref.shape[0] // c, chunk, tuple(s0_ref[0, j] for j in range(hp)), unroll=4)
    for j in range(hp):
        so_ref[0, j] = states[j]


def _short_state_pass(qd, kt, v, s0_ref, so_ref, hd, decay_of, l):
    rows, dk = v.shape
    row = lax.broadcasted_iota(jnp.int32, (rows, dk), 0)
    v_cols = jnp.concatenate([jnp.where(row // l == b, v, 0.0).astype(BF16) for b in range(rows // l)], axis=1)
    kv = jnp.dot(kt.astype(BF16), v_cols, preferred_element_type=F32)
    win = 16
    per = win // l
    wrow = lax.broadcasted_iota(jnp.int32, (win, dk), 0)
    outs = []
    for w in range(rows // win):
        qw = qd[w * win:(w + 1) * win].astype(BF16)
        oi = jnp.zeros((win, dk), F32)
        for j in range(per):
            b = w * per + j
            s = s0_ref[b, hd]
            oi = jnp.where(wrow // l == j, jnp.dot(qw, s.astype(BF16), preferred_element_type=F32), oi)
            so_ref[b, hd] = decay_of(b) * s + kv[:, b * dk:(b + 1) * dk]
        outs.append(oi)
    return jnp.concatenate(outs, axis=0)


def _ret_short_kernel(q_ref, k_ref, v_ref, g_ref, cc_ref, ss_ref, dec_ref, qd_ref, kd_ref, cd_ref, s0_ref,
                      y_ref, so_ref, *, l, hp):
    dk = q_ref.shape[1] // hp
    scale = dk ** -0.5
    cc, ss = cc_ref[...], ss_ref[...]
    for hd in range(hp):
        cols = slice(hd * dk, (hd + 1) * dk)
        q = _rotate(q_ref[:, cols], cc, ss)
        k = _rotate(k_ref[:, cols], cc, ss) * scale
        v = v_ref[:, cols]
        o_intra = _dot(_dot_nt(q, k) * dec_ref[hd], v)
        cd = cd_ref[hd, 0:1, :]
        o_inter = _short_state_pass(q * qd_ref[hd], (k * kd_ref[hd]).T, v, s0_ref, so_ref, hd, lambda b: cd, l)
        y_ref[:, cols] = _ret_gate(o_intra + o_inter, g_ref[:, cols]).astype(y_ref.dtype)


def _retention(x, nw, w_in_bf, w_kt_bf, col0, pos, s0, n_seq, l):
    n = x.shape[0]
    dk = s0.shape[-1]
    width = RET_HEADS * dk
    out_shape = [jax.ShapeDtypeStruct((n, width), _mix_dtype(l)), jax.ShapeDtypeStruct(s0.shape, F32)]
    cspec = lambda a, hp: pl.BlockSpec((hp,) + a.shape[1:], lambda b, h: (h, 0, 0))
    if l % RET_CHUNK == 0:
        c, hp = RET_CHUNK, RET_HEADS_PER_STEP
        n_hb = RET_HEADS // hp
        z, kt = _proj(x, nw, w_in_bf, ((col0, width), (col0 + 2 * width, 2 * width)), wt_bf=w_kt_bf, t_chunk=c)
        cc, ss, cct, sst = _rope_tables(pos, dk, t_chunk=c)
        decay, q_dec, k_dec, chunk_dec = _ret_consts(c, c)
        k_dec_t = jnp.swapaxes(k_dec, 1, 2)
        zspec = lambda field: pl.BlockSpec((l, hp * dk), lambda b, h: (b, field * n_hb + h))
        sspec = pl.BlockSpec((1, hp, dk, dk), lambda b, h: (b, h, 0, 0))
        return pl.pallas_call(
            functools.partial(_ret_long_kernel, c=c, hp=hp), grid=(n_seq, n_hb),
            in_specs=[zspec(0), pl.BlockSpec((l // c, hp * dk, c), lambda b, h: (b, h, 0)), zspec(1), zspec(2),
                      _full(cc.shape), _full(ss.shape), _full(cct.shape), _full(sst.shape),
                      cspec(decay, hp), cspec(q_dec, hp), cspec(k_dec_t, hp), cspec(chunk_dec, hp), sspec],
            out_specs=[pl.BlockSpec((l, hp * dk), lambda b, h: (b, h)), sspec],
            out_shape=out_shape, compiler_params=_cparams(2), name="retention")(
                z, kt, z, z, cc, ss, cct, sst, decay, q_dec, k_dec_t, chunk_dec, s0)
    rows, hp = LANES, RET_SHORT_HEADS_PER_STEP
    per_blk = rows // l
    n_hb = RET_HEADS // hp
    z = _proj(x, nw, w_in_bf, segments=((col0, 4 * width),))
    cc, ss = _rope_tables(pos, dk)
    decay, q_dec, k_dec, chunk_dec = _ret_consts(rows, l)
    zspec = lambda field: pl.BlockSpec((rows, hp * dk), lambda b, h: (b, field * n_hb + h))
    tspec = pl.BlockSpec((rows, dk), lambda b, h: (b, 0))
    sspec = pl.BlockSpec((per_blk, hp, dk, dk), lambda b, h: (b, h, 0, 0))
    return pl.pallas_call(
        functools.partial(_ret_short_kernel, l=l, hp=hp), grid=(n_seq // per_blk, n_hb),
        in_specs=[zspec(0), zspec(1), zspec(2), zspec(3), tspec, tspec,
                  cspec(decay, hp), cspec(q_dec, hp), cspec(k_dec, hp), cspec(chunk_dec, hp), sspec],
        out_specs=[pl.BlockSpec((rows, hp * dk), lambda b, h: (b, h)), sspec],
        out_shape=out_shape,
        compiler_params=_cparams(2), name="retention")(z, z, z, z, cc, ss, decay, q_dec, k_dec, chunk_dec, s0)


def _cumsum_rows(x, period):
    row = lax.broadcasted_iota(jnp.int32, x.shape, 0)
    s = 1
    while s < period:
        x = x + jnp.where(row % period >= s, pltpu.roll(x, s, 0), 0.0)
        s *= 2
    return x


def _row_of_block(x, period, offset):
    n, w = x.shape
    x3 = x.reshape(n // period, period, w)
    return jnp.broadcast_to(x3[:, offset:offset + 1, :], x3.shape).reshape(n, w)


def _hgrn_log2_forget(fl, lg, layer):
    e = jnp.exp(lg - jnp.max(lg, axis=0, keepdims=True))
    soft = e / jnp.sum(e, axis=0, keepdims=True)
    cum = soft[0:1]
    for i in range(1, layer + 1):
        cum = cum + soft[i:i + 1]
    lb = cum - soft[0:1]
    return jnp.log2(lb + (1.0 - lb) * _sigmoid(fl))


def _hgrn_out(o, gate, nw):
    o = o * lax.rsqrt(jnp.mean(o * o, axis=-1, keepdims=True) + NORM_EPS) * nw
    return o * gate


def _hgrn_levels(c):
    levels, half = [], HG_BLOCK
    while half < c:
        levels.append(half)
        half *= 2
    return levels


def _hgrn_long_kernel(q_ref, f_ref, v_ref, g_ref, nw_ref, s0_ref, y_ref, so_ref, *, c, hp):
    dk = q_ref.shape[1] // hp
    nw = nw_ref[...]
    nv = c // SUBLANES
    bv = HG_BLOCK // SUBLANES
    levels = _hgrn_levels(c)
    row = lax.broadcasted_iota(jnp.int32, (c, c), 0)
    col = lax.broadcasted_iota(jnp.int32, (c, c), 1)
    lvl = jnp.where((row // HG_BLOCK == col // HG_BLOCK) & (col <= row), 0, -1)
    for li, half in enumerate(levels, 1):
        lvl = jnp.where((row // (2 * half) == col // (2 * half)) & (row % (2 * half) >= half)
                        & (col % (2 * half) < half), li, lvl)
    sub = lax.broadcasted_iota(jnp.int32, (SUBLANES, dk), 0)
    zeros = jnp.zeros((SUBLANES, dk), F32)
    cat = lambda slabs: jnp.concatenate(slabs, axis=0)

    def one_head(rows, cols, s):
        q, v = q_ref[rows, cols], v_ref[rows, cols]
        lf = f_ref[rows, cols]
        k = 1.0 - jnp.exp2(lf)
        slabs = lambda a: [a[j * SUBLANES:(j + 1) * SUBLANES, :] for j in range(nv)]
        qs, ks = slabs(q), slabs(k)
        loc, run = [], [jnp.zeros((1, dk), F32)]
        for x in slabs(lf):
            step = 1
            while step < SUBLANES:
                x = x + jnp.where(sub >= step, pltpu.roll(x, step, 0), 0.0)
                step *= 2
            loc.append(x)
            run.append(run[-1] + x[SUBLANES - 1:, :])
        d0 = [loc[j] + (run[j] - run[j - j % bv]) if j % bv else loc[j] for j in range(nv)]
        scores = jnp.where(lvl == 0, _dot_nt(cat([qs[j] * jnp.exp2(d0[j]) for j in range(nv)]),
                                             cat([ks[j] * jnp.exp2(-d0[j]) for j in range(nv)])), 0.0)
        for li, half in enumerate(levels, 1):
            hv = half // SUBLANES
            q_side, k_side = [], []
            for j in range(nv):
                anchor = run[j - j % (2 * hv) + hv]
                if j % (2 * hv) >= hv:
                    q_side.append(qs[j] * jnp.exp2(loc[j] + (run[j] - anchor)))
                    k_side.append(zeros)
                else:
                    q_side.append(zeros)
                    k_side.append(ks[j] * jnp.exp2((anchor - run[j]) - loc[j]))
            scores = jnp.where(lvl == li, _dot_nt(cat(q_side), cat(k_side)), scores)
        o = _dot(scores, v) + _dot(cat([qs[j] * jnp.exp2(loc[j] + run[j]) for j in range(nv)]), s)
        k_hat = cat([ks[j] * jnp.exp2((run[nv] - run[j]) - loc[j]) for j in range(nv)])
        d_col = jnp.exp2(jnp.broadcast_to(run[nv], (dk, dk)).T)
        y_ref[rows, cols] = _hgrn_out(o, g_ref[rows, cols], nw).astype(y_ref.dtype)
        return d_col * s + _dot(k_hat.T, v)

    def chunk(i, states):
        rows = pl.ds(pl.multiple_of(i * c, c), c)
        return tuple(one_head(rows, slice(hd * dk, (hd + 1) * dk), states[hd]) for hd in range(hp))

    states = lax.fori_loop(0, q_ref.shape[0] // c, chunk, tuple(s0_ref[0, j] for j in range(hp)), unroll=4)
    for j in range(hp):
        so_ref[0, j] = states[j]


def _hgrn_short_kernel(q_ref, f_ref, v_ref, g_ref, nw_ref, s0_ref, y_ref, so_ref, *, l, hp):
    rows = q_ref.shape[0]
    dk = q_ref.shape[1] // hp
    row = lax.broadcasted_iota(jnp.int32, (rows, rows), 0)
    col = lax.broadcasted_iota(jnp.int32, (rows, rows), 1)
    causal = (row // l == col // l) & (col <= row)
    for hd in range(hp):
        cols = slice(hd * dk, (hd + 1) * dk)
        q, v = q_ref[:, cols], v_ref[:, cols]
        lf = f_ref[:, cols]
        k = 1.0 - jnp.exp2(lf)
        b = _cumsum_rows(lf, l)
        qe = q * jnp.exp2(b)
        o_intra = _dot(jnp.where(causal, _dot_nt(qe, k * jnp.exp2(-b)), 0.0), v)
        b_last = _row_of_block(b, l, l - 1)
        dect = jnp.exp2(b_last).T
        o_inter = _short_state_pass(qe, (k * jnp.exp2(b_last - b)).T, v, s0_ref, so_ref, hd,
                                    lambda s, dect=dect: dect[:, s * l:s * l + 1], l)
        y_ref[:, cols] = _hgrn_out(o_intra + o_inter, g_ref[:, cols], nw_ref[...]).astype(y_ref.dtype)


def _hgrn(z, norm_w, s0, n_seq, l):
    n = z.shape[0]
    dk = s0.shape[-1]
    if l % HG_CHUNK == 0:
        rows, n_blk, hp = l, n_seq, HG_HEADS_PER_STEP
        kern = functools.partial(_hgrn_long_kernel, c=HG_CHUNK, hp=hp)
        sspec = pl.BlockSpec((1, hp, dk, dk), lambda b, h: (b, h, 0, 0))
    else:
        assert HG_BLOCK % l == 0
        rows, hp = LANES, HG_SHORT_HEADS_PER_STEP
        per_blk = rows // l
        n_blk = n_seq // per_blk
        kern = functools.partial(_hgrn_short_kernel, l=l, hp=hp)
        sspec = pl.BlockSpec((per_blk, hp, dk, dk), lambda b, h: (b, h, 0, 0))
    n_hb = HG_HEADS // hp
    zspec = lambda rows, field: pl.BlockSpec((rows, hp * dk), lambda b, h: (b, field * n_hb + h))
    y, s_new = pl.pallas_call(
        kern, grid=(n_blk, n_hb),
        in_specs=[zspec(rows, 2), zspec(rows, 0), zspec(rows, 3), zspec(rows, 1), _full((1, dk)), sspec],
        out_specs=[pl.BlockSpec((rows, hp * dk), lambda b, h: (b, h)), sspec],
        out_shape=[jax.ShapeDtypeStruct((n, HG_HEADS * dk), _mix_dtype(l)), jax.ShapeDtypeStruct(s0.shape, F32)],
        compiler_params=_cparams(2), name="hgrn")(z, z, z, z, norm_w, s0)
    return y, s_new


def _ffn_kernel(*refs, n_parts, long_mode, final_norm, l):
    x_ref, parts = refs[0], refs[1:1 + n_parts]
    (wo_ref, nw_ref, wg_ref, wu_ref, cw_ref, cb_ref, wd_ref, fnw_ref, buf_ref, o_ref, bufo_ref,
     act_s) = refs[1 + n_parts:13 + n_parts]
    d_ff = wg_ref.shape[1]
    if long_mode:
        tail_s = refs[13 + n_parts]
        tm = x_ref.shape[1]
        rc = min(ROW_TILE, tm)
        load = lambda ref, r0: ref[0, r0:r0 + rc, :]

        @pl.when(pl.program_id(1) == 0)
        def _():
            tail_s[...] = buf_ref[0]
    else:
        nb = x_ref.shape[0]
        tm = rc = l * nb
        load = lambda ref, r0: jnp.concatenate([ref[:, t, :] for t in range(l)], axis=0)

    xs = []
    for r0 in range(0, tm, rc):
        mix = jnp.concatenate([load(p, r0).astype(BF16) for p in parts], axis=1)
        x = load(x_ref, r0) + jnp.dot(mix, wo_ref[...], preferred_element_type=F32)
        xs.append(x)
        h = _rmsnorm(x, nw_ref[...]).astype(BF16)
        for c0 in range(0, d_ff, FF_CHUNK):
            cs = slice(c0, c0 + FF_CHUNK)
            g = jnp.dot(h, wg_ref[:, cs], preferred_element_type=F32)
            up = jnp.dot(h, wu_ref[:, cs], preferred_element_type=F32)
            if long_mode:
                row = lax.broadcasted_iota(jnp.int32, g.shape, 0)
                p1 = jnp.where(row < 1, tail_s[1:2, cs], pltpu.roll(g, 1, 0))
                p2 = jnp.where(row < 2, jnp.where(row == 0, tail_s[0:1, cs], tail_s[1:2, cs]),
                               pltpu.roll(g, 2, 0))
                tail_s[:, cs] = g[rc - (CONV_W - 1):, :]
            else:
                b0, b1 = buf_ref[:, 0, cs], buf_ref[:, 1, cs]
                p1 = jnp.concatenate([b1, g[:tm - nb]], axis=0)
                p2 = jnp.concatenate([b0, b1, g[:tm - 2 * nb]], axis=0)
                bufo_ref[:, 0, cs] = g[tm - 2 * nb:tm - nb]
                bufo_ref[:, 1, cs] = g[tm - nb:]
            conv = cb_ref[:, cs] + cw_ref[0:1, cs] * p2
            conv = conv + cw_ref[1:2, cs] * p1
            conv = conv + cw_ref[2:3, cs] * g
            act_s[r0:r0 + rc, cs] = (jax.nn.silu(conv) * up).astype(BF16)
    y = jnp.concatenate(xs, axis=0) + jnp.dot(act_s[...], wd_ref[...], preferred_element_type=F32)
    if final_norm:
        y = _rmsnorm(y, fnw_ref[...])
    if long_mode:
        o_ref[0] = y
        bufo_ref[0] = tail_s[...]
    else:
        for t in range(l):
            o_ref[:, t, :] = y[t * nb:(t + 1) * nb]


def _ffn(x, parts, layer, wo, nw, wg, wu, conv_w, conv_b, wd, fnw, buf, n_seq, l, final_norm):
    assert CONV_W == 3 and l >= CONV_W - 1
    n, d = x.shape
    d_ff = wg.shape[2]
    weights = (wo, nw, wg, wu, conv_w, conv_b, wd, fnw)
    stacked = (wg, wu, wd)

    def wspec(w, single_buffer):
        kw = dict(pipeline_mode=pl.Buffered(1)) if single_buffer else {}
        if any(w is s for s in stacked):
            return pl.BlockSpec((None,) + w.shape[1:], lambda *_: (layer, 0, 0), **kw)
        return pl.BlockSpec(w.shape, lambda *_: (0,) * w.ndim, **kw)
    kern = functools.partial(_ffn_kernel, n_parts=len(parts), final_norm=final_norm, l=l)
    buf_shape = jax.ShapeDtypeStruct((n_seq, CONV_W - 1, d_ff), F32)
    rows3 = lambda a: a.reshape(n_seq, l, a.shape[1])
    if l % ROW_TILE == 0:
        tm = FFN_ROW_TILE
        assert l % tm == 0
        rspec = lambda w: pl.BlockSpec((1, tm, w), lambda b, i: (b, i, 0))
        bspec = pl.BlockSpec((1, CONV_W - 1, d_ff), lambda b, i: (b, 0, 0))
        y, buf_new = pl.pallas_call(
            functools.partial(kern, long_mode=True),
            grid=(n_seq, l // tm),
            in_specs=([rspec(d)] + [rspec(p.shape[1]) for p in parts]
                      + [wspec(w, True) for w in weights]
                      + [pl.BlockSpec((None, 1, CONV_W - 1, d_ff), lambda b, i: (layer, b, 0, 0))]),
            out_specs=[rspec(d), bspec],
            out_shape=[jax.ShapeDtypeStruct((n_seq, l, d), F32), buf_shape],
            scratch_shapes=[pltpu.VMEM((tm, d_ff), BF16), pltpu.VMEM((CONV_W - 1, d_ff), F32)],
            compiler_params=_cparams(2), name="ffn")(rows3(x), *map(rows3, parts), *weights, buf)
        return y.reshape(n, d), buf_new
    y, buf_new = pl.pallas_call(
        functools.partial(kern, long_mode=False),
        grid=(1,),
        in_specs=([_full((n_seq, l, d))] + [_full((n_seq, l, p.shape[1])) for p in parts]
                  + [wspec(w, False) for w in weights]
                  + [pl.BlockSpec((None,) + buf.shape[1:], lambda *_: (layer, 0, 0, 0))]),
        out_specs=[_full((n_seq, l, d)), _full(buf.shape[1:])],
        out_shape=[jax.ShapeDtypeStruct((n_seq, l, d), F32), buf_shape],
        scratch_shapes=[pltpu.VMEM((n, d_ff), BF16)],
        compiler_params=_cparams(1), name="ffn")(
            rows3(x), *map(rows3, parts), *weights, buf)
    return y.reshape(n, d), buf_new


def _trunk(x3, pos, s5_re, s5_im, ret, hg, conv, p, s5_tl):
    b, l, d = x3.shape
    n = b * l
    x = x3.reshape(n, d)
    depth = p['norm_mix'].shape[0]
    new_re, new_im, new_ret, new_hg, new_conv = [], [], [], [], []
    for layer in range(depth):
        j = layer // 2
        nw = p['norm_mix'][layer].reshape(1, d)
        if layer % 2 == 0:
            s5w = p['s5'][j]
            width = s5w['d'].shape[1]
            y_s5, r, im = _s5_mixer(x.reshape(b, l, d), nw, p['w_in_ab'][j], s5w['bmat_re'], s5w['bmat_im'],
                                    s5w['cmat_re'], s5w['cmat_im'], s5w['d'], s5w['glu_w'], s5w['glu_b'],
                                    s5w['a_re'], s5w['a_im'], s5_re[j].reshape(b, -1), s5_im[j].reshape(b, -1),
                                    s5_tl)
            y_ret, st = _retention(x, nw, p['w_in_ab'][j], p['w_kt'][j], width, pos, ret[j], b, l)
            parts, w_out = [y_s5, y_ret], p['w_out_ab'][j]
            new_re.append(r.reshape(s5_re.shape[1:]))
            new_im.append(im.reshape(s5_im.shape[1:]))
            new_ret.append(st)
        else:
            hw = p['hg_lb_logits'].shape[1]
            z = _proj(x, nw, p['w_in_c'][j],
                      segments=((hw, hw, "log2_forget"), (3 * hw, hw, "silu"), (0, hw), (2 * hw, hw)),
                      gate_logits=p['hg_lb_logits'], gate_layer=layer)
            y_hg, st = _hgrn(z, p['hg_norm_w'][j].reshape(1, -1), hg[j], b, l)
            parts, w_out = [y_hg], p['w_out_c'][j]
            new_hg.append(st)
        x, buf = _ffn(x, parts, layer, w_out, p['norm_ffn'][layer].reshape(1, d), p['ffn_w_gate'],
                      p['ffn_w_up'], p['ffn_conv_w'][layer], p['ffn_conv_b'][layer].reshape(1, -1),
                      p['ffn_w_down'], p['norm_final'].reshape(1, d), conv, b, l,
                      final_norm=(layer == depth - 1))
        new_conv.append(buf)
    return (x.reshape(b, l, d), jnp.stack(new_re), jnp.stack(new_im), jnp.stack(new_ret),
            jnp.stack(new_hg), jnp.stack(new_conv))


def kernel(x_prompt, x_sample, state_s5_re, state_s5_im, state_ret, state_hgrn, state_ffn_conv, pos_sample, norm_mix, norm_ffn, norm_final, w_in_ab, s5_lam_re, s5_lam_im, s5_log_dt, s5_b_re, s5_b_im, s5_c_re, s5_c_im, s5_d, s5_glu_w, s5_glu_b, w_out_ab, w_in_c, hg_lb_logits, hg_norm_w, w_out_c, ffn_w_gate, ffn_w_up, ffn_conv_w, ffn_conv_b, ffn_w_down):
    n_ab, n_grp, n_st = s5_lam_re.shape
    ch = s5_b_re.shape[-1]
    width = n_grp * ch
    ret_width = state_ret.shape[2] * state_ret.shape[3]
    grp_per_blk = MXU_DIM // ch
    s5 = []
    for j in range(n_ab):
        a_re, a_im, bmat_re, bmat_im, cmat_re, cmat_im = _s5_prep(
            s5_lam_re[j], s5_lam_im[j], s5_log_dt[j], s5_b_re[j], s5_b_im[j], s5_c_re[j], s5_c_im[j], grp_per_blk)
        s5.append(dict(bmat_re=bmat_re, bmat_im=bmat_im, cmat_re=cmat_re, cmat_im=cmat_im,
                       d=s5_d[j].reshape(1, width), glu_w=s5_glu_w[j].astype(BF16),
                       glu_b=s5_glu_b[j].reshape(1, width),
                       a_re=a_re.reshape(1, n_grp * n_st), a_im=a_im.reshape(1, n_grp * n_st)))
    per_layer_bf16 = lambda w: [w[i].astype(BF16) for i in range(w.shape[0])]
    p = dict(norm_mix=norm_mix, norm_ffn=norm_ffn, norm_final=norm_final, s5=s5,
             w_in_ab=per_layer_bf16(w_in_ab), w_out_ab=per_layer_bf16(w_out_ab),
             w_kt=[w_in_ab[j][:, width + ret_width:width + 2 * ret_width].T.astype(BF16) for j in range(n_ab)],
             w_in_c=per_layer_bf16(w_in_c), hg_lb_logits=hg_lb_logits, hg_norm_w=hg_norm_w,
             w_out_c=per_layer_bf16(w_out_c), ffn_w_gate=ffn_w_gate.astype(BF16),
             ffn_w_up=ffn_w_up.astype(BF16), ffn_conv_w=ffn_conv_w, ffn_conv_b=ffn_conv_b,
             ffn_w_down=ffn_w_down.astype(BF16))

    bp, lp, _ = x_prompt.shape
    z_s5 = jnp.zeros((n_ab, bp) + state_s5_re.shape[2:], F32)
    z_ret = jnp.zeros((n_ab, bp) + state_ret.shape[2:], F32)
    z_hg = jnp.zeros((state_hgrn.shape[0], bp) + state_hgrn.shape[2:], F32)
    z_conv = jnp.zeros((norm_mix.shape[0], bp) + state_ffn_conv.shape[2:], F32)
    outs_p = _trunk(x_prompt, jnp.arange(lp, dtype=jnp.int32), z_s5, z_s5, z_ret, z_hg, z_conv, p,
                    s5_tl=S5_ROW_TILE // bp)
    ls = x_sample.shape[1]
    pos_s = (pos_sample[:, None] + jnp.arange(ls, dtype=jnp.int32)[None, :]).reshape(-1)
    outs_s = _trunk(x_sample, pos_s, state_s5_re, state_s5_im, state_ret, state_hgrn, state_ffn_conv, p,
                    s5_tl=ls)
    return (outs_p[0], outs_s[0]) + outs_p[1:] + outs_s[1:]
```

```python
import functools

import jax
import jax.numpy as jnp
import numpy as np
from jax import lax
from jax.experimental import pallas as pl
from jax.experimental.pallas import tpu as pltpu

F32 = jnp.float32
BF16 = jnp.bfloat16

NORM_EPS = 1e-6
ROPE_BASE = 10000.0
RET_HEADS = 4
RET_CHUNK = 128
HG_HEADS = 8
HG_BLOCK = 16
HG_CHUNK = 128
RET_HEADS_PER_STEP = 4
HG_HEADS_PER_STEP = 4
RET_SHORT_HEADS_PER_STEP = 4
HG_SHORT_HEADS_PER_STEP = 2
CONV_W = 3

LANES = 128
SUBLANES = 8
MXU_DIM = 256
VMEM_LIMIT_BYTES = 56 * 1024 * 1024
ROW_TILE = 512
FFN_ROW_TILE = 1024
PROJ_ROW_TILE = 1024
S5_ROW_TILE = 1024
FF_CHUNK = 256
S5_SCAN_COLS = 512


def _cparams(n_grid_dims):
    return pltpu.CompilerParams(dimension_semantics=("arbitrary",) * n_grid_dims,
                                vmem_limit_bytes=VMEM_LIMIT_BYTES)


def _dot(a, b):
    return jnp.dot(a.astype(BF16), b.astype(BF16), preferred_element_type=F32)


def _dot_nt(a, b):
    return lax.dot_general(a.astype(BF16), b.astype(BF16), (((1,), (1,)), ((), ())),
                           preferred_element_type=F32)


def _rmsnorm(x, w):
    return x * lax.rsqrt(jnp.mean(x * x, axis=-1, keepdims=True) + NORM_EPS) * w


def _sigmoid(x):
    return 0.5 * jnp.tanh(0.5 * x) + 0.5


def _full(shape):
    nd = len(shape)
    return pl.BlockSpec(shape, lambda *_: (0,) * nd)


def _mix_dtype(l):
    return BF16 if l % ROW_TILE == 0 else F32


def _resident(shape):
    nd = len(shape)
    return pl.BlockSpec(shape, lambda *_: (0,) * nd, pipeline_mode=pl.Buffered(1))


def _proj_kernel(*refs, segments, col_chunk, row_chunk, t_chunk, gate_layer):
    refs = list(refs)
    x_ref, nw_ref, w_ref = refs[:3]
    rest = refs[3:]
    lg_ref = rest.pop(0) if gate_layer is not None else None
    wt_ref = rest.pop(0) if t_chunk else None
    o_ref = rest.pop(0)
    ot_ref = rest.pop(0) if t_chunk else None
    for r in range(0, x_ref.shape[0], row_chunk):
        rows = slice(r, r + row_chunk)
        h = _rmsnorm(x_ref[rows, :], nw_ref[...]).astype(BF16)
        out0 = 0
        for col0, width, kind in segments:
            for c in range(0, width, col_chunk):
                val = jnp.dot(h, w_ref[:, col0 + c:col0 + c + col_chunk], preferred_element_type=F32)
                if kind == "silu":
                    val = val * _sigmoid(val)
                elif kind == "log2_forget":
                    val = _hgrn_log2_forget(val, lg_ref[:, c:c + col_chunk], gate_layer)
                o_ref[rows, out0 + c:out0 + c + col_chunk] = val
            out0 += width
        if t_chunk:
            zt = _dot_nt(wt_ref[...], h)
            for j in range(row_chunk // t_chunk):
                ot_ref[r // t_chunk + j] = zt[:, j * t_chunk:(j + 1) * t_chunk]


def _proj(x, nw, w_bf, segments=None, wt_bf=None, t_chunk=0, gate_logits=None, gate_layer=None):
    n, d = x.shape
    segments = tuple((tuple(s) + (None,))[:3] for s in (segments or ((0, w_bf.shape[1]),)))
    n_out = sum(width for _, width, _ in segments)
    tm = min(PROJ_ROW_TILE, n)
    in_specs = [pl.BlockSpec((tm, d), lambda i: (i, 0)), _full((1, d)), _resident(w_bf.shape)]
    out_specs = [pl.BlockSpec((tm, n_out), lambda i: (i, 0))]
    out_shape = [jax.ShapeDtypeStruct((n, n_out), F32)]
    args = [x, nw, w_bf]
    if gate_layer is not None:
        in_specs.append(_full(gate_logits.shape))
        args.append(gate_logits)
    if t_chunk:
        t_width = wt_bf.shape[0]
        in_specs.append(_resident(wt_bf.shape))
        out_specs.append(pl.BlockSpec((tm // t_chunk, t_width, t_chunk), lambda i: (i, 0, 0)))
        out_shape.append(jax.ShapeDtypeStruct((n // t_chunk, t_width, t_chunk), F32))
        args.append(wt_bf)
    outs = pl.pallas_call(
        functools.partial(_proj_kernel, segments=segments, col_chunk=2 * MXU_DIM, row_chunk=min(ROW_TILE, tm),
                          t_chunk=t_chunk, gate_layer=gate_layer),
        grid=(n // tm,), in_specs=in_specs, out_specs=out_specs, out_shape=out_shape,
        compiler_params=_cparams(1), name="proj")(*args)
    return outs if t_chunk else outs[0]


def _s5_prep_kernel(lre_ref, lim_ref, ldt_ref, lre16_ref, lim16_ref, ldt16_ref, bre_ref, bim_ref, cre_ref, cim_ref,
                    are_ref, aim_ref, bmre_ref, bmim_ref, cmre_ref, cmim_ref, *, ch):
    def block_diag(m):
        rows, p = m.shape
        tiled = jnp.concatenate([m] * (rows // ch), axis=1)
        row = lax.broadcasted_iota(jnp.int32, tiled.shape, 0)
        col = lax.broadcasted_iota(jnp.int32, tiled.shape, 1)
        return jnp.where(col // p == row // ch, tiled, 0.0)

    def disc(lre, lim, ldt):
        dt = jnp.exp(ldt)
        mag = jnp.exp(lre * dt)
        ang = lim * dt
        return mag * jnp.cos(ang), mag * jnp.sin(ang)

    ab_re, ab_im = disc(lre_ref[...], lim_ref[...], ldt_ref[...])
    are_ref[...] = ab_re
    aim_ref[...] = ab_im
    lre, lim = lre16_ref[...], lim16_ref[...]
    ab_re, ab_im = disc(lre, lim, ldt16_ref[...])
    nr, ni = ab_re - 1.0, ab_im
    den = lre * lre + lim * lim
    f_re = (nr * lre + ni * lim) / den
    f_im = (ni * lre - nr * lim) / den
    b_re, b_im = bre_ref[...], bim_ref[...]
    bb_re = f_re * b_re - f_im * b_im
    bb_im = f_re * b_im + f_im * b_re
    kw = bmre_ref.shape[1]
    for kb in range(bmre_ref.shape[0]):
        rows = slice(kb * kw, (kb + 1) * kw)
        bmre_ref[kb] = block_diag(bb_re[rows]).astype(bmre_ref.dtype)
        bmim_ref[kb] = block_diag(bb_im[rows]).astype(bmim_ref.dtype)
        cmre_ref[kb] = block_diag(cre_ref[rows, :]).T.astype(cmre_ref.dtype)
        cmim_ref[kb] = block_diag(cim_ref[rows, :]).T.astype(cmim_ref.dtype)


def _s5_prep(lam_re, lam_im, log_dt, b_re, b_im, c_re, c_im, grp_per_blk):
    g, p = lam_re.shape
    ch = b_re.shape[-1]
    rep = lambda a: jnp.repeat(a, ch, axis=0)
    ldt = log_dt.reshape(g, 1)
    bt = lambda b: jnp.swapaxes(b, 1, 2).reshape(g * ch, p)
    args = (lam_re, lam_im, ldt, rep(lam_re), rep(lam_im), rep(ldt), bt(b_re), bt(b_im),
            c_re.reshape(g * ch, p), c_im.reshape(g * ch, p))
    n_blk, kw, cw = g // grp_per_blk, grp_per_blk * ch, grp_per_blk * p
    shapes = [(g, p)] * 2 + [(n_blk, kw, cw)] * 2 + [(n_blk, cw, kw)] * 2
    dtypes = [F32] * 2 + [BF16] * 4
    return pl.pallas_call(
        functools.partial(_s5_prep_kernel, ch=ch),
        in_specs=[_full(a.shape) for a in args],
        out_specs=[_full(s) for s in shapes],
        out_shape=[jax.ShapeDtypeStruct(s, t) for s, t in zip(shapes, dtypes)],
        name="s5_prep")(*args)


def _s5_kernel(x_ref, nw_ref, wu_ref, bre_ref, bim_ref, cre_ref, cim_ref, d_ref, gw_ref, gb_ref,
               are_ref, aim_ref, s0re_ref, s0im_ref,
               y_ref, sre_ref, sim_ref, bure_s, buim_s, stre_s, stim_s, *, tb, tl):
    @pl.when(pl.program_id(0) == 0)
    def _():
        stre_s[...] = s0re_ref[...]
        stim_s[...] = s0im_ref[...]

    x = jnp.concatenate([x_ref[:, t, :] for t in range(tl)], axis=0)
    u = jnp.dot(_rmsnorm(x, nw_ref[...]).astype(BF16), wu_ref[...], preferred_element_type=F32)
    ub = u.astype(BF16)

    n_kb, kw, cw = bre_ref.shape
    for kb in range(n_kb):
        uk = ub[:, kb * kw:(kb + 1) * kw]
        bure_s[:, kb * cw:(kb + 1) * cw] = jnp.dot(uk, bre_ref[kb], preferred_element_type=F32)
        buim_s[:, kb * cw:(kb + 1) * cw] = jnp.dot(uk, bim_ref[kb], preferred_element_type=F32)

    n_state = are_ref.shape[1]
    sc = min(S5_SCAN_COLS, n_state)
    for c0 in range(0, n_state, sc):
        cs = slice(c0, c0 + sc)
        a_re = jnp.broadcast_to(are_ref[:, cs], (tb, sc))
        a_im = jnp.broadcast_to(aim_ref[:, cs], (tb, sc))

        def step(t, carry, cs=cs, a_re=a_re, a_im=a_im):
            s_re, s_im = carry
            rows = pl.ds(pl.multiple_of(t * tb, tb), tb)
            n_re = a_re * s_re - a_im * s_im + bure_s[rows, cs]
            n_im = a_re * s_im + a_im * s_re + buim_s[rows, cs]
            bure_s[rows, cs] = n_re
            buim_s[rows, cs] = n_im
            return n_re, n_im

        s_re, s_im = lax.fori_loop(0, tl, step, (stre_s[:, cs], stim_s[:, cs]), unroll=True)
        stre_s[:, cs] = s_re
        stim_s[:, cs] = s_im

    n_ob, ckw, ocw = cre_ref.shape
    ys = []
    for ob in range(n_ob):
        ks = slice(ob * ckw, (ob + 1) * ckw)
        ys.append(_dot(bure_s[:, ks], cre_ref[ob]) - _dot(buim_s[:, ks], cim_ref[ob]))
    y = jnp.concatenate(ys, axis=1) + d_ref[...] * u
    y = jax.nn.gelu(y)
    out = y * jax.nn.sigmoid(_dot(y, gw_ref[...]) + gb_ref[...])
    for t in range(tl):
        y_ref[:, t, :] = out[t * tb:(t + 1) * tb, :]
    sre_ref[...] = stre_s[...]
    sim_ref[...] = stim_s[...]


def _s5_mixer(x3, nw, w_in_bf, bmat_re, bmat_im, cmat_re, cmat_im, d_row, glu_w_bf, glu_b, a_re, a_im,
              s0_re, s0_im, tl):
    b, l, d = x3.shape
    width = d_row.shape[1]
    n_state = a_re.shape[1]
    rows = b * tl
    args = (x3, nw, w_in_bf, bmat_re, bmat_im, cmat_re, cmat_im, d_row, glu_w_bf, glu_b,
            a_re, a_im, s0_re, s0_im)
    in_specs = ([pl.BlockSpec((b, tl, d), lambda i: (0, i, 0)), _full(nw.shape),
                 pl.BlockSpec((d, width), lambda i: (0, 0))] + [_full(a.shape) for a in args[3:]])
    y, s_re, s_im = pl.pallas_call(
        functools.partial(_s5_kernel, tb=b, tl=tl),
        grid=(l // tl,), in_specs=in_specs,
        out_specs=[pl.BlockSpec((b, tl, width), lambda i: (0, i, 0)), _full((b, n_state)), _full((b, n_state))],
        out_shape=[jax.ShapeDtypeStruct((b, l, width), F32),
                   jax.ShapeDtypeStruct((b, n_state), F32), jax.ShapeDtypeStruct((b, n_state), F32)],
        scratch_shapes=[pltpu.VMEM((rows, n_state), F32), pltpu.VMEM((rows, n_state), F32),
                        pltpu.VMEM((b, n_state), F32), pltpu.VMEM((b, n_state), F32)],
        compiler_params=_cparams(1), name="s5_mixer")(*args)
    return y.reshape(b * l, width), s_re, s_im


def _rope_kernel(pos_ref, inv_ref, cc_ref, ss_ref, *t_refs, t_chunk):
    ang = pos_ref[...] * inv_ref[...]
    lane = lax.broadcasted_iota(jnp.int32, ang.shape, 1)
    cc = jnp.cos(ang)
    ss = jnp.where(lane < ang.shape[1] // 2, -jnp.sin(ang), jnp.sin(ang))
    cc_ref[...] = cc
    ss_ref[...] = ss
    if t_chunk:
        cct_ref, sst_ref = t_refs
        for i in range(cct_ref.shape[0]):
            cct_ref[i] = cc[i * t_chunk:(i + 1) * t_chunk].T
            sst_ref[i] = ss[i * t_chunk:(i + 1) * t_chunk].T


def _rope_tables(pos, dk, t_chunk=0):
    n = pos.shape[0]
    inv = 1.0 / (ROPE_BASE ** jnp.linspace(0.0, 1.0, dk // 2, dtype=F32))
    inv = jnp.concatenate([inv, inv]).reshape(1, dk)
    shapes = [(n, dk)] * 2 + ([(n // t_chunk, dk, t_chunk)] * 2 if t_chunk else [])
    return pl.pallas_call(
        functools.partial(_rope_kernel, t_chunk=t_chunk),
        in_specs=[_full((n, 1)), _full((1, dk))],
        out_specs=[_full(s) for s in shapes],
        out_shape=[jax.ShapeDtypeStruct(s, F32) for s in shapes],
        name="rope_tables")(pos.astype(F32).reshape(n, 1), inv)


def _rotate(t, cc, ss):
    return t * cc + pltpu.roll(t, t.shape[1] // 2, 1) * ss


def _rotate_t(t, cc, ss):
    half = t.shape[0] // 2
    return t * cc + jnp.concatenate([t[half:], t[:half]], axis=0) * ss


def _ret_consts(rows, c):
    f32 = np.float32
    lg = np.log(f32(1.0) - f32(2.0) ** (f32(-5.0) - np.arange(RET_HEADS, dtype=f32)))
    r = np.arange(rows)
    idx, blk = r % c, r // c
    diff = idx[:, None] - idx[None, :]
    same = blk[:, None] == blk[None, :]
    decay = np.where((same & (diff >= 0))[None],
                     np.exp(np.maximum(diff, 0)[None].astype(f32) * lg[:, None, None]), f32(0.0))
    wide = lambda v: np.broadcast_to(v[:, :, None], (RET_HEADS, rows, LANES))
    q_dec = wide(np.exp((idx + 1).astype(f32)[None, :] * lg[:, None]))
    k_dec = wide(np.exp((c - 1 - idx).astype(f32)[None, :] * lg[:, None]))
    chunk_dec = np.broadcast_to(np.exp(f32(c) * lg)[:, None, None], (RET_HEADS, SUBLANES, LANES))
    return tuple(jnp.asarray(np.ascontiguousarray(a), F32) for a in (decay, q_dec, k_dec, chunk_dec))


def _ret_gate(o, g):
    o = o * lax.rsqrt(jnp.mean(o * o, axis=-1, keepdims=True) + NORM_EPS)
    return jax.nn.silu(g) * o


def _ret_long_kernel(q_ref, kt_ref, v_ref, g_ref, cc_ref, ss_ref, cct_ref, sst_ref, dec_ref, qd_ref, kdt_ref,
                     cd_ref, s0_ref, y_ref, so_ref, *, c, hp):
    dk = q_ref.shape[1] // hp
    scale = dk ** -0.5

    def chunk(i, states):
        rows = pl.ds(pl.multiple_of(i * c, c), c)
        cc, ss = cc_ref[rows, :], ss_ref[rows, :]
        cct, sst = cct_ref[i], sst_ref[i]
        new_states = []
        for j in range(hp):
            cols = slice(j * dk, (j + 1) * dk)
            q = _rotate(q_ref[rows, cols], cc, ss)
            kt = _rotate_t(kt_ref[i, cols, :], cct, sst) * scale
            v = v_ref[rows, cols]
            scores = _dot(q, kt) * dec_ref[j]
            o = _dot(scores, v) + _dot(q * qd_ref[j], states[j])
            new_states.append(cd_ref[j, 0:1, :] * states[j] + _dot(kt * kdt_ref[j], v))
            y_ref[rows, cols] = _ret_gate(o, g_ref[rows, cols]).astype(y_ref.dtype)
        return tuple(new_states)

    states = lax.fori_loop(0, q_ref.shape[0] // c, chunk, tuple(s0_ref[0, j] for j in range(hp)), unroll=4)
    for j in range(hp):
        so_ref[0, j] = states[j]


def _short_state_pass(qd, kt, v, s0_ref, so_ref, hd, decay_of, l):
    rows, dk = v.shape
    row = lax.broadcasted_iota(jnp.int32, (rows, dk), 0)
    v_cols = jnp.concatenate([jnp.where(row // l == b, v, 0.0).astype(BF16) for b in range(rows // l)], axis=1)
    kv = jnp.dot(kt.astype(BF16), v_cols, preferred_element_type=F32)
    win = 16
    per = win // l
    wrow = lax.broadcasted_iota(jnp.int32, (win, dk), 0)
    outs = []
    for w in range(rows // win):
        qw = qd[w * win:(w + 1) * win].astype(BF16)
        oi = jnp.zeros((win, dk), F32)
        for j in range(per):
            b = w * per + j
            s = s0_ref[b, hd]
            oi = jnp.where(wrow // l == j, jnp.dot(qw, s.astype(BF16), preferred_element_type=F32), oi)
            so_ref[b, hd] = decay_of(b) * s + kv[:, b * dk:(b + 1) * dk]
        outs.append(oi)
    return jnp.concatenate(outs, axis=0)


def _ret_short_kernel(q_ref, k_ref, v_ref, g_ref, cc_ref, ss_ref, dec_ref, qd_ref, kd_ref, cd_ref, s0_ref,
                      y_ref, so_ref, *, l, hp):
    dk = q_ref.shape[1] // hp
    scale = dk ** -0.5
    cc, ss = cc_ref[...], ss_ref[...]
    for hd in range(hp):
        cols = slice(hd * dk, (hd + 1) * dk)
        q = _rotate(q_ref[:, cols], cc, ss)
        k = _rotate(k_ref[:, cols], cc, ss) * scale
        v = v_ref[:, cols]
        o_intra = _dot(_dot_nt(q, k) * dec_ref[hd], v)
        cd = cd_ref[hd, 0:1, :]
        o_inter = _short_state_pass(q * qd_ref[hd], (k * kd_ref[hd]).T, v, s0_ref, so_ref, hd, lambda b: cd, l)
        y_ref[:, cols] = _ret_gate(o_intra + o_inter, g_ref[:, cols]).astype(y_ref.dtype)


def _retention(x, nw, w_in_bf, w_kt_bf, col0, pos, s0, n_seq, l):
    n = x.shape[0]
    dk = s0.shape[-1]
    width = RET_HEADS * dk
    out_shape = [jax.ShapeDtypeStruct((n, width), _mix_dtype(l)), jax.ShapeDtypeStruct(s0.shape, F32)]
    cspec = lambda a, hp: pl.BlockSpec((hp,) + a.shape[1:], lambda b, h: (h, 0, 0))
    if l % RET_CHUNK == 0:
        c, hp = RET_CHUNK, RET_HEADS_PER_STEP
        n_hb = RET_HEADS // hp
        z, kt = _proj(x, nw, w_in_bf, ((col0, width), (col0 + 2 * width, 2 * width)), wt_bf=w_kt_bf, t_chunk=c)
        cc, ss, cct, sst = _rope_tables(pos, dk, t_chunk=c)
        decay, q_dec, k_dec, chunk_dec = _ret_consts(c, c)
        k_dec_t = jnp.swapaxes(k_dec, 1, 2)
        zspec = lambda field: pl.BlockSpec((l, hp * dk), lambda b, h: (b, field * n_hb + h))
        sspec = pl.BlockSpec((1, hp, dk, dk), lambda b, h: (b, h, 0, 0))
        return pl.pallas_call(
            functools.partial(_ret_long_kernel, c=c, hp=hp), grid=(n_seq, n_hb),
            in_specs=[zspec(0), pl.BlockSpec((l // c, hp * dk, c), lambda b, h: (b, h, 0)), zspec(1), zspec(2),
                      _full(cc.shape), _full(ss.shape), _full(cct.shape), _full(sst.shape),
                      cspec(decay, hp), cspec(q_dec, hp), cspec(k_dec_t, hp), cspec(chunk_dec, hp), sspec],
            out_specs=[pl.BlockSpec((l, hp * dk), lambda b, h: (b, h)), sspec],
            out_shape=out_shape, compiler_params=_cparams(2), name="retention")(
                z, kt, z, z, cc, ss, cct, sst, decay, q_dec, k_dec_t, chunk_dec, s0)
    rows, hp = LANES, RET_SHORT_HEADS_PER_STEP
    per_blk = rows // l
    n_hb = RET_HEADS // hp
    z = _proj(x, nw, w_in_bf, segments=((col0, 4 * width),))
    cc, ss = _rope_tables(pos, dk)
    decay, q_dec, k_dec, chunk_dec = _ret_consts(rows, l)
    zspec = lambda field: pl.BlockSpec((rows, hp * dk), lambda b, h: (b, field * n_hb + h))
    tspec = pl.BlockSpec((rows, dk), lambda b, h: (b, 0))
    sspec = pl.BlockSpec((per_blk, hp, dk, dk), lambda b, h: (b, h, 0, 0))
    return pl.pallas_call(
        functools.partial(_ret_short_kernel, l=l, hp=hp), grid=(n_seq // per_blk, n_hb),
        in_specs=[zspec(0), zspec(1), zspec(2), zspec(3), tspec, tspec,
                  cspec(decay, hp), cspec(q_dec, hp), cspec(k_dec, hp), cspec(chunk_dec, hp), sspec],
        out_specs=[pl.BlockSpec((rows, hp * dk), lambda b, h: (b, h)), sspec],
        out_shape=out_shape,
        compiler_params=_cparams(2), name="retention")(z, z, z, z, cc, ss, decay, q_dec, k_dec, chunk_dec, s0)


def _cumsum_rows(x, period):
    row = lax.broadcasted_iota(jnp.int32, x.shape, 0)
    s = 1
    while s < period:
        x = x + jnp.where(row % period >= s, pltpu.roll(x, s, 0), 0.0)
        s *= 2
    return x


def _row_of_block(x, period, offset):
    n, w = x.shape
    x3 = x.reshape(n // period, period, w)
    return jnp.broadcast_to(x3[:, offset:offset + 1, :], x3.shape).reshape(n, w)


def _hgrn_log2_forget(fl, lg, layer):
    e = jnp.exp(lg - jnp.max(lg, axis=0, keepdims=True))
    soft = e / jnp.sum(e, axis=0, keepdims=True)
    cum = soft[0:1]
    for i in range(1, layer + 1):
        cum = cum + soft[i:i + 1]
    lb = cum - soft[0:1]
    return jnp.log2(lb + (1.0 - lb) * _sigmoid(fl))


def _hgrn_out(o, gate, nw):
    o = o * lax.rsqrt(jnp.mean(o * o, axis=-1, keepdims=True) + NORM_EPS) * nw
    return o * gate


def _hgrn_levels(c):
    levels, half = [], HG_BLOCK
    while half < c:
        levels.append(half)
        half *= 2
    return levels


def _hgrn_long_kernel(q_ref, f_ref, v_ref, g_ref, nw_ref, s0_ref, y_ref, so_ref, *, c, hp):
    dk = q_ref.shape[1] // hp
    nw = nw_ref[...]
    nv = c // SUBLANES
    bv = HG_BLOCK // SUBLANES
    levels = _hgrn_levels(c)
    row = lax.broadcasted_iota(jnp.int32, (c, c), 0)
    col = lax.broadcasted_iota(jnp.int32, (c, c), 1)
    lvl = jnp.where((row // HG_BLOCK == col // HG_BLOCK) & (col <= row), 0, -1)
    for li, half in enumerate(levels, 1):
        lvl = jnp.where((row // (2 * half) == col // (2 * half)) & (row % (2 * half) >= half)
                        & (col % (2 * half) < half), li, lvl)
    sub = lax.broadcasted_iota(jnp.int32, (SUBLANES, dk), 0)
    zeros = jnp.zeros((SUBLANES, dk), F32)
    cat = lambda slabs: jnp.concatenate(slabs, axis=0)

    def one_head(rows, cols, s):
        q, v = q_ref[rows, cols], v_ref[rows, cols]
        lf = f_ref[rows, cols]
        k = 1.0 - jnp.exp2(lf)
        slabs = lambda a: [a[j * SUBLANES:(j + 1) * SUBLANES, :] for j in range(nv)]
        qs, ks = slabs(q), slabs(k)
        loc, run = [], [jnp.zeros((1, dk), F32)]
        for x in slabs(lf):
            step = 1
            while step < SUBLANES:
                x = x + jnp.where(sub >= step, pltpu.roll(x, step, 0), 0.0)
                step *= 2
            loc.append(x)
            run.append(run[-1] + x[SUBLANES - 1:, :])
        d0 = [loc[j] + (run[j] - run[j - j % bv]) if j % bv else loc[j] for j in range(nv)]
        scores = jnp.where(lvl == 0, _dot_nt(cat([qs[j] * jnp.exp2(d0[j]) for j in range(nv)]),
                                             cat([ks[j] * jnp.exp2(-d0[j]) for j in range(nv)])), 0.0)
        for li, half in enumerate(levels, 1):
            hv = half // SUBLANES
            q_side, k_side = [], []
            for j in range(nv):
                anchor = run[j - j % (2 * hv) + hv]
                if j % (2 * hv) >= hv:
                    q_side.append(qs[j] * jnp.exp2(loc[j] + (run[j] - anchor)))
                    k_side.append(zeros)
                else:
                    q_side.append(zeros)
                    k_side.append(ks[j] * jnp.exp2((anchor - run[j]) - loc[j]))
            scores = jnp.where(lvl == li, _dot_nt(cat(q_side), cat(k_side)), scores)
        o = _dot(scores, v) + _dot(cat([qs[j] * jnp.exp2(loc[j] + run[j]) for j in range(nv)]), s)
        k_hat = cat([ks[j] * jnp.exp2((run[nv] - run[j]) - loc[j]) for j in range(nv)])
        d_col = jnp.exp2(jnp.broadcast_to(run[nv], (dk, dk)).T)
        y_ref[rows, cols] = _hgrn_out(o, g_ref[rows, cols], nw).astype(y_ref.dtype)
        return d_col * s + _dot(k_hat.T, v)

    def chunk(i, states):
        rows = pl.ds(pl.multiple_of(i * c, c), c)
        return tuple(one_head(rows, slice(hd * dk, (hd + 1) * dk), states[hd]) for hd in range(hp))

    states = lax.fori_loop(0, q_ref.shape[0] // c, chunk, tuple(s0_ref[0, j] for j in range(hp)), unroll=4)
    for j in range(hp):
        so_ref[0, j] = states[j]


def _hgrn_short_kernel(q_ref, f_ref, v_ref, g_ref, nw_ref, s0_ref, y_ref, so_ref, *, l, hp):
    rows = q_ref.shape[0]
    dk = q_ref.shape[1] // hp
    row = lax.broadcasted_iota(jnp.int32, (rows, rows), 0)
    col = lax.broadcasted_iota(jnp.int32, (rows, rows), 1)
    causal = (row // l == col // l) & (col <= row)
    for hd in range(hp):
        cols = slice(hd * dk, (hd + 1) * dk)
        q, v = q_ref[:, cols], v_ref[:, cols]
        lf = f_ref[:, cols]
        k = 1.0 - jnp.exp2(lf)
        b = _cumsum_rows(lf, l)
        qe = q * jnp.exp2(b)
        o_intra = _dot(jnp.where(causal, _dot_nt(qe, k * jnp.exp2(-b)), 0.0), v)
        b_last = _row_of_block(b, l, l - 1)
        dect = jnp.exp2(b_last).T
        o_inter = _short_state_pass(qe, (k * jnp.exp2(b_last - b)).T, v, s0_ref, so_ref, hd,
                                    lambda s, dect=dect: dect[:, s * l:s * l + 1], l)
        y_ref[:, cols] = _hgrn_out(o_intra + o_inter, g_ref[:, cols], nw_ref[...]).astype(y_ref.dtype)


def _hgrn(z, norm_w, s0, n_seq, l):
    n = z.shape[0]
    dk = s0.shape[-1]
    if l % HG_CHUNK == 0:
        rows, n_blk, hp = l, n_seq, HG_HEADS_PER_STEP
        kern = functools.partial(_hgrn_long_kernel, c=HG_CHUNK, hp=hp)
        sspec = pl.BlockSpec((1, hp, dk, dk), lambda b, h: (b, h, 0, 0))
    else:
        assert HG_BLOCK % l == 0
        rows, hp = LANES, HG_SHORT_HEADS_PER_STEP
        per_blk = rows // l
        n_blk = n_seq // per_blk
        kern = functools.partial(_hgrn_short_kernel, l=l, hp=hp)
        sspec = pl.BlockSpec((per_blk, hp, dk, dk), lambda b, h: (b, h, 0, 0))
    n_hb = HG_HEADS // hp
    zspec = lambda rows, field: pl.BlockSpec((rows, hp * dk), lambda b, h: (b, field * n_hb + h))
    in_specs = [zspec(rows, 2), zspec(rows, 0), zspec(rows, 3), zspec(rows, 1), _full((1, dk)), sspec]
    out_specs = [pl.BlockSpec((rows, hp * dk), lambda b, h: (b, h)), sspec]
    out_shape = [jax.ShapeDtypeStruct((n, HG_HEADS * dk), _mix_dtype(l)), jax.ShapeDtypeStruct(s0.shape, F32)]
    if l % HG_CHUNK:
        in_specs[-1] = pl.BlockSpec(sspec.block_shape, sspec.index_map, pipeline_mode=pl.Buffered(3))
        pipeline = pltpu.emit_pipeline(kern, grid=(n_blk, n_hb), in_specs=in_specs, out_specs=out_specs)

        def outer(z_ref, nw_ref, s0_ref, y_ref, so_ref):
            pipeline(z_ref, z_ref, z_ref, z_ref, nw_ref, s0_ref, y_ref, so_ref)

        any_spec = pl.BlockSpec(memory_space=pl.ANY)
        return pl.pallas_call(
            outer, in_specs=[any_spec] * 3, out_specs=[any_spec] * 2, out_shape=out_shape,
            compiler_params=pltpu.CompilerParams(vmem_limit_bytes=VMEM_LIMIT_BYTES), name="hgrn")(z, norm_w, s0)
    y, s_new = pl.pallas_call(
        kern, grid=(n_blk, n_hb), in_specs=in_specs, out_specs=out_specs, out_shape=out_shape,
        compiler_params=_cparams(2), name="hgrn")(z, z, z, z, norm_w, s0)
    return y, s_new


def _ffn_kernel(*refs, n_parts, long_mode, final_norm, l):
    x_ref, parts = refs[0], refs[1:1 + n_parts]
    (wo_ref, nw_ref, wg_ref, wu_ref, cw_ref, cb_ref, wd_ref, fnw_ref, buf_ref, o_ref, bufo_ref,
     act_s) = refs[1 + n_parts:13 + n_parts]
    d_ff = wg_ref.shape[1]
    if long_mode:
        tail_s = refs[13 + n_parts]
        tm = x_ref.shape[1]
        rc = min(ROW_TILE, tm)
        load = lambda ref, r0: ref[0, r0:r0 + rc, :]

        @pl.when(pl.program_id(1) == 0)
        def _():
            tail_s[...] = buf_ref[0]
    else:
        nb = x_ref.shape[0]
        tm = rc = l * nb
        load = lambda ref, r0: jnp.concatenate([ref[:, t, :] for t in range(l)], axis=0)

    xs = []
    for r0 in range(0, tm, rc):
        mix = jnp.concatenate([load(p, r0).astype(BF16) for p in parts], axis=1)
        x = load(x_ref, r0) + jnp.dot(mix, wo_ref[...], preferred_element_type=F32)
        xs.append(x)
        h = _rmsnorm(x, nw_ref[...]).astype(BF16)
        for c0 in range(0, d_ff, FF_CHUNK):
            cs = slice(c0, c0 + FF_CHUNK)
            g = jnp.dot(h, wg_ref[:, cs], preferred_element_type=F32)
            up = jnp.dot(h, wu_ref[:, cs], preferred_element_type=F32)
            if long_mode:
                row = lax.broadcasted_iota(jnp.int32, g.shape, 0)
                p1 = jnp.where(row < 1, tail_s[1:2, cs], pltpu.roll(g, 1, 0))
                p2 = jnp.where(row < 2, jnp.where(row == 0, tail_s[0:1, cs], tail_s[1:2, cs]),
                               pltpu.roll(g, 2, 0))
                tail_s[:, cs] = g[rc - (CONV_W - 1):, :]
            else:
                b0, b1 = buf_ref[:, 0, cs], buf_ref[:, 1, cs]
                p1 = jnp.concatenate([b1, g[:tm - nb]], axis=0)
                p2 = jnp.concatenate([b0, b1, g[:tm - 2 * nb]], axis=0)
                bufo_ref[:, 0, cs] = g[tm - 2 * nb:tm - nb]
                bufo_ref[:, 1, cs] = g[tm - nb:]
            conv = cb_ref[:, cs] + cw_ref[0:1, cs] * p2
            conv = conv + cw_ref[1:2, cs] * p1
            conv = conv + cw_ref[2:3, cs] * g
            act_s[r0:r0 + rc, cs] = (jax.nn.silu(conv) * up).astype(BF16)
    y = jnp.concatenate(xs, axis=0) + jnp.dot(act_s[...], wd_ref[...], preferred_element_type=F32)
    if final_norm:
        y = _rmsnorm(y, fnw_ref[...])
    if long_mode:
        o_ref[0] = y
        bufo_ref[0] = tail_s[...]
    else:
        for t in range(l):
            o_ref[:, t, :] = y[t * nb:(t + 1) * nb]


def _ffn(x, parts, layer, wo, nw, wg, wu, conv_w, conv_b, wd, fnw, buf, n_seq, l, final_norm):
    assert CONV_W == 3 and l >= CONV_W - 1
    n, d = x.shape
    d_ff = wg.shape[2]
    weights = (wo, nw, wg, wu, conv_w, conv_b, wd, fnw)
    stacked = (wg, wu, wd)

    def wspec(w, single_buffer):
        kw = dict(pipeline_mode=pl.Buffered(1)) if single_buffer else {}
        if any(w is s for s in stacked):
            return pl.BlockSpec((None,) + w.shape[1:], lambda *_: (layer, 0, 0), **kw)
        return pl.BlockSpec(w.shape, lambda *_: (0,) * w.ndim, **kw)
    kern = functools.partial(_ffn_kernel, n_parts=len(parts), final_norm=final_norm, l=l)
    buf_shape = jax.ShapeDtypeStruct((n_seq, CONV_W - 1, d_ff), F32)
    rows3 = lambda a: a.reshape(n_seq, l, a.shape[1])
    if l % ROW_TILE == 0:
        tm = FFN_ROW_TILE
        assert l % tm == 0
        rspec = lambda w: pl.BlockSpec((1, tm, w), lambda b, i: (b, i, 0))
        bspec = pl.BlockSpec((1, CONV_W - 1, d_ff), lambda b, i: (b, 0, 0))
        y, buf_new = pl.pallas_call(
            functools.partial(kern, long_mode=True),
            grid=(n_seq, l // tm),
            in_specs=([rspec(d)] + [rspec(p.shape[1]) for p in parts]
                      + [wspec(w, True) for w in weights]
                      + [pl.BlockSpec((None, 1, CONV_W - 1, d_ff), lambda b, i: (layer, b, 0, 0))]),
            out_specs=[rspec(d), bspec],
            out_shape=[jax.ShapeDtypeStruct((n_seq, l, d), F32), buf_shape],
            scratch_shapes=[pltpu.VMEM((tm, d_ff), BF16), pltpu.VMEM((CONV_W - 1, d_ff), F32)],
            compiler_params=_cparams(2), name="ffn")(rows3(x), *map(rows3, parts), *weights, buf)
        return y.reshape(n, d), buf_new
    y, buf_new = pl.pallas_call(
        functools.partial(kern, long_mode=False),
        grid=(1,),
        in_specs=([_full((n_seq, l, d))] + [_full((n_seq, l, p.shape[1])) for p in parts]
                  + [wspec(w, False) for w in weights]
                  + [pl.BlockSpec((None,) + buf.shape[1:], lambda *_: (layer, 0, 0, 0))]),
        out_specs=[_full((n_seq, l, d)), _full(buf.shape[1:])],
        out_shape=[jax.ShapeDtypeStruct((n_seq, l, d), F32), buf_shape],
        scratch_shapes=[pltpu.VMEM((n, d_ff), BF16)],
        compiler_params=_cparams(1), name="ffn")(
            rows3(x), *map(rows3, parts), *weights, buf)
    return y.reshape(n, d), buf_new


def _trunk(x3, pos, s5_re, s5_im, ret, hg, conv, p, s5_tl):
    b, l, d = x3.shape
    n = b * l
    x = x3.reshape(n, d)
    depth = p['norm_mix'].shape[0]
    new_re, new_im, new_ret, new_hg, new_conv = [], [], [], [], []
    for layer in range(depth):
        j = layer // 2
        nw = p['norm_mix'][layer].reshape(1, d)
        if layer % 2 == 0:
            s5w = p['s5'][j]
            width = s5w['d'].shape[1]
            y_s5, r, im = _s5_mixer(x.reshape(b, l, d), nw, p['w_in_ab'][j], s5w['bmat_re'], s5w['bmat_im'],
                                    s5w['cmat_re'], s5w['cmat_im'], s5w['d'], s5w['glu_w'], s5w['glu_b'],
                                    s5w['a_re'], s5w['a_im'], s5_re[j].reshape(b, -1), s5_im[j].reshape(b, -1),
                                    s5_tl)
            y_ret, st = _retention(x, nw, p['w_in_ab'][j], p['w_kt'][j], width, pos, ret[j], b, l)
            parts, w_out = [y_s5, y_ret], p['w_out_ab'][j]
            new_re.append(r.reshape(s5_re.shape[1:]))
            new_im.append(im.reshape(s5_im.shape[1:]))
            new_ret.append(st)
        else:
            hw = p['hg_lb_logits'].shape[1]
            z = _proj(x, nw, p['w_in_c'][j],
                      segments=((hw, hw, "log2_forget"), (3 * hw, hw, "silu"), (0, hw), (2 * hw, hw)),
                      gate_logits=p['hg_lb_logits'], gate_layer=layer)
            y_hg, st = _hgrn(z, p['hg_norm_w'][j].reshape(1, -1), hg[j], b, l)
            parts, w_out = [y_hg], p['w_out_c'][j]
            new_hg.append(st)
        x, buf = _ffn(x, parts, layer, w_out, p['norm_ffn'][layer].reshape(1, d), p['ffn_w_gate'],
                      p['ffn_w_up'], p['ffn_conv_w'][layer], p['ffn_conv_b'][layer].reshape(1, -1),
                      p['ffn_w_down'], p['norm_final'].reshape(1, d), conv, b, l,
                      final_norm=(layer == depth - 1))
        new_conv.append(buf)
    return (x.reshape(b, l, d), jnp.stack(new_re), jnp.stack(new_im), jnp.stack(new_ret),
            jnp.stack(new_hg), jnp.stack(new_conv))


def kernel(x_prompt, x_sample, state_s5_re, state_s5_im, state_ret, state_hgrn, state_ffn_conv, pos_sample, norm_mix, norm_ffn, norm_final, w_in_ab, s5_lam_re, s5_lam_im, s5_log_dt, s5_b_re, s5_b_im, s5_c_re, s5_c_im, s5_d, s5_glu_w, s5_glu_b, w_out_ab, w_in_c, hg_lb_logits, hg_norm_w, w_out_c, ffn_w_gate, ffn_w_up, ffn_conv_w, ffn_conv_b, ffn_w_down):
    n_ab, n_grp, n_st = s5_lam_re.shape
    ch = s5_b_re.shape[-1]
    width = n_grp * ch
    ret_width = state_ret.shape[2] * state_ret.shape[3]
    grp_per_blk = MXU_DIM // ch
    s5 = []
    for j in range(n_ab):
        a_re, a_im, bmat_re, bmat_im, cmat_re, cmat_im = _s5_prep(
            s5_lam_re[j], s5_lam_im[j], s5_log_dt[j], s5_b_re[j], s5_b_im[j], s5_c_re[j], s5_c_im[j], grp_per_blk)
        s5.append(dict(bmat_re=bmat_re, bmat_im=bmat_im, cmat_re=cmat_re, cmat_im=cmat_im,
                       d=s5_d[j].reshape(1, width), glu_w=s5_glu_w[j].astype(BF16),
                       glu_b=s5_glu_b[j].reshape(1, width),
                       a_re=a_re.reshape(1, n_grp * n_st), a_im=a_im.reshape(1, n_grp * n_st)))
    per_layer_bf16 = lambda w: [w[i].astype(BF16) for i in range(w.shape[0])]
    p = dict(norm_mix=norm_mix, norm_ffn=norm_ffn, norm_final=norm_final, s5=s5,
             w_in_ab=per_layer_bf16(w_in_ab), w_out_ab=per_layer_bf16(w_out_ab),
             w_kt=[w_in_ab[j][:, width + ret_width:width + 2 * ret_width].T.astype(BF16) for j in range(n_ab)],
             w_in_c=per_layer_bf16(w_in_c), hg_lb_logits=hg_lb_logits, hg_norm_w=hg_norm_w,
             w_out_c=per_layer_bf16(w_out_c), ffn_w_gate=ffn_w_gate.astype(BF16),
             ffn_w_up=ffn_w_up.astype(BF16), ffn_conv_w=ffn_conv_w, ffn_conv_b=ffn_conv_b,
             ffn_w_down=ffn_w_down.astype(BF16))

    bp, lp, _ = x_prompt.shape
    z_s5 = jnp.zeros((n_ab, bp) + state_s5_re.shape[2:], F32)
    z_ret = jnp.zeros((n_ab, bp) + state_ret.shape[2:], F32)
    z_hg = jnp.zeros((state_hgrn.shape[0], bp) + state_hgrn.shape[2:], F32)
    z_conv = jnp.zeros((norm_mix.shape[0], bp) + state_ffn_conv.shape[2:], F32)
    outs_p = _trunk(x_prompt, jnp.arange(lp, dtype=jnp.int32), z_s5, z_s5, z_ret, z_hg, z_conv, p,
                    s5_tl=S5_ROW_TILE // bp)
    ls = x_sample.shape[1]
    pos_s = (pos_sample[:, None] + jnp.arange(ls, dtype=jnp.int32)[None, :]).reshape(-1)
    outs_s = _trunk(x_sample, pos_s, state_s5_re, state_s5_im, state_ret, state_hgrn, state_ffn_conv, p,
                    s5_tl=ls)
    return (outs_p[0], outs_s[0]) + outs_p[1:] + outs_s[1:]
```

```python
import functools

import jax
import jax.numpy as jnp
import numpy as np
from jax import lax
from jax.experimental import pallas as pl
from jax.experimental.pallas import tpu as pltpu

F32 = jnp.float32
BF16 = jnp.bfloat16

NORM_EPS = 1e-6
ROPE_BASE = 10000.0
RET_HEADS = 4
RET_CHUNK = 128
HG_HEADS = 8
HG_BLOCK = 16
HG_CHUNK = 128
RET_HEADS_PER_STEP = 4
HG_HEADS_PER_STEP = 4
RET_SHORT_HEADS_PER_STEP = 4
HG_SHORT_HEADS_PER_STEP = 2
CONV_W = 3

LANES = 128
SUBLANES = 8
MXU_DIM = 256
VMEM_LIMIT_BYTES = 56 * 1024 * 1024
ROW_TILE = 512
FFN_ROW_TILE = 1024
PROJ_ROW_TILE = 1024
S5_ROW_TILE = 1024
FF_CHUNK = 256
S5_SCAN_COLS = 512


def _cparams(n_grid_dims):
    return pltpu.CompilerParams(dimension_semantics=("arbitrary",) * n_grid_dims,
                                vmem_limit_bytes=VMEM_LIMIT_BYTES)


def _dot(a, b):
    return jnp.dot(a.astype(BF16), b.astype(BF16), preferred_element_type=F32)


def _dot_nt(a, b):
    return lax.dot_general(a.astype(BF16), b.astype(BF16), (((1,), (1,)), ((), ())),
                           preferred_element_type=F32)


def _rmsnorm(x, w):
    return x * lax.rsqrt(jnp.mean(x * x, axis=-1, keepdims=True) + NORM_EPS) * w


def _sigmoid(x):
    return 0.5 * jnp.tanh(0.5 * x) + 0.5


def _full(shape):
    nd = len(shape)
    return pl.BlockSpec(shape, lambda *_: (0,) * nd)


def _mix_dtype(l):
    return BF16 if l % ROW_TILE == 0 else F32


def _resident(shape):
    nd = len(shape)
    return pl.BlockSpec(shape, lambda *_: (0,) * nd, pipeline_mode=pl.Buffered(1))


def _proj_kernel(*refs, segments, col_chunk, row_chunk, t_chunk, gate_layer):
    refs = list(refs)
    x_ref, nw_ref, w_ref = refs[:3]
    rest = refs[3:]
    lg_ref = rest.pop(0) if gate_layer is not None else None
    wt_ref = rest.pop(0) if t_chunk else None
    o_ref = rest.pop(0)
    ot_ref = rest.pop(0) if t_chunk else None
    for r in range(0, x_ref.shape[0], row_chunk):
        rows = slice(r, r + row_chunk)
        h = _rmsnorm(x_ref[rows, :], nw_ref[...]).astype(BF16)
        out0 = 0
        for col0, width, kind in segments:
            for c in range(0, width, col_chunk):
                val = jnp.dot(h, w_ref[:, col0 + c:col0 + c + col_chunk], preferred_element_type=F32)
                if kind == "silu":
                    val = val * _sigmoid(val)
                elif kind == "log2_forget":
                    val = _hgrn_log2_forget(val, lg_ref[:, c:c + col_chunk], gate_layer)
                o_ref[rows, out0 + c:out0 + c + col_chunk] = val
            out0 += width
        if t_chunk:
            zt = _dot_nt(wt_ref[...], h)
            for j in range(row_chunk // t_chunk):
                ot_ref[r // t_chunk + j] = zt[:, j * t_chunk:(j + 1) * t_chunk]


def _proj(x, nw, w_bf, segments=None, wt_bf=None, t_chunk=0, gate_logits=None, gate_layer=None):
    n, d = x.shape
    segments = tuple((tuple(s) + (None,))[:3] for s in (segments or ((0, w_bf.shape[1]),)))
    n_out = sum(width for _, width, _ in segments)
    tm = min(PROJ_ROW_TILE, n)
    in_specs = [pl.BlockSpec((tm, d), lambda i: (i, 0)), _full((1, d)), _resident(w_bf.shape)]
    out_specs = [pl.BlockSpec((tm, n_out), lambda i: (i, 0))]
    out_shape = [jax.ShapeDtypeStruct((n, n_out), F32)]
    args = [x, nw, w_bf]
    if gate_layer is not None:
        in_specs.append(_full(gate_logits.shape))
        args.append(gate_logits)
    if t_chunk:
        t_width = wt_bf.shape[0]
        in_specs.append(_resident(wt_bf.shape))
        out_specs.append(pl.BlockSpec((tm // t_chunk, t_width, t_chunk), lambda i: (i, 0, 0)))
        out_shape.append(jax.ShapeDtypeStruct((n // t_chunk, t_width, t_chunk), F32))
        args.append(wt_bf)
    outs = pl.pallas_call(
        functools.partial(_proj_kernel, segments=segments, col_chunk=2 * MXU_DIM, row_chunk=min(ROW_TILE, tm),
                          t_chunk=t_chunk, gate_layer=gate_layer),
        grid=(n // tm,), in_specs=in_specs, out_specs=out_specs, out_shape=out_shape,
        compiler_params=_cparams(1), name="proj")(*args)
    return outs if t_chunk else outs[0]


def _s5_prep_kernel(lre_ref, lim_ref, ldt_ref, lre16_ref, lim16_ref, ldt16_ref, bre_ref, bim_ref, cre_ref, cim_ref,
                    are_ref, aim_ref, bmre_ref, bmim_ref, cmre_ref, cmim_ref, *, ch):
    def block_diag(m):
        rows, p = m.shape
        tiled = jnp.concatenate([m] * (rows // ch), axis=1)
        row = lax.broadcasted_iota(jnp.int32, tiled.shape, 0)
        col = lax.broadcasted_iota(jnp.int32, tiled.shape, 1)
        return jnp.where(col // p == row // ch, tiled, 0.0)

    def disc(lre, lim, ldt):
        dt = jnp.exp(ldt)
        mag = jnp.exp(lre * dt)
        ang = lim * dt
        return mag * jnp.cos(ang), mag * jnp.sin(ang)

    ab_re, ab_im = disc(lre_ref[...], lim_ref[...], ldt_ref[...])
    are_ref[...] = ab_re
    aim_ref[...] = ab_im
    lre, lim = lre16_ref[...], lim16_ref[...]
    ab_re, ab_im = disc(lre, lim, ldt16_ref[...])
    nr, ni = ab_re - 1.0, ab_im
    den = lre * lre + lim * lim
    f_re = (nr * lre + ni * lim) / den
    f_im = (ni * lre - nr * lim) / den
    b_re, b_im = bre_ref[...], bim_ref[...]
    bb_re = f_re * b_re - f_im * b_im
    bb_im = f_re * b_im + f_im * b_re
    kw = bmre_ref.shape[1]
    for kb in range(bmre_ref.shape[0]):
        rows = slice(kb * kw, (kb + 1) * kw)
        bmre_ref[kb] = block_diag(bb_re[rows]).astype(bmre_ref.dtype)
        bmim_ref[kb] = block_diag(bb_im[rows]).astype(bmim_ref.dtype)
        cmre_ref[kb] = block_diag(cre_ref[rows, :]).T.astype(cmre_ref.dtype)
        cmim_ref[kb] = block_diag(cim_ref[rows, :]).T.astype(cmim_ref.dtype)


def _s5_prep(lam_re, lam_im, log_dt, b_re, b_im, c_re, c_im, grp_per_blk):
    g, p = lam_re.shape
    ch = b_re.shape[-1]
    rep = lambda a: jnp.repeat(a, ch, axis=0)
    ldt = log_dt.reshape(g, 1)
    bt = lambda b: jnp.swapaxes(b, 1, 2).reshape(g * ch, p)
    args = (lam_re, lam_im, ldt, rep(lam_re), rep(lam_im), rep(ldt), bt(b_re), bt(b_im),
            c_re.reshape(g * ch, p), c_im.reshape(g * ch, p))
    n_blk, kw, cw = g // grp_per_blk, grp_per_blk * ch, grp_per_blk * p
    shapes = [(g, p)] * 2 + [(n_blk, kw, cw)] * 2 + [(n_blk, cw, kw)] * 2
    dtypes = [F32] * 2 + [BF16] * 4
    return pl.pallas_call(
        functools.partial(_s5_prep_kernel, ch=ch),
        in_specs=[_full(a.shape) for a in args],
        out_specs=[_full(s) for s in shapes],
        out_shape=[jax.ShapeDtypeStruct(s, t) for s, t in zip(shapes, dtypes)],
        name="s5_prep")(*args)


def _s5_kernel(x_ref, nw_ref, wu_ref, bre_ref, bim_ref, cre_ref, cim_ref, d_ref, gw_ref, gb_ref,
               are_ref, aim_ref, s0re_ref, s0im_ref,
               y_ref, sre_ref, sim_ref, bure_s, buim_s, stre_s, stim_s, *, tb, tl):
    @pl.when(pl.program_id(0) == 0)
    def _():
        stre_s[...] = s0re_ref[...]
        stim_s[...] = s0im_ref[...]

    x = jnp.concatenate([x_ref[:, t, :] for t in range(tl)], axis=0)
    u = jnp.dot(_rmsnorm(x, nw_ref[...]).astype(BF16), wu_ref[...], preferred_element_type=F32)
    ub = u.astype(BF16)

    n_kb, kw, cw = bre_ref.shape
    for kb in range(n_kb):
        uk = ub[:, kb * kw:(kb + 1) * kw]
        bure_s[:, kb * cw:(kb + 1) * cw] = jnp.dot(uk, bre_ref[kb], preferred_element_type=F32)
        buim_s[:, kb * cw:(kb + 1) * cw] = jnp.dot(uk, bim_ref[kb], preferred_element_type=F32)

    n_state = are_ref.shape[1]
    sc = min(S5_SCAN_COLS, n_state)
    for c0 in range(0, n_state, sc):
        cs = slice(c0, c0 + sc)
        a_re = jnp.broadcast_to(are_ref[:, cs], (tb, sc))
        a_im = jnp.broadcast_to(aim_ref[:, cs], (tb, sc))

        def step(t, carry, cs=cs, a_re=a_re, a_im=a_im):
            s_re, s_im = carry
            rows = pl.ds(pl.multiple_of(t * tb, tb), tb)
            n_re = a_re * s_re - a_im * s_im + bure_s[rows, cs]
            n_im = a_re * s_im + a_im * s_re + buim_s[rows, cs]
            bure_s[rows, cs] = n_re
            buim_s[rows, cs] = n_im
            return n_re, n_im

        s_re, s_im = lax.fori_loop(0, tl, step, (stre_s[:, cs], stim_s[:, cs]), unroll=True)
        stre_s[:, cs] = s_re
        stim_s[:, cs] = s_im

    n_ob, ckw, ocw = cre_ref.shape
    ys = []
    for ob in range(n_ob):
        ks = slice(ob * ckw, (ob + 1) * ckw)
        ys.append(_dot(bure_s[:, ks], cre_ref[ob]) - _dot(buim_s[:, ks], cim_ref[ob]))
    y = jnp.concatenate(ys, axis=1) + d_ref[...] * u
    y = jax.nn.gelu(y)
    out = y * jax.nn.sigmoid(_dot(y, gw_ref[...]) + gb_ref[...])
    for t in range(tl):
        y_ref[:, t, :] = out[t * tb:(t + 1) * tb, :]
    sre_ref[...] = stre_s[...]
    sim_ref[...] = stim_s[...]


def _s5_mixer(x3, nw, w_in_bf, bmat_re, bmat_im, cmat_re, cmat_im, d_row, glu_w_bf, glu_b, a_re, a_im,
              s0_re, s0_im, tl):
    b, l, d = x3.shape
    width = d_row.shape[1]
    n_state = a_re.shape[1]
    rows = b * tl
    args = (x3, nw, w_in_bf, bmat_re, bmat_im, cmat_re, cmat_im, d_row, glu_w_bf, glu_b,
            a_re, a_im, s0_re, s0_im)
    in_specs = ([pl.BlockSpec((b, tl, d), lambda i: (0, i, 0)), _full(nw.shape),
                 pl.BlockSpec((d, width), lambda i: (0, 0))] + [_full(a.shape) for a in args[3:]])
    y, s_re, s_im = pl.pallas_call(
        functools.partial(_s5_kernel, tb=b, tl=tl),
        grid=(l // tl,), in_specs=in_specs,
        out_specs=[pl.BlockSpec((b, tl, width), lambda i: (0, i, 0)), _full((b, n_state)), _full((b, n_state))],
        out_shape=[jax.ShapeDtypeStruct((b, l, width), F32),
                   jax.ShapeDtypeStruct((b, n_state), F32), jax.ShapeDtypeStruct((b, n_state), F32)],
        scratch_shapes=[pltpu.VMEM((rows, n_state), F32), pltpu.VMEM((rows, n_state), F32),
                        pltpu.VMEM((b, n_state), F32), pltpu.VMEM((b, n_state), F32)],
        compiler_params=_cparams(1), name="s5_mixer")(*args)
    return y.reshape(b * l, width), s_re, s_im


def _rope_kernel(pos_ref, inv_ref, cc_ref, ss_ref, *t_refs, t_chunk):
    ang = pos_ref[...] * inv_ref[...]
    lane = lax.broadcasted_iota(jnp.int32, ang.shape, 1)
    cc = jnp.cos(ang)
    ss = jnp.where(lane < ang.shape[1] // 2, -jnp.sin(ang), jnp.sin(ang))
    cc_ref[...] = cc
    ss_ref[...] = ss
    if t_chunk:
        cct_ref, sst_ref = t_refs
        for i in range(cct_ref.shape[0]):
            cct_ref[i] = cc[i * t_chunk:(i + 1) * t_chunk].T
            sst_ref[i] = ss[i * t_chunk:(i + 1) * t_chunk].T


def _rope_tables(pos, dk, t_chunk=0):
    n = pos.shape[0]
    inv = 1.0 / (ROPE_BASE ** jnp.linspace(0.0, 1.0, dk // 2, dtype=F32))
    inv = jnp.concatenate([inv, inv]).reshape(1, dk)
    shapes = [(n, dk)] * 2 + ([(n // t_chunk, dk, t_chunk)] * 2 if t_chunk else [])
    return pl.pallas_call(
        functools.partial(_rope_kernel, t_chunk=t_chunk),
        in_specs=[_full((n, 1)), _full((1, dk))],
        out_specs=[_full(s) for s in shapes],
        out_shape=[jax.ShapeDtypeStruct(s, F32) for s in shapes],
        name="rope_tables")(pos.astype(F32).reshape(n, 1), inv)


def _rotate(t, cc, ss):
    return t * cc + pltpu.roll(t, t.shape[1] // 2, 1) * ss


def _rotate_t(t, cc, ss):
    half = t.shape[0] // 2
    return t * cc + jnp.concatenate([t[half:], t[:half]], axis=0) * ss


def _ret_consts(rows, c):
    f32 = np.float32
    lg = np.log(f32(1.0) - f32(2.0) ** (f32(-5.0) - np.arange(RET_HEADS, dtype=f32)))
    r = np.arange(rows)
    idx, blk = r % c, r // c
    diff = idx[:, None] - idx[None, :]
    same = blk[:, None] == blk[None, :]
    decay = np.where((same & (diff >= 0))[None],
                     np.exp(np.maximum(diff, 0)[None].astype(f32) * lg[:, None, None]), f32(0.0))
    wide = lambda v: np.broadcast_to(v[:, :, None], (RET_HEADS, rows, LANES))
    q_dec = wide(np.exp((idx + 1).astype(f32)[None, :] * lg[:, None]))
    k_dec = wide(np.exp((c - 1 - idx).astype(f32)[None, :] * lg[:, None]))
    chunk_dec = np.broadcast_to(np.exp(f32(c) * lg)[:, None, None], (RET_HEADS, SUBLANES, LANES))
    return tuple(jnp.asarray(np.ascontiguousarray(a), F32) for a in (decay, q_dec, k_dec, chunk_dec))


def _ret_gate(o, g):
    o = o * lax.rsqrt(jnp.mean(o * o, axis=-1, keepdims=True) + NORM_EPS)
    return jax.nn.silu(g) * o


def _ret_long_kernel(q_ref, kt_ref, v_ref, g_ref, cc_ref, ss_ref, cct_ref, sst_ref, dec_ref, qd_ref, kdt_ref,
                     cd_ref, s0_ref, y_ref, so_ref, *, c, hp):
    dk = q_ref.shape[1] // hp
    scale = dk ** -0.5

    def chunk(i, states):
        rows = pl.ds(pl.multiple_of(i * c, c), c)
        cc, ss = cc_ref[rows, :], ss_ref[rows, :]
        cct, sst = cct_ref[i], sst_ref[i]
        new_states = []
        for j in range(hp):
            cols = slice(j * dk, (j + 1) * dk)
            q = _rotate(q_ref[rows, cols], cc, ss)
            kt = _rotate_t(kt_ref[i, cols, :], cct, sst) * scale
            v = v_ref[rows, cols]
            scores = _dot(q, kt) * dec_ref[j]
            o = _dot(scores, v) + _dot(q * qd_ref[j], states[j])
            new_states.append(cd_ref[j, 0:1, :] * states[j] + _dot(kt * kdt_ref[j], v))
            y_ref[rows, cols] = _ret_gate(o, g_ref[rows, cols]).astype(y_ref.dtype)
        return tuple(new_states)

    states = lax.fori_loop(0, q_ref.shape[0] // c, chunk, tuple(s0_ref[0, j] for j in range(hp)), unroll=4)
    for j in range(hp):
        so_ref[0, j] = states[j]


def _short_state_pass(qd, kt, v, s0_ref, so_ref, hd, decay_of, l):
    rows, dk = v.shape
    row = lax.broadcasted_iota(jnp.int32, (rows, dk), 0)
    v_cols = jnp.concatenate([jnp.where(row // l == b, v, 0.0).astype(BF16) for b in range(rows // l)], axis=1)
    kv = jnp.dot(kt.astype(BF16), v_cols, preferred_element_type=F32)
    win = 16
    per = win // l
    wrow = lax.broadcasted_iota(jnp.int32, (win, dk), 0)
    outs = []
    for w in range(rows // win):
        qw = qd[w * win:(w + 1) * win].astype(BF16)
        oi = jnp.zeros((win, dk), F32)
        for j in range(per):
            b = w * per + j
            s = s0_ref[b, hd]
            oi = jnp.where(wrow // l == j, jnp.dot(qw, s.astype(BF16), preferred_element_type=F32), oi)
            so_ref[b, hd] = decay_of(b) * s + kv[:, b * dk:(b + 1) * dk]
        outs.append(oi)
    return jnp.concatenate(outs, axis=0)


def _ret_short_kernel(q_ref, k_ref, v_ref, g_ref, cc_ref, ss_ref, dec_ref, qd_ref, kd_ref, cd_ref, s0_ref,
                      y_ref, so_ref, *, l, hp):
    dk = q_ref.shape[1] // hp
    scale = dk ** -0.5
    cc, ss = cc_ref[...], ss_ref[...]
    for hd in range(hp):
        cols = slice(hd * dk, (hd + 1) * dk)
        q = _rotate(q_ref[:, cols], cc, ss)
        k = _rotate(k_ref[:, cols], cc, ss) * scale
        v = v_ref[:, cols]
        o_intra = _dot(_dot_nt(q, k) * dec_ref[hd], v)
        cd = cd_ref[hd, 0:1, :]
        o_inter = _short_state_pass(q * qd_ref[hd], (k * kd_ref[hd]).T, v, s0_ref, so_ref, hd, lambda b: cd, l)
        y_ref[:, cols] = _ret_gate(o_intra + o_inter, g_ref[:, cols]).astype(y_ref.dtype)


def _retention(x, nw, w_in_bf, w_kt_bf, col0, pos, s0, n_seq, l):
    n = x.shape[0]
    dk = s0.shape[-1]
    width = RET_HEADS * dk
    out_shape = [jax.ShapeDtypeStruct((n, width), _mix_dtype(l)), jax.ShapeDtypeStruct(s0.shape, F32)]
    cspec = lambda a, hp: pl.BlockSpec((hp,) + a.shape[1:], lambda b, h: (h, 0, 0))
    if l % RET_CHUNK == 0:
        c, hp = RET_CHUNK, RET_HEADS_PER_STEP
        n_hb = RET_HEADS // hp
        z, kt = _proj(x, nw, w_in_bf, ((col0, width), (col0 + 2 * width, 2 * width)), wt_bf=w_kt_bf, t_chunk=c)
        cc, ss, cct, sst = _rope_tables(pos, dk, t_chunk=c)
        decay, q_dec, k_dec, chunk_dec = _ret_consts(c, c)
        k_dec_t = jnp.swapaxes(k_dec, 1, 2)
        zspec = lambda field: pl.BlockSpec((l, hp * dk), lambda b, h: (b, field * n_hb + h))
        sspec = pl.BlockSpec((1, hp, dk, dk), lambda b, h: (b, h, 0, 0))
        return pl.pallas_call(
            functools.partial(_ret_long_kernel, c=c, hp=hp), grid=(n_seq, n_hb),
            in_specs=[zspec(0), pl.BlockSpec((l // c, hp * dk, c), lambda b, h: (b, h, 0)), zspec(1), zspec(2),
                      _full(cc.shape), _full(ss.shape), _full(cct.shape), _full(sst.shape),
                      cspec(decay, hp), cspec(q_dec, hp), cspec(k_dec_t, hp), cspec(chunk_dec, hp), sspec],
            out_specs=[pl.BlockSpec((l, hp * dk), lambda b, h: (b, h)), sspec],
            out_shape=out_shape, compiler_params=_cparams(2), name="retention")(
                z, kt, z, z, cc, ss, cct, sst, decay, q_dec, k_dec_t, chunk_dec, s0)
    rows, hp = LANES, RET_SHORT_HEADS_PER_STEP
    per_blk = rows // l
    n_hb = RET_HEADS // hp
    z = _proj(x, nw, w_in_bf, segments=((col0, 4 * width),))
    cc, ss = _rope_tables(pos, dk)
    decay, q_dec, k_dec, chunk_dec = _ret_consts(rows, l)
    zspec = lambda field: pl.BlockSpec((rows, hp * dk), lambda b, h: (b, field * n_hb + h))
    tspec = pl.BlockSpec((rows, dk), lambda b, h: (b, 0))
    sspec = pl.BlockSpec((per_blk, hp, dk, dk), lambda b, h: (b, h, 0, 0))
    pipeline = pltpu.emit_pipeline(
        functools.partial(_ret_short_kernel, l=l, hp=hp), grid=(n_seq // per_blk, n_hb),
        in_specs=[zspec(0), zspec(1), zspec(2), zspec(3), tspec, tspec,
                  cspec(decay, hp), cspec(q_dec, hp), cspec(k_dec, hp), cspec(chunk_dec, hp),
                  pl.BlockSpec(sspec.block_shape, sspec.index_map, pipeline_mode=pl.Buffered(3))],
        out_specs=[pl.BlockSpec((rows, hp * dk), lambda b, h: (b, h)), sspec])

    def outer(z_ref, cc_ref, ss_ref, dec_ref, qd_ref, kd_ref, cd_ref, s0_ref, y_ref, so_ref):
        pipeline(z_ref, z_ref, z_ref, z_ref, cc_ref, ss_ref, dec_ref, qd_ref, kd_ref, cd_ref, s0_ref, y_ref, so_ref)

    any_spec = pl.BlockSpec(memory_space=pl.ANY)
    return pl.pallas_call(
        outer, in_specs=[any_spec] * 8, out_specs=[any_spec] * 2, out_shape=out_shape,
        compiler_params=pltpu.CompilerParams(vmem_limit_bytes=VMEM_LIMIT_BYTES), name="retention")(
            z, cc, ss, decay, q_dec, k_dec, chunk_dec, s0)


def _cumsum_rows(x, period):
    row = lax.broadcasted_iota(jnp.int32, x.shape, 0)
    s = 1
    while s < period:
        x = x + jnp.where(row % period >= s, pltpu.roll(x, s, 0), 0.0)
        s *= 2
    return x


def _row_of_block(x, period, offset):
    n, w = x.shape
    x3 = x.reshape(n // period, period, w)
    return jnp.broadcast_to(x3[:, offset:offset + 1, :], x3.shape).reshape(n, w)


def _hgrn_log2_forget(fl, lg, layer):
    e = jnp.exp(lg - jnp.max(lg, axis=0, keepdims=True))
    soft = e / jnp.sum(e, axis=0, keepdims=True)
    cum = soft[0:1]
    for i in range(1, layer + 1):
        cum = cum + soft[i:i + 1]
    lb = cum - soft[0:1]
    return jnp.log2(lb + (1.0 - lb) * _sigmoid(fl))


def _hgrn_out(o, gate, nw):
    o = o * lax.rsqrt(jnp.mean(o * o, axis=-1, keepdims=True) + NORM_EPS) * nw
    return o * gate


def _hgrn_levels(c):
    levels, half = [], HG_BLOCK
    while half < c:
        levels.append(half)
        half *= 2
    return levels


def _hgrn_long_kernel(q_ref, f_ref, v_ref, g_ref, nw_ref, s0_ref, y_ref, so_ref, *, c, hp):
    dk = q_ref.shape[1] // hp
    nw = nw_ref[...]
    nv = c // SUBLANES
    bv = HG_BLOCK // SUBLANES
    levels = _hgrn_levels(c)
    row = lax.broadcasted_iota(jnp.int32, (c, c), 0)
    col = lax.broadcasted_iota(jnp.int32, (c, c), 1)
    lvl = jnp.where((row // HG_BLOCK == col // HG_BLOCK) & (col <= row), 0, -1)
    for li, half in enumerate(levels, 1):
        lvl = jnp.where((row // (2 * half) == col // (2 * half)) & (row % (2 * half) >= half)
                        & (col % (2 * half) < half), li, lvl)
    sub = lax.broadcasted_iota(jnp.int32, (SUBLANES, dk), 0)
    zeros = jnp.zeros((SUBLANES, dk), F32)
    cat = lambda slabs: jnp.concatenate(slabs, axis=0)

    def one_head(rows, cols, s):
        q, v = q_ref[rows, cols], v_ref[rows, cols]
        lf = f_ref[rows, cols]
        k = 1.0 - jnp.exp2(lf)
        slabs = lambda a: [a[j * SUBLANES:(j + 1) * SUBLANES, :] for j in range(nv)]
        qs, ks = slabs(q), slabs(k)
        loc, run = [], [jnp.zeros((1, dk), F32)]
        for x in slabs(lf):
            step = 1
            while step < SUBLANES:
                x = x + jnp.where(sub >= step, pltpu.roll(x, step, 0), 0.0)
                step *= 2
            loc.append(x)
            run.append(run[-1] + x[SUBLANES - 1:, :])
        d0 = [loc[j] + (run[j] - run[j - j % bv]) if j % bv else loc[j] for j in range(nv)]
        scores = jnp.where(lvl == 0, _dot_nt(cat([qs[j] * jnp.exp2(d0[j]) for j in range(nv)]),
                                             cat([ks[j] * jnp.exp2(-d0[j]) for j in range(nv)])), 0.0)
        for li, half in enumerate(levels, 1):
            hv = half // SUBLANES
            q_side, k_side = [], []
            for j in range(nv):
                anchor = run[j - j % (2 * hv) + hv]
                if j % (2 * hv) >= hv:
                    q_side.append(qs[j] * jnp.exp2(loc[j] + (run[j] - anchor)))
                    k_side.append(zeros)
                else:
                    q_side.append(zeros)
                    k_side.append(ks[j] * jnp.exp2((anchor - run[j]) - loc[j]))
            scores = jnp.where(lvl == li, _dot_nt(cat(q_side), cat(k_side)), scores)
        o = _dot(scores, v) + _dot(cat([qs[j] * jnp.exp2(loc[j] + run[j]) for j in range(nv)]), s)
        k_hat = cat([ks[j] * jnp.exp2((run[nv] - run[j]) - loc[j]) for j in range(nv)])
        d_col = jnp.exp2(jnp.broadcast_to(run[nv], (dk, dk)).T)
        y_ref[rows, cols] = _hgrn_out(o, g_ref[rows, cols], nw).astype(y_ref.dtype)
        return d_col * s + _dot(k_hat.T, v)

    def chunk(i, states):
        rows = pl.ds(pl.multiple_of(i * c, c), c)
        return tuple(one_head(rows, slice(hd * dk, (hd + 1) * dk), states[hd]) for hd in range(hp))

    states = lax.fori_loop(0, q_ref.shape[0] // c, chunk, tuple(s0_ref[0, j] for j in range(hp)), unroll=4)
    for j in range(hp):
        so_ref[0, j] = states[j]


def _hgrn_short_kernel(q_ref, f_ref, v_ref, g_ref, nw_ref, s0_ref, y_ref, so_ref, *, l, hp):
    rows = q_ref.shape[0]
    dk = q_ref.shape[1] // hp
    row = lax.broadcasted_iota(jnp.int32, (rows, rows), 0)
    col = lax.broadcasted_iota(jnp.int32, (rows, rows), 1)
    causal = (row // l == col // l) & (col <= row)
    for hd in range(hp):
        cols = slice(hd * dk, (hd + 1) * dk)
        q, v = q_ref[:, cols], v_ref[:, cols]
        lf = f_ref[:, cols]
        k = 1.0 - jnp.exp2(lf)
        b = _cumsum_rows(lf, l)
        qe = q * jnp.exp2(b)
        o_intra = _dot(jnp.where(causal, _dot_nt(qe, k * jnp.exp2(-b)), 0.0), v)
        b_last = _row_of_block(b, l, l - 1)
        dect = jnp.exp2(b_last).T
        o_inter = _short_state_pass(qe, (k * jnp.exp2(b_last - b)).T, v, s0_ref, so_ref, hd,
                                    lambda s, dect=dect: dect[:, s * l:s * l + 1], l)
        y_ref[:, cols] = _hgrn_out(o_intra + o_inter, g_ref[:, cols], nw_ref[...]).astype(y_ref.dtype)


def _hgrn(z, norm_w, s0, n_seq, l):
    n = z.shape[0]
    dk = s0.shape[-1]
    if l % HG_CHUNK == 0:
        rows, n_blk, hp = l, n_seq, HG_HEADS_PER_STEP
        kern = functools.partial(_hgrn_long_kernel, c=HG_CHUNK, hp=hp)
        sspec = pl.BlockSpec((1, hp, dk, dk), lambda b, h: (b, h, 0, 0))
    else:
        assert HG_BLOCK % l == 0
        rows, hp = LANES, HG_SHORT_HEADS_PER_STEP
        per_blk = rows // l
        n_blk = n_seq // per_blk
        kern = functools.partial(_hgrn_short_kernel, l=l, hp=hp)
        sspec = pl.BlockSpec((per_blk, hp, dk, dk), lambda b, h: (b, h, 0, 0))
    n_hb = HG_HEADS // hp
    zspec = lambda rows, field: pl.BlockSpec((rows, hp * dk), lambda b, h: (b, field * n_hb + h))
    in_specs = [zspec(rows, 2), zspec(rows, 0), zspec(rows, 3), zspec(rows, 1), _full((1, dk)), sspec]
    out_specs = [pl.BlockSpec((rows, hp * dk), lambda b, h: (b, h)), sspec]
    out_shape = [jax.ShapeDtypeStruct((n, HG_HEADS * dk), _mix_dtype(l)), jax.ShapeDtypeStruct(s0.shape, F32)]
    if l % HG_CHUNK:
        in_specs[-1] = pl.BlockSpec(sspec.block_shape, sspec.index_map, pipeline_mode=pl.Buffered(3))
        pipeline = pltpu.emit_pipeline(kern, grid=(n_blk, n_hb), in_specs=in_specs, out_specs=out_specs)

        def outer(z_ref, nw_ref, s0_ref, y_ref, so_ref):
            pipeline(z_ref, z_ref, z_ref, z_ref, nw_ref, s0_ref, y_ref, so_ref)

        any_spec = pl.BlockSpec(memory_space=pl.ANY)
        return pl.pallas_call(
            outer, in_specs=[any_spec] * 3, out_specs=[any_spec] * 2, out_shape=out_shape,
            compiler_params=pltpu.CompilerParams(vmem_limit_bytes=VMEM_LIMIT_BYTES), name="hgrn")(z, norm_w, s0)
    y, s_new = pl.pallas_call(
        kern, grid=(n_blk, n_hb), in_specs=in_specs, out_specs=out_specs, out_shape=out_shape,
        compiler_params=_cparams(2), name="hgrn")(z, z, z, z, norm_w, s0)
    return y, s_new


def _ffn_kernel(*refs, n_parts, long_mode, final_norm, l):
    x_ref, parts = refs[0], refs[1:1 + n_parts]
    (wo_ref, nw_ref, wg_ref, wu_ref, cw_ref, cb_ref, wd_ref, fnw_ref, buf_ref, o_ref, bufo_ref,
     act_s) = refs[1 + n_parts:13 + n_parts]
    d_ff = wg_ref.shape[1]
    if long_mode:
        tail_s = refs[13 + n_parts]
        tm = x_ref.shape[1]
        rc = min(ROW_TILE, tm)
        load = lambda ref, r0: ref[0, r0:r0 + rc, :]

        @pl.when(pl.program_id(1) == 0)
        def _():
            tail_s[...] = buf_ref[0]
    else:
        nb = x_ref.shape[0]
        tm = rc = l * nb
        load = lambda ref, r0: jnp.concatenate([ref[:, t, :] for t in range(l)], axis=0)

    xs = []
    for r0 in range(0, tm, rc):
        mix = jnp.concatenate([load(p, r0).astype(BF16) for p in parts], axis=1)
        x = load(x_ref, r0) + jnp.dot(mix, wo_ref[...], preferred_element_type=F32)
        xs.append(x)
        h = _rmsnorm(x, nw_ref[...]).astype(BF16)
        for c0 in range(0, d_ff, FF_CHUNK):
            cs = slice(c0, c0 + FF_CHUNK)
            g = jnp.dot(h, wg_ref[:, cs], preferred_element_type=F32)
            up = jnp.dot(h, wu_ref[:, cs], preferred_element_type=F32)
            if long_mode:
                row = lax.broadcasted_iota(jnp.int32, g.shape, 0)
                p1 = jnp.where(row < 1, tail_s[1:2, cs], pltpu.roll(g, 1, 0))
                p2 = jnp.where(row < 2, jnp.where(row == 0, tail_s[0:1, cs], tail_s[1:2, cs]),
                               pltpu.roll(g, 2, 0))
                tail_s[:, cs] = g[rc - (CONV_W - 1):, :]
            else:
                b0, b1 = buf_ref[:, 0, cs], buf_ref[:, 1, cs]
                p1 = jnp.concatenate([b1, g[:tm - nb]], axis=0)
                p2 = jnp.concatenate([b0, b1, g[:tm - 2 * nb]], axis=0)
                bufo_ref[:, 0, cs] = g[tm - 2 * nb:tm - nb]
                bufo_ref[:, 1, cs] = g[tm - nb:]
            conv = cb_ref[:, cs] + cw_ref[0:1, cs] * p2
            conv = conv + cw_ref[1:2, cs] * p1
            conv = conv + cw_ref[2:3, cs] * g
            act_s[r0:r0 + rc, cs] = (jax.nn.silu(conv) * up).astype(BF16)
    y = jnp.concatenate(xs, axis=0) + jnp.dot(act_s[...], wd_ref[...], preferred_element_type=F32)
    if final_norm:
        y = _rmsnorm(y, fnw_ref[...])
    if long_mode:
        o_ref[0] = y
        bufo_ref[0] = tail_s[...]
    else:
        for t in range(l):
            o_ref[:, t, :] = y[t * nb:(t + 1) * nb]


def _ffn(x, parts, layer, wo, nw, wg, wu, conv_w, conv_b, wd, fnw, buf, n_seq, l, final_norm):
    assert CONV_W == 3 and l >= CONV_W - 1
    n, d = x.shape
    d_ff = wg.shape[2]
    weights = (wo, nw, wg, wu, conv_w, conv_b, wd, fnw)
    stacked = (wg, wu, wd)

    def wspec(w, single_buffer):
        kw = dict(pipeline_mode=pl.Buffered(1)) if single_buffer else {}
        if any(w is s for s in stacked):
            return pl.BlockSpec((None,) + w.shape[1:], lambda *_: (layer, 0, 0), **kw)
        return pl.BlockSpec(w.shape, lambda *_: (0,) * w.ndim, **kw)
    kern = functools.partial(_ffn_kernel, n_parts=len(parts), final_norm=final_norm, l=l)
    buf_shape = jax.ShapeDtypeStruct((n_seq, CONV_W - 1, d_ff), F32)
    rows3 = lambda a: a.reshape(n_seq, l, a.shape[1])
    if l % ROW_TILE == 0:
        tm = FFN_ROW_TILE
        assert l % tm == 0
        rspec = lambda w: pl.BlockSpec((1, tm, w), lambda b, i: (b, i, 0))
        bspec = pl.BlockSpec((1, CONV_W - 1, d_ff), lambda b, i: (b, 0, 0))
        y, buf_new = pl.pallas_call(
            functools.partial(kern, long_mode=True),
            grid=(n_seq, l // tm),
            in_specs=([rspec(d)] + [rspec(p.shape[1]) for p in parts]
                      + [wspec(w, True) for w in weights]
                      + [pl.BlockSpec((None, 1, CONV_W - 1, d_ff), lambda b, i: (layer, b, 0, 0))]),
            out_specs=[rspec(d), bspec],
            out_shape=[jax.ShapeDtypeStruct((n_seq, l, d), F32), buf_shape],
            scratch_shapes=[pltpu.VMEM((tm, d_ff), BF16), pltpu.VMEM((CONV_W - 1, d_ff), F32)],
            compiler_params=_cparams(2), name="ffn")(rows3(x), *map(rows3, parts), *weights, buf)
        return y.reshape(n, d), buf_new
    y, buf_new = pl.pallas_call(
        functools.partial(kern, long_mode=False),
        grid=(1,),
        in_specs=([_full((n_seq, l, d))] + [_full((n_seq, l, p.shape[1])) for p in parts]
                  + [wspec(w, False) for w in weights]
                  + [pl.BlockSpec((None,) + buf.shape[1:], lambda *_: (layer, 0, 0, 0))]),
        out_specs=[_full((n_seq, l, d)), _full(buf.shape[1:])],
        out_shape=[jax.ShapeDtypeStruct((n_seq, l, d), F32), buf_shape],
        scratch_shapes=[pltpu.VMEM((n, d_ff), BF16)],
        compiler_params=_cparams(1), name="ffn")(
            rows3(x), *map(rows3, parts), *weights, buf)
    return y.reshape(n, d), buf_new


def _trunk(x3, pos, s5_re, s5_im, ret, hg, conv, p, s5_tl):
    b, l, d = x3.shape
    n = b * l
    x = x3.reshape(n, d)
    depth = p['norm_mix'].shape[0]
    new_re, new_im, new_ret, new_hg, new_conv = [], [], [], [], []
    for layer in range(depth):
        j = layer // 2
        nw = p['norm_mix'][layer].reshape(1, d)
        if layer % 2 == 0:
            s5w = p['s5'][j]
            width = s5w['d'].shape[1]
            y_s5, r, im = _s5_mixer(x.reshape(b, l, d), nw, p['w_in_ab'][j], s5w['bmat_re'], s5w['bmat_im'],
                                    s5w['cmat_re'], s5w['cmat_im'], s5w['d'], s5w['glu_w'], s5w['glu_b'],
                                    s5w['a_re'], s5w['a_im'], s5_re[j].reshape(b, -1), s5_im[j].reshape(b, -1),
                                    s5_tl)
            y_ret, st = _retention(x, nw, p['w_in_ab'][j], p['w_kt'][j], width, pos, ret[j], b, l)
            parts, w_out = [y_s5, y_ret], p['w_out_ab'][j]
            new_re.append(r.reshape(s5_re.shape[1:]))
            new_im.append(im.reshape(s5_im.shape[1:]))
            new_ret.append(st)
        else:
            hw = p['hg_lb_logits'].shape[1]
            z = _proj(x, nw, p['w_in_c'][j],
                      segments=((hw, hw, "log2_forget"), (3 * hw, hw, "silu"), (0, hw), (2 * hw, hw)),
                      gate_logits=p['hg_lb_logits'], gate_layer=layer)
            y_hg, st = _hgrn(z, p['hg_norm_w'][j].reshape(1, -1), hg[j], b, l)
            parts, w_out = [y_hg], p['w_out_c'][j]
            new_hg.append(st)
        x, buf = _ffn(x, parts, layer, w_out, p['norm_ffn'][layer].reshape(1, d), p['ffn_w_gate'],
                      p['ffn_w_up'], p['ffn_conv_w'][layer], p['ffn_conv_b'][layer].reshape(1, -1),
                      p['ffn_w_down'], p['norm_final'].reshape(1, d), conv, b, l,
                      final_norm=(layer == depth - 1))
        new_conv.append(buf)
    return (x.reshape(b, l, d), jnp.stack(new_re), jnp.stack(new_im), jnp.stack(new_ret),
            jnp.stack(new_hg), jnp.stack(new_conv))


def kernel(x_prompt, x_sample, state_s5_re, state_s5_im, state_ret, state_hgrn, state_ffn_conv, pos_sample, norm_mix, norm_ffn, norm_final, w_in_ab, s5_lam_re, s5_lam_im, s5_log_dt, s5_b_re, s5_b_im, s5_c_re, s5_c_im, s5_d, s5_glu_w, s5_glu_b, w_out_ab, w_in_c, hg_lb_logits, hg_norm_w, w_out_c, ffn_w_gate, ffn_w_up, ffn_conv_w, ffn_conv_b, ffn_w_down):
    n_ab, n_grp, n_st = s5_lam_re.shape
    ch = s5_b_re.shape[-1]
    width = n_grp * ch
    ret_width = state_ret.shape[2] * state_ret.shape[3]
    grp_per_blk = MXU_DIM // ch
    s5 = []
    for j in range(n_ab):
        a_re, a_im, bmat_re, bmat_im, cmat_re, cmat_im = _s5_prep(
            s5_lam_re[j], s5_lam_im[j], s5_log_dt[j], s5_b_re[j], s5_b_im[j], s5_c_re[j], s5_c_im[j], grp_per_blk)
        s5.append(dict(bmat_re=bmat_re, bmat_im=bmat_im, cmat_re=cmat_re, cmat_im=cmat_im,
                       d=s5_d[j].reshape(1, width), glu_w=s5_glu_w[j].astype(BF16),
                       glu_b=s5_glu_b[j].reshape(1, width),
                       a_re=a_re.reshape(1, n_grp * n_st), a_im=a_im.reshape(1, n_grp * n_st)))
    per_layer_bf16 = lambda w: [w[i].astype(BF16) for i in range(w.shape[0])]
    p = dict(norm_mix=norm_mix, norm_ffn=norm_ffn, norm_final=norm_final, s5=s5,
             w_in_ab=per_layer_bf16(w_in_ab), w_out_ab=per_layer_bf16(w_out_ab),
             w_kt=[w_in_ab[j][:, width + ret_width:width + 2 * ret_width].T.astype(BF16) for j in range(n_ab)],
             w_in_c=per_layer_bf16(w_in_c), hg_lb_logits=hg_lb_logits, hg_norm_w=hg_norm_w,
             w_out_c=per_layer_bf16(w_out_c), ffn_w_gate=ffn_w_gate.astype(BF16),
             ffn_w_up=ffn_w_up.astype(BF16), ffn_conv_w=ffn_conv_w, ffn_conv_b=ffn_conv_b,
             ffn_w_down=ffn_w_down.astype(BF16))

    bp, lp, _ = x_prompt.shape
    z_s5 = jnp.zeros((n_ab, bp) + state_s5_re.shape[2:], F32)
    z_ret = jnp.zeros((n_ab, bp) + state_ret.shape[2:], F32)
    z_hg = jnp.zeros((state_hgrn.shape[0], bp) + state_hgrn.shape[2:], F32)
    z_conv = jnp.zeros((norm_mix.shape[0], bp) + state_ffn_conv.shape[2:], F32)
    outs_p = _trunk(x_prompt, jnp.arange(lp, dtype=jnp.int32), z_s5, z_s5, z_ret, z_hg, z_conv, p,
                    s5_tl=S5_ROW_TILE // bp)
    ls = x_sample.shape[1]
    pos_s = (pos_sample[:, None] + jnp.arange(ls, dtype=jnp.int32)[None, :]).reshape(-1)
    outs_s = _trunk(x_sample, pos_s, state_s5_re, state_s5_im, state_ret, state_hgrn, state_ffn_conv, p,
                    s5_tl=ls)
    return (outs_p[0], outs_s[0]) + outs_p[1:] + outs_s[1:]
```
